```python
import math
import jax, jax.numpy as jnp
from jax import lax
import numpy as np

D_MODEL = 1024
BATCH = 8
SEQ = 2048
DEPTH = 2
DEC_BATCH = 128
DEC_SEQ = 8
PAST_LEN = 16384
PAGE_SIZE = 128

MIX_W = D_MODEL
SSM_W = MIX_W // 2
SSM_P = 16
SSM_G = SSM_W // SSM_P
SSM_N = 64
ATTN_W = MIX_W - SSM_W
HEAD_DIM = 64
N_HEADS = ATTN_W // HEAD_DIM
N_KV = 2
GRP = N_HEADS // N_KV
KV_W = N_KV * HEAD_DIM
IN_W = SSM_W + ATTN_W + 2 * KV_W
WINDOW = 128
D_FF = 2816
PLE_DIM = 256
EPS = 1e-6
NEG_INF = -1e30
DT_MIN = 1e-3
DT_MAX = 1e-1

kernel_name = 'hymba_s5_swa_sink_macaron_step'

F32 = jnp.float32


def rms_norm(x, g):
    xf = x.astype(F32)
    y = xf * lax.rsqrt(jnp.mean(xf * xf, axis=-1, keepdims=True) + EPS)
    return (y * g.astype(F32)).astype(x.dtype)


def swiglu(h, w_gate, w_up, w_down):
    return (jax.nn.silu(h @ w_gate) * (h @ w_up)) @ w_down


def _ssm_combine(e1, e2):
    a1, b1 = e1
    a2, b2 = e2
    return a1 * a2, a2[:, None] * b1 + b2


def ssm_scan(u, h0, lam_re, lam_im, log_dt, b_re, b_im, c_re, c_im, d, w_glu):
    bt, l, _ = u.shape
    lam = lax.complex(lam_re.astype(F32), lam_im.astype(F32))
    dt = jnp.exp(log_dt.astype(F32))[:, None]
    a_bar = jnp.exp(lam * dt)
    b = lax.complex(b_re.astype(F32), b_im.astype(F32))
    b_bar = ((a_bar - 1.0) / lam)[..., None] * b
    c = lax.complex(c_re.astype(F32), c_im.astype(F32))
    ug = u.astype(F32).reshape(bt, l, SSM_G, SSM_P).astype(jnp.complex64)
    bu = jnp.einsum('gnp,blgp->lbgn', b_bar, ug)
    if h0 is not None:
        bu = bu.at[0].add(a_bar * h0)
    a = jnp.broadcast_to(a_bar, (l, SSM_G, SSM_N))
    _, h = lax.associative_scan(_ssm_combine, (a, bu), axis=0)
    y = jnp.einsum('gpn,lbgn->blgp', c, h).real.reshape(bt, l, SSM_W)
    y = y + d.astype(F32) * u.astype(F32)
    y = jax.nn.gelu(y)
    y = y * jax.nn.sigmoid(y @ w_glu.astype(F32))
    return y.astype(u.dtype), h[-1]


def sink_softmax(scores, sinks):
    s = jnp.broadcast_to(sinks.astype(F32).reshape(N_KV, GRP, 1, 1), scores.shape[:-1] + (1,))
    return jax.nn.softmax(jnp.concatenate([scores, s], axis=-1), axis=-1)[..., :-1]


def band_mask(n_q, n_k, offset):
    diff = (jnp.arange(n_q)[:, None] + offset) - jnp.arange(n_k)[None, :]
    return (diff >= 0) & (diff < WINDOW)


def attn_prompt(q, k, v, sinks):
    b, l = q.shape[:2]
    nb = l // WINDOW
    qb = q.reshape(b, nb, WINDOW, N_KV, GRP, HEAD_DIM)
    kb = k.reshape(b, nb, WINDOW, N_KV, HEAD_DIM)
    vb = v.reshape(b, nb, WINDOW, N_KV, HEAD_DIM)
    pad = ((0, 0), (1, 0), (0, 0), (0, 0), (0, 0))
    kk = jnp.concatenate([jnp.pad(kb, pad)[:, :-1], kb], axis=2)
    vv = jnp.concatenate([jnp.pad(vb, pad)[:, :-1], vb], axis=2)
    band = band_mask(WINDOW, 2 * WINDOW, WINDOW)
    key_pos = jnp.arange(nb)[:, None, None] * WINDOW - WINDOW + jnp.arange(2 * WINDOW)[None, None, :]
    mask = band[None] & (key_pos >= 0)
    scores = jnp.einsum('bnqkgd,bnskd->bnkgqs', qb, kk).astype(F32) * (HEAD_DIM ** -0.5)
    scores = jnp.where(mask[None, :, None, None], scores, NEG_INF)
    probs = sink_softmax(scores, sinks)
    out = jnp.einsum('bnkgqs,bnskd->bnqkgd', probs.astype(v.dtype), vv)
    return out.reshape(b, l, ATTN_W)


def attn_sample(q, k, v, cache_k, cache_v, sinks):
    bt, s = q.shape[:2]
    kk = jnp.concatenate([cache_k.astype(k.dtype), k], axis=1)
    vv = jnp.concatenate([cache_v.astype(v.dtype), v], axis=1)
    qb = q.reshape(bt, s, N_KV, GRP, HEAD_DIM)
    mask = band_mask(s, WINDOW + s, WINDOW)
    scores = jnp.einsum('bqkgd,bskd->bkgqs', qb, kk).astype(F32) * (HEAD_DIM ** -0.5)
    scores = jnp.where(mask, scores, NEG_INF)
    probs = sink_softmax(scores, sinks)
    out = jnp.einsum('bkgqs,bskd->bqkgd', probs.astype(v.dtype), vv)
    return out.reshape(bt, s, ATTN_W), kk[:, -WINDOW:], vv[:, -WINDOW:]


def token_mixer(h, lw, h0, cache_k, cache_v):
    bt, l, _ = h.shape
    z = h @ lw['w_in']
    u = z[..., :SSM_W]
    q = z[..., SSM_W:SSM_W + ATTN_W].reshape(bt, l, N_HEADS, HEAD_DIM)
    k = z[..., SSM_W + ATTN_W:SSM_W + ATTN_W + KV_W].reshape(bt, l, N_KV, HEAD_DIM)
    v = z[..., SSM_W + ATTN_W + KV_W:].reshape(bt, l, N_KV, HEAD_DIM)
    y_ssm, h_last = ssm_scan(u, h0, lw['ssm_lam_re'], lw['ssm_lam_im'], lw['ssm_log_dt'],
                             lw['ssm_b_re'], lw['ssm_b_im'], lw['ssm_c_re'], lw['ssm_c_im'],
                             lw['ssm_d'], lw['ssm_w_glu'])
    if cache_k is None:
        y_attn = attn_prompt(q, k, v, lw['attn_sinks'])
        new_k, new_v = k[:, -WINDOW:], v[:, -WINDOW:]
    else:
        y_attn, new_k, new_v = attn_sample(q, k, v, cache_k, cache_v, lw['attn_sinks'])
    y = jnp.concatenate([rms_norm(y_ssm, lw['ssm_out_norm']), rms_norm(y_attn, lw['attn_out_norm'])], axis=-1)
    return y @ lw['w_out'], new_k, new_v, h_last


def layer(x, p, lw, h0, cache_k, cache_v):
    x = x + 0.5 * swiglu(rms_norm(x, lw['ffn1_norm']), lw['ffn1_w_gate'], lw['ffn1_w_up'], lw['ffn1_w_down'])
    mix, new_k, new_v, h_last = token_mixer(rms_norm(x, lw['mix_norm']), lw, h0, cache_k, cache_v)
    x = x + mix
    x = x + 0.5 * swiglu(rms_norm(x, lw['ffn2_norm']), lw['ffn2_w_gate'], lw['ffn2_w_up'], lw['ffn2_w_down'])
    gate = jax.nn.sigmoid((rms_norm(x, lw['ple_norm']) @ lw['ple_w_gate']).astype(F32))
    x = x + ((p @ lw['ple_w_proj']).astype(F32) * gate).astype(x.dtype)
    return x, new_k, new_v, h_last


def setup_inputs(seed: int = 0) -> dict:
    key = jax.random.key(seed)
    ks = iter(jax.random.split(key, 48))

    def nrm(shape, scale=1.0):
        return jax.random.normal(next(ks), shape, F32) * scale

    def gain(shape):
        return 1.0 + nrm(shape, 0.05)

    L = DEPTH
    inp = {}
    inp['x_prompt'] = nrm((BATCH, SEQ, D_MODEL))
    inp['x_sample'] = nrm((DEC_BATCH, DEC_SEQ, D_MODEL))
    inp['cache_k'] = nrm((L, DEC_BATCH, WINDOW, N_KV, HEAD_DIM))
    inp['cache_v'] = nrm((L, DEC_BATCH, WINDOW, N_KV, HEAD_DIM))
    inp['state_ssm_re'] = nrm((L, DEC_BATCH, SSM_G, SSM_N), 0.5)
    inp['state_ssm_im'] = nrm((L, DEC_BATCH, SSM_G, SSM_N), 0.5)
    inp['p_prompt'] = nrm((L, BATCH, SEQ, PLE_DIM))
    inp['p_sample'] = nrm((L, DEC_BATCH, DEC_SEQ, PLE_DIM))
    inp['ffn1_norm'] = gain((L, D_MODEL))
    inp['ffn1_w_gate'] = nrm((L, D_MODEL, D_FF), D_MODEL ** -0.5)
    inp['ffn1_w_up'] = nrm((L, D_MODEL, D_FF), D_MODEL ** -0.5)
    inp['ffn1_w_down'] = nrm((L, D_FF, D_MODEL), D_FF ** -0.5)
    inp['mix_norm'] = gain((L, D_MODEL))
    inp['w_in'] = nrm((L, D_MODEL, IN_W), D_MODEL ** -0.5)
    inp['ssm_lam_re'] = -0.5 + nrm((L, SSM_G, SSM_N), 0.01)
    inp['ssm_lam_im'] = jnp.pi * jnp.arange(SSM_N, dtype=F32) + nrm((L, SSM_G, SSM_N), 0.01)
    inp['ssm_log_dt'] = jax.random.uniform(next(ks), (L, SSM_G), F32, math.log(DT_MIN), math.log(DT_MAX))
    inp['ssm_b_re'] = nrm((L, SSM_G, SSM_N, SSM_P), (2 * SSM_P) ** -0.5)
    inp['ssm_b_im'] = nrm((L, SSM_G, SSM_N, SSM_P), (2 * SSM_P) ** -0.5)
    inp['ssm_c_re'] = nrm((L, SSM_G, SSM_P, SSM_N), (2 * SSM_N) ** -0.5)
    inp['ssm_c_im'] = nrm((L, SSM_G, SSM_P, SSM_N), (2 * SSM_N) ** -0.5)
    inp['ssm_d'] = nrm((L, SSM_W))
    inp['ssm_w_glu'] = nrm((L, SSM_W, SSM_W), SSM_W ** -0.5)
    inp['ssm_out_norm'] = gain((L, SSM_W))
    inp['attn_sinks'] = nrm((L, N_HEADS), 0.5)
    inp['attn_out_norm'] = gain((L, ATTN_W))
    inp['w_out'] = nrm((L, MIX_W, D_MODEL), MIX_W ** -0.5)
    inp['ffn2_norm'] = gain((L, D_MODEL))
    inp['ffn2_w_gate'] = nrm((L, D_MODEL, D_FF), D_MODEL ** -0.5)
    inp['ffn2_w_up'] = nrm((L, D_MODEL, D_FF), D_MODEL ** -0.5)
    inp['ffn2_w_down'] = nrm((L, D_FF, D_MODEL), D_FF ** -0.5)
    inp['ple_norm'] = gain((L, D_MODEL))
    inp['ple_w_gate'] = nrm((L, D_MODEL, D_MODEL), D_MODEL ** -0.5)
    inp['ple_w_proj'] = nrm((L, PLE_DIM, D_MODEL), PLE_DIM ** -0.5)
    inp['final_norm'] = gain((D_MODEL,))
    return inp


def reference(x_prompt, x_sample, cache_k, cache_v, state_ssm_re, state_ssm_im, p_prompt, p_sample,
              ffn1_norm, ffn1_w_gate, ffn1_w_up, ffn1_w_down, mix_norm, w_in,
              ssm_lam_re, ssm_lam_im, ssm_log_dt, ssm_b_re, ssm_b_im, ssm_c_re, ssm_c_im,
              ssm_d, ssm_w_glu, ssm_out_norm, attn_sinks, attn_out_norm, w_out,
              ffn2_norm, ffn2_w_gate, ffn2_w_up, ffn2_w_down, ple_norm, ple_w_gate, ple_w_proj,
              final_norm):
    stacked = dict(ffn1_norm=ffn1_norm, ffn1_w_gate=ffn1_w_gate, ffn1_w_up=ffn1_w_up, ffn1_w_down=ffn1_w_down,
                   mix_norm=mix_norm, w_in=w_in, ssm_lam_re=ssm_lam_re, ssm_lam_im=ssm_lam_im,
                   ssm_log_dt=ssm_log_dt, ssm_b_re=ssm_b_re, ssm_b_im=ssm_b_im, ssm_c_re=ssm_c_re,
                   ssm_c_im=ssm_c_im, ssm_d=ssm_d, ssm_w_glu=ssm_w_glu, ssm_out_norm=ssm_out_norm,
                   attn_sinks=attn_sinks, attn_out_norm=attn_out_norm, w_out=w_out,
                   ffn2_norm=ffn2_norm, ffn2_w_gate=ffn2_w_gate, ffn2_w_up=ffn2_w_up, ffn2_w_down=ffn2_w_down,
                   ple_norm=ple_norm, ple_w_gate=ple_w_gate, ple_w_proj=ple_w_proj)
    xp, xs = x_prompt, x_sample
    kp_l, vp_l, hp_l, ks_l, vs_l, hs_l = [], [], [], [], [], []
    for i in range(DEPTH):
        lw = {name: arr[i] for name, arr in stacked.items()}
        xp, kp, vp, hp = layer(xp, p_prompt[i], lw, None, None, None)
        h0 = lax.complex(state_ssm_re[i].astype(F32), state_ssm_im[i].astype(F32))
        xs, ks_, vs_, hs = layer(xs, p_sample[i], lw, h0, cache_k[i], cache_v[i])
        kp_l.append(kp); vp_l.append(vp); hp_l.append(hp)
        ks_l.append(ks_); vs_l.append(vs_); hs_l.append(hs)
    y_prompt = rms_norm(xp, final_norm)
    y_sample = rms_norm(xs, final_norm)
    hp_all = jnp.stack(hp_l)
    hs_all = jnp.stack(hs_l)
    return (y_prompt, y_sample,
            jnp.stack(kp_l), jnp.stack(vp_l), hp_all.real, hp_all.imag,
            jnp.stack(ks_l), jnp.stack(vs_l), hs_all.real, hs_all.imag)
```

```python
import functools

import jax
import jax.numpy as jnp
from jax import lax
from jax.experimental import pallas as pl
from jax.experimental.pallas import tpu as pltpu

F32 = jnp.float32
BF16 = jnp.bfloat16

D_MODEL = 1024
BATCH = 8
SEQ = 2048
DEPTH = 2
DEC_BATCH = 128
DEC_SEQ = 8
SSM_W = 512
SSM_P = 16
SSM_G = 32
SSM_N = 64
ATTN_W = 512
HEAD_DIM = 64
N_HEADS = 8
N_KV = 2
GRP = N_HEADS // N_KV
KV_W = N_KV * HEAD_DIM
IN_W = SSM_W + ATTN_W + 2 * KV_W
WINDOW = 128
D_FF = 2816
PLE_DIM = 256
EPS = 1e-6
NEG_INF = -1e30

N_PROMPT = BATCH * SEQ
N_SAMPLE = DEC_BATCH * DEC_SEQ
N_TOK = N_PROMPT + N_SAMPLE

SSM_T = 16
SSM_TP = SSM_T * SSM_P
N_CHUNK = SEQ // SSM_T
SSM_SP = DEC_SEQ * SSM_P

TOKEN_TILE = 512
FF_CHUNK = 256
VMEM_LIMIT = 56 * 1024 * 1024


def _const_spec(shape):
    nd = len(shape)
    return pl.BlockSpec(shape, lambda *_: (0,) * nd, pipeline_mode=pl.Buffered(1))


def _row_spec(tile, width):
    return pl.BlockSpec((tile, width), lambda i: (i, 0))


def _rms(x, g):
    return x * lax.rsqrt(jnp.mean(x * x, axis=-1, keepdims=True) + EPS) * g


def _half_swiglu(xf, g_ref, wg_ref, wu_ref, wd_ref):
    h = _rms(xf, g_ref[...]).astype(BF16)
    acc = jnp.zeros(xf.shape, F32)
    for c in range(0, D_FF, FF_CHUNK):
        gate = jnp.dot(h, wg_ref[:, c:c + FF_CHUNK], preferred_element_type=F32)
        up = jnp.dot(h, wu_ref[:, c:c + FF_CHUNK], preferred_element_type=F32)
        act = (gate * jax.nn.sigmoid(gate) * up).astype(BF16)
        acc = acc + jnp.dot(act, wd_ref[c:c + FF_CHUNK, :], preferred_element_type=F32)
    return xf + 0.5 * acc


def _ffn_proj_kernel(x_ref, g1_ref, wg_ref, wu_ref, wd_ref, gm_ref, win_ref,
                     xo_ref, u_ref, q_ref, kv_ref):
    x1 = _half_swiglu(x_ref[...], g1_ref, wg_ref, wu_ref, wd_ref)
    xo_ref[...] = x1
    h = _rms(x1, gm_ref[...]).astype(BF16)
    z = jnp.dot(h, win_ref[...], preferred_element_type=F32)
    u_ref[...] = z[:, :SSM_W]
    q_ref[...] = z[:, SSM_W:SSM_W + ATTN_W].astype(BF16)
    kv_ref[...] = z[:, SSM_W + ATTN_W:]


def _ffn_proj(x, g1, wg, wu, wd, gm, win):
    n = x.shape[0]
    return pl.pallas_call(
        _ffn_proj_kernel,
        grid=(n // TOKEN_TILE,),
        in_specs=[_row_spec(TOKEN_TILE, D_MODEL), _const_spec((1, D_MODEL)),
                  _const_spec((D_MODEL, D_FF)), _const_spec((D_MODEL, D_FF)), _const_spec((D_FF, D_MODEL)),
                  _const_spec((1, D_MODEL)), _const_spec((D_MODEL, IN_W))],
        out_specs=[_row_spec(TOKEN_TILE, D_MODEL), _row_spec(TOKEN_TILE, SSM_W),
                   _row_spec(TOKEN_TILE, ATTN_W), _row_spec(TOKEN_TILE, 2 * KV_W)],
        out_shape=[jax.ShapeDtypeStruct((n, D_MODEL), F32), jax.ShapeDtypeStruct((n, SSM_W), F32),
                   jax.ShapeDtypeStruct((n, ATTN_W), BF16), jax.ShapeDtypeStruct((n, 2 * KV_W), F32)],
        compiler_params=pltpu.CompilerParams(dimension_semantics=("parallel",), vmem_limit_bytes=VMEM_LIMIT),
        name="ffn1_proj",
    )(x, g1, wg, wu, wd, gm, win)


def _ssm_prep_kernel(lre_ref, lim_ref, ldt_ref, btre_ref, btim_ref, cre_ref, cim_ref,
                     toep_ref, bpow_ref, cpow_ref, apw_ref):
    lam_re = lre_ref[...]
    lam_im = lim_ref[...]
    dt = jnp.exp(ldt_ref[...])
    mag = jnp.exp(lam_re * dt)
    ang = lam_im * dt
    a_re = mag * jnp.cos(ang)
    a_im = mag * jnp.sin(ang)
    den = lam_re * lam_re + lam_im * lam_im
    nr = a_re - 1.0
    k_re = (nr * lam_re + a_im * lam_im) / den
    k_im = (a_im * lam_re - nr * lam_im) / den
    bt_re = btre_ref[...]
    bt_im = btim_ref[...]
    bb_re = k_re * bt_re - k_im * bt_im
    bb_im = k_re * bt_im + k_im * bt_re
    c_re = cre_ref[...]
    c_im = cim_ref[...]

    p_re = [jnp.ones_like(a_re)]
    p_im = [jnp.zeros_like(a_re)]
    for _ in range(SSM_T):
        r, i = p_re[-1], p_im[-1]
        p_re.append(r * a_re - i * a_im)
        p_im.append(r * a_im + i * a_re)

    ca_rows = []
    for tau in range(SSM_T + 1):
        ca_re = c_re * p_re[tau] - c_im * p_im[tau]
        ca_im = c_re * p_im[tau] + c_im * p_re[tau]
        if tau < SSM_T:
            ca_rows.append(jnp.concatenate([ca_re, ca_im], axis=1))
        if tau >= 1:
            cpow_ref[(tau - 1) * SSM_P:tau * SSM_P, :] = jnp.concatenate([ca_re, -ca_im], axis=1)
    ca_all = jnp.concatenate(ca_rows, axis=0)

    for s in range(SSM_T):
        w_re, w_im = p_re[SSM_T - 1 - s], p_im[SSM_T - 1 - s]
        bpow_ref[s * SSM_P:(s + 1) * SSM_P, :] = jnp.concatenate(
            [bb_re * w_re - bb_im * w_im, bb_re * w_im + bb_im * w_re], axis=1)

    bb2 = jnp.concatenate([bb_re, -bb_im], axis=1)
    krow = lax.dot_general(bb2, ca_all, (((1,), (1,)), ((), ())),
                           precision=lax.Precision.HIGHEST, preferred_element_type=F32)
    lane = lax.broadcasted_iota(jnp.int32, krow.shape, 1)
    toep_ref[0:SSM_P, :] = krow
    for s in range(1, SSM_T):
        shifted = pltpu.roll(krow, s * SSM_P, axis=1)
        toep_ref[s * SSM_P:(s + 1) * SSM_P, :] = jnp.where(lane >= s * SSM_P, shifted, 0.0)

    apw_ref[0:1, :] = p_re[DEC_SEQ]
    apw_ref[1:2, :] = p_im[DEC_SEQ]
    apw_ref[2:3, :] = p_re[SSM_T]
    apw_ref[3:4, :] = p_im[SSM_T]
    apw_ref[4:8, :] = jnp.zeros((4, SSM_N), F32)


def _ssm_prep(lam_re, lam_im, log_dt, bt_re, bt_im, c_re, c_im):
    lg = lam_re.shape[0]
    vec = pl.BlockSpec((None, 1, SSM_N), lambda i: (i, 0, 0))
    mat = pl.BlockSpec((None, SSM_P, SSM_N), lambda i: (i, 0, 0))
    return pl.pallas_call(
        _ssm_prep_kernel,
        grid=(lg,),
        in_specs=[vec, vec, vec, mat, mat, mat, mat],
        out_specs=[pl.BlockSpec((None, SSM_TP, SSM_TP), lambda i: (i, 0, 0)),
                   pl.BlockSpec((None, SSM_TP, 2 * SSM_N), lambda i: (i, 0, 0)),
                   pl.BlockSpec((None, SSM_TP, 2 * SSM_N), lambda i: (i, 0, 0)),
                   pl.BlockSpec((None, 8, SSM_N), lambda i: (i, 0, 0))],
        out_shape=[jax.ShapeDtypeStruct((lg, SSM_TP, SSM_TP), F32),
                   jax.ShapeDtypeStruct((lg, SSM_TP, 2 * SSM_N), F32),
                   jax.ShapeDtypeStruct((lg, SSM_TP, 2 * SSM_N), F32),
                   jax.ShapeDtypeStruct((lg, 8, SSM_N), F32)],
        compiler_params=pltpu.CompilerParams(dimension_semantics=("parallel",)),
        name="ssm_prep",
    )(lam_re, lam_im, log_dt, bt_re, bt_im, c_re, c_im)


def _ssm_prompt_kernel(u_ref, toep_ref, bre_ref, bim_ref, cre_ref, cim_ref, apw_ref,
                       y_ref, hre_ref, him_ref, sre, sim, hpre, hpim):
    u = u_ref[...]
    sre[...] = jnp.dot(u, bre_ref[...], preferred_element_type=F32)
    sim[...] = jnp.dot(u, bim_ref[...], preferred_element_type=F32)
    ar = jnp.broadcast_to(apw_ref[2:3, :], (BATCH, SSM_N))
    ai = jnp.broadcast_to(apw_ref[3:4, :], (BATCH, SSM_N))

    def step(c, carry):
        hr, hi = carry
        r0 = pl.multiple_of(c * BATCH, BATCH)
        hpre[pl.ds(r0, BATCH), :] = hr
        hpim[pl.ds(r0, BATCH), :] = hi
        s_r = sre[pl.ds(r0, BATCH), :]
        s_i = sim[pl.ds(r0, BATCH), :]
        return ar * hr - ai * hi + s_r, ar * hi + ai * hr + s_i

    zero = jnp.zeros((BATCH, SSM_N), F32)
    hr, hi = lax.fori_loop(0, N_CHUNK, step, (zero, zero), unroll=8)
    hre_ref[...] = hr
    him_ref[...] = hi
    y = jnp.dot(u, toep_ref[...], preferred_element_type=F32)
    y = y + jnp.dot(hpre[...].astype(BF16), cre_ref[...], preferred_element_type=F32)
    y = y + jnp.dot(hpim[...].astype(BF16), cim_ref[...], preferred_element_type=F32)
    y_ref[...] = y


def _ssm_prompt(u, toep, bre, bim, cre, cim, apw):
    rows = N_CHUNK * BATCH

    def blk(r, c):
        return pl.BlockSpec((None, r, c), lambda g: (g, 0, 0))

    return pl.pallas_call(
        _ssm_prompt_kernel,
        grid=(SSM_G,),
        in_specs=[blk(rows, SSM_TP), blk(SSM_TP, SSM_TP), blk(SSM_TP, SSM_N), blk(SSM_TP, SSM_N),
                  blk(SSM_N, SSM_TP), blk(SSM_N, SSM_TP), blk(8, SSM_N)],
        out_specs=[blk(rows, SSM_TP), blk(BATCH, SSM_N), blk(BATCH, SSM_N)],
        out_shape=[jax.ShapeDtypeStruct((SSM_G, rows, SSM_TP), F32),
                   jax.ShapeDtypeStruct((SSM_G, BATCH, SSM_N), F32),
                   jax.ShapeDtypeStruct((SSM_G, BATCH, SSM_N), F32)],
        scratch_shapes=[pltpu.VMEM((rows, SSM_N), F32)] * 4,
        compiler_params=pltpu.CompilerParams(dimension_semantics=("parallel",)),
        name="ssm_prompt",
    )(u, toep, bre, bim, cre, cim, apw)


def _ssm_sample_kernel(u_ref, h0re_ref, h0im_ref, toep_ref, bre_ref, bim_ref, cre_ref, cim_ref, apw_ref,
                       y_ref, hre_ref, him_ref):
    u = u_ref[...]
    h0r = h0re_ref[...]
    h0i = h0im_ref[...]
    ar = apw_ref[0:1, :]
    ai = apw_ref[1:2, :]
    hi_prec = lax.Precision.HIGHEST
    s_r = jnp.dot(u, bre_ref[...], precision=hi_prec, preferred_element_type=F32)
    s_i = jnp.dot(u, bim_ref[...], precision=hi_prec, preferred_element_type=F32)
    hre_ref[...] = ar * h0r - ai * h0i + s_r
    him_ref[...] = ar * h0i + ai * h0r + s_i
    y = jnp.dot(u.astype(BF16), toep_ref[...], preferred_element_type=F32)
    y = y + jnp.dot(h0r.astype(BF16), cre_ref[...], preferred_element_type=F32)
    y = y + jnp.dot(h0i.astype(BF16), cim_ref[...], preferred_element_type=F32)
    y_ref[...] = y


def _ssm_sample(u, h0re, h0im, toep, bre, bim, cre, cim, apw):
    def blk(r, c):
        return pl.BlockSpec((None, r, c), lambda g: (g, 0, 0))

    return pl.pallas_call(
        _ssm_sample_kernel,
        grid=(SSM_G,),
        in_specs=[blk(DEC_BATCH, SSM_SP), blk(DEC_BATCH, SSM_N), blk(DEC_BATCH, SSM_N),
                  blk(SSM_SP, SSM_SP), blk(SSM_SP, SSM_N), blk(SSM_SP, SSM_N),
                  blk(SSM_N, SSM_SP), blk(SSM_N, SSM_SP), blk(8, SSM_N)],
        out_specs=[blk(DEC_BATCH, SSM_SP), blk(DEC_BATCH, SSM_N), blk(DEC_BATCH, SSM_N)],
        out_shape=[jax.ShapeDtypeStruct((SSM_G, DEC_BATCH, SSM_SP), F32),
                   jax.ShapeDtypeStruct((SSM_G, DEC_BATCH, SSM_N), F32),
                   jax.ShapeDtypeStruct((SSM_G, DEC_BATCH, SSM_N), F32)],
        compiler_params=pltpu.CompilerParams(dimension_semantics=("parallel",)),
        name="ssm_sample",
    )(u, h0re, h0im, toep, bre, bim, cre, cim, apw)


def _sink_softmax_pv(s, valid, sink, vv):
    s = jnp.where(valid, s, NEG_INF)
    m = jnp.maximum(jnp.max(s, axis=-1, keepdims=True), sink)
    e = jnp.exp(s - m)
    denom = jnp.sum(e, axis=-1, keepdims=True) + jnp.exp(sink - m)
    return jnp.dot(e.astype(BF16), vv, preferred_element_type=F32) / denom


def _attn_prompt_kernel(q_ref, kp_ref, kc_ref, vp_ref, vc_ref, sink_ref, o_ref):
    rows = GRP * WINDOW
    q = q_ref[...].reshape(rows, HEAD_DIM)
    kk = jnp.concatenate([kp_ref[...], kc_ref[...]], axis=0)
    vv = jnp.concatenate([vp_ref[...], vc_ref[...]], axis=0)
    s = lax.dot_general(q, kk, (((1,), (1,)), ((), ())), preferred_element_type=F32) * (HEAD_DIM ** -0.5)
    qi = lax.broadcasted_iota(jnp.int32, s.shape, 0) & (WINDOW - 1)
    kj = lax.broadcasted_iota(jnp.int32, s.shape, 1)
    lo = jnp.where(pl.program_id(2) > 0, qi, WINDOW - 1)
    valid = (kj > lo) & (kj <= qi + WINDOW)
    o = _sink_softmax_pv(s, valid, sink_ref[...], vv)
    o_ref[...] = o.reshape(GRP, WINDOW, HEAD_DIM).astype(o_ref.dtype)


def _attn_prompt(q, k, v, sink_col):
    nb = SEQ // WINDOW
    kv_cur = pl.BlockSpec((None, None, WINDOW, HEAD_DIM), lambda b, h, i: (b, h, i, 0))
    kv_prev = pl.BlockSpec((None, None, WINDOW, HEAD_DIM), lambda b, h, i: (b, h, jnp.maximum(i - 1, 0), 0))
    q_spec = pl.BlockSpec((None, GRP, WINDOW, HEAD_DIM), lambda b, h, i: (b, h, i, 0))
    return pl.pallas_call(
        _attn_prompt_kernel,
        grid=(BATCH, N_KV, nb),
        in_specs=[q_spec, kv_prev, kv_cur, kv_prev, kv_cur,
                  pl.BlockSpec((None, GRP * WINDOW, 1), lambda b, h, i: (h, 0, 0))],
        out_specs=q_spec,
        out_shape=jax.ShapeDtypeStruct((BATCH, N_HEADS, SEQ, HEAD_DIM), BF16),
        compiler_params=pltpu.CompilerParams(dimension_semantics=("parallel", "parallel", "arbitrary")),
        name="attn_prompt",
    )(q, k, k, v, v, sink_col)


SAMPLE_SEQ_BLOCK = 8


def _attn_sample_kernel(q_ref, ck_ref, cv_ref, kn_ref, vn_ref, sink_ref, o_ref, nk_ref, nv_ref):
    rows = GRP * DEC_SEQ
    n_keys = WINDOW + DEC_SEQ
    qi = lax.broadcasted_iota(jnp.int32, (rows, n_keys), 0) & (DEC_SEQ - 1)
    kj = lax.broadcasted_iota(jnp.int32, (rows, n_keys), 1)
    valid = (kj > qi) & (kj <= qi + WINDOW)
    head_lane = lax.broadcasted_iota(jnp.int32, (rows, KV_W), 1) // HEAD_DIM
    for s in range(SAMPLE_SEQ_BLOCK):
        ck = ck_ref[s]
        cv = cv_ref[s]
        kn = kn_ref[s]
        vn = vn_ref[s]
        nk_ref[s, 0:WINDOW - DEC_SEQ, :] = ck[DEC_SEQ:, :]
        nk_ref[s, WINDOW - DEC_SEQ:WINDOW, :] = kn
        nv_ref[s, 0:WINDOW - DEC_SEQ, :] = cv[DEC_SEQ:, :]
        nv_ref[s, WINDOW - DEC_SEQ:WINDOW, :] = vn
        kk = jnp.concatenate([ck, kn], axis=0).astype(BF16)
        vv = jnp.concatenate([cv, vn], axis=0).astype(BF16)
        outs = []
        for h in range(N_KV):
            sc = lax.dot_general(q_ref[s, h], kk, (((1,), (1,)), ((), ())),
                                 preferred_element_type=F32) * (HEAD_DIM ** -0.5)
            outs.append(_sink_softmax_pv(sc, valid, sink_ref[h], vv))
        o_ref[s] = jnp.where(head_lane == 0, outs[0], outs[1])


def _attn_sample(qx, ck, cv, kn, vn, sink_col):
    sb = SAMPLE_SEQ_BLOCK
    rows = GRP * DEC_SEQ
    q_spec = pl.BlockSpec((sb, N_KV, rows, KV_W), lambda i: (i, 0, 0, 0))
    c_spec = pl.BlockSpec((sb, WINDOW, KV_W), lambda i: (i, 0, 0))
    n_spec = pl.BlockSpec((sb, DEC_SEQ, KV_W), lambda i: (i, 0, 0))
    return pl.pallas_call(
        _attn_sample_kernel,
        grid=(DEC_BATCH // sb,),
        in_specs=[q_spec, c_spec, c_spec, n_spec, n_spec,
                  pl.BlockSpec((N_KV, rows, 1), lambda i: (0, 0, 0))],
        out_specs=[pl.BlockSpec((sb, rows, KV_W), lambda i: (i, 0, 0)), c_spec, c_spec],
        out_shape=[jax.ShapeDtypeStruct((DEC_BATCH, rows, KV_W), F32),
                   jax.ShapeDtypeStruct((DEC_BATCH, WINDOW, KV_W), F32),
                   jax.ShapeDtypeStruct((DEC_BATCH, WINDOW, KV_W), F32)],
        compiler_params=pltpu.CompilerParams(dimension_semantics=("parallel",)),
        name="attn_sample",
    )(qx, ck, cv, kn, vn, sink_col)


def _mix_out_kernel(x_ref, y_ref, u_ref, o_ref, d_ref, wglu_ref, gs_ref, ga_ref, wout_ref, xo_ref):
    y = y_ref[...] + d_ref[...] * u_ref[...]
    y = jax.nn.gelu(y)
    y = y * jax.nn.sigmoid(jnp.dot(y.astype(BF16), wglu_ref[...], preferred_element_type=F32))
    ys = _rms(y, gs_ref[...]).astype(BF16)
    ya = _rms(o_ref[...].astype(F32), ga_ref[...]).astype(BF16)
    mix = jnp.dot(ys, wout_ref[0:SSM_W, :], preferred_element_type=F32)
    mix = mix + jnp.dot(ya, wout_ref[SSM_W:, :], preferred_element_type=F32)
    xo_ref[...] = x_ref[...] + mix


def _mix_out(x, y, u, o, d, wglu, gs, ga, wout):
    n = x.shape[0]
    return pl.pallas_call(
        _mix_out_kernel,
        grid=(n // TOKEN_TILE,),
        in_specs=[_row_spec(TOKEN_TILE, D_MODEL), _row_spec(TOKEN_TILE, SSM_W), _row_spec(TOKEN_TILE, SSM_W),
                  _row_spec(TOKEN_TILE, ATTN_W), _const_spec((1, SSM_W)), _const_spec((SSM_W, SSM_W)),
                  _const_spec((1, SSM_W)), _const_spec((1, ATTN_W)), _const_spec((D_MODEL, D_MODEL))],
        out_specs=_row_spec(TOKEN_TILE, D_MODEL),
        out_shape=jax.ShapeDtypeStruct((n, D_MODEL), F32),
        compiler_params=pltpu.CompilerParams(dimension_semantics=("parallel",), vmem_limit_bytes=VMEM_LIMIT),
        name="mix_out",
    )(x, y, u, o, d, wglu, gs, ga, wout)


def _ffn_ple_kernel(x_ref, p_ref, g2_ref, wg_ref, wu_ref, wd_ref, gp_ref, wpg_ref, wpp_ref, gf_ref,
                    xo_ref, *, final):
    x2 = _half_swiglu(x_ref[...], g2_ref, wg_ref, wu_ref, wd_ref)
    h = _rms(x2, gp_ref[...]).astype(BF16)
    gate = jax.nn.sigmoid(jnp.dot(h, wpg_ref[...], preferred_element_type=F32))
    proj = jnp.dot(p_ref[...].astype(BF16), wpp_ref[...], preferred_element_type=F32)
    x3 = x2 + proj * gate
    xo_ref[...] = _rms(x3, gf_ref[...]) if final else x3


def _ffn_ple(x, p, g2, wg, wu, wd, gp, wpg, wpp, gf, final):
    n = x.shape[0]
    return pl.pallas_call(
        functools.partial(_ffn_ple_kernel, final=final),
        grid=(n // TOKEN_TILE,),
        in_specs=[_row_spec(TOKEN_TILE, D_MODEL), _row_spec(TOKEN_TILE, PLE_DIM), _const_spec((1, D_MODEL)),
                  _const_spec((D_MODEL, D_FF)), _const_spec((D_MODEL, D_FF)), _const_spec((D_FF, D_MODEL)),
                  _const_spec((1, D_MODEL)), _const_spec((D_MODEL, D_MODEL)), _const_spec((PLE_DIM, D_MODEL)),
                  _const_spec((1, D_MODEL))],
        out_specs=_row_spec(TOKEN_TILE, D_MODEL),
        out_shape=jax.ShapeDtypeStruct((n, D_MODEL), F32),
        compiler_params=pltpu.CompilerParams(dimension_semantics=("parallel",), vmem_limit_bytes=VMEM_LIMIT),
        name="ffn2_ple",
    )(x, p, g2, wg, wu, wd, gp, wpg, wpp, gf)


def _sink_column(sinks, rows_per_head):
    return jnp.repeat(sinks.astype(F32).reshape(N_KV, GRP), rows_per_head, axis=1)[..., None]


def kernel(x_prompt, x_sample, cache_k, cache_v, state_ssm_re, state_ssm_im, p_prompt, p_sample, ffn1_norm, ffn1_w_gate, ffn1_w_up, ffn1_w_down, mix_norm, w_in, ssm_lam_re, ssm_lam_im, ssm_log_dt, ssm_b_re, ssm_b_im, ssm_c_re, ssm_c_im, ssm_d, ssm_w_glu, ssm_out_norm, attn_sinks, attn_out_norm, w_out, ffn2_norm, ffn2_w_gate, ffn2_w_up, ffn2_w_down, ple_norm, ple_w_gate, ple_w_proj, final_norm):
    lg = DEPTH * SSM_G
    toep, bpow, cpow_t, apw = _ssm_prep(
        ssm_lam_re.reshape(lg, 1, SSM_N), ssm_lam_im.reshape(lg, 1, SSM_N),
        jnp.broadcast_to(ssm_log_dt.reshape(lg, 1, 1), (lg, 1, SSM_N)),
        ssm_b_re.transpose(0, 1, 3, 2).reshape(lg, SSM_P, SSM_N),
        ssm_b_im.transpose(0, 1, 3, 2).reshape(lg, SSM_P, SSM_N),
        ssm_c_re.reshape(lg, SSM_P, SSM_N), ssm_c_im.reshape(lg, SSM_P, SSM_N))
    toep = toep.reshape(DEPTH, SSM_G, SSM_TP, SSM_TP)
    bpow = bpow.reshape(DEPTH, SSM_G, SSM_TP, 2 * SSM_N)
    cpow = cpow_t.reshape(DEPTH, SSM_G, SSM_TP, 2 * SSM_N).transpose(0, 1, 3, 2)
    apw = apw.reshape(DEPTH, SSM_G, 8, SSM_N)

    x = jnp.concatenate([x_prompt.reshape(N_PROMPT, D_MODEL), x_sample.reshape(N_SAMPLE, D_MODEL)], axis=0)
    p_all = jnp.concatenate([p_prompt.reshape(DEPTH, N_PROMPT, PLE_DIM),
                             p_sample.reshape(DEPTH, N_SAMPLE, PLE_DIM)], axis=1)
    row = lambda v: v.reshape(1, -1)
    bf = lambda w: w.astype(BF16)

    kp_l, vp_l, hpr_l, hpi_l, ks_l, vs_l, hsr_l, hsi_l = [], [], [], [], [], [], [], []
    for i in range(DEPTH):
        x, u, q, kv = _ffn_proj(x, row(ffn1_norm[i]), bf(ffn1_w_gate[i]), bf(ffn1_w_up[i]), bf(ffn1_w_down[i]),
                                row(mix_norm[i]), bf(w_in[i]))

        u_p = u[:N_PROMPT].reshape(BATCH, N_CHUNK, SSM_T, SSM_G, SSM_P).transpose(3, 1, 0, 2, 4)
        u_p = u_p.reshape(SSM_G, N_CHUNK * BATCH, SSM_TP).astype(BF16)
        y_p, hpr, hpi = _ssm_prompt(u_p, bf(toep[i]), bf(bpow[i, :, :, :SSM_N]), bf(bpow[i, :, :, SSM_N:]),
                                    bf(cpow[i, :, :SSM_N]), bf(cpow[i, :, SSM_N:]), apw[i])
        y_p = y_p.reshape(SSM_G, N_CHUNK, BATCH, SSM_T, SSM_P).transpose(2, 1, 3, 0, 4).reshape(N_PROMPT, SSM_W)
        hpr_l.append(hpr.transpose(1, 0, 2))
        hpi_l.append(hpi.transpose(1, 0, 2))

        u_s = u[N_PROMPT:].reshape(DEC_BATCH, DEC_SEQ, SSM_G, SSM_P).transpose(2, 0, 1, 3)
        u_s = u_s.reshape(SSM_G, DEC_BATCH, SSM_SP)
        y_s, hsr, hsi = _ssm_sample(
            u_s, state_ssm_re[i].transpose(1, 0, 2), state_ssm_im[i].transpose(1, 0, 2),
            bf(toep[i, :, :SSM_SP, :SSM_SP]), bpow[i, :, SSM_TP - SSM_SP:, :SSM_N], bpow[i, :, SSM_TP - SSM_SP:, SSM_N:],
            bf(cpow[i, :, :SSM_N, :SSM_SP]), bf(cpow[i, :, SSM_N:, :SSM_SP]), apw[i])
        y_s = y_s.reshape(SSM_G, DEC_BATCH, DEC_SEQ, SSM_P).transpose(1, 2, 0, 3).reshape(N_SAMPLE, SSM_W)
        hsr_l.append(hsr.transpose(1, 0, 2))
        hsi_l.append(hsi.transpose(1, 0, 2))

        kv_p = kv[:N_PROMPT].reshape(BATCH, SEQ, 2, N_KV, HEAD_DIM)
        kp_l.append(kv_p[:, -WINDOW:, 0])
        vp_l.append(kv_p[:, -WINDOW:, 1])
        q_p = q[:N_PROMPT].reshape(BATCH, SEQ, N_HEADS, HEAD_DIM).transpose(0, 2, 1, 3)
        k_p = kv_p[:, :, 0].transpose(0, 2, 1, 3).astype(BF16)
        v_p = kv_p[:, :, 1].transpose(0, 2, 1, 3).astype(BF16)
        o_p = _attn_prompt(q_p, k_p, v_p, _sink_column(attn_sinks[i], WINDOW))
        o_p = o_p.transpose(0, 2, 1, 3).reshape(N_PROMPT, ATTN_W)

        q_s = q[N_PROMPT:].reshape(DEC_BATCH, DEC_SEQ, N_KV, GRP, HEAD_DIM).transpose(0, 2, 3, 1, 4)
        q_s = q_s.reshape(DEC_BATCH, N_KV, GRP * DEC_SEQ, HEAD_DIM)
        zq = jnp.zeros_like(q_s[:, 0])
        qx = jnp.stack([jnp.concatenate([q_s[:, 0], zq], axis=-1), jnp.concatenate([zq, q_s[:, 1]], axis=-1)], axis=1)
        kv_s = kv[N_PROMPT:].reshape(DEC_BATCH, DEC_SEQ, 2, KV_W)
        ox, nk, nv = _attn_sample(qx, cache_k[i].reshape(DEC_BATCH, WINDOW, KV_W),
                                  cache_v[i].reshape(DEC_BATCH, WINDOW, KV_W),
                                  kv_s[:, :, 0], kv_s[:, :, 1], _sink_column(attn_sinks[i], DEC_SEQ))
        ks_l.append(nk.reshape(DEC_BATCH, WINDOW, N_KV, HEAD_DIM))
        vs_l.append(nv.reshape(DEC_BATCH, WINDOW, N_KV, HEAD_DIM))
        o_s = ox.reshape(DEC_BATCH, GRP, DEC_SEQ, N_KV, HEAD_DIM).transpose(0, 2, 3, 1, 4)
        o_s = o_s.reshape(N_SAMPLE, ATTN_W).astype(BF16)

        x = _mix_out(x, jnp.concatenate([y_p, y_s], axis=0), u, jnp.concatenate([o_p, o_s], axis=0),
                     row(ssm_d[i]), bf(ssm_w_glu[i]), row(ssm_out_norm[i]), row(attn_out_norm[i]), bf(w_out[i]))
        x = _ffn_ple(x, p_all[i], row(ffn2_norm[i]), bf(ffn2_w_gate[i]), bf(ffn2_w_up[i]), bf(ffn2_w_down[i]),
                     row(ple_norm[i]), bf(ple_w_gate[i]), bf(ple_w_proj[i]), row(final_norm),
                     final=(i == DEPTH - 1))

    y_prompt = x[:N_PROMPT].reshape(BATCH, SEQ, D_MODEL)
    y_sample = x[N_PROMPT:].reshape(DEC_BATCH, DEC_SEQ, D_MODEL)
    return (y_prompt, y_sample, jnp.stack(kp_l), jnp.stack(vp_l), jnp.stack(hpr_l), jnp.stack(hpi_l),
            jnp.stack(ks_l), jnp.stack(vs_l), jnp.stack(hsr_l), jnp.stack(hsi_l))
```

```python
import functools

import jax
import jax.numpy as jnp
from jax import lax
from jax.experimental import pallas as pl
from jax.experimental.pallas import tpu as pltpu

F32 = jnp.float32
BF16 = jnp.bfloat16

D_MODEL = 1024
BATCH = 8
SEQ = 2048
DEPTH = 2
DEC_BATCH = 128
DEC_SEQ = 8
SSM_W = 512
SSM_P = 16
SSM_G = 32
SSM_N = 64
ATTN_W = 512
HEAD_DIM = 64
N_HEADS = 8
N_KV = 2
GRP = N_HEADS // N_KV
KV_W = N_KV * HEAD_DIM
IN_W = SSM_W + ATTN_W + 2 * KV_W
WINDOW = 128
D_FF = 2816
PLE_DIM = 256
EPS = 1e-6
NEG_INF = -1e30

N_PROMPT = BATCH * SEQ
N_SAMPLE = DEC_BATCH * DEC_SEQ
N_TOK = N_PROMPT + N_SAMPLE

SSM_T = 16
SSM_TP = SSM_T * SSM_P
N_CHUNK = SEQ // SSM_T
SSM_SP = DEC_SEQ * SSM_P

TOKEN_TILE = 512
FF_CHUNK = 256
VMEM_LIMIT = 56 * 1024 * 1024


def _const_spec(shape):
    nd = len(shape)
    return pl.BlockSpec(shape, lambda *_: (0,) * nd, pipeline_mode=pl.Buffered(1))


def _row_spec(tile, width):
    return pl.BlockSpec((tile, width), lambda i: (i, 0))


def _rms(x, g):
    return x * lax.rsqrt(jnp.mean(x * x, axis=-1, keepdims=True) + EPS) * g


def _half_swiglu(xf, g_ref, wg_ref, wu_ref, wd_ref):
    h = _rms(xf, g_ref[...]).astype(BF16)
    acc = jnp.zeros(xf.shape, F32)
    for c in range(0, D_FF, FF_CHUNK):
        gate = jnp.dot(h, wg_ref[:, c:c + FF_CHUNK], preferred_element_type=F32)
        up = jnp.dot(h, wu_ref[:, c:c + FF_CHUNK], preferred_element_type=F32)
        act = (gate * jax.nn.sigmoid(gate) * up).astype(BF16)
        acc = acc + jnp.dot(act, wd_ref[c:c + FF_CHUNK, :], preferred_element_type=F32)
    return xf + 0.5 * acc


def _ffn_proj_kernel(x_ref, g1_ref, wg_ref, wu_ref, wd_ref, gm_ref, win_ref,
                     xo_ref, u_ref, q_ref, kv_ref):
    x1 = _half_swiglu(x_ref[...], g1_ref, wg_ref, wu_ref, wd_ref)
    xo_ref[...] = x1
    h = _rms(x1, gm_ref[...]).astype(BF16)
    z = jnp.dot(h, win_ref[...], preferred_element_type=F32)
    u_ref[...] = z[:, :SSM_W]
    q_ref[...] = z[:, SSM_W:SSM_W + ATTN_W].astype(BF16)
    kv_ref[...] = z[:, SSM_W + ATTN_W:]


def _ffn_proj(x, g1, wg, wu, wd, gm, win):
    n = x.shape[0]
    return pl.pallas_call(
        _ffn_proj_kernel,
        grid=(n // TOKEN_TILE,),
        in_specs=[_row_spec(TOKEN_TILE, D_MODEL), _const_spec((1, D_MODEL)),
                  _const_spec((D_MODEL, D_FF)), _const_spec((D_MODEL, D_FF)), _const_spec((D_FF, D_MODEL)),
                  _const_spec((1, D_MODEL)), _const_spec((D_MODEL, IN_W))],
        out_specs=[_row_spec(TOKEN_TILE, D_MODEL), _row_spec(TOKEN_TILE, SSM_W),
                   _row_spec(TOKEN_TILE, ATTN_W), _row_spec(TOKEN_TILE, 2 * KV_W)],
        out_shape=[jax.ShapeDtypeStruct((n, D_MODEL), F32), jax.ShapeDtypeStruct((n, SSM_W), F32),
                   jax.ShapeDtypeStruct((n, ATTN_W), BF16), jax.ShapeDtypeStruct((n, 2 * KV_W), F32)],
        compiler_params=pltpu.CompilerParams(dimension_semantics=("parallel",), vmem_limit_bytes=VMEM_LIMIT),
        name="ffn1_proj",
    )(x, g1, wg, wu, wd, gm, win)


def _ssm_prep_kernel(lre_ref, lim_ref, ldt_ref, btre_ref, btim_ref, cre_ref, cim_ref,
                     toep_ref, bpow_ref, cpow_ref, apw_ref):
    lam_re = lre_ref[...]
    lam_im = lim_ref[...]
    dt = jnp.exp(ldt_ref[...])
    mag = jnp.exp(lam_re * dt)
    ang = lam_im * dt
    a_re = mag * jnp.cos(ang)
    a_im = mag * jnp.sin(ang)
    den = lam_re * lam_re + lam_im * lam_im
    nr = a_re - 1.0
    k_re = (nr * lam_re + a_im * lam_im) / den
    k_im = (a_im * lam_re - nr * lam_im) / den
    bt_re = btre_ref[...]
    bt_im = btim_ref[...]
    bb_re = k_re * bt_re - k_im * bt_im
    bb_im = k_re * bt_im + k_im * bt_re
    c_re = cre_ref[...]
    c_im = cim_ref[...]

    p_re = [jnp.ones_like(a_re)]
    p_im = [jnp.zeros_like(a_re)]
    for _ in range(SSM_T):
        r, i = p_re[-1], p_im[-1]
        p_re.append(r * a_re - i * a_im)
        p_im.append(r * a_im + i * a_re)

    ca_rows = []
    for tau in range(SSM_T + 1):
        ca_re = c_re * p_re[tau] - c_im * p_im[tau]
        ca_im = c_re * p_im[tau] + c_im * p_re[tau]
        if tau < SSM_T:
            ca_rows.append(jnp.concatenate([ca_re, ca_im], axis=1))
        if tau >= 1:
            cpow_ref[(tau - 1) * SSM_P:tau * SSM_P, :] = jnp.concatenate([ca_re, -ca_im], axis=1)
    ca_all = jnp.concatenate(ca_rows, axis=0)

    for s in range(SSM_T):
        w_re, w_im = p_re[SSM_T - 1 - s], p_im[SSM_T - 1 - s]
        bpow_ref[s * SSM_P:(s + 1) * SSM_P, :] = jnp.concatenate(
            [bb_re * w_re - bb_im * w_im, bb_re * w_im + bb_im * w_re], axis=1)

    bb2 = jnp.concatenate([bb_re, -bb_im], axis=1)
    krow = lax.dot_general(bb2, ca_all, (((1,), (1,)), ((), ())),
                           precision=lax.Precision.HIGHEST, preferred_element_type=F32)
    lane = lax.broadcasted_iota(jnp.int32, krow.shape, 1)
    toep_ref[0:SSM_P, :] = krow
    for s in range(1, SSM_T):
        shifted = pltpu.roll(krow, s * SSM_P, axis=1)
        toep_ref[s * SSM_P:(s + 1) * SSM_P, :] = jnp.where(lane >= s * SSM_P, shifted, 0.0)

    apw_ref[0:1, :] = p_re[DEC_SEQ]
    apw_ref[1:2, :] = p_im[DEC_SEQ]
    apw_ref[2:3, :] = p_re[SSM_T]
    apw_ref[3:4, :] = p_im[SSM_T]
    apw_ref[4:8, :] = jnp.zeros((4, SSM_N), F32)


def _ssm_prep(lam_re, lam_im, log_dt, bt_re, bt_im, c_re, c_im):
    lg = lam_re.shape[0]
    vec = pl.BlockSpec((None, 1, SSM_N), lambda i: (i, 0, 0))
    mat = pl.BlockSpec((None, SSM_P, SSM_N), lambda i: (i, 0, 0))
    return pl.pallas_call(
        _ssm_prep_kernel,
        grid=(lg,),
        in_specs=[vec, vec, vec, mat, mat, mat, mat],
        out_specs=[pl.BlockSpec((None, SSM_TP, SSM_TP), lambda i: (i, 0, 0)),
                   pl.BlockSpec((None, SSM_TP, 2 * SSM_N), lambda i: (i, 0, 0)),
                   pl.BlockSpec((None, SSM_TP, 2 * SSM_N), lambda i: (i, 0, 0)),
                   pl.BlockSpec((None, 8, SSM_N), lambda i: (i, 0, 0))],
        out_shape=[jax.ShapeDtypeStruct((lg, SSM_TP, SSM_TP), F32),
                   jax.ShapeDtypeStruct((lg, SSM_TP, 2 * SSM_N), F32),
                   jax.ShapeDtypeStruct((lg, SSM_TP, 2 * SSM_N), F32),
                   jax.ShapeDtypeStruct((lg, 8, SSM_N), F32)],
        compiler_params=pltpu.CompilerParams(dimension_semantics=("parallel",)),
        name="ssm_prep",
    )(lam_re, lam_im, log_dt, bt_re, bt_im, c_re, c_im)


def _ssm_prompt_kernel(u_ref, toep_ref, bre_ref, bim_ref, cre_ref, cim_ref, apw_ref,
                       y_ref, hre_ref, him_ref, sre, sim, hpre, hpim):
    u = u_ref[...]
    sre[...] = jnp.dot(u, bre_ref[...], preferred_element_type=F32)
    sim[...] = jnp.dot(u, bim_ref[...], preferred_element_type=F32)
    ar = jnp.broadcast_to(apw_ref[2:3, :], (BATCH, SSM_N))
    ai = jnp.broadcast_to(apw_ref[3:4, :], (BATCH, SSM_N))

    def step(c, carry):
        hr, hi = carry
        r0 = pl.multiple_of(c * BATCH, BATCH)
        hpre[pl.ds(r0, BATCH), :] = hr
        hpim[pl.ds(r0, BATCH), :] = hi
        s_r = sre[pl.ds(r0, BATCH), :]
        s_i = sim[pl.ds(r0, BATCH), :]
        return ar * hr - ai * hi + s_r, ar * hi + ai * hr + s_i

    zero = jnp.zeros((BATCH, SSM_N), F32)
    hr, hi = lax.fori_loop(0, N_CHUNK, step, (zero, zero), unroll=8)
    hre_ref[...] = hr
    him_ref[...] = hi
    y = jnp.dot(u, toep_ref[...], preferred_element_type=F32)
    y = y + jnp.dot(hpre[...].astype(BF16), cre_ref[...], preferred_element_type=F32)
    y = y + jnp.dot(hpim[...].astype(BF16), cim_ref[...], preferred_element_type=F32)
    y_ref[...] = y


def _ssm_prompt(u, toep, bre, bim, cre, cim, apw):
    rows = N_CHUNK * BATCH

    def blk(r, c):
        return pl.BlockSpec((None, r, c), lambda g: (g, 0, 0))

    return pl.pallas_call(
        _ssm_prompt_kernel,
        grid=(SSM_G,),
        in_specs=[blk(rows, SSM_TP), blk(SSM_TP, SSM_TP), blk(SSM_TP, SSM_N), blk(SSM_TP, SSM_N),
                  blk(SSM_N, SSM_TP), blk(SSM_N, SSM_TP), blk(8, SSM_N)],
        out_specs=[blk(rows, SSM_TP), blk(BATCH, SSM_N), blk(BATCH, SSM_N)],
        out_shape=[jax.ShapeDtypeStruct((SSM_G, rows, SSM_TP), F32),
                   jax.ShapeDtypeStruct((SSM_G, BATCH, SSM_N), F32),
                   jax.ShapeDtypeStruct((SSM_G, BATCH, SSM_N), F32)],
        scratch_shapes=[pltpu.VMEM((rows, SSM_N), F32)] * 4,
        compiler_params=pltpu.CompilerParams(dimension_semantics=("parallel",)),
        name="ssm_prompt",
    )(u, toep, bre, bim, cre, cim, apw)


def _ssm_sample_kernel(u_ref, h0re_ref, h0im_ref, toep_ref, bre_ref, bim_ref, cre_ref, cim_ref, apw_ref,
                       y_ref, hre_ref, him_ref):
    u = u_ref[...]
    h0r = h0re_ref[...]
    h0i = h0im_ref[...]
    ar = apw_ref[0:1, :]
    ai = apw_ref[1:2, :]
    hi_prec = lax.Precision.HIGHEST
    s_r = jnp.dot(u, bre_ref[...], precision=hi_prec, preferred_element_type=F32)
    s_i = jnp.dot(u, bim_ref[...], precision=hi_prec, preferred_element_type=F32)
    hre_ref[...] = ar * h0r - ai * h0i + s_r
    him_ref[...] = ar * h0i + ai * h0r + s_i
    y = jnp.dot(u.astype(BF16), toep_ref[...], preferred_element_type=F32)
    y = y + jnp.dot(h0r.astype(BF16), cre_ref[...], preferred_element_type=F32)
    y = y + jnp.dot(h0i.astype(BF16), cim_ref[...], preferred_element_type=F32)
    y_ref[...] = y


def _ssm_sample(u, h0re, h0im, toep, bre, bim, cre, cim, apw):
    def blk(r, c):
        return pl.BlockSpec((None, r, c), lambda g: (g, 0, 0))

    return pl.pallas_call(
        _ssm_sample_kernel,
        grid=(SSM_G,),
        in_specs=[blk(DEC_BATCH, SSM_SP), blk(DEC_BATCH, SSM_N), blk(DEC_BATCH, SSM_N),
                  blk(SSM_SP, SSM_SP), blk(SSM_SP, SSM_N), blk(SSM_SP, SSM_N),
                  blk(SSM_N, SSM_SP), blk(SSM_N, SSM_SP), blk(8, SSM_N)],
        out_specs=[blk(DEC_BATCH, SSM_SP), blk(DEC_BATCH, SSM_N), blk(DEC_BATCH, SSM_N)],
        out_shape=[jax.ShapeDtypeStruct((SSM_G, DEC_BATCH, SSM_SP), F32),
                   jax.ShapeDtypeStruct((SSM_G, DEC_BATCH, SSM_N), F32),
                   jax.ShapeDtypeStruct((SSM_G, DEC_BATCH, SSM_N), F32)],
        compiler_params=pltpu.CompilerParams(dimension_semantics=("parallel",)),
        name="ssm_sample",
    )(u, h0re, h0im, toep, bre, bim, cre, cim, apw)


def _sink_softmax_pv(s, valid, sink, vv):
    s = jnp.where(valid, s, NEG_INF)
    m = jnp.maximum(jnp.max(s, axis=-1, keepdims=True), sink)
    e = jnp.exp(s - m)
    denom = jnp.sum(e, axis=-1, keepdims=True) + jnp.exp(sink - m)
    return jnp.dot(e.astype(BF16), vv, preferred_element_type=F32) / denom


def _band_valid(n_q, n_k, rows):
    qi = lax.broadcasted_iota(jnp.int32, (rows, n_k), 0) & (n_q - 1)
    kj = lax.broadcasted_iota(jnp.int32, (rows, n_k), 1)
    return kj, ((kj - qi - 1).astype(jnp.uint32) < WINDOW)


def _head_masks():
    lane_kv = lax.broadcasted_iota(jnp.int32, (1, KV_W), 1) // HEAD_DIM
    return [jnp.where(lane_kv == h, HEAD_DIM ** -0.5, 0.0) for h in range(N_KV)], lane_kv == 0


ATTN_Q_TILE = 512


def _attn_prompt_kernel(q_ref, kvp_ref, kvc_ref, sink_ref, o_ref):
    rows = GRP * WINDOW
    scale_masks, first_kv = _head_masks()
    kj, valid = _band_valid(WINDOW, 2 * WINDOW, rows)
    valid_first = valid & ((kj >= WINDOW) | (pl.program_id(1) > 0))
    for r in range(ATTN_Q_TILE // WINDOW):
        cur = kvc_ref[r * WINDOW:(r + 1) * WINDOW, :]
        prev = kvp_ref[...] if r == 0 else kvc_ref[(r - 1) * WINDOW:r * WINDOW, :]
        kk = jnp.concatenate([prev[:, :KV_W], cur[:, :KV_W]], axis=0).astype(BF16)
        vv = jnp.concatenate([prev[:, KV_W:], cur[:, KV_W:]], axis=0).astype(BF16)
        q = q_ref[r * WINDOW:(r + 1) * WINDOW, :].astype(F32)
        outs = []
        for h in range(N_KV):
            qx = jnp.concatenate([q[:, c * KV_W:(c + 1) * KV_W] * scale_masks[h] for c in range(GRP)], axis=0)
            s = lax.dot_general(qx.astype(BF16), kk, (((1,), (1,)), ((), ())), preferred_element_type=F32)
            outs.append(_sink_softmax_pv(s, valid_first if r == 0 else valid, sink_ref[h], vv))
        for c in range(GRP):
            o = jnp.where(first_kv, outs[0][c * WINDOW:(c + 1) * WINDOW], outs[1][c * WINDOW:(c + 1) * WINDOW])
            o_ref[r * WINDOW:(r + 1) * WINDOW, c * KV_W:(c + 1) * KV_W] = o.astype(o_ref.dtype)


def _attn_prompt(q, kv, sink_col):
    tiles = SEQ // ATTN_Q_TILE
    per_tile = ATTN_Q_TILE // WINDOW
    blocks = SEQ // WINDOW
    return pl.pallas_call(
        _attn_prompt_kernel,
        grid=(BATCH, tiles),
        in_specs=[pl.BlockSpec((ATTN_Q_TILE, ATTN_W), lambda b, j: (b * tiles + j, 0)),
                  pl.BlockSpec((WINDOW, 2 * KV_W), lambda b, j: (b * blocks + jnp.maximum(j * per_tile - 1, 0), 0)),
                  pl.BlockSpec((ATTN_Q_TILE, 2 * KV_W), lambda b, j: (b * tiles + j, 0)),
                  pl.BlockSpec((N_KV, GRP * WINDOW, 1), lambda b, j: (0, 0, 0))],
        out_specs=pl.BlockSpec((ATTN_Q_TILE, ATTN_W), lambda b, j: (b * tiles + j, 0)),
        out_shape=jax.ShapeDtypeStruct((N_PROMPT, ATTN_W), BF16),
        compiler_params=pltpu.CompilerParams(dimension_semantics=("parallel", "arbitrary")),
        name="attn_prompt",
    )(q, kv, kv, sink_col)


SAMPLE_SEQ_BLOCK = 8


def _attn_sample_kernel(q_ref, ck_ref, cv_ref, kvn_ref, sink_ref, o_ref, nk_ref, nv_ref):
    rows = GRP * DEC_SEQ
    scale_masks, first_kv = _head_masks()
    _, valid = _band_valid(DEC_SEQ, WINDOW + DEC_SEQ, rows)
    q_all = q_ref[...].astype(F32)
    for s in range(SAMPLE_SEQ_BLOCK):
        tok = slice(s * DEC_SEQ, (s + 1) * DEC_SEQ)
        ck = ck_ref[s]
        cv = cv_ref[s]
        kn = kvn_ref[tok, :KV_W]
        vn = kvn_ref[tok, KV_W:]
        nk_ref[s, 0:WINDOW - DEC_SEQ, :] = ck[DEC_SEQ:, :]
        nk_ref[s, WINDOW - DEC_SEQ:WINDOW, :] = kn
        nv_ref[s, 0:WINDOW - DEC_SEQ, :] = cv[DEC_SEQ:, :]
        nv_ref[s, WINDOW - DEC_SEQ:WINDOW, :] = vn
        kk = jnp.concatenate([ck, kn], axis=0).astype(BF16)
        vv = jnp.concatenate([cv, vn], axis=0).astype(BF16)
        q = q_all[tok, :]
        outs = []
        for h in range(N_KV):
            qx = jnp.concatenate([q[:, c * KV_W:(c + 1) * KV_W] * scale_masks[h] for c in range(GRP)], axis=0)
            sc = lax.dot_general(qx.astype(BF16), kk, (((1,), (1,)), ((), ())), preferred_element_type=F32)
            outs.append(_sink_softmax_pv(sc, valid, sink_ref[h], vv))
        for c in range(GRP):
            o_ref[tok, c * KV_W:(c + 1) * KV_W] = jnp.where(
                first_kv, outs[0][c * DEC_SEQ:(c + 1) * DEC_SEQ], outs[1][c * DEC_SEQ:(c + 1) * DEC_SEQ])


def _attn_sample(q, kv, ck, cv, sink_col):
    sb = SAMPLE_SEQ_BLOCK
    tok = sb * DEC_SEQ
    first = N_PROMPT // tok
    c_spec = pl.BlockSpec((sb, WINDOW, KV_W), lambda i: (i, 0, 0))
    return pl.pallas_call(
        _attn_sample_kernel,
        grid=(DEC_BATCH // sb,),
        in_specs=[pl.BlockSpec((tok, ATTN_W), lambda i: (first + i, 0)), c_spec, c_spec,
                  pl.BlockSpec((tok, 2 * KV_W), lambda i: (first + i, 0)),
                  pl.BlockSpec((N_KV, GRP * DEC_SEQ, 1), lambda i: (0, 0, 0))],
        out_specs=[pl.BlockSpec((tok, ATTN_W), lambda i: (i, 0)), c_spec, c_spec],
        out_shape=[jax.ShapeDtypeStruct((N_SAMPLE, ATTN_W), F32),
                   jax.ShapeDtypeStruct((DEC_BATCH, WINDOW, KV_W), F32),
                   jax.ShapeDtypeStruct((DEC_BATCH, WINDOW, KV_W), F32)],
        compiler_params=pltpu.CompilerParams(dimension_semantics=("parallel",)),
        name="attn_sample",
    )(q, ck, cv, kv, sink_col)


PROMPT_TILES = N_PROMPT // TOKEN_TILE


def _mix_out_kernel(x_ref, y_ref, u_ref, op_ref, os_ref, d_ref, wglu_ref, gs_ref, ga_ref, wout_ref, xo_ref):
    y = y_ref[...] + d_ref[...] * u_ref[...]
    y = jax.nn.gelu(y)
    y = y * jax.nn.sigmoid(jnp.dot(y.astype(BF16), wglu_ref[...], preferred_element_type=F32))
    ys = _rms(y, gs_ref[...]).astype(BF16)
    o = jnp.where(pl.program_id(0) < PROMPT_TILES, op_ref[...].astype(F32), os_ref[...])
    ya = _rms(o, ga_ref[...]).astype(BF16)
    mix = jnp.dot(ys, wout_ref[0:SSM_W, :], preferred_element_type=F32)
    mix = mix + jnp.dot(ya, wout_ref[SSM_W:, :], preferred_element_type=F32)
    xo_ref[...] = x_ref[...] + mix


def _mix_out(x, y, u, o_prompt, o_sample, d, wglu, gs, ga, wout):
    n = x.shape[0]
    return pl.pallas_call(
        _mix_out_kernel,
        grid=(n // TOKEN_TILE,),
        in_specs=[_row_spec(TOKEN_TILE, D_MODEL), _row_spec(TOKEN_TILE, SSM_W), _row_spec(TOKEN_TILE, SSM_W),
                  pl.BlockSpec((TOKEN_TILE, ATTN_W), lambda i: (jnp.minimum(i, PROMPT_TILES - 1), 0)),
                  pl.BlockSpec((TOKEN_TILE, ATTN_W), lambda i: (jnp.maximum(i - PROMPT_TILES, 0), 0)),
                  _const_spec((1, SSM_W)), _const_spec((SSM_W, SSM_W)),
                  _const_spec((1, SSM_W)), _const_spec((1, ATTN_W)), _const_spec((D_MODEL, D_MODEL))],
        out_specs=_row_spec(TOKEN_TILE, D_MODEL),
        out_shape=jax.ShapeDtypeStruct((n, D_MODEL), F32),
        compiler_params=pltpu.CompilerParams(dimension_semantics=("parallel",), vmem_limit_bytes=VMEM_LIMIT),
        name="mix_out",
    )(x, y, u, o_prompt, o_sample, d, wglu, gs, ga, wout)


def _ffn_ple_kernel(x_ref, p_ref, g2_ref, wg_ref, wu_ref, wd_ref, gp_ref, wpg_ref, wpp_ref, gf_ref,
                    xo_ref, *, final):
    x2 = _half_swiglu(x_ref[...], g2_ref, wg_ref, wu_ref, wd_ref)
    h = _rms(x2, gp_ref[...]).astype(BF16)
    gate = jax.nn.sigmoid(jnp.dot(h, wpg_ref[...], preferred_element_type=F32))
    proj = jnp.dot(p_ref[...].astype(BF16), wpp_ref[...], preferred_element_type=F32)
    x3 = x2 + proj * gate
    xo_ref[...] = _rms(x3, gf_ref[...]) if final else x3


def _ffn_ple(x, p, g2, wg, wu, wd, gp, wpg, wpp, gf, final):
    n = x.shape[0]
    return pl.pallas_call(
        functools.partial(_ffn_ple_kernel, final=final),
        grid=(n // TOKEN_TILE,),
        in_specs=[_row_spec(TOKEN_TILE, D_MODEL), _row_spec(TOKEN_TILE, PLE_DIM), _const_spec((1, D_MODEL)),
                  _const_spec((D_MODEL, D_FF)), _const_spec((D_MODEL, D_FF)), _const_spec((D_FF, D_MODEL)),
                  _const_spec((1, D_MODEL)), _const_spec((D_MODEL, D_MODEL)), _const_spec((PLE_DIM, D_MODEL)),
                  _const_spec((1, D_MODEL))],
        out_specs=_row_spec(TOKEN_TILE, D_MODEL),
        out_shape=jax.ShapeDtypeStruct((n, D_MODEL), F32),
        compiler_params=pltpu.CompilerParams(dimension_semantics=("parallel",), vmem_limit_bytes=VMEM_LIMIT),
        name="ffn2_ple",
    )(x, p, g2, wg, wu, wd, gp, wpg, wpp, gf)


def _sink_column(sinks, rows_per_head):
    return jnp.repeat(sinks.astype(F32).reshape(N_KV, GRP), rows_per_head, axis=1)[..., None]


def _head_major(a, axis):
    shape = a.shape
    a = a.reshape(shape[:axis] + (N_KV, GRP, HEAD_DIM) + shape[axis + 1:])
    return jnp.swapaxes(a, axis, axis + 1).reshape(shape)


def kernel(x_prompt, x_sample, cache_k, cache_v, state_ssm_re, state_ssm_im, p_prompt, p_sample, ffn1_norm, ffn1_w_gate, ffn1_w_up, ffn1_w_down, mix_norm, w_in, ssm_lam_re, ssm_lam_im, ssm_log_dt, ssm_b_re, ssm_b_im, ssm_c_re, ssm_c_im, ssm_d, ssm_w_glu, ssm_out_norm, attn_sinks, attn_out_norm, w_out, ffn2_norm, ffn2_w_gate, ffn2_w_up, ffn2_w_down, ple_norm, ple_w_gate, ple_w_proj, final_norm):
    lg = DEPTH * SSM_G
    toep, bpow, cpow_t, apw = _ssm_prep(
        ssm_lam_re.reshape(lg, 1, SSM_N), ssm_lam_im.reshape(lg, 1, SSM_N),
        jnp.broadcast_to(ssm_log_dt.reshape(lg, 1, 1), (lg, 1, SSM_N)),
        ssm_b_re.transpose(0, 1, 3, 2).reshape(lg, SSM_P, SSM_N),
        ssm_b_im.transpose(0, 1, 3, 2).reshape(lg, SSM_P, SSM_N),
        ssm_c_re.reshape(lg, SSM_P, SSM_N), ssm_c_im.reshape(lg, SSM_P, SSM_N))
    toep = toep.reshape(DEPTH, SSM_G, SSM_TP, SSM_TP)
    bpow = bpow.reshape(DEPTH, SSM_G, SSM_TP, 2 * SSM_N)
    cpow = cpow_t.reshape(DEPTH, SSM_G, SSM_TP, 2 * SSM_N).transpose(0, 1, 3, 2)
    apw = apw.reshape(DEPTH, SSM_G, 8, SSM_N)

    x = jnp.concatenate([x_prompt.reshape(N_PROMPT, D_MODEL), x_sample.reshape(N_SAMPLE, D_MODEL)], axis=0)
    p_all = jnp.concatenate([p_prompt.reshape(DEPTH, N_PROMPT, PLE_DIM),
                             p_sample.reshape(DEPTH, N_SAMPLE, PLE_DIM)], axis=1)
    row = lambda v: v.reshape(1, -1)
    bf = lambda w: w.astype(BF16)

    kp_l, vp_l, hpr_l, hpi_l, ks_l, vs_l, hsr_l, hsi_l = [], [], [], [], [], [], [], []
    for i in range(DEPTH):
        w_in_i = jnp.concatenate([w_in[i, :, :SSM_W], _head_major(w_in[i, :, SSM_W:SSM_W + ATTN_W], 1),
                                  w_in[i, :, SSM_W + ATTN_W:]], axis=1)
        x, u, q, kv = _ffn_proj(x, row(ffn1_norm[i]), bf(ffn1_w_gate[i]), bf(ffn1_w_up[i]), bf(ffn1_w_down[i]),
                                row(mix_norm[i]), bf(w_in_i))

        u_p = u[:N_PROMPT].reshape(BATCH, N_CHUNK, SSM_T, SSM_G, SSM_P).transpose(3, 1, 0, 2, 4)
        u_p = u_p.reshape(SSM_G, N_CHUNK * BATCH, SSM_TP).astype(BF16)
        y_p, hpr, hpi = _ssm_prompt(u_p, bf(toep[i]), bf(bpow[i, :, :, :SSM_N]), bf(bpow[i, :, :, SSM_N:]),
                                    bf(cpow[i, :, :SSM_N]), bf(cpow[i, :, SSM_N:]), apw[i])
        y_p = y_p.reshape(SSM_G, N_CHUNK, BATCH, SSM_T, SSM_P).transpose(2, 1, 3, 0, 4).reshape(N_PROMPT, SSM_W)
        hpr_l.append(hpr.transpose(1, 0, 2))
        hpi_l.append(hpi.transpose(1, 0, 2))

        u_s = u[N_PROMPT:].reshape(DEC_BATCH, DEC_SEQ, SSM_G, SSM_P).transpose(2, 0, 1, 3)
        u_s = u_s.reshape(SSM_G, DEC_BATCH, SSM_SP)
        y_s, hsr, hsi = _ssm_sample(
            u_s, state_ssm_re[i].transpose(1, 0, 2), state_ssm_im[i].transpose(1, 0, 2),
            bf(toep[i, :, :SSM_SP, :SSM_SP]), bpow[i, :, SSM_TP - SSM_SP:, :SSM_N], bpow[i, :, SSM_TP - SSM_SP:, SSM_N:],
            bf(cpow[i, :, :SSM_N, :SSM_SP]), bf(cpow[i, :, SSM_N:, :SSM_SP]), apw[i])
        y_s = y_s.reshape(SSM_G, DEC_BATCH, DEC_SEQ, SSM_P).transpose(1, 2, 0, 3).reshape(N_SAMPLE, SSM_W)
        hsr_l.append(hsr.transpose(1, 0, 2))
        hsi_l.append(hsi.transpose(1, 0, 2))

        kv_last = kv[:N_PROMPT].reshape(BATCH, SEQ, 2, N_KV, HEAD_DIM)[:, -WINDOW:]
        kp_l.append(kv_last[:, :, 0])
        vp_l.append(kv_last[:, :, 1])
        o_p = _attn_prompt(q, kv, _sink_column(attn_sinks[i], WINDOW))
        o_s, nk, nv = _attn_sample(q, kv, cache_k[i].reshape(DEC_BATCH, WINDOW, KV_W),
                                   cache_v[i].reshape(DEC_BATCH, WINDOW, KV_W), _sink_column(attn_sinks[i], DEC_SEQ))
        ks_l.append(nk.reshape(DEC_BATCH, WINDOW, N_KV, HEAD_DIM))
        vs_l.append(nv.reshape(DEC_BATCH, WINDOW, N_KV, HEAD_DIM))

        x = _mix_out(x, jnp.concatenate([y_p, y_s], axis=0), u, o_p, o_s,
                     row(ssm_d[i]), bf(ssm_w_glu[i]), row(ssm_out_norm[i]), row(_head_major(attn_out_norm[i], 0)),
                     bf(jnp.concatenate([w_out[i, :SSM_W], _head_major(w_out[i, SSM_W:], 0)], axis=0)))
        x = _ffn_ple(x, p_all[i], row(ffn2_norm[i]), bf(ffn2_w_gate[i]), bf(ffn2_w_up[i]), bf(ffn2_w_down[i]),
                     row(ple_norm[i]), bf(ple_w_gate[i]), bf(ple_w_proj[i]), row(final_norm),
                     final=(i == DEPTH - 1))

    y_prompt = x[:N_PROMPT].reshape(BATCH, SEQ, D_MODEL)
    y_sample = x[N_PROMPT:].reshape(DEC_BATCH, DEC_SEQ, D_MODEL)
    return (y_prompt, y_sample, jnp.stack(kp_l), jnp.stack(vp_l), jnp.stack(hpr_l), jnp.stack(hpi_l),
            jnp.stack(ks_l), jnp.stack(vs_l), jnp.stack(hsr_l), jnp.stack(hsi_l))
```

```python
import functools

import jax
import jax.numpy as jnp
from jax import lax
from jax.experimental import pallas as pl
from jax.experimental.pallas import tpu as pltpu

F32 = jnp.float32
BF16 = jnp.bfloat16

D_MODEL = 1024
BATCH = 8
SEQ = 2048
DEPTH = 2
DEC_BATCH = 128
DEC_SEQ = 8
SSM_W = 512
SSM_P = 16
SSM_G = 32
SSM_N = 64
ATTN_W = 512
HEAD_DIM = 64
N_HEADS = 8
N_KV = 2
GRP = N_HEADS // N_KV
KV_W = N_KV * HEAD_DIM
IN_W = SSM_W + ATTN_W + 2 * KV_W
WINDOW = 128
D_FF = 2816
PLE_DIM = 256
EPS = 1e-6
NEG_INF = -1e30

N_PROMPT = BATCH * SEQ
N_SAMPLE = DEC_BATCH * DEC_SEQ
N_TOK = N_PROMPT + N_SAMPLE

LANES = 128
SSM_T = 16
N_CHUNK = SEQ // SSM_T
SLAB_G = LANES // SSM_P
N_SLAB = SSM_G // SLAB_G
SLAB_N = SLAB_G * SSM_N
SSM_HALF_B = BATCH // 2

TOKEN_TILE = 512
PROMPT_TILES = N_PROMPT // TOKEN_TILE
FF_CHUNK = 256
VMEM_LIMIT = 56 * 1024 * 1024


def _const_spec(shape):
    nd = len(shape)
    return pl.BlockSpec(shape, lambda *_: (0,) * nd, pipeline_mode=pl.Buffered(1))


def _row_spec(width):
    return pl.BlockSpec((TOKEN_TILE, width), lambda i: (i, 0))


def _pair_specs(width):
    return [pl.BlockSpec((TOKEN_TILE, width), lambda i: (jnp.minimum(i, PROMPT_TILES - 1), 0)),
            pl.BlockSpec((TOKEN_TILE, width), lambda i: (jnp.maximum(i - PROMPT_TILES, 0), 0))]


def _pick(prompt_ref, sample_ref):
    return jnp.where(pl.program_id(0) < PROMPT_TILES, prompt_ref[...].astype(F32), sample_ref[...].astype(F32))


def _rms(x, g):
    return x * lax.rsqrt(jnp.mean(x * x, axis=-1, keepdims=True) + EPS) * g


def _half_swiglu(xf, g_ref, wg_ref, wu_ref, wd_ref):
    h = _rms(xf, g_ref[...]).astype(BF16)
    acc = jnp.zeros(xf.shape, F32)
    for c in range(0, D_FF, FF_CHUNK):
        gate = jnp.dot(h, wg_ref[:, c:c + FF_CHUNK], preferred_element_type=F32)
        up = jnp.dot(h, wu_ref[:, c:c + FF_CHUNK], preferred_element_type=F32)
        act = (gate * jax.nn.sigmoid(gate) * up).astype(BF16)
        acc = acc + jnp.dot(act, wd_ref[c:c + FF_CHUNK, :], preferred_element_type=F32)
    return xf + 0.5 * acc


def _ffn_proj_kernel(*refs, paired):
    if paired:
        xp_ref, xs_ref, *refs = refs
        x = _pick(xp_ref, xs_ref)
    else:
        x_ref, *refs = refs
        x = x_ref[...]
    g1_ref, wg_ref, wu_ref, wd_ref, gm_ref, win_ref, xo_ref, u_ref, q_ref, kv_ref = refs
    x1 = _half_swiglu(x, g1_ref, wg_ref, wu_ref, wd_ref)
    xo_ref[...] = x1
    h = _rms(x1, gm_ref[...]).astype(BF16)
    z = jnp.dot(h, win_ref[...], preferred_element_type=F32)
    u_ref[...] = z[:, :SSM_W]
    q_ref[...] = z[:, SSM_W:SSM_W + ATTN_W].astype(BF16)
    kv_ref[...] = z[:, SSM_W + ATTN_W:]


def _ffn_proj(xs, g1, wg, wu, wd, gm, win):
    paired = len(xs) == 2
    x_specs = _pair_specs(D_MODEL) if paired else [_row_spec(D_MODEL)]
    return pl.pallas_call(
        functools.partial(_ffn_proj_kernel, paired=paired),
        grid=(N_TOK // TOKEN_TILE,),
        in_specs=x_specs + [_const_spec((1, D_MODEL)),
                            _const_spec((D_MODEL, D_FF)), _const_spec((D_MODEL, D_FF)), _const_spec((D_FF, D_MODEL)),
                            _const_spec((1, D_MODEL)), _const_spec((D_MODEL, IN_W))],
        out_specs=[_row_spec(D_MODEL), _row_spec(SSM_W), _row_spec(ATTN_W), _row_spec(2 * KV_W)],
        out_shape=[jax.ShapeDtypeStruct((N_TOK, D_MODEL), F32), jax.ShapeDtypeStruct((N_TOK, SSM_W), F32),
                   jax.ShapeDtypeStruct((N_TOK, ATTN_W), BF16), jax.ShapeDtypeStruct((N_TOK, 2 * KV_W), F32)],
        compiler_params=pltpu.CompilerParams(dimension_semantics=("parallel",), vmem_limit_bytes=VMEM_LIMIT),
        name="ffn1_proj",
    )(*xs, g1, wg, wu, wd, gm, win)


def _discretise(lam_re, lam_im, log_dt):
    dt = jnp.exp(log_dt)
    mag = jnp.exp(lam_re * dt)
    ang = lam_im * dt
    a_re = mag * jnp.cos(ang)
    a_im = mag * jnp.sin(ang)
    den = lam_re * lam_re + lam_im * lam_im
    nr = a_re - 1.0
    k_re = (nr * lam_re + a_im * lam_im) / den
    k_im = (a_im * lam_re - nr * lam_im) / den
    return a_re, a_im, k_re, k_im


def _powers(a_re, a_im, n):
    p_re = [jnp.ones_like(a_re)]
    p_im = [jnp.zeros_like(a_re)]
    for _ in range(n):
        r, i = p_re[-1], p_im[-1]
        p_re.append(r * a_re - i * a_im)
        p_im.append(r * a_im + i * a_re)
    return p_re, p_im


def _ssm_prep_kernel(lr_re, lr_im, lr_dt, btr_re, btr_im, cr_re, cr_im,
                     lw_re, lw_im, lw_dt, btw_re, btw_im,
                     lt_re, lt_im, lt_dt, ct_re, ct_im,
                     lf_re, lf_im, lf_dt,
                     wpair_ref, bfull_ref, bhalf_ref, cfull_ref, apw_ref):
    a_re, a_im, k_re, k_im = _discretise(lr_re[...], lr_im[...], lr_dt[...])
    bb_re = k_re * btr_re[...] - k_im * btr_im[...]
    bb_im = k_re * btr_im[...] + k_im * btr_re[...]
    p_re, p_im = _powers(a_re, a_im, SSM_T - 1)
    c_re, c_im = cr_re[...], cr_im[...]
    ca_all = jnp.concatenate(
        [jnp.concatenate([c_re * p_re[t] - c_im * p_im[t], c_re * p_im[t] + c_im * p_re[t]], axis=1)
         for t in range(SSM_T)], axis=0)
    bb2 = jnp.concatenate([bb_re, -bb_im], axis=1)
    drow = lax.dot_general(bb2, ca_all, (((1,), (1,)), ((), ())),
                           precision=lax.Precision.HIGHEST, preferred_element_type=F32)
    row_g = lax.broadcasted_iota(jnp.int32, drow.shape, 0) // SSM_P
    col_g = (lax.broadcasted_iota(jnp.int32, drow.shape, 1) % LANES) // SSM_P
    drow = jnp.where(row_g == col_g, drow, 0.0).astype(BF16)
    wpair_ref[0:LANES, :] = drow
    wpair_ref[LANES:, 0:LANES] = jnp.zeros((LANES, LANES), BF16)
    wpair_ref[LANES:, LANES:] = drow[:, :(SSM_T - 1) * LANES]

    a_re, a_im, k_re, k_im = _discretise(lw_re[...], lw_im[...], lw_dt[...])
    bb_re = k_re * btw_re[...] - k_im * btw_im[...]
    bb_im = k_re * btw_im[...] + k_im * btw_re[...]
    p_re, p_im = _powers(a_re, a_im, SSM_T - 1)
    diag = (lax.broadcasted_iota(jnp.int32, bb_re.shape, 0) // SSM_P
            == lax.broadcasted_iota(jnp.int32, bb_re.shape, 1) // SSM_N)
    for s in range(SSM_T):
        w_re, w_im = p_re[SSM_T - 1 - s], p_im[SSM_T - 1 - s]
        blk = jnp.concatenate([jnp.where(diag, bb_re * w_re - bb_im * w_im, 0.0),
                               jnp.where(diag, bb_re * w_im + bb_im * w_re, 0.0)], axis=1)
        bfull_ref[s * LANES:(s + 1) * LANES, :] = blk.astype(BF16)
        if s >= SSM_T - DEC_SEQ:
            bhalf_ref[(s - SSM_T + DEC_SEQ) * LANES:(s - SSM_T + DEC_SEQ + 1) * LANES, :] = blk

    a_re, a_im, _, _ = _discretise(lt_re[...], lt_im[...], lt_dt[...])
    p_re, p_im = _powers(a_re, a_im, SSM_T)
    c_re, c_im = ct_re[...], ct_im[...]
    diag = (lax.broadcasted_iota(jnp.int32, c_re.shape, 0) // SSM_N
            == lax.broadcasted_iota(jnp.int32, c_re.shape, 1) // SSM_P)
    for t in range(SSM_T):
        ca_re = c_re * p_re[t + 1] - c_im * p_im[t + 1]
        ca_im = c_re * p_im[t + 1] + c_im * p_re[t + 1]
        cfull_ref[0:SLAB_N, t * LANES:(t + 1) * LANES] = jnp.where(diag, ca_re, 0.0).astype(BF16)
        cfull_ref[SLAB_N:, t * LANES:(t + 1) * LANES] = jnp.where(diag, -ca_im, 0.0).astype(BF16)

    a_re, a_im, _, _ = _discretise(lf_re[...], lf_im[...], lf_dt[...])
    sq = []
    for _ in range(4):
        a_re, a_im = a_re * a_re - a_im * a_im, 2.0 * a_re * a_im
        sq.append((a_re, a_im))
    apw_ref[0:1, :] = sq[2][0]
    apw_ref[1:2, :] = sq[2][1]
    apw_ref[2:3, :] = sq[3][0]
    apw_ref[3:4, :] = sq[3][1]
    apw_ref[4:8, :] = jnp.zeros((4, SLAB_N), F32)


def _ssm_prep(lam_re, lam_im, log_dt, b_re, b_im, c_re, c_im):
    n = DEPTH * N_SLAB
    ldt = jnp.broadcast_to(log_dt[..., None], lam_re.shape)
    rows = lambda v: jnp.repeat(v.reshape(n, SLAB_G, 1, SSM_N), SSM_P, axis=2).reshape(n, LANES, SSM_N)
    bt = lambda v: v.transpose(0, 1, 3, 2).reshape(n, LANES, SSM_N)
    wide = lambda v: jnp.tile(v, (1, 1, SLAB_G))
    tall = lambda v: jnp.broadcast_to(v.reshape(n, SLAB_N, 1), (n, SLAB_N, LANES))
    ct = lambda v: jnp.tile(v.transpose(0, 1, 3, 2).reshape(n, SLAB_N, SSM_P), (1, 1, SLAB_G))
    flat = lambda v: v.reshape(n, 1, SLAB_N)
    lam = (lam_re, lam_im, ldt)
    args = ([rows(v) for v in lam] + [bt(b_re), bt(b_im), c_re.reshape(n, LANES, SSM_N), c_im.reshape(n, LANES, SSM_N)]
            + [wide(rows(v)) for v in lam] + [wide(bt(b_re)), wide(bt(b_im))]
            + [tall(v) for v in lam] + [ct(c_re), ct(c_im)]
            + [flat(v) for v in lam])

    def blk(a):
        return pl.BlockSpec((None,) + a.shape[1:], lambda i: (i, 0, 0))

    out_shape = [jax.ShapeDtypeStruct((n, 2 * LANES, SSM_T * LANES), BF16),
                 jax.ShapeDtypeStruct((n, SSM_T * LANES, 2 * SLAB_N), BF16),
                 jax.ShapeDtypeStruct((n, DEC_SEQ * LANES, 2 * SLAB_N), F32),
                 jax.ShapeDtypeStruct((n, 2 * SLAB_N, SSM_T * LANES), BF16),
                 jax.ShapeDtypeStruct((n, 8, SLAB_N), F32)]
    return pl.pallas_call(
        _ssm_prep_kernel,
        grid=(n,),
        in_specs=[blk(a) for a in args],
        out_specs=[blk(s) for s in out_shape],
        out_shape=out_shape,
        compiler_params=pltpu.CompilerParams(dimension_semantics=("parallel",), vmem_limit_bytes=VMEM_LIMIT),
        name="ssm_prep",
    )(*args)


def _ssm_prompt_kernel(u_ref, wpair_ref, bfull_ref, cfull_ref, apw_ref, y_ref, hst_ref,
                       ub_ref, s_ref, yacc_ref, *col_refs):
    rows = SSM_HALF_B * N_CHUNK
    n_col = SLAB_N // LANES
    sp_refs, hp_refs = col_refs[:n_col], col_refs[n_col:]
    for t in range(SSM_T):
        ub_ref[:, t * LANES:(t + 1) * LANES] = u_ref[pl.ds(t, rows, stride=SSM_T), :].astype(BF16)
    s_ref[...] = jnp.dot(ub_ref[...], bfull_ref[...], preferred_element_type=F32)
    for b in range(SSM_HALF_B):
        seq = slice(b * N_CHUNK, (b + 1) * N_CHUNK)
        for k in range(n_col):
            sp_refs[k][pl.ds(b, N_CHUNK, stride=8), :] = s_ref[seq, k * LANES:(k + 1) * LANES]
            sp_refs[k][pl.ds(SSM_HALF_B + b, N_CHUNK, stride=8), :] = (
                s_ref[seq, SLAB_N + k * LANES:SLAB_N + (k + 1) * LANES])
    upper = lax.broadcasted_iota(jnp.int32, (8, LANES), 0) < SSM_HALF_B
    a1 = [jnp.broadcast_to(apw_ref[2:3, k * LANES:(k + 1) * LANES], (8, LANES)) for k in range(n_col)]
    a2 = [jnp.where(upper, -1.0, 1.0) * jnp.broadcast_to(apw_ref[3:4, k * LANES:(k + 1) * LANES], (8, LANES))
          for k in range(n_col)]

    def step(c, h):
        r0 = pl.multiple_of(c * 8, 8)
        new = []
        for k in range(n_col):
            hp_refs[k][pl.ds(r0, 8), :] = h[k]
            new.append(a1[k] * h[k] + a2[k] * pltpu.roll(h[k], SSM_HALF_B, axis=0) + sp_refs[k][pl.ds(r0, 8), :])
        return tuple(new)

    h = lax.fori_loop(0, N_CHUNK, step, tuple(jnp.zeros((8, LANES), F32) for _ in range(n_col)), unroll=8)
    hst_ref[...] = jnp.concatenate(h, axis=1)
    hbc = jnp.concatenate(
        [jnp.concatenate([hp_refs[k][pl.ds(part + b, N_CHUNK, stride=8), :]
                          for part in (0, SSM_HALF_B) for k in range(n_col)], axis=1)
         for b in range(SSM_HALF_B)], axis=0)
    yacc_ref[...] = jnp.dot(hbc.astype(BF16), cfull_ref[...], preferred_element_type=F32)
    for t in range(0, SSM_T, 2):
        yacc_ref[:, t * LANES:] += jnp.dot(ub_ref[:, t * LANES:(t + 2) * LANES],
                                           wpair_ref[:, 0:(SSM_T - t) * LANES], preferred_element_type=F32)
    for t in range(SSM_T):
        y_ref[pl.ds(t, rows, stride=SSM_T), :] = yacc_ref[:, t * LANES:(t + 1) * LANES]


def _ssm_prompt(u, wpair, bfull, cfull, apw, layer):
    rows = SSM_HALF_B * N_CHUNK
    tok = SSM_HALF_B * SEQ

    def wspec(a):
        return pl.BlockSpec((None,) + a.shape[1:], lambda j, h: (layer * N_SLAB + j, 0, 0))

    return pl.pallas_call(
        _ssm_prompt_kernel,
        grid=(N_SLAB, BATCH // SSM_HALF_B),
        in_specs=[pl.BlockSpec((tok, LANES), lambda j, h: (h, j)), wspec(wpair), wspec(bfull), wspec(cfull), wspec(apw)],
        out_specs=[pl.BlockSpec((tok, LANES), lambda j, h: (h, j)),
                   pl.BlockSpec((None, 8, SLAB_N), lambda j, h: (h, 0, j))],
        out_shape=[jax.ShapeDtypeStruct((N_PROMPT, SSM_W), F32),
                   jax.ShapeDtypeStruct((BATCH // SSM_HALF_B, 8, SSM_G * SSM_N), F32)],
        scratch_shapes=[pltpu.VMEM((rows, SSM_T * LANES), BF16), pltpu.VMEM((rows, 2 * SLAB_N), F32),
                        pltpu.VMEM((rows, SSM_T * LANES), F32)]
                       + [pltpu.VMEM((N_CHUNK * 8, LANES), F32)] * (2 * SLAB_N // LANES),
        compiler_params=pltpu.CompilerParams(dimension_semantics=("arbitrary", "arbitrary"),
                                             vmem_limit_bytes=VMEM_LIMIT),
        name="ssm_prompt",
    )(u, wpair, bfull, cfull, apw)


def _ssm_sample_kernel(u_ref, h0r_ref, h0i_ref, wpair_ref, bhalf_ref, cfull_ref, apw_ref,
                       y_ref, hr_ref, hi_ref, uf_ref, yacc_ref):
    for t in range(DEC_SEQ):
        uf_ref[:, t * LANES:(t + 1) * LANES] = u_ref[pl.ds(t, DEC_BATCH, stride=DEC_SEQ), :]
    uf = uf_ref[...]
    h0r = h0r_ref[...]
    h0i = h0i_ref[...]
    ar = apw_ref[0:1, :]
    ai = apw_ref[1:2, :]
    s = jnp.dot(uf, bhalf_ref[...], precision=lax.Precision.HIGHEST, preferred_element_type=F32)
    hr_ref[...] = ar * h0r - ai * h0i + s[:, :SLAB_N]
    hi_ref[...] = ar * h0i + ai * h0r + s[:, SLAB_N:]
    ub = uf.astype(BF16)
    h0 = jnp.concatenate([h0r, h0i], axis=1).astype(BF16)
    yacc_ref[...] = jnp.dot(h0, cfull_ref[...], preferred_element_type=F32)
    for t in range(0, DEC_SEQ, 2):
        yacc_ref[:, t * LANES:] += jnp.dot(ub[:, t * LANES:(t + 2) * LANES],
                                           wpair_ref[:, 0:(DEC_SEQ - t) * LANES], preferred_element_type=F32)
    for t in range(DEC_SEQ):
        y_ref[pl.ds(t, DEC_BATCH, stride=DEC_SEQ), :] = yacc_ref[:, t * LANES:(t + 1) * LANES]


def _ssm_sample(u, h0re, h0im, wpair, bhalf, cfull, apw, layer):
    width = DEC_SEQ * LANES

    def wspec(r, c):
        return pl.BlockSpec((None, r, c), lambda j: (layer * N_SLAB + j, 0, 0))

    st = pl.BlockSpec((DEC_BATCH, SLAB_N), lambda j: (0, j))
    return pl.pallas_call(
        _ssm_sample_kernel,
        grid=(N_SLAB,),
        in_specs=[pl.BlockSpec((N_SAMPLE, LANES), lambda j: (N_PROMPT // N_SAMPLE, j)), st, st,
                  wspec(2 * LANES, width), wspec(width, 2 * SLAB_N), wspec(2 * SLAB_N, width), wspec(8, SLAB_N)],
        out_specs=[pl.BlockSpec((N_SAMPLE, LANES), lambda j: (0, j)), st, st],
        out_shape=[jax.ShapeDtypeStruct((N_SAMPLE, SSM_W), F32),
                   jax.ShapeDtypeStruct((DEC_BATCH, SSM_G * SSM_N), F32),
                   jax.ShapeDtypeStruct((DEC_BATCH, SSM_G * SSM_N), F32)],
        scratch_shapes=[pltpu.VMEM((DEC_BATCH, width), F32), pltpu.VMEM((DEC_BATCH, width), F32)],
        compiler_params=pltpu.CompilerParams(dimension_semantics=("arbitrary",), vmem_limit_bytes=VMEM_LIMIT),
        name="ssm_sample",
    )(u, h0re, h0im, wpair, bhalf, cfull, apw)


def _sink_softmax_pv(s, valid, sink, vv):
    s = jnp.where(valid, s, NEG_INF)
    m = jnp.maximum(jnp.max(s, axis=-1, keepdims=True), sink)
    e = jnp.exp(s - m)
    denom = jnp.sum(e, axis=-1, keepdims=True) + jnp.exp(sink - m)
    return jnp.dot(e.astype(BF16), vv, preferred_element_type=F32) / denom


def _band_valid(n_q, n_k, rows):
    qi = lax.broadcasted_iota(jnp.int32, (rows, n_k), 0) & (n_q - 1)
    kj = lax.broadcasted_iota(jnp.int32, (rows, n_k), 1)
    return kj, ((kj - qi - 1).astype(jnp.uint32) < WINDOW)


def _head_masks():
    lane_kv = lax.broadcasted_iota(jnp.int32, (1, KV_W), 1) // HEAD_DIM
    return [jnp.where(lane_kv == h, HEAD_DIM ** -0.5, 0.0) for h in range(N_KV)], lane_kv == 0


ATTN_Q_TILE = 512


def _attn_prompt_kernel(q_ref, kvp_ref, kvc_ref, sink_ref, o_ref):
    rows = GRP * WINDOW
    scale_masks, first_kv = _head_masks()
    kj, valid = _band_valid(WINDOW, 2 * WINDOW, rows)
    valid_first = valid & ((kj >= WINDOW) | (pl.program_id(1) > 0))
    for r in range(ATTN_Q_TILE // WINDOW):
        cur = kvc_ref[r * WINDOW:(r + 1) * WINDOW, :]
        prev = kvp_ref[...] if r == 0 else kvc_ref[(r - 1) * WINDOW:r * WINDOW, :]
        kk = jnp.concatenate([prev[:, :KV_W], cur[:, :KV_W]], axis=0).astype(BF16)
        vv = jnp.concatenate([prev[:, KV_W:], cur[:, KV_W:]], axis=0).astype(BF16)
        q = q_ref[r * WINDOW:(r + 1) * WINDOW, :].astype(F32)
        outs = []
        for h in range(N_KV):
            qx = jnp.concatenate([q[:, c * KV_W:(c + 1) * KV_W] * scale_masks[h] for c in range(GRP)], axis=0)
            s = lax.dot_general(qx.astype(BF16), kk, (((1,), (1,)), ((), ())), preferred_element_type=F32)
            outs.append(_sink_softmax_pv(s, valid_first if r == 0 else valid, sink_ref[h], vv))
        for c in range(GRP):
            o = jnp.where(first_kv, outs[0][c * WINDOW:(c + 1) * WINDOW], outs[1][c * WINDOW:(c + 1) * WINDOW])
            o_ref[r * WINDOW:(r + 1) * WINDOW, c * KV_W:(c + 1) * KV_W] = o.astype(o_ref.dtype)


def _attn_prompt(q, kv, sink_col):
    tiles = SEQ // ATTN_Q_TILE
    per_tile = ATTN_Q_TILE // WINDOW
    blocks = SEQ // WINDOW
    return pl.pallas_call(
        _attn_prompt_kernel,
        grid=(BATCH, tiles),
        in_specs=[pl.BlockSpec((ATTN_Q_TILE, ATTN_W), lambda b, j: (b * tiles + j, 0)),
                  pl.BlockSpec((WINDOW, 2 * KV_W), lambda b, j: (b * blocks + jnp.maximum(j * per_tile - 1, 0), 0)),
                  pl.BlockSpec((ATTN_Q_TILE, 2 * KV_W), lambda b, j: (b * tiles + j, 0)),
                  pl.BlockSpec((N_KV, GRP * WINDOW, 1), lambda b, j: (0, 0, 0))],
        out_specs=pl.BlockSpec((ATTN_Q_TILE, ATTN_W), lambda b, j: (b * tiles + j, 0)),
        out_shape=jax.ShapeDtypeStruct((N_PROMPT, ATTN_W), BF16),
        compiler_params=pltpu.CompilerParams(dimension_semantics=("parallel", "arbitrary")),
        name="attn_prompt",
    )(q, kv, kv, sink_col)


SAMPLE_SEQ_BLOCK = 8


def _attn_sample_kernel(q_ref, ck_ref, cv_ref, kvn_ref, sink_ref, o_ref, nk_ref, nv_ref):
    rows = GRP * DEC_SEQ
    scale_masks, first_kv = _head_masks()
    _, valid = _band_valid(DEC_SEQ, WINDOW + DEC_SEQ, rows)
    q_all = q_ref[...].astype(F32)
    for s in range(SAMPLE_SEQ_BLOCK):
        tok = slice(s * DEC_SEQ, (s + 1) * DEC_SEQ)
        ck = ck_ref[s]
        cv = cv_ref[s]
        kn = kvn_ref[tok, :KV_W]
        vn = kvn_ref[tok, KV_W:]
        nk_ref[s, 0:WINDOW - DEC_SEQ, :] = ck[DEC_SEQ:, :]
        nk_ref[s, WINDOW - DEC_SEQ:WINDOW, :] = kn
        nv_ref[s, 0:WINDOW - DEC_SEQ, :] = cv[DEC_SEQ:, :]
        nv_ref[s, WINDOW - DEC_SEQ:WINDOW, :] = vn
        kk = jnp.concatenate([ck, kn], axis=0).astype(BF16)
        vv = jnp.concatenate([cv, vn], axis=0).astype(BF16)
        q = q_all[tok, :]
        outs = []
        for h in range(N_KV):
            qx = jnp.concatenate([q[:, c * KV_W:(c + 1) * KV_W] * scale_masks[h] for c in range(GRP)], axis=0)
            sc = lax.dot_general(qx.astype(BF16), kk, (((1,), (1,)), ((), ())), preferred_element_type=F32)
            outs.append(_sink_softmax_pv(sc, valid, sink_ref[h], vv))
        for c in range(GRP):
            o_ref[tok, c * KV_W:(c + 1) * KV_W] = jnp.where(
                first_kv, outs[0][c * DEC_SEQ:(c + 1) * DEC_SEQ], outs[1][c * DEC_SEQ:(c + 1) * DEC_SEQ])


def _attn_sample(q, kv, ck, cv, sink_col):
    sb = SAMPLE_SEQ_BLOCK
    tok = sb * DEC_SEQ
    first = N_PROMPT // tok
    c_spec = pl.BlockSpec((sb, WINDOW, KV_W), lambda i: (i, 0, 0))
    return pl.pallas_call(
        _attn_sample_kernel,
        grid=(DEC_BATCH // sb,),
        in_specs=[pl.BlockSpec((tok, ATTN_W), lambda i: (first + i, 0)), c_spec, c_spec,
                  pl.BlockSpec((tok, 2 * KV_W), lambda i: (first + i, 0)),
                  pl.BlockSpec((N_KV, GRP * DEC_SEQ, 1), lambda i: (0, 0, 0))],
        out_specs=[pl.BlockSpec((tok, ATTN_W), lambda i: (i, 0)), c_spec, c_spec],
        out_shape=[jax.ShapeDtypeStruct((N_SAMPLE, ATTN_W), F32),
                   jax.ShapeDtypeStruct((DEC_BATCH, WINDOW, KV_W), F32),
                   jax.ShapeDtypeStruct((DEC_BATCH, WINDOW, KV_W), F32)],
        compiler_params=pltpu.CompilerParams(dimension_semantics=("parallel",)),
        name="attn_sample",
    )(q, ck, cv, kv, sink_col)


def _mix_out_kernel(x_ref, yp_ref, ys_ref, u_ref, op_ref, os_ref, d_ref, wglu_ref, gs_ref, ga_ref, wout_ref,
                    xo_ref):
    y = _pick(yp_ref, ys_ref) + d_ref[...] * u_ref[...]
    y = jax.nn.gelu(y)
    y = y * jax.nn.sigmoid(jnp.dot(y.astype(BF16), wglu_ref[...], preferred_element_type=F32))
    ys = _rms(y, gs_ref[...]).astype(BF16)
    ya = _rms(_pick(op_ref, os_ref), ga_ref[...]).astype(BF16)
    mix = jnp.dot(ys, wout_ref[0:SSM_W, :], preferred_element_type=F32)
    mix = mix + jnp.dot(ya, wout_ref[SSM_W:, :], preferred_element_type=F32)
    xo_ref[...] = x_ref[...] + mix


def _mix_out(x, y_prompt, y_sample, u, o_prompt, o_sample, d, wglu, gs, ga, wout):
    return pl.pallas_call(
        _mix_out_kernel,
        grid=(N_TOK // TOKEN_TILE,),
        in_specs=[_row_spec(D_MODEL)] + _pair_specs(SSM_W) + [_row_spec(SSM_W)] + _pair_specs(ATTN_W)
                 + [_const_spec((1, SSM_W)), _const_spec((SSM_W, SSM_W)),
                    _const_spec((1, SSM_W)), _const_spec((1, ATTN_W)), _const_spec((D_MODEL, D_MODEL))],
        out_specs=_row_spec(D_MODEL),
        out_shape=jax.ShapeDtypeStruct((N_TOK, D_MODEL), F32),
        compiler_params=pltpu.CompilerParams(dimension_semantics=("parallel",), vmem_limit_bytes=VMEM_LIMIT),
        name="mix_out",
    )(x, y_prompt, y_sample, u, o_prompt, o_sample, d, wglu, gs, ga, wout)


def _ffn_ple_kernel(x_ref, pp_ref, ps_ref, g2_ref, wg_ref, wu_ref, wd_ref, gp_ref, wpg_ref, wpp_ref, gf_ref,
                    *out_refs, final):
    x2 = _half_swiglu(x_ref[...], g2_ref, wg_ref, wu_ref, wd_ref)
    h = _rms(x2, gp_ref[...]).astype(BF16)
    gate = jax.nn.sigmoid(jnp.dot(h, wpg_ref[...], preferred_element_type=F32))
    proj = jnp.dot(_pick(pp_ref, ps_ref).astype(BF16), wpp_ref[...], preferred_element_type=F32)
    x3 = x2 + proj * gate
    if not final:
        out_refs[0][...] = x3
        return
    y = _rms(x3, gf_ref[...])
    yp_ref, ys_ref = out_refs

    @pl.when(pl.program_id(0) < PROMPT_TILES)
    def _():
        yp_ref[...] = y

    @pl.when(pl.program_id(0) >= PROMPT_TILES)
    def _():
        ys_ref[...] = y


def _ffn_ple(x, p_prompt, p_sample, g2, wg, wu, wd, gp, wpg, wpp, gf, final):
    if final:
        out_specs = _pair_specs(D_MODEL)
        out_shape = [jax.ShapeDtypeStruct((N_PROMPT, D_MODEL), F32), jax.ShapeDtypeStruct((N_SAMPLE, D_MODEL), F32)]
    else:
        out_specs = _row_spec(D_MODEL)
        out_shape = jax.ShapeDtypeStruct((N_TOK, D_MODEL), F32)
    return pl.pallas_call(
        functools.partial(_ffn_ple_kernel, final=final),
        grid=(N_TOK // TOKEN_TILE,),
        in_specs=[_row_spec(D_MODEL)] + _pair_specs(PLE_DIM) + [
            _const_spec((1, D_MODEL)),
            _const_spec((D_MODEL, D_FF)), _const_spec((D_MODEL, D_FF)), _const_spec((D_FF, D_MODEL)),
            _const_spec((1, D_MODEL)), _const_spec((D_MODEL, D_MODEL)), _const_spec((PLE_DIM, D_MODEL)),
            _const_spec((1, D_MODEL))],
        out_specs=out_specs,
        out_shape=out_shape,
        compiler_params=pltpu.CompilerParams(dimension_semantics=("arbitrary",), vmem_limit_bytes=VMEM_LIMIT),
        name="ffn2_ple",
    )(x, p_prompt, p_sample, g2, wg, wu, wd, gp, wpg, wpp, gf)


def _sink_column(sinks, rows_per_head):
    return jnp.repeat(sinks.astype(F32).reshape(N_KV, GRP), rows_per_head, axis=1)[..., None]


def _head_major(a, axis):
    shape = a.shape
    a = a.reshape(shape[:axis] + (N_KV, GRP, HEAD_DIM) + shape[axis + 1:])
    return jnp.swapaxes(a, axis, axis + 1).reshape(shape)


def kernel(x_prompt, x_sample, cache_k, cache_v, state_ssm_re, state_ssm_im, p_prompt, p_sample, ffn1_norm, ffn1_w_gate, ffn1_w_up, ffn1_w_down, mix_norm, w_in, ssm_lam_re, ssm_lam_im, ssm_log_dt, ssm_b_re, ssm_b_im, ssm_c_re, ssm_c_im, ssm_d, ssm_w_glu, ssm_out_norm, attn_sinks, attn_out_norm, w_out, ffn2_norm, ffn2_w_gate, ffn2_w_up, ffn2_w_down, ple_norm, ple_w_gate, ple_w_proj, final_norm):
    wpair, bfull, bhalf, cfull, apw = _ssm_prep(ssm_lam_re, ssm_lam_im, ssm_log_dt, ssm_b_re, ssm_b_im,
                                                ssm_c_re, ssm_c_im)
    row = lambda v: v.reshape(1, -1)
    bf = lambda w: w.astype(BF16)
    xs = (x_prompt.reshape(N_PROMPT, D_MODEL), x_sample.reshape(N_SAMPLE, D_MODEL))
    n_state = SSM_G * SSM_N

    kp_l, vp_l, hpr_l, hpi_l, ks_l, vs_l, hsr_l, hsi_l = [], [], [], [], [], [], [], []
    for i in range(DEPTH):
        w_in_i = jnp.concatenate([w_in[i, :, :SSM_W], _head_major(w_in[i, :, SSM_W:SSM_W + ATTN_W], 1),
                                  w_in[i, :, SSM_W + ATTN_W:]], axis=1)
        x, u, q, kv = _ffn_proj(xs, row(ffn1_norm[i]), bf(ffn1_w_gate[i]), bf(ffn1_w_up[i]), bf(ffn1_w_down[i]),
                                row(mix_norm[i]), bf(w_in_i))

        y_p, hst = _ssm_prompt(u, wpair, bfull, cfull, apw, i)
        hst = hst.reshape(BATCH // SSM_HALF_B, 2, SSM_HALF_B, SSM_G, SSM_N)
        hpr_l.append(hst[:, 0].reshape(BATCH, SSM_G, SSM_N))
        hpi_l.append(hst[:, 1].reshape(BATCH, SSM_G, SSM_N))
        y_s, hsr, hsi = _ssm_sample(u, state_ssm_re[i].reshape(DEC_BATCH, n_state),
                                    state_ssm_im[i].reshape(DEC_BATCH, n_state), wpair, bhalf, cfull, apw, i)
        hsr_l.append(hsr.reshape(DEC_BATCH, SSM_G, SSM_N))
        hsi_l.append(hsi.reshape(DEC_BATCH, SSM_G, SSM_N))

        kv_last = kv[:N_PROMPT].reshape(BATCH, SEQ, 2, N_KV, HEAD_DIM)[:, -WINDOW:]
        kp_l.append(kv_last[:, :, 0])
        vp_l.append(kv_last[:, :, 1])
        o_p = _attn_prompt(q, kv, _sink_column(attn_sinks[i], WINDOW))
        o_s, nk, nv = _attn_sample(q, kv, cache_k[i].reshape(DEC_BATCH, WINDOW, KV_W),
                                   cache_v[i].reshape(DEC_BATCH, WINDOW, KV_W), _sink_column(attn_sinks[i], DEC_SEQ))
        ks_l.append(nk.reshape(DEC_BATCH, WINDOW, N_KV, HEAD_DIM))
        vs_l.append(nv.reshape(DEC_BATCH, WINDOW, N_KV, HEAD_DIM))

        x = _mix_out(x, y_p, y_s, u, o_p, o_s,
                     row(ssm_d[i]), bf(ssm_w_glu[i]), row(ssm_out_norm[i]), row(_head_major(attn_out_norm[i], 0)),
                     bf(jnp.concatenate([w_out[i, :SSM_W], _head_major(w_out[i, SSM_W:], 0)], axis=0)))
        x = _ffn_ple(x, p_prompt[i].reshape(N_PROMPT, PLE_DIM), p_sample[i].reshape(N_SAMPLE, PLE_DIM),
                     row(ffn2_norm[i]), bf(ffn2_w_gate[i]), bf(ffn2_w_up[i]), bf(ffn2_w_down[i]),
                     row(ple_norm[i]), bf(ple_w_gate[i]), bf(ple_w_proj[i]), row(final_norm),
                     final=(i == DEPTH - 1))
        xs = (x,)

    y_prompt, y_sample = x
    return (y_prompt.reshape(BATCH, SEQ, D_MODEL), y_sample.reshape(DEC_BATCH, DEC_SEQ, D_MODEL),
            jnp.stack(kp_l), jnp.stack(vp_l), jnp.stack(hpr_l), jnp.stack(hpi_l),
            jnp.stack(ks_l), jnp.stack(vs_l), jnp.stack(hsr_l), jnp.stack(hsi_l))
```

```python
import functools

import jax
import jax.numpy as jnp
from jax import lax
from jax.experimental import pallas as pl
from jax.experimental.pallas import tpu as pltpu

F32 = jnp.float32
BF16 = jnp.bfloat16

D_MODEL = 1024
BATCH = 8
SEQ = 2048
DEPTH = 2
DEC_BATCH = 128
DEC_SEQ = 8
SSM_W = 512
SSM_P = 16
SSM_G = 32
SSM_N = 64
ATTN_W = 512
HEAD_DIM = 64
N_HEADS = 8
N_KV = 2
GRP = N_HEADS // N_KV
KV_W = N_KV * HEAD_DIM
IN_W = SSM_W + ATTN_W + 2 * KV_W
WINDOW = 128
D_FF = 2816
PLE_DIM = 256
EPS = 1e-6
NEG_INF = -1e30

N_PROMPT = BATCH * SEQ
N_SAMPLE = DEC_BATCH * DEC_SEQ
N_TOK = N_PROMPT + N_SAMPLE

LANES = 128
SSM_T = 16
N_CHUNK = SEQ // SSM_T
SLAB_G = LANES // SSM_P
N_SLAB = SSM_G // SLAB_G
SLAB_N = SLAB_G * SSM_N
SSM_HALF_B = BATCH // 2

TOKEN_TILE = 512
PROMPT_TILES = N_PROMPT // TOKEN_TILE
FF_CHUNK = 256
VMEM_LIMIT = 56 * 1024 * 1024


def _const_spec(shape):
    nd = len(shape)
    return pl.BlockSpec(shape, lambda *_: (0,) * nd, pipeline_mode=pl.Buffered(1))


def _row_spec(width):
    return pl.BlockSpec((TOKEN_TILE, width), lambda i: (i, 0))


def _pair_specs(width):
    return [pl.BlockSpec((TOKEN_TILE, width), lambda i: (jnp.minimum(i, PROMPT_TILES - 1), 0)),
            pl.BlockSpec((TOKEN_TILE, width), lambda i: (jnp.maximum(i - PROMPT_TILES, 0), 0))]


def _pick(prompt_ref, sample_ref):
    return jnp.where(pl.program_id(0) < PROMPT_TILES, prompt_ref[...].astype(F32), sample_ref[...].astype(F32))


def _rms(x, g):
    return x * lax.rsqrt(jnp.mean(x * x, axis=-1, keepdims=True) + EPS) * g


def _half_swiglu(xf, g_ref, wg_ref, wu_ref, wd_ref):
    h = _rms(xf, g_ref[...]).astype(BF16)
    acc = jnp.zeros(xf.shape, F32)
    for c in range(0, D_FF, FF_CHUNK):
        gate = jnp.dot(h, wg_ref[:, c:c + FF_CHUNK], preferred_element_type=F32)
        up = jnp.dot(h, wu_ref[:, c:c + FF_CHUNK], preferred_element_type=F32)
        act = (gate * jax.nn.sigmoid(gate) * up).astype(BF16)
        acc = acc + jnp.dot(act, wd_ref[c:c + FF_CHUNK, :], preferred_element_type=F32)
    return xf + 0.5 * acc


def _ffn_proj_kernel(*refs, paired):
    if paired:
        xp_ref, xs_ref, *refs = refs
        x = _pick(xp_ref, xs_ref)
    else:
        x_ref, *refs = refs
        x = x_ref[...]
    g1_ref, wg_ref, wu_ref, wd_ref, gm_ref, win_ref, xo_ref, u_ref, q_ref, kv_ref = refs
    x1 = _half_swiglu(x, g1_ref, wg_ref, wu_ref, wd_ref)
    xo_ref[...] = x1
    h = _rms(x1, gm_ref[...]).astype(BF16)
    z = jnp.dot(h, win_ref[...], preferred_element_type=F32)
    u_ref[...] = z[:, :SSM_W]
    q_ref[...] = z[:, SSM_W:SSM_W + ATTN_W].astype(BF16)
    kv_ref[...] = z[:, SSM_W + ATTN_W:]


def _ffn_proj(xs, g1, wg, wu, wd, gm, win):
    paired = len(xs) == 2
    x_specs = _pair_specs(D_MODEL) if paired else [_row_spec(D_MODEL)]
    return pl.pallas_call(
        functools.partial(_ffn_proj_kernel, paired=paired),
        grid=(N_TOK // TOKEN_TILE,),
        in_specs=x_specs + [_const_spec((1, D_MODEL)),
                            _const_spec((D_MODEL, D_FF)), _const_spec((D_MODEL, D_FF)), _const_spec((D_FF, D_MODEL)),
                            _const_spec((1, D_MODEL)), _const_spec((D_MODEL, IN_W))],
        out_specs=[_row_spec(D_MODEL), _row_spec(SSM_W), _row_spec(ATTN_W), _row_spec(2 * KV_W)],
        out_shape=[jax.ShapeDtypeStruct((N_TOK, D_MODEL), F32), jax.ShapeDtypeStruct((N_TOK, SSM_W), F32),
                   jax.ShapeDtypeStruct((N_TOK, ATTN_W), BF16), jax.ShapeDtypeStruct((N_TOK, 2 * KV_W), F32)],
        compiler_params=pltpu.CompilerParams(dimension_semantics=("parallel",), vmem_limit_bytes=VMEM_LIMIT),
        name="ffn1_proj",
    )(*xs, g1, wg, wu, wd, gm, win)


def _discretise(lam_re, lam_im, log_dt):
    dt = jnp.exp(log_dt)
    mag = jnp.exp(lam_re * dt)
    ang = lam_im * dt
    a_re = mag * jnp.cos(ang)
    a_im = mag * jnp.sin(ang)
    den = lam_re * lam_re + lam_im * lam_im
    nr = a_re - 1.0
    k_re = (nr * lam_re + a_im * lam_im) / den
    k_im = (a_im * lam_re - nr * lam_im) / den
    return a_re, a_im, k_re, k_im


def _powers(a_re, a_im, n):
    p_re = [jnp.ones_like(a_re)]
    p_im = [jnp.zeros_like(a_re)]
    for _ in range(n):
        r, i = p_re[-1], p_im[-1]
        p_re.append(r * a_re - i * a_im)
        p_im.append(r * a_im + i * a_re)
    return p_re, p_im


def _ssm_disc_kernel(lre_ref, lim_ref, ldt_ref, are_ref, aim_ref, kre_ref, kim_ref):
    a_re, a_im, k_re, k_im = _discretise(lre_ref[...], lim_ref[...], ldt_ref[...])
    are_ref[...] = a_re
    aim_ref[...] = a_im
    kre_ref[...] = k_re
    kim_ref[...] = k_im


def _ssm_disc(lam_re, lam_im, log_dt):
    n = DEPTH * SSM_G
    ldt = jnp.broadcast_to(log_dt[..., None], lam_re.shape)
    outs = pl.pallas_call(
        _ssm_disc_kernel,
        out_shape=[jax.ShapeDtypeStruct((n, SSM_N), F32)] * 4,
        name="ssm_disc",
    )(lam_re.reshape(n, SSM_N), lam_im.reshape(n, SSM_N), ldt.reshape(n, SSM_N))
    return [o.reshape(DEPTH, SSM_G, SSM_N) for o in outs]


def _ssm_prep_kernel(ar_re, ar_im, kr_re, kr_im, btr_re, btr_im, cr_re, cr_im,
                     aw_re, aw_im, kw_re, kw_im, btw_re, btw_im,
                     at_re, at_im, ct_re, ct_im,
                     af_re, af_im,
                     wpair_ref, bfull_ref, bhalf_ref, cfull_ref, apw_ref):
    a_re, a_im, k_re, k_im = ar_re[...], ar_im[...], kr_re[...], kr_im[...]
    bb_re = k_re * btr_re[...] - k_im * btr_im[...]
    bb_im = k_re * btr_im[...] + k_im * btr_re[...]
    p_re, p_im = _powers(a_re, a_im, SSM_T - 1)
    c_re, c_im = cr_re[...], cr_im[...]
    ca_all = jnp.concatenate(
        [jnp.concatenate([c_re * p_re[t] - c_im * p_im[t], c_re * p_im[t] + c_im * p_re[t]], axis=1)
         for t in range(SSM_T)], axis=0)
    bb2 = jnp.concatenate([bb_re, -bb_im], axis=1)
    drow = lax.dot_general(bb2, ca_all, (((1,), (1,)), ((), ())),
                           precision=lax.Precision.HIGHEST, preferred_element_type=F32)
    row_g = lax.broadcasted_iota(jnp.int32, drow.shape, 0) // SSM_P
    col_g = (lax.broadcasted_iota(jnp.int32, drow.shape, 1) % LANES) // SSM_P
    drow = jnp.where(row_g == col_g, drow, 0.0).astype(BF16)
    wpair_ref[0:LANES, :] = drow
    wpair_ref[LANES:, 0:LANES] = jnp.zeros((LANES, LANES), BF16)
    wpair_ref[LANES:, LANES:] = drow[:, :(SSM_T - 1) * LANES]

    a_re, a_im, k_re, k_im = aw_re[...], aw_im[...], kw_re[...], kw_im[...]
    bb_re = k_re * btw_re[...] - k_im * btw_im[...]
    bb_im = k_re * btw_im[...] + k_im * btw_re[...]
    p_re, p_im = _powers(a_re, a_im, SSM_T - 1)
    diag = (lax.broadcasted_iota(jnp.int32, bb_re.shape, 0) // SSM_P
            == lax.broadcasted_iota(jnp.int32, bb_re.shape, 1) // SSM_N)
    for s in range(SSM_T):
        w_re, w_im = p_re[SSM_T - 1 - s], p_im[SSM_T - 1 - s]
        blk = jnp.concatenate([jnp.where(diag, bb_re * w_re - bb_im * w_im, 0.0),
                               jnp.where(diag, bb_re * w_im + bb_im * w_re, 0.0)], axis=1)
        bfull_ref[s * LANES:(s + 1) * LANES, :] = blk.astype(BF16)
        if s >= SSM_T - DEC_SEQ:
            bhalf_ref[(s - SSM_T + DEC_SEQ) * LANES:(s - SSM_T + DEC_SEQ + 1) * LANES, :] = blk

    p_re, p_im = _powers(at_re[...], at_im[...], SSM_T)
    c_re, c_im = ct_re[...], ct_im[...]
    diag = (lax.broadcasted_iota(jnp.int32, c_re.shape, 0) // SSM_N
            == lax.broadcasted_iota(jnp.int32, c_re.shape, 1) // SSM_P)
    for t in range(SSM_T):
        ca_re = c_re * p_re[t + 1] - c_im * p_im[t + 1]
        ca_im = c_re * p_im[t + 1] + c_im * p_re[t + 1]
        cfull_ref[0:SLAB_N, t * LANES:(t + 1) * LANES] = jnp.where(diag, ca_re, 0.0).astype(BF16)
        cfull_ref[SLAB_N:, t * LANES:(t + 1) * LANES] = jnp.where(diag, -ca_im, 0.0).astype(BF16)

    a_re, a_im = af_re[...], af_im[...]
    sq = []
    for _ in range(4):
        a_re, a_im = a_re * a_re - a_im * a_im, 2.0 * a_re * a_im
        sq.append((a_re, a_im))
    apw_ref[0:1, :] = sq[2][0]
    apw_ref[1:2, :] = sq[2][1]
    apw_ref[2:3, :] = sq[3][0]
    apw_ref[3:4, :] = sq[3][1]
    apw_ref[4:8, :] = jnp.zeros((4, SLAB_N), F32)


def _ssm_prep(lam_re, lam_im, log_dt, b_re, b_im, c_re, c_im):
    n = DEPTH * N_SLAB
    a_re, a_im, k_re, k_im = _ssm_disc(lam_re, lam_im, log_dt)
    rows = lambda v: jnp.repeat(v.reshape(n, SLAB_G, 1, SSM_N), SSM_P, axis=2).reshape(n, LANES, SSM_N)
    bt = lambda v: v.transpose(0, 1, 3, 2).reshape(n, LANES, SSM_N)
    wide = lambda v: jnp.tile(v, (1, 1, SLAB_G))
    tall = lambda v: jnp.broadcast_to(v.reshape(n, SLAB_N, 1), (n, SLAB_N, LANES))
    ct = lambda v: jnp.tile(v.transpose(0, 1, 3, 2).reshape(n, SLAB_N, SSM_P), (1, 1, SLAB_G))
    flat = lambda v: v.reshape(n, 1, SLAB_N)
    a, ak = (a_re, a_im), (a_re, a_im, k_re, k_im)
    args = ([rows(v) for v in ak] + [bt(b_re), bt(b_im), c_re.reshape(n, LANES, SSM_N), c_im.reshape(n, LANES, SSM_N)]
            + [wide(rows(v)) for v in ak] + [wide(bt(b_re)), wide(bt(b_im))]
            + [tall(v) for v in a] + [ct(c_re), ct(c_im)]
            + [flat(v) for v in a])

    def blk(a):
        return pl.BlockSpec((None,) + a.shape[1:], lambda i: (i, 0, 0))

    out_shape = [jax.ShapeDtypeStruct((n, 2 * LANES, SSM_T * LANES), BF16),
                 jax.ShapeDtypeStruct((n, SSM_T * LANES, 2 * SLAB_N), BF16),
                 jax.ShapeDtypeStruct((n, DEC_SEQ * LANES, 2 * SLAB_N), F32),
                 jax.ShapeDtypeStruct((n, 2 * SLAB_N, SSM_T * LANES), BF16),
                 jax.ShapeDtypeStruct((n, 8, SLAB_N), F32)]
    return pl.pallas_call(
        _ssm_prep_kernel,
        grid=(n,),
        in_specs=[blk(a) for a in args],
        out_specs=[blk(s) for s in out_shape],
        out_shape=out_shape,
        compiler_params=pltpu.CompilerParams(dimension_semantics=("parallel",), vmem_limit_bytes=VMEM_LIMIT),
        name="ssm_prep",
    )(*args)


def _ssm_prompt_kernel(u_ref, wpair_ref, bfull_ref, cfull_ref, apw_ref, y_ref, hst_ref,
                       ub_ref, s_ref, yacc_ref, *col_refs):
    rows = SSM_HALF_B * N_CHUNK
    n_col = SLAB_N // LANES
    sp_refs, hp_refs = col_refs[:n_col], col_refs[n_col:]
    for t in range(SSM_T):
        ub_ref[:, t * LANES:(t + 1) * LANES] = u_ref[pl.ds(t, rows, stride=SSM_T), :].astype(BF16)
    s_ref[...] = jnp.dot(ub_ref[...], bfull_ref[...], preferred_element_type=F32)
    for b in range(SSM_HALF_B):
        seq = slice(b * N_CHUNK, (b + 1) * N_CHUNK)
        for k in range(n_col):
            sp_refs[k][pl.ds(b, N_CHUNK, stride=8), :] = s_ref[seq, k * LANES:(k + 1) * LANES]
            sp_refs[k][pl.ds(SSM_HALF_B + b, N_CHUNK, stride=8), :] = (
                s_ref[seq, SLAB_N + k * LANES:SLAB_N + (k + 1) * LANES])
    upper = lax.broadcasted_iota(jnp.int32, (8, LANES), 0) < SSM_HALF_B
    a1 = [jnp.broadcast_to(apw_ref[2:3, k * LANES:(k + 1) * LANES], (8, LANES)) for k in range(n_col)]
    a2 = [jnp.where(upper, -1.0, 1.0) * jnp.broadcast_to(apw_ref[3:4, k * LANES:(k + 1) * LANES], (8, LANES))
          for k in range(n_col)]

    def step(c, h):
        r0 = pl.multiple_of(c * 8, 8)
        new = []
        for k in range(n_col):
            hp_refs[k][pl.ds(r0, 8), :] = h[k]
            new.append(a1[k] * h[k] + a2[k] * pltpu.roll(h[k], SSM_HALF_B, axis=0) + sp_refs[k][pl.ds(r0, 8), :])
        return tuple(new)

    h = lax.fori_loop(0, N_CHUNK, step, tuple(jnp.zeros((8, LANES), F32) for _ in range(n_col)), unroll=8)
    hst_ref[...] = jnp.concatenate(h, axis=1)
    hbc = jnp.concatenate(
        [jnp.concatenate([hp_refs[k][pl.ds(part + b, N_CHUNK, stride=8), :]
                          for part in (0, SSM_HALF_B) for k in range(n_col)], axis=1)
         for b in range(SSM_HALF_B)], axis=0)
    yacc_ref[...] = jnp.dot(hbc.astype(BF16), cfull_ref[...], preferred_element_type=F32)
    for t in range(0, SSM_T, 2):
        yacc_ref[:, t * LANES:] += jnp.dot(ub_ref[:, t * LANES:(t + 2) * LANES],
                                           wpair_ref[:, 0:(SSM_T - t) * LANES], preferred_element_type=F32)
    for t in range(SSM_T):
        y_ref[pl.ds(t, rows, stride=SSM_T), :] = yacc_ref[:, t * LANES:(t + 1) * LANES]


def _ssm_prompt(u, wpair, bfull, cfull, apw, layer):
    rows = SSM_HALF_B * N_CHUNK
    tok = SSM_HALF_B * SEQ

    def wspec(a):
        return pl.BlockSpec((None,) + a.shape[1:], lambda j, h: (layer * N_SLAB + j, 0, 0))

    return pl.pallas_call(
        _ssm_prompt_kernel,
        grid=(N_SLAB, BATCH // SSM_HALF_B),
        in_specs=[pl.BlockSpec((tok, LANES), lambda j, h: (h, j)), wspec(wpair), wspec(bfull), wspec(cfull), wspec(apw)],
        out_specs=[pl.BlockSpec((tok, LANES), lambda j, h: (h, j)),
                   pl.BlockSpec((None, 8, SLAB_N), lambda j, h: (h, 0, j))],
        out_shape=[jax.ShapeDtypeStruct((N_PROMPT, SSM_W), F32),
                   jax.ShapeDtypeStruct((BATCH // SSM_HALF_B, 8, SSM_G * SSM_N), F32)],
        scratch_shapes=[pltpu.VMEM((rows, SSM_T * LANES), BF16), pltpu.VMEM((rows, 2 * SLAB_N), F32),
                        pltpu.VMEM((rows, SSM_T * LANES), F32)]
                       + [pltpu.VMEM((N_CHUNK * 8, LANES), F32)] * (2 * SLAB_N // LANES),
        compiler_params=pltpu.CompilerParams(dimension_semantics=("arbitrary", "arbitrary"),
                                             vmem_limit_bytes=VMEM_LIMIT),
        name="ssm_prompt",
    )(u, wpair, bfull, cfull, apw)


def _ssm_sample_kernel(u_ref, h0r_ref, h0i_ref, wpair_ref, bhalf_ref, cfull_ref, apw_ref,
                       y_ref, hr_ref, hi_ref, uf_ref, yacc_ref):
    for t in range(DEC_SEQ):
        uf_ref[:, t * LANES:(t + 1) * LANES] = u_ref[pl.ds(t, DEC_BATCH, stride=DEC_SEQ), :]
    uf = uf_ref[...]
    h0r = h0r_ref[...]
    h0i = h0i_ref[...]
    ar = apw_ref[0:1, :]
    ai = apw_ref[1:2, :]
    s = jnp.dot(uf, bhalf_ref[...], precision=lax.Precision.HIGHEST, preferred_element_type=F32)
    hr_ref[...] = ar * h0r - ai * h0i + s[:, :SLAB_N]
    hi_ref[...] = ar * h0i + ai * h0r + s[:, SLAB_N:]
    ub = uf.astype(BF16)
    h0 = jnp.concatenate([h0r, h0i], axis=1).astype(BF16)
    yacc_ref[...] = jnp.dot(h0, cfull_ref[...], preferred_element_type=F32)
    for t in range(0, DEC_SEQ, 2):
        yacc_ref[:, t * LANES:] += jnp.dot(ub[:, t * LANES:(t + 2) * LANES],
                                           wpair_ref[:, 0:(DEC_SEQ - t) * LANES], preferred_element_type=F32)
    for t in range(DEC_SEQ):
        y_ref[pl.ds(t, DEC_BATCH, stride=DEC_SEQ), :] = yacc_ref[:, t * LANES:(t + 1) * LANES]


def _ssm_sample(u, h0re, h0im, wpair, bhalf, cfull, apw, layer):
    width = DEC_SEQ * LANES

    def wspec(r, c):
        return pl.BlockSpec((None, r, c), lambda j: (layer * N_SLAB + j, 0, 0))

    st = pl.BlockSpec((DEC_BATCH, SLAB_N), lambda j: (0, j))
    return pl.pallas_call(
        _ssm_sample_kernel,
        grid=(N_SLAB,),
        in_specs=[pl.BlockSpec((N_SAMPLE, LANES), lambda j: (N_PROMPT // N_SAMPLE, j)), st, st,
                  wspec(2 * LANES, width), wspec(width, 2 * SLAB_N), wspec(2 * SLAB_N, width), wspec(8, SLAB_N)],
        out_specs=[pl.BlockSpec((N_SAMPLE, LANES), lambda j: (0, j)), st, st],
        out_shape=[jax.ShapeDtypeStruct((N_SAMPLE, SSM_W), F32),
                   jax.ShapeDtypeStruct((DEC_BATCH, SSM_G * SSM_N), F32),
                   jax.ShapeDtypeStruct((DEC_BATCH, SSM_G * SSM_N), F32)],
        scratch_shapes=[pltpu.VMEM((DEC_BATCH, width), F32), pltpu.VMEM((DEC_BATCH, width), F32)],
        compiler_params=pltpu.CompilerParams(dimension_semantics=("arbitrary",), vmem_limit_bytes=VMEM_LIMIT),
        name="ssm_sample",
    )(u, h0re, h0im, wpair, bhalf, cfull, apw)


def _mask_cap(valid):
    return jnp.where(valid, jnp.inf, NEG_INF).astype(F32)


def _sink_softmax(s, cap, sink):
    s = jnp.minimum(s, cap)
    m = jnp.maximum(jnp.max(s, axis=-1, keepdims=True), sink)
    e = jnp.exp(s - m)
    denom = jnp.sum(e, axis=-1, keepdims=True) + jnp.exp(sink - m)
    return e.astype(BF16), denom


def _scores(q, scale_mask, kk):
    qx = jnp.concatenate([q[:, c * KV_W:(c + 1) * KV_W] * scale_mask for c in range(GRP)], axis=0)
    return lax.dot_general(qx.astype(BF16), kk, (((1,), (1,)), ((), ())), preferred_element_type=F32)


def _band_valid(n_q, n_k, rows):
    qi = lax.broadcasted_iota(jnp.int32, (rows, n_k), 0) & (n_q - 1)
    kj = lax.broadcasted_iota(jnp.int32, (rows, n_k), 1)
    return kj, ((kj - qi - 1).astype(jnp.uint32) < WINDOW)


def _head_masks():
    lane_kv = lax.broadcasted_iota(jnp.int32, (1, KV_W), 1) // HEAD_DIM
    return [jnp.where(lane_kv == h, HEAD_DIM ** -0.5, 0.0) for h in range(N_KV)], lane_kv == 0


ATTN_Q_TILE = 512


def _attn_prompt_kernel(q_ref, kvp_ref, kvc_ref, sink_ref, o_ref):
    rows = GRP * WINDOW
    scale_masks, first_kv = _head_masks()
    kj, valid = _band_valid(WINDOW, 2 * WINDOW, rows)
    cap_first = _mask_cap(valid & ((kj >= WINDOW) | (pl.program_id(1) > 0)))
    cap = _mask_cap(valid)
    n_blk = ATTN_Q_TILE // WINDOW

    def stage(r):
        cur = kvc_ref[r * WINDOW:(r + 1) * WINDOW, :]
        prev = kvp_ref[...] if r == 0 else kvc_ref[(r - 1) * WINDOW:r * WINDOW, :]
        kk = jnp.concatenate([prev[:, :KV_W], cur[:, :KV_W]], axis=0).astype(BF16)
        vv = jnp.concatenate([prev[:, KV_W:], cur[:, KV_W:]], axis=0).astype(BF16)
        q = q_ref[r * WINDOW:(r + 1) * WINDOW, :].astype(F32)
        return [_scores(q, scale_masks[h], kk) for h in range(N_KV)], vv

    staged = [stage(r) for r in range(n_blk)]
    weights = [[_sink_softmax(staged[r][0][h], cap_first if r == 0 else cap, sink_ref[h]) for h in range(N_KV)]
               for r in range(n_blk)]
    for r in range(n_blk):
        vv = staged[r][1]
        outs = [jnp.dot(e, vv, preferred_element_type=F32) / denom for e, denom in weights[r]]
        for c in range(GRP):
            o = jnp.where(first_kv, outs[0][c * WINDOW:(c + 1) * WINDOW], outs[1][c * WINDOW:(c + 1) * WINDOW])
            o_ref[r * WINDOW:(r + 1) * WINDOW, c * KV_W:(c + 1) * KV_W] = o.astype(o_ref.dtype)


def _attn_prompt(q, kv, sink_col):
    tiles = SEQ // ATTN_Q_TILE
    per_tile = ATTN_Q_TILE // WINDOW
    blocks = SEQ // WINDOW
    return pl.pallas_call(
        _attn_prompt_kernel,
        grid=(BATCH, tiles),
        in_specs=[pl.BlockSpec((ATTN_Q_TILE, ATTN_W), lambda b, j: (b * tiles + j, 0)),
                  pl.BlockSpec((WINDOW, 2 * KV_W), lambda b, j: (b * blocks + jnp.maximum(j * per_tile - 1, 0), 0)),
                  pl.BlockSpec((ATTN_Q_TILE, 2 * KV_W), lambda b, j: (b * tiles + j, 0)),
                  pl.BlockSpec((N_KV, GRP * WINDOW, 1), lambda b, j: (0, 0, 0))],
        out_specs=pl.BlockSpec((ATTN_Q_TILE, ATTN_W), lambda b, j: (b * tiles + j, 0)),
        out_shape=jax.ShapeDtypeStruct((N_PROMPT, ATTN_W), BF16),
        compiler_params=pltpu.CompilerParams(dimension_semantics=("parallel", "arbitrary")),
        name="attn_prompt",
    )(q, kv, kv, sink_col)


SAMPLE_SEQ_BLOCK = 8


def _attn_sample_kernel(q_ref, ck_ref, cv_ref, kvn_ref, sink_ref, o_ref, nk_ref, nv_ref):
    rows = GRP * DEC_SEQ
    scale_masks, first_kv = _head_masks()
    cap = _mask_cap(_band_valid(DEC_SEQ, WINDOW + DEC_SEQ, rows)[1])
    q_all = q_ref[...].astype(F32)
    toks = [slice(s * DEC_SEQ, (s + 1) * DEC_SEQ) for s in range(SAMPLE_SEQ_BLOCK)]
    scores, values = [], []
    for s, tok in enumerate(toks):
        ck = ck_ref[s]
        cv = cv_ref[s]
        kn = kvn_ref[tok, :KV_W]
        vn = kvn_ref[tok, KV_W:]
        nk_ref[s, 0:WINDOW - DEC_SEQ, :] = ck[DEC_SEQ:, :]
        nk_ref[s, WINDOW - DEC_SEQ:WINDOW, :] = kn
        nv_ref[s, 0:WINDOW - DEC_SEQ, :] = cv[DEC_SEQ:, :]
        nv_ref[s, WINDOW - DEC_SEQ:WINDOW, :] = vn
        kk = jnp.concatenate([ck, kn], axis=0).astype(BF16)
        values.append(jnp.concatenate([cv, vn], axis=0).astype(BF16))
        scores.append([_scores(q_all[tok, :], scale_masks[h], kk) for h in range(N_KV)])
    weights = [[_sink_softmax(sc[h], cap, sink_ref[h]) for h in range(N_KV)] for sc in scores]
    for s, tok in enumerate(toks):
        outs = [jnp.dot(e, values[s], preferred_element_type=F32) / denom for e, denom in weights[s]]
        for c in range(GRP):
            o_ref[tok, c * KV_W:(c + 1) * KV_W] = jnp.where(
                first_kv, outs[0][c * DEC_SEQ:(c + 1) * DEC_SEQ], outs[1][c * DEC_SEQ:(c + 1) * DEC_SEQ])


def _attn_sample(q, kv, ck, cv, sink_col):
    sb = SAMPLE_SEQ_BLOCK
    tok = sb * DEC_SEQ
    first = N_PROMPT // tok
    c_spec = pl.BlockSpec((sb, WINDOW, KV_W), lambda i: (i, 0, 0))
    return pl.pallas_call(
        _attn_sample_kernel,
        grid=(DEC_BATCH // sb,),
        in_specs=[pl.BlockSpec((tok, ATTN_W), lambda i: (first + i, 0)), c_spec, c_spec,
                  pl.BlockSpec((tok, 2 * KV_W), lambda i: (first + i, 0)),
                  pl.BlockSpec((N_KV, GRP * DEC_SEQ, 1), lambda i: (0, 0, 0))],
        out_specs=[pl.BlockSpec((tok, ATTN_W), lambda i: (i, 0)), c_spec, c_spec],
        out_shape=[jax.ShapeDtypeStruct((N_SAMPLE, ATTN_W), F32),
                   jax.ShapeDtypeStruct((DEC_BATCH, WINDOW, KV_W), F32),
                   jax.ShapeDtypeStruct((DEC_BATCH, WINDOW, KV_W), F32)],
        compiler_params=pltpu.CompilerParams(dimension_semantics=("parallel",)),
        name="attn_sample",
    )(q, ck, cv, kv, sink_col)


def _mix_out(x, y, u, o, d_ref, wglu_ref, gs_ref, ga_ref, wout_ref):
    y = y + d_ref[...] * u
    y = jax.nn.gelu(y)
    y = y * jax.nn.sigmoid(jnp.dot(y.astype(BF16), wglu_ref[...], preferred_element_type=F32))
    ys = _rms(y, gs_ref[...]).astype(BF16)
    ya = _rms(o, ga_ref[...]).astype(BF16)
    mix = jnp.dot(ys, wout_ref[0:SSM_W, :], preferred_element_type=F32)
    mix = mix + jnp.dot(ya, wout_ref[SSM_W:, :], preferred_element_type=F32)
    return x + mix


def _tail_kernel(x_ref, yp_ref, ys_ref, u_ref, op_ref, os_ref, pp_ref, ps_ref,
                 d_ref, wglu_ref, gs_ref, ga_ref, wout_ref,
                 g2_ref, wg_ref, wu_ref, wd_ref, gp_ref, wpg_ref, wpp_ref, gf_ref,
                 *out_refs, final):
    x1 = _mix_out(x_ref[...], _pick(yp_ref, ys_ref), u_ref[...], _pick(op_ref, os_ref),
                  d_ref, wglu_ref, gs_ref, ga_ref, wout_ref)
    x2 = _half_swiglu(x1, g2_ref, wg_ref, wu_ref, wd_ref)
    h = _rms(x2, gp_ref[...]).astype(BF16)
    gate = jax.nn.sigmoid(jnp.dot(h, wpg_ref[...], preferred_element_type=F32))
    proj = jnp.dot(_pick(pp_ref, ps_ref).astype(BF16), wpp_ref[...], preferred_element_type=F32)
    x3 = x2 + proj * gate
    if not final:
        out_refs[0][...] = x3
        return
    y = _rms(x3, gf_ref[...])
    yp_ref, ys_ref = out_refs

    @pl.when(pl.program_id(0) < PROMPT_TILES)
    def _():
        yp_ref[...] = y

    @pl.when(pl.program_id(0) >= PROMPT_TILES)
    def _():
        ys_ref[...] = y


def _layer_tail(x, y_prompt, y_sample, u, o_prompt, o_sample, p_prompt, p_sample, layer,
                d, wglu, gs, ga, wout, g2, wg, wu, wd, gp, wpg, wpp, gf, final):
    if final:
        out_specs = _pair_specs(D_MODEL)
        out_shape = [jax.ShapeDtypeStruct((N_PROMPT, D_MODEL), F32), jax.ShapeDtypeStruct((N_SAMPLE, D_MODEL), F32)]
    else:
        out_specs = _row_spec(D_MODEL)
        out_shape = jax.ShapeDtypeStruct((N_TOK, D_MODEL), F32)
    p_specs = [pl.BlockSpec((None, TOKEN_TILE, PLE_DIM), lambda i: (layer, jnp.minimum(i, PROMPT_TILES - 1), 0)),
               pl.BlockSpec((None, TOKEN_TILE, PLE_DIM), lambda i: (layer, jnp.maximum(i - PROMPT_TILES, 0), 0))]
    return pl.pallas_call(
        functools.partial(_tail_kernel, final=final),
        grid=(N_TOK // TOKEN_TILE,),
        in_specs=[_row_spec(D_MODEL)] + _pair_specs(SSM_W) + [_row_spec(SSM_W)] + _pair_specs(ATTN_W) + p_specs + [
            _const_spec((1, SSM_W)), _const_spec((SSM_W, SSM_W)), _const_spec((1, SSM_W)), _const_spec((1, ATTN_W)),
            _const_spec((D_MODEL, D_MODEL)),
            _const_spec((1, D_MODEL)),
            _const_spec((D_MODEL, D_FF)), _const_spec((D_MODEL, D_FF)), _const_spec((D_FF, D_MODEL)),
            _const_spec((1, D_MODEL)), _const_spec((D_MODEL, D_MODEL)), _const_spec((PLE_DIM, D_MODEL)),
            _const_spec((1, D_MODEL))],
        out_specs=out_specs,
        out_shape=out_shape,
        compiler_params=pltpu.CompilerParams(dimension_semantics=("arbitrary",), vmem_limit_bytes=VMEM_LIMIT),
        name="layer_tail",
    )(x, y_prompt, y_sample, u, o_prompt, o_sample, p_prompt, p_sample,
      d, wglu, gs, ga, wout, g2, wg, wu, wd, gp, wpg, wpp, gf)


def _sink_column(sinks, rows_per_head):
    return jnp.repeat(sinks.astype(F32).reshape(N_KV, GRP), rows_per_head, axis=1)[..., None]


def _head_major(a, axis):
    shape = a.shape
    a = a.reshape(shape[:axis] + (N_KV, GRP, HEAD_DIM) + shape[axis + 1:])
    return jnp.swapaxes(a, axis, axis + 1).reshape(shape)


def kernel(x_prompt, x_sample, cache_k, cache_v, state_ssm_re, state_ssm_im, p_prompt, p_sample, ffn1_norm, ffn1_w_gate, ffn1_w_up, ffn1_w_down, mix_norm, w_in, ssm_lam_re, ssm_lam_im, ssm_log_dt, ssm_b_re, ssm_b_im, ssm_c_re, ssm_c_im, ssm_d, ssm_w_glu, ssm_out_norm, attn_sinks, attn_out_norm, w_out, ffn2_norm, ffn2_w_gate, ffn2_w_up, ffn2_w_down, ple_norm, ple_w_gate, ple_w_proj, final_norm):
    wpair, bfull, bhalf, cfull, apw = _ssm_prep(ssm_lam_re, ssm_lam_im, ssm_log_dt, ssm_b_re, ssm_b_im,
                                                ssm_c_re, ssm_c_im)
    row = lambda v: v.reshape(1, -1)
    bf = lambda w: w.astype(BF16)
    xs = (x_prompt.reshape(N_PROMPT, D_MODEL), x_sample.reshape(N_SAMPLE, D_MODEL))
    n_state = SSM_G * SSM_N

    kp_l, vp_l, hpr_l, hpi_l, ks_l, vs_l, hsr_l, hsi_l = [], [], [], [], [], [], [], []
    for i in range(DEPTH):
        w_in_i = jnp.concatenate([w_in[i, :, :SSM_W], _head_major(w_in[i, :, SSM_W:SSM_W + ATTN_W], 1),
                                  w_in[i, :, SSM_W + ATTN_W:]], axis=1)
        x, u, q, kv = _ffn_proj(xs, row(ffn1_norm[i]), bf(ffn1_w_gate[i]), bf(ffn1_w_up[i]), bf(ffn1_w_down[i]),
                                row(mix_norm[i]), bf(w_in_i))

        y_p, hst = _ssm_prompt(u, wpair, bfull, cfull, apw, i)
        hst = hst.reshape(BATCH // SSM_HALF_B, 2, SSM_HALF_B, SSM_G, SSM_N)
        hpr_l.append(hst[:, 0].reshape(BATCH, SSM_G, SSM_N))
        hpi_l.append(hst[:, 1].reshape(BATCH, SSM_G, SSM_N))
        y_s, hsr, hsi = _ssm_sample(u, state_ssm_re[i].reshape(DEC_BATCH, n_state),
                                    state_ssm_im[i].reshape(DEC_BATCH, n_state), wpair, bhalf, cfull, apw, i)
        hsr_l.append(hsr.reshape(DEC_BATCH, SSM_G, SSM_N))
        hsi_l.append(hsi.reshape(DEC_BATCH, SSM_G, SSM_N))

        kv_last = kv[:N_PROMPT].reshape(BATCH, SEQ, 2, N_KV, HEAD_DIM)[:, -WINDOW:]
        kp_l.append(kv_last[:, :, 0])
        vp_l.append(kv_last[:, :, 1])
        o_p = _attn_prompt(q, kv, _sink_column(attn_sinks[i], WINDOW))
        o_s, nk, nv = _attn_sample(q, kv, cache_k[i].reshape(DEC_BATCH, WINDOW, KV_W),
                                   cache_v[i].reshape(DEC_BATCH, WINDOW, KV_W), _sink_column(attn_sinks[i], DEC_SEQ))
        ks_l.append(nk.reshape(DEC_BATCH, WINDOW, N_KV, HEAD_DIM))
        vs_l.append(nv.reshape(DEC_BATCH, WINDOW, N_KV, HEAD_DIM))

        x = _layer_tail(x, y_p, y_s, u, o_p, o_s,
                        p_prompt.reshape(DEPTH, N_PROMPT, PLE_DIM), p_sample.reshape(DEPTH, N_SAMPLE, PLE_DIM), i,
                        row(ssm_d[i]), bf(ssm_w_glu[i]), row(ssm_out_norm[i]), row(_head_major(attn_out_norm[i], 0)),
                        bf(jnp.concatenate([w_out[i, :SSM_W], _head_major(w_out[i, SSM_W:], 0)], axis=0)),
                        row(ffn2_norm[i]), bf(ffn2_w_gate[i]), bf(ffn2_w_up[i]), bf(ffn2_w_down[i]),
                        row(ple_norm[i]), bf(ple_w_gate[i]), bf(ple_w_proj[i]), row(final_norm),
                        final=(i == DEPTH - 1))
        xs = (x,)

    y_prompt, y_sample = x
    return (y_prompt.reshape(BATCH, SEQ, D_MODEL), y_sample.reshape(DEC_BATCH, DEC_SEQ, D_MODEL),
            jnp.stack(kp_l), jnp.stack(vp_l), jnp.stack(hpr_l), jnp.stack(hpi_l),
            jnp.stack(ks_l), jnp.stack(vs_l), jnp.stack(hsr_l), jnp.stack(hsi_l))
```

```python
import functools

import jax
import jax.numpy as jnp
from jax import lax
from jax.experimental import pallas as pl
from jax.experimental.pallas import tpu as pltpu

F32 = jnp.float32
BF16 = jnp.bfloat16

D_MODEL = 1024
BATCH = 8
SEQ = 2048
DEPTH = 2
DEC_BATCH = 128
DEC_SEQ = 8
SSM_W = 512
SSM_P = 16
SSM_G = 32
SSM_N = 64
ATTN_W = 512
HEAD_DIM = 64
N_HEADS = 8
N_KV = 2
GRP = N_HEADS // N_KV
KV_W = N_KV * HEAD_DIM
IN_W = SSM_W + ATTN_W + 2 * KV_W
WINDOW = 128
D_FF = 2816
PLE_DIM = 256
EPS = 1e-6
NEG_INF = -1e30

N_PROMPT = BATCH * SEQ
N_SAMPLE = DEC_BATCH * DEC_SEQ
N_TOK = N_PROMPT + N_SAMPLE

LANES = 128
SSM_T = 16
N_CHUNK = SEQ // SSM_T
SLAB_G = LANES // SSM_P
N_SLAB = SSM_G // SLAB_G
SLAB_N = SLAB_G * SSM_N
SSM_HALF_B = BATCH // 2

TOKEN_TILE = 512
PROMPT_TILES = N_PROMPT // TOKEN_TILE
FF_CHUNK = 256
VMEM_LIMIT = 56 * 1024 * 1024


def _const_spec(shape):
    nd = len(shape)
    return pl.BlockSpec(shape, lambda *_: (0,) * nd, pipeline_mode=pl.Buffered(1))


def _row_spec(width):
    return pl.BlockSpec((TOKEN_TILE, width), lambda i: (i, 0))


def _pair_specs(width):
    return [pl.BlockSpec((TOKEN_TILE, width), lambda i: (jnp.minimum(i, PROMPT_TILES - 1), 0)),
            pl.BlockSpec((TOKEN_TILE, width), lambda i: (jnp.maximum(i - PROMPT_TILES, 0), 0))]


def _pick(prompt_ref, sample_ref):
    return jnp.where(pl.program_id(0) < PROMPT_TILES, prompt_ref[...].astype(F32), sample_ref[...].astype(F32))


def _rms(x, g):
    return x * lax.rsqrt(jnp.mean(x * x, axis=-1, keepdims=True) + EPS) * g


def _half_swiglu(xf, g_ref, wg_ref, wu_ref, wd_ref):
    h = _rms(xf, g_ref[...]).astype(BF16)
    acc = jnp.zeros(xf.shape, F32)
    for c in range(0, D_FF, FF_CHUNK):
        gate = jnp.dot(h, wg_ref[:, c:c + FF_CHUNK], preferred_element_type=F32)
        up = jnp.dot(h, wu_ref[:, c:c + FF_CHUNK], preferred_element_type=F32)
        act = (gate * jax.nn.sigmoid(gate) * up).astype(BF16)
        acc = acc + jnp.dot(act, wd_ref[c:c + FF_CHUNK, :], preferred_element_type=F32)
    return xf + 0.5 * acc


def _ffn_proj_kernel(*refs, paired):
    if paired:
        xp_ref, xs_ref, *refs = refs
        x = _pick(xp_ref, xs_ref)
    else:
        x_ref, *refs = refs
        x = x_ref[...]
    g1_ref, wg_ref, wu_ref, wd_ref, gm_ref, win_ref, xo_ref, u_ref, q_ref, kv_ref = refs
    x1 = _half_swiglu(x, g1_ref, wg_ref, wu_ref, wd_ref)
    xo_ref[...] = x1
    h = _rms(x1, gm_ref[...]).astype(BF16)
    z = jnp.dot(h, win_ref[...], preferred_element_type=F32)
    u_ref[...] = z[:, :SSM_W]
    q_ref[...] = z[:, SSM_W:SSM_W + ATTN_W].astype(BF16)
    kv_ref[...] = z[:, SSM_W + ATTN_W:]


def _ffn_proj(xs, g1, wg, wu, wd, gm, win):
    paired = len(xs) == 2
    x_specs = _pair_specs(D_MODEL) if paired else [_row_spec(D_MODEL)]
    return pl.pallas_call(
        functools.partial(_ffn_proj_kernel, paired=paired),
        grid=(N_TOK // TOKEN_TILE,),
        in_specs=x_specs + [_const_spec((1, D_MODEL)),
                            _const_spec((D_MODEL, D_FF)), _const_spec((D_MODEL, D_FF)), _const_spec((D_FF, D_MODEL)),
                            _const_spec((1, D_MODEL)), _const_spec((D_MODEL, IN_W))],
        out_specs=[_row_spec(D_MODEL), _row_spec(SSM_W), _row_spec(ATTN_W), _row_spec(2 * KV_W)],
        out_shape=[jax.ShapeDtypeStruct((N_TOK, D_MODEL), F32), jax.ShapeDtypeStruct((N_TOK, SSM_W), F32),
                   jax.ShapeDtypeStruct((N_TOK, ATTN_W), BF16), jax.ShapeDtypeStruct((N_TOK, 2 * KV_W), F32)],
        compiler_params=pltpu.CompilerParams(dimension_semantics=("parallel",), vmem_limit_bytes=VMEM_LIMIT),
        name="ffn1_proj",
    )(*xs, g1, wg, wu, wd, gm, win)


def _discretise(lam_re, lam_im, log_dt):
    dt = jnp.exp(log_dt)
    mag = jnp.exp(lam_re * dt)
    ang = lam_im * dt
    a_re = mag * jnp.cos(ang)
    a_im = mag * jnp.sin(ang)
    den = lam_re * lam_re + lam_im * lam_im
    nr = a_re - 1.0
    k_re = (nr * lam_re + a_im * lam_im) / den
    k_im = (a_im * lam_re - nr * lam_im) / den
    return a_re, a_im, k_re, k_im


def _powers(a_re, a_im, n):
    p_re = [jnp.ones_like(a_re)]
    p_im = [jnp.zeros_like(a_re)]
    for _ in range(n):
        r, i = p_re[-1], p_im[-1]
        p_re.append(r * a_re - i * a_im)
        p_im.append(r * a_im + i * a_re)
    return p_re, p_im


def _ssm_disc_kernel(lre_ref, lim_ref, ldt_ref, are_ref, aim_ref, kre_ref, kim_ref):
    a_re, a_im, k_re, k_im = _discretise(lre_ref[...], lim_ref[...], ldt_ref[...])
    are_ref[...] = a_re
    aim_ref[...] = a_im
    kre_ref[...] = k_re
    kim_ref[...] = k_im


def _ssm_disc(lam_re, lam_im, log_dt):
    n = DEPTH * SSM_G
    ldt = jnp.broadcast_to(log_dt[..., None], lam_re.shape)
    outs = pl.pallas_call(
        _ssm_disc_kernel,
        out_shape=[jax.ShapeDtypeStruct((n, SSM_N), F32)] * 4,
        name="ssm_disc",
    )(lam_re.reshape(n, SSM_N), lam_im.reshape(n, SSM_N), ldt.reshape(n, SSM_N))
    return [o.reshape(DEPTH, SSM_G, SSM_N) for o in outs]


def _ssm_prep_kernel(ar_re, ar_im, kr_re, kr_im, btr_re, btr_im, cr_re, cr_im, af_re, af_im,
                     wpair_ref, bfull_ref, bhalf_ref, cfull_ref, apw_ref):
    a_re, a_im, k_re, k_im = ar_re[...], ar_im[...], kr_re[...], kr_im[...]
    bb_re = k_re * btr_re[...] - k_im * btr_im[...]
    bb_im = k_re * btr_im[...] + k_im * btr_re[...]
    p_re, p_im = _powers(a_re, a_im, SSM_T)
    c_re, c_im = cr_re[...], cr_im[...]
    ca = [jnp.concatenate([c_re * p_re[t] - c_im * p_im[t], c_re * p_im[t] + c_im * p_re[t]], axis=1)
          for t in range(SSM_T + 1)]
    ca_all = jnp.concatenate(ca[:SSM_T], axis=0)
    bb2 = jnp.concatenate([bb_re, -bb_im], axis=1)
    drow = lax.dot_general(bb2, ca_all, (((1,), (1,)), ((), ())),
                           precision=lax.Precision.HIGHEST, preferred_element_type=F32)
    row_g = lax.broadcasted_iota(jnp.int32, drow.shape, 0) // SSM_P
    col_g = (lax.broadcasted_iota(jnp.int32, drow.shape, 1) % LANES) // SSM_P
    drow = jnp.where(row_g == col_g, drow, 0.0).astype(BF16)
    wpair_ref[0:LANES, :] = drow
    wpair_ref[LANES:, 0:LANES] = jnp.zeros((LANES, LANES), BF16)
    wpair_ref[LANES:, LANES:] = drow[:, :(SSM_T - 1) * LANES]

    row_grp = lax.broadcasted_iota(jnp.int32, (LANES, LANES), 0) // SSM_P
    lane_half = lax.broadcasted_iota(jnp.int32, (LANES, LANES), 1) // SSM_N
    for s in range(SSM_T):
        w_re, w_im = p_re[SSM_T - 1 - s], p_im[SSM_T - 1 - s]
        parts = [bb_re * w_re - bb_im * w_im, bb_re * w_im + bb_im * w_re]
        parts = [jnp.concatenate([v, v], axis=1) for v in parts]
        blk = jnp.concatenate([jnp.where(row_grp == 2 * k + lane_half, v, 0.0)
                               for v in parts for k in range(SLAB_N // LANES)], axis=1)
        bfull_ref[s * LANES:(s + 1) * LANES, :] = blk.astype(BF16)
        if s >= SSM_T - DEC_SEQ:
            bhalf_ref[(s - SSM_T + DEC_SEQ) * LANES:(s - SSM_T + DEC_SEQ + 1) * LANES, :] = blk

    lane_grp = lax.broadcasted_iota(jnp.int32, (SSM_N, LANES), 1) // SSM_P
    for t in range(SSM_T):
        cat = ca[t + 1].T
        ca_re, ca_im = cat[:SSM_N], cat[SSM_N:]
        cols = slice(t * LANES, (t + 1) * LANES)
        for g in range(SLAB_G):
            cfull_ref[g * SSM_N:(g + 1) * SSM_N, cols] = jnp.where(lane_grp == g, ca_re, 0.0).astype(BF16)
            cfull_ref[SLAB_N + g * SSM_N:SLAB_N + (g + 1) * SSM_N, cols] = (
                jnp.where(lane_grp == g, -ca_im, 0.0).astype(BF16))

    a_re, a_im = af_re[...], af_im[...]
    sq = []
    for _ in range(4):
        a_re, a_im = a_re * a_re - a_im * a_im, 2.0 * a_re * a_im
        sq.append((a_re, a_im))
    apw_ref[0:1, :] = sq[2][0]
    apw_ref[1:2, :] = sq[2][1]
    apw_ref[2:3, :] = sq[3][0]
    apw_ref[3:4, :] = sq[3][1]
    apw_ref[4:8, :] = jnp.zeros((4, SLAB_N), F32)


def _ssm_prep(lam_re, lam_im, log_dt, b_re, b_im, c_re, c_im):
    n = DEPTH * N_SLAB
    a_re, a_im, k_re, k_im = _ssm_disc(lam_re, lam_im, log_dt)
    rows = lambda v: jnp.repeat(v.reshape(n, SLAB_G, 1, SSM_N), SSM_P, axis=2).reshape(n, LANES, SSM_N)
    bt = lambda v: v.transpose(0, 1, 3, 2).reshape(n, LANES, SSM_N)
    flat = lambda v: v.reshape(n, 1, SLAB_N)
    args = ([rows(v) for v in (a_re, a_im, k_re, k_im)]
            + [bt(b_re), bt(b_im), c_re.reshape(n, LANES, SSM_N), c_im.reshape(n, LANES, SSM_N)]
            + [flat(a_re), flat(a_im)])

    def blk(a):
        return pl.BlockSpec((None,) + a.shape[1:], lambda i: (i, 0, 0))

    out_shape = [jax.ShapeDtypeStruct((n, 2 * LANES, SSM_T * LANES), BF16),
                 jax.ShapeDtypeStruct((n, SSM_T * LANES, 2 * SLAB_N), BF16),
                 jax.ShapeDtypeStruct((n, DEC_SEQ * LANES, 2 * SLAB_N), F32),
                 jax.ShapeDtypeStruct((n, 2 * SLAB_N, SSM_T * LANES), BF16),
                 jax.ShapeDtypeStruct((n, 8, SLAB_N), F32)]
    return pl.pallas_call(
        _ssm_prep_kernel,
        grid=(n,),
        in_specs=[blk(a) for a in args],
        out_specs=[blk(s) for s in out_shape],
        out_shape=out_shape,
        compiler_params=pltpu.CompilerParams(dimension_semantics=("parallel",), vmem_limit_bytes=VMEM_LIMIT),
        name="ssm_prep",
    )(*args)


def _ssm_prompt_kernel(u_ref, wpair_ref, bfull_ref, cfull_ref, apw_ref, y_ref, hst_ref,
                       ub_ref, s_ref, yacc_ref, *col_refs):
    rows = SSM_HALF_B * N_CHUNK
    n_col = SLAB_N // LANES
    sp_refs, hp_refs = col_refs[:n_col], col_refs[n_col:]
    for t in range(SSM_T):
        ub_ref[:, t * LANES:(t + 1) * LANES] = u_ref[pl.ds(t, rows, stride=SSM_T), :].astype(BF16)
    s_ref[...] = jnp.dot(ub_ref[...], bfull_ref[...], preferred_element_type=F32)
    for b in range(SSM_HALF_B):
        seq = slice(b * N_CHUNK, (b + 1) * N_CHUNK)
        for k in range(n_col):
            sp_refs[k][pl.ds(b, N_CHUNK, stride=8), :] = s_ref[seq, k * LANES:(k + 1) * LANES]
            sp_refs[k][pl.ds(SSM_HALF_B + b, N_CHUNK, stride=8), :] = (
                s_ref[seq, SLAB_N + k * LANES:SLAB_N + (k + 1) * LANES])
    upper = lax.broadcasted_iota(jnp.int32, (8, LANES), 0) < SSM_HALF_B
    a1 = [jnp.broadcast_to(apw_ref[2:3, k * LANES:(k + 1) * LANES], (8, LANES)) for k in range(n_col)]
    a2 = [jnp.where(upper, -1.0, 1.0) * jnp.broadcast_to(apw_ref[3:4, k * LANES:(k + 1) * LANES], (8, LANES))
          for k in range(n_col)]

    def step(c, h):
        r0 = pl.multiple_of(c * 8, 8)
        new = []
        for k in range(n_col):
            hp_refs[k][pl.ds(r0, 8), :] = h[k]
            new.append(a1[k] * h[k] + a2[k] * pltpu.roll(h[k], SSM_HALF_B, axis=0) + sp_refs[k][pl.ds(r0, 8), :])
        return tuple(new)

    h = lax.fori_loop(0, N_CHUNK, step, tuple(jnp.zeros((8, LANES), F32) for _ in range(n_col)), unroll=8)
    hst_ref[...] = jnp.concatenate(h, axis=1)
    hbc = jnp.concatenate(
        [jnp.concatenate([hp_refs[k][pl.ds(part + b, N_CHUNK, stride=8), :]
                          for part in (0, SSM_HALF_B) for k in range(n_col)], axis=1)
         for b in range(SSM_HALF_B)], axis=0)
    yacc_ref[...] = jnp.dot(hbc.astype(BF16), cfull_ref[...], preferred_element_type=F32)
    for t in range(0, SSM_T, 2):
        yacc_ref[:, t * LANES:] += jnp.dot(ub_ref[:, t * LANES:(t + 2) * LANES],
                                           wpair_ref[:, 0:(SSM_T - t) * LANES], preferred_element_type=F32)
    for t in range(SSM_T):
        y_ref[pl.ds(t, rows, stride=SSM_T), :] = yacc_ref[:, t * LANES:(t + 1) * LANES]


def _ssm_prompt(u, wpair, bfull, cfull, apw, layer):
    rows = SSM_HALF_B * N_CHUNK
    tok = SSM_HALF_B * SEQ

    def wspec(a):
        return pl.BlockSpec((None,) + a.shape[1:], lambda j, h: (layer * N_SLAB + j, 0, 0))

    return pl.pallas_call(
        _ssm_prompt_kernel,
        grid=(N_SLAB, BATCH // SSM_HALF_B),
        in_specs=[pl.BlockSpec((tok, LANES), lambda j, h: (h, j)), wspec(wpair), wspec(bfull), wspec(cfull), wspec(apw)],
        out_specs=[pl.BlockSpec((tok, LANES), lambda j, h: (h, j)),
                   pl.BlockSpec((None, 8, SLAB_N), lambda j, h: (h, 0, j))],
        out_shape=[jax.ShapeDtypeStruct((N_PROMPT, SSM_W), F32),
                   jax.ShapeDtypeStruct((BATCH // SSM_HALF_B, 8, SSM_G * SSM_N), F32)],
        scratch_shapes=[pltpu.VMEM((rows, SSM_T * LANES), BF16), pltpu.VMEM((rows, 2 * SLAB_N), F32),
                        pltpu.VMEM((rows, SSM_T * LANES), F32)]
                       + [pltpu.VMEM((N_CHUNK * 8, LANES), F32)] * (2 * SLAB_N // LANES),
        compiler_params=pltpu.CompilerParams(dimension_semantics=("arbitrary", "arbitrary"),
                                             vmem_limit_bytes=VMEM_LIMIT),
        name="ssm_prompt",
    )(u, wpair, bfull, cfull, apw)


def _ssm_sample_kernel(u_ref, h0r_ref, h0i_ref, wpair_ref, bhalf_ref, cfull_ref, apw_ref,
                       y_ref, hr_ref, hi_ref, uf_ref, yacc_ref):
    for t in range(DEC_SEQ):
        uf_ref[:, t * LANES:(t + 1) * LANES] = u_ref[pl.ds(t, DEC_BATCH, stride=DEC_SEQ), :]
    uf = uf_ref[...]
    h0r = h0r_ref[...]
    h0i = h0i_ref[...]
    ar = apw_ref[0:1, :]
    ai = apw_ref[1:2, :]
    s = jnp.dot(uf, bhalf_ref[...], precision=lax.Precision.HIGHEST, preferred_element_type=F32)
    hr_ref[...] = ar * h0r - ai * h0i + s[:, :SLAB_N]
    hi_ref[...] = ar * h0i + ai * h0r + s[:, SLAB_N:]
    ub = uf.astype(BF16)
    h0 = jnp.concatenate([h0r, h0i], axis=1).astype(BF16)
    yacc_ref[...] = jnp.dot(h0, cfull_ref[...], preferred_element_type=F32)
    for t in range(0, DEC_SEQ, 2):
        yacc_ref[:, t * LANES:] += jnp.dot(ub[:, t * LANES:(t + 2) * LANES],
                                           wpair_ref[:, 0:(DEC_SEQ - t) * LANES], preferred_element_type=F32)
    for t in range(DEC_SEQ):
        y_ref[pl.ds(t, DEC_BATCH, stride=DEC_SEQ), :] = yacc_ref[:, t * LANES:(t + 1) * LANES]


def _ssm_sample(u, h0re, h0im, wpair, bhalf, cfull, apw, layer):
    width = DEC_SEQ * LANES

    def wspec(r, c):
        return pl.BlockSpec((None, r, c), lambda j: (layer * N_SLAB + j, 0, 0))

    st = pl.BlockSpec((DEC_BATCH, SLAB_N), lambda j: (0, j))
    return pl.pallas_call(
        _ssm_sample_kernel,
        grid=(N_SLAB,),
        in_specs=[pl.BlockSpec((N_SAMPLE, LANES), lambda j: (N_PROMPT // N_SAMPLE, j)), st, st,
                  wspec(2 * LANES, width), wspec(width, 2 * SLAB_N), wspec(2 * SLAB_N, width), wspec(8, SLAB_N)],
        out_specs=[pl.BlockSpec((N_SAMPLE, LANES), lambda j: (0, j)), st, st],
        out_shape=[jax.ShapeDtypeStruct((N_SAMPLE, SSM_W), F32),
                   jax.ShapeDtypeStruct((DEC_BATCH, SSM_G * SSM_N), F32),
                   jax.ShapeDtypeStruct((DEC_BATCH, SSM_G * SSM_N), F32)],
        scratch_shapes=[pltpu.VMEM((DEC_BATCH, width), F32), pltpu.VMEM((DEC_BATCH, width), F32)],
        compiler_params=pltpu.CompilerParams(dimension_semantics=("arbitrary",), vmem_limit_bytes=VMEM_LIMIT),
        name="ssm_sample",
    )(u, h0re, h0im, wpair, bhalf, cfull, apw)


def _mask_cap(valid):
    return jnp.where(valid, jnp.inf, NEG_INF).astype(F32)


def _sink_softmax(s, cap, sink):
    s = jnp.minimum(s, cap)
    m = jnp.maximum(jnp.max(s, axis=-1, keepdims=True), sink)
    e = jnp.exp(s - m)
    denom = jnp.sum(e, axis=-1, keepdims=True) + jnp.exp(sink - m)
    return e.astype(BF16), denom


def _scores(q, scale_mask, kk):
    qx = jnp.concatenate([q[:, c * KV_W:(c + 1) * KV_W] * scale_mask for c in range(GRP)], axis=0)
    return lax.dot_general(qx.astype(BF16), kk, (((1,), (1,)), ((), ())), preferred_element_type=F32)


def _band_valid(n_q, n_k, rows):
    qi = lax.broadcasted_iota(jnp.int32, (rows, n_k), 0) & (n_q - 1)
    kj = lax.broadcasted_iota(jnp.int32, (rows, n_k), 1)
    return kj, ((kj - qi - 1).astype(jnp.uint32) < WINDOW)


def _head_masks():
    lane_kv = lax.broadcasted_iota(jnp.int32, (1, KV_W), 1) // HEAD_DIM
    return [jnp.where(lane_kv == h, HEAD_DIM ** -0.5, 0.0) for h in range(N_KV)], lane_kv == 0


ATTN_Q_TILE = 512


def _attn_prompt_kernel(q_ref, kvp_ref, kvc_ref, sink_ref, o_ref):
    rows = GRP * WINDOW
    scale_masks, first_kv = _head_masks()
    kj, valid = _band_valid(WINDOW, 2 * WINDOW, rows)
    cap_first = _mask_cap(valid & ((kj >= WINDOW) | (pl.program_id(1) > 0)))
    cap = _mask_cap(valid)
    n_blk = ATTN_Q_TILE // WINDOW

    def stage(r):
        cur = kvc_ref[r * WINDOW:(r + 1) * WINDOW, :]
        prev = kvp_ref[...] if r == 0 else kvc_ref[(r - 1) * WINDOW:r * WINDOW, :]
        kk = jnp.concatenate([prev[:, :KV_W], cur[:, :KV_W]], axis=0).astype(BF16)
        vv = jnp.concatenate([prev[:, KV_W:], cur[:, KV_W:]], axis=0).astype(BF16)
        q = q_ref[r * WINDOW:(r + 1) * WINDOW, :].astype(F32)
        return [_scores(q, scale_masks[h], kk) for h in range(N_KV)], vv

    staged = [stage(r) for r in range(n_blk)]
    weights = [[_sink_softmax(staged[r][0][h], cap_first if r == 0 else cap, sink_ref[h]) for h in range(N_KV)]
               for r in range(n_blk)]
    for r in range(n_blk):
        vv = staged[r][1]
        outs = [jnp.dot(e, vv, preferred_element_type=F32) / denom for e, denom in weights[r]]
        for c in range(GRP):
            o = jnp.where(first_kv, outs[0][c * WINDOW:(c + 1) * WINDOW], outs[1][c * WINDOW:(c + 1) * WINDOW])
            o_ref[r * WINDOW:(r + 1) * WINDOW, c * KV_W:(c + 1) * KV_W] = o.astype(o_ref.dtype)


def _attn_prompt(q, kv, sink_col):
    tiles = SEQ // ATTN_Q_TILE
    per_tile = ATTN_Q_TILE // WINDOW
    blocks = SEQ // WINDOW
    return pl.pallas_call(
        _attn_prompt_kernel,
        grid=(BATCH, tiles),
        in_specs=[pl.BlockSpec((ATTN_Q_TILE, ATTN_W), lambda b, j: (b * tiles + j, 0)),
                  pl.BlockSpec((WINDOW, 2 * KV_W), lambda b, j: (b * blocks + jnp.maximum(j * per_tile - 1, 0), 0)),
                  pl.BlockSpec((ATTN_Q_TILE, 2 * KV_W), lambda b, j: (b * tiles + j, 0)),
                  pl.BlockSpec((N_KV, GRP * WINDOW, 1), lambda b, j: (0, 0, 0))],
        out_specs=pl.BlockSpec((ATTN_Q_TILE, ATTN_W), lambda b, j: (b * tiles + j, 0)),
        out_shape=jax.ShapeDtypeStruct((N_PROMPT, ATTN_W), BF16),
        compiler_params=pltpu.CompilerParams(dimension_semantics=("parallel", "arbitrary")),
        name="attn_prompt",
    )(q, kv, kv, sink_col)


SAMPLE_SEQ_BLOCK = 8


def _attn_sample_kernel(q_ref, ck_ref, cv_ref, kvn_ref, sink_ref, o_ref, nk_ref, nv_ref):
    rows = GRP * DEC_SEQ
    scale_masks, first_kv = _head_masks()
    cap = _mask_cap(_band_valid(DEC_SEQ, WINDOW + DEC_SEQ, rows)[1])
    q_all = q_ref[...].astype(F32)
    toks = [slice(s * DEC_SEQ, (s + 1) * DEC_SEQ) for s in range(SAMPLE_SEQ_BLOCK)]
    scores, values = [], []
    for s, tok in enumerate(toks):
        ck = ck_ref[s]
        cv = cv_ref[s]
        kn = kvn_ref[tok, :KV_W]
        vn = kvn_ref[tok, KV_W:]
        nk_ref[s, 0:WINDOW - DEC_SEQ, :] = ck[DEC_SEQ:, :]
        nk_ref[s, WINDOW - DEC_SEQ:WINDOW, :] = kn
        nv_ref[s, 0:WINDOW - DEC_SEQ, :] = cv[DEC_SEQ:, :]
        nv_ref[s, WINDOW - DEC_SEQ:WINDOW, :] = vn
        kk = jnp.concatenate([ck, kn], axis=0).astype(BF16)
        values.append(jnp.concatenate([cv, vn], axis=0).astype(BF16))
        scores.append([_scores(q_all[tok, :], scale_masks[h], kk) for h in range(N_KV)])
    weights = [[_sink_softmax(sc[h], cap, sink_ref[h]) for h in range(N_KV)] for sc in scores]
    for s, tok in enumerate(toks):
        outs = [jnp.dot(e, values[s], preferred_element_type=F32) / denom for e, denom in weights[s]]
        for c in range(GRP):
            o_ref[tok, c * KV_W:(c + 1) * KV_W] = jnp.where(
                first_kv, outs[0][c * DEC_SEQ:(c + 1) * DEC_SEQ], outs[1][c * DEC_SEQ:(c + 1) * DEC_SEQ])


def _attn_sample(q, kv, ck, cv, sink_col):
    sb = SAMPLE_SEQ_BLOCK
    tok = sb * DEC_SEQ
    first = N_PROMPT // tok
    c_spec = pl.BlockSpec((sb, WINDOW, KV_W), lambda i: (i, 0, 0))
    return pl.pallas_call(
        _attn_sample_kernel,
        grid=(DEC_BATCH // sb,),
        in_specs=[pl.BlockSpec((tok, ATTN_W), lambda i: (first + i, 0)), c_spec, c_spec,
                  pl.BlockSpec((tok, 2 * KV_W), lambda i: (first + i, 0)),
                  pl.BlockSpec((N_KV, GRP * DEC_SEQ, 1), lambda i: (0, 0, 0))],
        out_specs=[pl.BlockSpec((tok, ATTN_W), lambda i: (i, 0)), c_spec, c_spec],
        out_shape=[jax.ShapeDtypeStruct((N_SAMPLE, ATTN_W), F32),
                   jax.ShapeDtypeStruct((DEC_BATCH, WINDOW, KV_W), F32),
                   jax.ShapeDtypeStruct((DEC_BATCH, WINDOW, KV_W), F32)],
        compiler_params=pltpu.CompilerParams(dimension_semantics=("parallel",)),
        name="attn_sample",
    )(q, ck, cv, kv, sink_col)


def _mix_out(x, y, u, o, d_ref, wglu_ref, gs_ref, ga_ref, wout_ref):
    y = y + d_ref[...] * u
    y = jax.nn.gelu(y)
    y = y * jax.nn.sigmoid(jnp.dot(y.astype(BF16), wglu_ref[...], preferred_element_type=F32))
    ys = _rms(y, gs_ref[...]).astype(BF16)
    ya = _rms(o, ga_ref[...]).astype(BF16)
    mix = jnp.dot(ys, wout_ref[0:SSM_W, :], preferred_element_type=F32)
    mix = mix + jnp.dot(ya, wout_ref[SSM_W:, :], preferred_element_type=F32)
    return x + mix


def _tail_kernel(x_ref, yp_ref, ys_ref, u_ref, op_ref, os_ref, pp_ref, ps_ref,
                 d_ref, wglu_ref, gs_ref, ga_ref, wout_ref,
                 g2_ref, wg_ref, wu_ref, wd_ref, gp_ref, wpg_ref, wpp_ref, gf_ref,
                 *out_refs, final):
    x1 = _mix_out(x_ref[...], _pick(yp_ref, ys_ref), u_ref[...], _pick(op_ref, os_ref),
                  d_ref, wglu_ref, gs_ref, ga_ref, wout_ref)
    x2 = _half_swiglu(x1, g2_ref, wg_ref, wu_ref, wd_ref)
    h = _rms(x2, gp_ref[...]).astype(BF16)
    gate = jax.nn.sigmoid(jnp.dot(h, wpg_ref[...], preferred_element_type=F32))
    proj = jnp.dot(_pick(pp_ref, ps_ref).astype(BF16), wpp_ref[...], preferred_element_type=F32)
    x3 = x2 + proj * gate
    if not final:
        out_refs[0][...] = x3
        return
    y = _rms(x3, gf_ref[...])
    yp_ref, ys_ref = out_refs

    @pl.when(pl.program_id(0) < PROMPT_TILES)
    def _():
        yp_ref[...] = y

    @pl.when(pl.program_id(0) >= PROMPT_TILES)
    def _():
        ys_ref[...] = y


def _layer_tail(x, y_prompt, y_sample, u, o_prompt, o_sample, p_prompt, p_sample, layer,
                d, wglu, gs, ga, wout, g2, wg, wu, wd, gp, wpg, wpp, gf, final):
    if final:
        out_specs = _pair_specs(D_MODEL)
        out_shape = [jax.ShapeDtypeStruct((N_PROMPT, D_MODEL), F32), jax.ShapeDtypeStruct((N_SAMPLE, D_MODEL), F32)]
    else:
        out_specs = _row_spec(D_MODEL)
        out_shape = jax.ShapeDtypeStruct((N_TOK, D_MODEL), F32)
    p_specs = [pl.BlockSpec((None, TOKEN_TILE, PLE_DIM), lambda i: (layer, jnp.minimum(i, PROMPT_TILES - 1), 0)),
               pl.BlockSpec((None, TOKEN_TILE, PLE_DIM), lambda i: (layer, jnp.maximum(i - PROMPT_TILES, 0), 0))]
    return pl.pallas_call(
        functools.partial(_tail_kernel, final=final),
        grid=(N_TOK // TOKEN_TILE,),
        in_specs=[_row_spec(D_MODEL)] + _pair_specs(SSM_W) + [_row_spec(SSM_W)] + _pair_specs(ATTN_W) + p_specs + [
            _const_spec((1, SSM_W)), _const_spec((SSM_W, SSM_W)), _const_spec((1, SSM_W)), _const_spec((1, ATTN_W)),
            _const_spec((D_MODEL, D_MODEL)),
            _const_spec((1, D_MODEL)),
            _const_spec((D_MODEL, D_FF)), _const_spec((D_MODEL, D_FF)), _const_spec((D_FF, D_MODEL)),
            _const_spec((1, D_MODEL)), _const_spec((D_MODEL, D_MODEL)), _const_spec((PLE_DIM, D_MODEL)),
            _const_spec((1, D_MODEL))],
        out_specs=out_specs,
        out_shape=out_shape,
        compiler_params=pltpu.CompilerParams(dimension_semantics=("arbitrary",), vmem_limit_bytes=VMEM_LIMIT),
        name="layer_tail",
    )(x, y_prompt, y_sample, u, o_prompt, o_sample, p_prompt, p_sample,
      d, wglu, gs, ga, wout, g2, wg, wu, wd, gp, wpg, wpp, gf)


def _sink_column(sinks, rows_per_head):
    return jnp.repeat(sinks.astype(F32).reshape(N_KV, GRP), rows_per_head, axis=1)[..., None]


def _head_major(a, axis):
    shape = a.shape
    a = a.reshape(shape[:axis] + (N_KV, GRP, HEAD_DIM) + shape[axis + 1:])
    return jnp.swapaxes(a, axis, axis + 1).reshape(shape)


def kernel(x_prompt, x_sample, cache_k, cache_v, state_ssm_re, state_ssm_im, p_prompt, p_sample, ffn1_norm, ffn1_w_gate, ffn1_w_up, ffn1_w_down, mix_norm, w_in, ssm_lam_re, ssm_lam_im, ssm_log_dt, ssm_b_re, ssm_b_im, ssm_c_re, ssm_c_im, ssm_d, ssm_w_glu, ssm_out_norm, attn_sinks, attn_out_norm, w_out, ffn2_norm, ffn2_w_gate, ffn2_w_up, ffn2_w_down, ple_norm, ple_w_gate, ple_w_proj, final_norm):
    wpair, bfull, bhalf, cfull, apw = _ssm_prep(ssm_lam_re, ssm_lam_im, ssm_log_dt, ssm_b_re, ssm_b_im,
                                                ssm_c_re, ssm_c_im)
    row = lambda v: v.reshape(1, -1)
    bf = lambda w: w.astype(BF16)
    xs = (x_prompt.reshape(N_PROMPT, D_MODEL), x_sample.reshape(N_SAMPLE, D_MODEL))
    n_state = SSM_G * SSM_N

    kp_l, vp_l, hpr_l, hpi_l, ks_l, vs_l, hsr_l, hsi_l = [], [], [], [], [], [], [], []
    for i in range(DEPTH):
        w_in_i = jnp.concatenate([w_in[i, :, :SSM_W], _head_major(w_in[i, :, SSM_W:SSM_W + ATTN_W], 1),
                                  w_in[i, :, SSM_W + ATTN_W:]], axis=1)
        x, u, q, kv = _ffn_proj(xs, row(ffn1_norm[i]), bf(ffn1_w_gate[i]), bf(ffn1_w_up[i]), bf(ffn1_w_down[i]),
                                row(mix_norm[i]), bf(w_in_i))

        y_p, hst = _ssm_prompt(u, wpair, bfull, cfull, apw, i)
        hst = hst.reshape(BATCH // SSM_HALF_B, 2, SSM_HALF_B, SSM_G, SSM_N)
        hpr_l.append(hst[:, 0].reshape(BATCH, SSM_G, SSM_N))
        hpi_l.append(hst[:, 1].reshape(BATCH, SSM_G, SSM_N))
        y_s, hsr, hsi = _ssm_sample(u, state_ssm_re[i].reshape(DEC_BATCH, n_state),
                                    state_ssm_im[i].reshape(DEC_BATCH, n_state), wpair, bhalf, cfull, apw, i)
        hsr_l.append(hsr.reshape(DEC_BATCH, SSM_G, SSM_N))
        hsi_l.append(hsi.reshape(DEC_BATCH, SSM_G, SSM_N))

        kv_last = jnp.stack([kv[(b + 1) * SEQ - WINDOW:(b + 1) * SEQ] for b in range(BATCH)])
        kv_last = kv_last.reshape(BATCH, WINDOW, 2, N_KV, HEAD_DIM)
        kp_l.append(kv_last[:, :, 0])
        vp_l.append(kv_last[:, :, 1])
        o_p = _attn_prompt(q, kv, _sink_column(attn_sinks[i], WINDOW))
        o_s, nk, nv = _attn_sample(q, kv, cache_k[i].reshape(DEC_BATCH, WINDOW, KV_W),
                                   cache_v[i].reshape(DEC_BATCH, WINDOW, KV_W), _sink_column(attn_sinks[i], DEC_SEQ))
        ks_l.append(nk.reshape(DEC_BATCH, WINDOW, N_KV, HEAD_DIM))
        vs_l.append(nv.reshape(DEC_BATCH, WINDOW, N_KV, HEAD_DIM))

        x = _layer_tail(x, y_p, y_s, u, o_p, o_s,
                        p_prompt.reshape(DEPTH, N_PROMPT, PLE_DIM), p_sample.reshape(DEPTH, N_SAMPLE, PLE_DIM), i,
                        row(ssm_d[i]), bf(ssm_w_glu[i]), row(ssm_out_norm[i]), row(_head_major(attn_out_norm[i], 0)),
                        bf(jnp.concatenate([w_out[i, :SSM_W], _head_major(w_out[i, SSM_W:], 0)], axis=0)),
                        row(ffn2_norm[i]), bf(ffn2_w_gate[i]), bf(ffn2_w_up[i]), bf(ffn2_w_down[i]),
                        row(ple_norm[i]), bf(ple_w_gate[i]), bf(ple_w_proj[i]), row(final_norm),
                        final=(i == DEPTH - 1))
        xs = (x,)

    y_prompt, y_sample = x
    return (y_prompt.reshape(BATCH, SEQ, D_MODEL), y_sample.reshape(DEC_BATCH, DEC_SEQ, D_MODEL),
            jnp.stack(kp_l), jnp.stack(vp_l), jnp.stack(hpr_l), jnp.stack(hpi_l),
            jnp.stack(ks_l), jnp.stack(vs_l), jnp.stack(hsr_l), jnp.stack(hsi_l))
```

```python
import functools

import jax
import jax.numpy as jnp
from jax import lax
from jax.experimental import pallas as pl
from jax.experimental.pallas import tpu as pltpu

F32 = jnp.float32
BF16 = jnp.bfloat16

D_MODEL = 1024
BATCH = 8
SEQ = 2048
DEPTH = 2
DEC_BATCH = 128
DEC_SEQ = 8
SSM_W = 512
SSM_P = 16
SSM_G = 32
SSM_N = 64
ATTN_W = 512
HEAD_DIM = 64
N_HEADS = 8
N_KV = 2
GRP = N_HEADS // N_KV
KV_W = N_KV * HEAD_DIM
IN_W = SSM_W + ATTN_W + 2 * KV_W
WINDOW = 128
D_FF = 2816
PLE_DIM = 256
EPS = 1e-6
NEG_INF = -1e30

N_PROMPT = BATCH * SEQ
N_SAMPLE = DEC_BATCH * DEC_SEQ
N_TOK = N_PROMPT + N_SAMPLE

LANES = 128
SSM_T = 16
N_CHUNK = SEQ // SSM_T
SLAB_G = LANES // SSM_P
N_SLAB = SSM_G // SLAB_G
SLAB_N = SLAB_G * SSM_N
SSM_HALF_B = BATCH // 2

TOKEN_TILE = 512
PROMPT_TILES = N_PROMPT // TOKEN_TILE
FF_CHUNK = 256
VMEM_LIMIT = 56 * 1024 * 1024


def _const_spec(shape):
    nd = len(shape)
    return pl.BlockSpec(shape, lambda *_: (0,) * nd, pipeline_mode=pl.Buffered(1))


def _row_spec(width):
    return pl.BlockSpec((TOKEN_TILE, width), lambda i: (i, 0))


def _pair_specs(width):
    return [pl.BlockSpec((TOKEN_TILE, width), lambda i: (jnp.minimum(i, PROMPT_TILES - 1), 0)),
            pl.BlockSpec((TOKEN_TILE, width), lambda i: (jnp.maximum(i - PROMPT_TILES, 0), 0))]


def _pick(prompt_ref, sample_ref):
    return jnp.where(pl.program_id(0) < PROMPT_TILES, prompt_ref[...].astype(F32), sample_ref[...].astype(F32))


def _rms(x, g):
    return x * lax.rsqrt(jnp.mean(x * x, axis=-1, keepdims=True) + EPS) * g


def _half_swiglu(xf, g_ref, wg_ref, wu_ref, wd_ref):
    h = _rms(xf, g_ref[...]).astype(BF16)
    acc = jnp.zeros(xf.shape, F32)
    for c in range(0, D_FF, FF_CHUNK):
        gate = jnp.dot(h, wg_ref[:, c:c + FF_CHUNK], preferred_element_type=F32)
        up = jnp.dot(h, wu_ref[:, c:c + FF_CHUNK], preferred_element_type=F32)
        act = (gate * jax.nn.sigmoid(gate) * up).astype(BF16)
        acc = acc + jnp.dot(act, wd_ref[c:c + FF_CHUNK, :], preferred_element_type=F32)
    return xf + 0.5 * acc


def _ffn_proj_kernel(*refs, paired):
    if paired:
        xp_ref, xs_ref, *refs = refs
        x = _pick(xp_ref, xs_ref)
    else:
        x_ref, *refs = refs
        x = x_ref[...]
    g1_ref, wg_ref, wu_ref, wd_ref, gm_ref, wu_in_ref, wq_in_ref, wkv_in_ref, xo_ref, u_ref, q_ref, kv_ref = refs
    x1 = _half_swiglu(x, g1_ref, wg_ref, wu_ref, wd_ref)
    xo_ref[...] = x1
    h = _rms(x1, gm_ref[...]).astype(BF16)
    u_ref[...] = jnp.dot(h, wu_in_ref[...], preferred_element_type=F32)
    q_ref[...] = jnp.dot(h, wq_in_ref[...], preferred_element_type=F32).astype(BF16)
    kv_ref[...] = jnp.dot(h, wkv_in_ref[...], preferred_element_type=F32)


def _ffn_proj(xs, g1, wg, wu, wd, gm, win_u, win_q, win_kv):
    paired = len(xs) == 2
    x_specs = _pair_specs(D_MODEL) if paired else [_row_spec(D_MODEL)]
    return pl.pallas_call(
        functools.partial(_ffn_proj_kernel, paired=paired),
        grid=(N_TOK // TOKEN_TILE,),
        in_specs=x_specs + [_const_spec((1, D_MODEL)),
                            _const_spec((D_MODEL, D_FF)), _const_spec((D_MODEL, D_FF)), _const_spec((D_FF, D_MODEL)),
                            _const_spec((1, D_MODEL)), _const_spec((D_MODEL, SSM_W)),
                            _const_spec((D_MODEL, ATTN_W)), _const_spec((D_MODEL, 2 * KV_W))],
        out_specs=[_row_spec(D_MODEL), _row_spec(SSM_W), _row_spec(ATTN_W), _row_spec(2 * KV_W)],
        out_shape=[jax.ShapeDtypeStruct((N_TOK, D_MODEL), F32), jax.ShapeDtypeStruct((N_TOK, SSM_W), F32),
                   jax.ShapeDtypeStruct((N_TOK, ATTN_W), BF16), jax.ShapeDtypeStruct((N_TOK, 2 * KV_W), F32)],
        compiler_params=pltpu.CompilerParams(dimension_semantics=("parallel",), vmem_limit_bytes=VMEM_LIMIT),
        name="ffn1_proj",
    )(*xs, g1, wg, wu, wd, gm, win_u, win_q, win_kv)


def _discretise(lam_re, lam_im, log_dt):
    dt = jnp.exp(log_dt)
    mag = jnp.exp(lam_re * dt)
    ang = lam_im * dt
    a_re = mag * jnp.cos(ang)
    a_im = mag * jnp.sin(ang)
    den = lam_re * lam_re + lam_im * lam_im
    nr = a_re - 1.0
    k_re = (nr * lam_re + a_im * lam_im) / den
    k_im = (a_im * lam_re - nr * lam_im) / den
    return a_re, a_im, k_re, k_im


def _powers(a_re, a_im, n):
    p_re = [jnp.ones_like(a_re)]
    p_im = [jnp.zeros_like(a_re)]
    for _ in range(n):
        r, i = p_re[-1], p_im[-1]
        p_re.append(r * a_re - i * a_im)
        p_im.append(r * a_im + i * a_re)
    return p_re, p_im


def _ssm_disc_kernel(lre_ref, lim_ref, ldt_ref, are_ref, aim_ref, kre_ref, kim_ref):
    a_re, a_im, k_re, k_im = _discretise(lre_ref[...], lim_ref[...], ldt_ref[...])
    are_ref[...] = a_re
    aim_ref[...] = a_im
    kre_ref[...] = k_re
    kim_ref[...] = k_im


def _ssm_disc(lam_re, lam_im, log_dt):
    n = DEPTH * SSM_G
    ldt = jnp.broadcast_to(log_dt[..., None], lam_re.shape)
    outs = pl.pallas_call(
        _ssm_disc_kernel,
        out_shape=[jax.ShapeDtypeStruct((n, SSM_N), F32)] * 4,
        name="ssm_disc",
    )(lam_re.reshape(n, SSM_N), lam_im.reshape(n, SSM_N), ldt.reshape(n, SSM_N))
    return [o.reshape(DEPTH, SSM_G, SSM_N) for o in outs]


def _ssm_prep_kernel(ar_re, ar_im, kr_re, kr_im, btr_re, btr_im, cr_re, cr_im, af_re, af_im,
                     wpair_ref, bfull_ref, bhalf_ref, cfull_ref, apw_ref):
    a_re, a_im, k_re, k_im = ar_re[...], ar_im[...], kr_re[...], kr_im[...]
    bb_re = k_re * btr_re[...] - k_im * btr_im[...]
    bb_im = k_re * btr_im[...] + k_im * btr_re[...]
    p_re, p_im = _powers(a_re, a_im, SSM_T)
    c_re, c_im = cr_re[...], cr_im[...]
    ca = [jnp.concatenate([c_re * p_re[t] - c_im * p_im[t], c_re * p_im[t] + c_im * p_re[t]], axis=1)
          for t in range(SSM_T + 1)]
    ca_all = jnp.concatenate(ca[:SSM_T], axis=0)
    bb2 = jnp.concatenate([bb_re, -bb_im], axis=1)
    drow = lax.dot_general(bb2, ca_all, (((1,), (1,)), ((), ())),
                           precision=lax.Precision.HIGHEST, preferred_element_type=F32)
    row_g = lax.broadcasted_iota(jnp.int32, drow.shape, 0) // SSM_P
    col_g = (lax.broadcasted_iota(jnp.int32, drow.shape, 1) % LANES) // SSM_P
    drow = jnp.where(row_g == col_g, drow, 0.0).astype(BF16)
    wpair_ref[0:LANES, :] = drow
    wpair_ref[LANES:, 0:LANES] = jnp.zeros((LANES, LANES), BF16)
    wpair_ref[LANES:, LANES:] = drow[:, :(SSM_T - 1) * LANES]

    row_grp = lax.broadcasted_iota(jnp.int32, (LANES, LANES), 0) // SSM_P
    lane_half = lax.broadcasted_iota(jnp.int32, (LANES, LANES), 1) // SSM_N
    for s in range(SSM_T):
        w_re, w_im = p_re[SSM_T - 1 - s], p_im[SSM_T - 1 - s]
        parts = [bb_re * w_re - bb_im * w_im, bb_re * w_im + bb_im * w_re]
        parts = [jnp.concatenate([v, v], axis=1) for v in parts]
        blk = jnp.concatenate([jnp.where(row_grp == 2 * k + lane_half, v, 0.0)
                               for v in parts for k in range(SLAB_N // LANES)], axis=1)
        bfull_ref[s * LANES:(s + 1) * LANES, :] = blk.astype(BF16)
        if s >= SSM_T - DEC_SEQ:
            bhalf_ref[(s - SSM_T + DEC_SEQ) * LANES:(s - SSM_T + DEC_SEQ + 1) * LANES, :] = blk

    lane_grp = lax.broadcasted_iota(jnp.int32, (SSM_N, LANES), 1) // SSM_P
    for t in range(SSM_T):
        cat = ca[t + 1].T
        ca_re, ca_im = cat[:SSM_N], cat[SSM_N:]
        cols = slice(t * LANES, (t + 1) * LANES)
        for g in range(SLAB_G):
            cfull_ref[g * SSM_N:(g + 1) * SSM_N, cols] = jnp.where(lane_grp == g, ca_re, 0.0).astype(BF16)
            cfull_ref[SLAB_N + g * SSM_N:SLAB_N + (g + 1) * SSM_N, cols] = (
                jnp.where(lane_grp == g, -ca_im, 0.0).astype(BF16))

    a_re, a_im = af_re[...], af_im[...]
    sq = []
    for _ in range(4):
        a_re, a_im = a_re * a_re - a_im * a_im, 2.0 * a_re * a_im
        sq.append((a_re, a_im))
    apw_ref[0:1, :] = sq[2][0]
    apw_ref[1:2, :] = sq[2][1]
    apw_ref[2:3, :] = sq[3][0]
    apw_ref[3:4, :] = sq[3][1]
    apw_ref[4:8, :] = jnp.zeros((4, SLAB_N), F32)


def _ssm_prep(lam_re, lam_im, log_dt, b_re, b_im, c_re, c_im):
    n = DEPTH * N_SLAB
    a_re, a_im, k_re, k_im = _ssm_disc(lam_re, lam_im, log_dt)
    rows = lambda v: jnp.repeat(v.reshape(n, SLAB_G, 1, SSM_N), SSM_P, axis=2).reshape(n, LANES, SSM_N)
    bt = lambda v: v.transpose(0, 1, 3, 2).reshape(n, LANES, SSM_N)
    flat = lambda v: v.reshape(n, 1, SLAB_N)
    args = ([rows(v) for v in (a_re, a_im, k_re, k_im)]
            + [bt(b_re), bt(b_im), c_re.reshape(n, LANES, SSM_N), c_im.reshape(n, LANES, SSM_N)]
            + [flat(a_re), flat(a_im)])

    def blk(a):
        return pl.BlockSpec((None,) + a.shape[1:], lambda i: (i, 0, 0))

    out_shape = [jax.ShapeDtypeStruct((n, 2 * LANES, SSM_T * LANES), BF16),
                 jax.ShapeDtypeStruct((n, SSM_T * LANES, 2 * SLAB_N), BF16),
                 jax.ShapeDtypeStruct((n, DEC_SEQ * LANES, 2 * SLAB_N), F32),
                 jax.ShapeDtypeStruct((n, 2 * SLAB_N, SSM_T * LANES), BF16),
                 jax.ShapeDtypeStruct((n, 8, SLAB_N), F32)]
    return pl.pallas_call(
        _ssm_prep_kernel,
        grid=(n,),
        in_specs=[blk(a) for a in args],
        out_specs=[blk(s) for s in out_shape],
        out_shape=out_shape,
        compiler_params=pltpu.CompilerParams(dimension_semantics=("parallel",), vmem_limit_bytes=VMEM_LIMIT),
        name="ssm_prep",
    )(*args)


def _ssm_prompt_kernel(u_ref, wpair_ref, bfull_ref, cfull_ref, apw_ref, y_ref, hst_ref,
                       ub_ref, s_ref, yacc_ref, *col_refs):
    rows = SSM_HALF_B * N_CHUNK
    n_col = SLAB_N // LANES
    sp_refs, hp_refs = col_refs[:n_col], col_refs[n_col:]
    for t in range(SSM_T):
        ub_ref[:, t * LANES:(t + 1) * LANES] = u_ref[pl.ds(t, rows, stride=SSM_T), :].astype(BF16)
    s_ref[...] = jnp.dot(ub_ref[...], bfull_ref[...], preferred_element_type=F32)
    for b in range(SSM_HALF_B):
        seq = slice(b * N_CHUNK, (b + 1) * N_CHUNK)
        for k in range(n_col):
            sp_refs[k][pl.ds(b, N_CHUNK, stride=8), :] = s_ref[seq, k * LANES:(k + 1) * LANES]
            sp_refs[k][pl.ds(SSM_HALF_B + b, N_CHUNK, stride=8), :] = (
                s_ref[seq, SLAB_N + k * LANES:SLAB_N + (k + 1) * LANES])
    upper = lax.broadcasted_iota(jnp.int32, (8, LANES), 0) < SSM_HALF_B
    a1 = [jnp.broadcast_to(apw_ref[2:3, k * LANES:(k + 1) * LANES], (8, LANES)) for k in range(n_col)]
    a2 = [jnp.where(upper, -1.0, 1.0) * jnp.broadcast_to(apw_ref[3:4, k * LANES:(k + 1) * LANES], (8, LANES))
          for k in range(n_col)]

    def step(c, h):
        r0 = pl.multiple_of(c * 8, 8)
        new = []
        for k in range(n_col):
            hp_refs[k][pl.ds(r0, 8), :] = h[k]
            new.append(a1[k] * h[k] + a2[k] * pltpu.roll(h[k], SSM_HALF_B, axis=0) + sp_refs[k][pl.ds(r0, 8), :])
        return tuple(new)

    h = lax.fori_loop(0, N_CHUNK, step, tuple(jnp.zeros((8, LANES), F32) for _ in range(n_col)), unroll=8)
    hst_ref[...] = jnp.concatenate(h, axis=1)
    hbc = jnp.concatenate(
        [jnp.concatenate([hp_refs[k][pl.ds(part + b, N_CHUNK, stride=8), :]
                          for part in (0, SSM_HALF_B) for k in range(n_col)], axis=1)
         for b in range(SSM_HALF_B)], axis=0)
    yacc_ref[...] = jnp.dot(hbc.astype(BF16), cfull_ref[...], preferred_element_type=F32)
    for t in range(0, SSM_T, 2):
        yacc_ref[:, t * LANES:] += jnp.dot(ub_ref[:, t * LANES:(t + 2) * LANES],
                                           wpair_ref[:, 0:(SSM_T - t) * LANES], preferred_element_type=F32)
    for t in range(SSM_T):
        y_ref[pl.ds(t, rows, stride=SSM_T), :] = yacc_ref[:, t * LANES:(t + 1) * LANES]


def _ssm_prompt(u, wpair, bfull, cfull, apw, layer):
    rows = SSM_HALF_B * N_CHUNK
    tok = SSM_HALF_B * SEQ

    def wspec(a):
        return pl.BlockSpec((None,) + a.shape[1:], lambda j, h: (layer * N_SLAB + j, 0, 0))

    return pl.pallas_call(
        _ssm_prompt_kernel,
        grid=(N_SLAB, BATCH // SSM_HALF_B),
        in_specs=[pl.BlockSpec((tok, LANES), lambda j, h: (h, j)), wspec(wpair), wspec(bfull), wspec(cfull), wspec(apw)],
        out_specs=[pl.BlockSpec((tok, LANES), lambda j, h: (h, j)),
                   pl.BlockSpec((None, 8, SLAB_N), lambda j, h: (h, 0, j))],
        out_shape=[jax.ShapeDtypeStruct((N_PROMPT, SSM_W), F32),
                   jax.ShapeDtypeStruct((BATCH // SSM_HALF_B, 8, SSM_G * SSM_N), F32)],
        scratch_shapes=[pltpu.VMEM((rows, SSM_T * LANES), BF16), pltpu.VMEM((rows, 2 * SLAB_N), F32),
                        pltpu.VMEM((rows, SSM_T * LANES), F32)]
                       + [pltpu.VMEM((N_CHUNK * 8, LANES), F32)] * (2 * SLAB_N // LANES),
        compiler_params=pltpu.CompilerParams(dimension_semantics=("arbitrary", "arbitrary"),
                                             vmem_limit_bytes=VMEM_LIMIT),
        name="ssm_prompt",
    )(u, wpair, bfull, cfull, apw)


def _ssm_sample_kernel(u_ref, h0r_ref, h0i_ref, wpair_ref, bhalf_ref, cfull_ref, apw_ref,
                       y_ref, hr_ref, hi_ref, uf_ref, yacc_ref):
    for t in range(DEC_SEQ):
        uf_ref[:, t * LANES:(t + 1) * LANES] = u_ref[pl.ds(t, DEC_BATCH, stride=DEC_SEQ), :]
    uf = uf_ref[...]
    h0r = h0r_ref[...]
    h0i = h0i_ref[...]
    ar = apw_ref[0:1, :]
    ai = apw_ref[1:2, :]
    s = jnp.dot(uf, bhalf_ref[...], precision=lax.Precision.HIGHEST, preferred_element_type=F32)
    hr_ref[...] = ar * h0r - ai * h0i + s[:, :SLAB_N]
    hi_ref[...] = ar * h0i + ai * h0r + s[:, SLAB_N:]
    ub = uf.astype(BF16)
    h0 = jnp.concatenate([h0r, h0i], axis=1).astype(BF16)
    yacc_ref[...] = jnp.dot(h0, cfull_ref[...], preferred_element_type=F32)
    for t in range(0, DEC_SEQ, 2):
        yacc_ref[:, t * LANES:] += jnp.dot(ub[:, t * LANES:(t + 2) * LANES],
                                           wpair_ref[:, 0:(DEC_SEQ - t) * LANES], preferred_element_type=F32)
    for t in range(DEC_SEQ):
        y_ref[pl.ds(t, DEC_BATCH, stride=DEC_SEQ), :] = yacc_ref[:, t * LANES:(t + 1) * LANES]


def _ssm_sample(u, h0re, h0im, wpair, bhalf, cfull, apw, layer):
    width = DEC_SEQ * LANES

    def wspec(r, c):
        return pl.BlockSpec((None, r, c), lambda j: (layer * N_SLAB + j, 0, 0))

    st = pl.BlockSpec((DEC_BATCH, SLAB_N), lambda j: (0, j))
    return pl.pallas_call(
        _ssm_sample_kernel,
        grid=(N_SLAB,),
        in_specs=[pl.BlockSpec((N_SAMPLE, LANES), lambda j: (N_PROMPT // N_SAMPLE, j)), st, st,
                  wspec(2 * LANES, width), wspec(width, 2 * SLAB_N), wspec(2 * SLAB_N, width), wspec(8, SLAB_N)],
        out_specs=[pl.BlockSpec((N_SAMPLE, LANES), lambda j: (0, j)), st, st],
        out_shape=[jax.ShapeDtypeStruct((N_SAMPLE, SSM_W), F32),
                   jax.ShapeDtypeStruct((DEC_BATCH, SSM_G * SSM_N), F32),
                   jax.ShapeDtypeStruct((DEC_BATCH, SSM_G * SSM_N), F32)],
        scratch_shapes=[pltpu.VMEM((DEC_BATCH, width), F32), pltpu.VMEM((DEC_BATCH, width), F32)],
        compiler_params=pltpu.CompilerParams(dimension_semantics=("arbitrary",), vmem_limit_bytes=VMEM_LIMIT),
        name="ssm_sample",
    )(u, h0re, h0im, wpair, bhalf, cfull, apw)


def _mask_cap(valid):
    return jnp.where(valid, jnp.inf, NEG_INF).astype(F32)


def _sink_softmax(s, cap, sink):
    s = jnp.minimum(s, cap)
    m = jnp.maximum(jnp.max(s, axis=-1, keepdims=True), sink)
    e = jnp.exp(s - m)
    denom = jnp.sum(e, axis=-1, keepdims=True) + jnp.exp(sink - m)
    return e.astype(BF16), denom


def _scores(q, scale_mask, kk):
    qx = jnp.concatenate([q[:, c * KV_W:(c + 1) * KV_W] * scale_mask for c in range(GRP)], axis=0)
    return lax.dot_general(qx.astype(BF16), kk, (((1,), (1,)), ((), ())), preferred_element_type=F32)


def _band_valid(n_q, n_k, rows):
    qi = lax.broadcasted_iota(jnp.int32, (rows, n_k), 0) & (n_q - 1)
    kj = lax.broadcasted_iota(jnp.int32, (rows, n_k), 1)
    return kj, ((kj - qi - 1).astype(jnp.uint32) < WINDOW)


def _head_masks():
    lane_kv = lax.broadcasted_iota(jnp.int32, (1, KV_W), 1) // HEAD_DIM
    return [jnp.where(lane_kv == h, HEAD_DIM ** -0.5, 0.0) for h in range(N_KV)], lane_kv == 0


ATTN_Q_TILE = 512


def _attn_prompt_kernel(q_ref, kvp_ref, kvc_ref, sink_ref, o_ref):
    cols = GRP * WINDOW
    scale_masks, _ = _head_masks()
    kj = lax.broadcasted_iota(jnp.int32, (2 * WINDOW, cols), 0)
    qi = lax.broadcasted_iota(jnp.int32, (2 * WINDOW, cols), 1) & (WINDOW - 1)
    valid = (kj - qi - 1).astype(jnp.uint32) < WINDOW
    cap_first = _mask_cap(valid & ((kj >= WINDOW) | (pl.program_id(1) > 0)))
    cap = _mask_cap(valid)
    n_blk = ATTN_Q_TILE // WINDOW

    def stage(r):
        cur = kvc_ref[r * WINDOW:(r + 1) * WINDOW, :]
        prev = kvp_ref[...] if r == 0 else kvc_ref[(r - 1) * WINDOW:r * WINDOW, :]
        kk = jnp.concatenate([prev[:, :KV_W], cur[:, :KV_W]], axis=0).astype(BF16)
        v_t = jnp.concatenate([prev[:, KV_W:], cur[:, KV_W:]], axis=0).T.astype(BF16)
        q = q_ref[r * WINDOW:(r + 1) * WINDOW, :].astype(F32)
        s_t = []
        for h in range(N_KV):
            qx = jnp.concatenate([q[:, c * KV_W:(c + 1) * KV_W] * scale_masks[h] for c in range(GRP)], axis=0)
            s_t.append(lax.dot_general(kk, qx.astype(BF16), (((1,), (1,)), ((), ())), preferred_element_type=F32))
        return s_t, v_t

    def softmax_t(s, cap_r, sink):
        s = jnp.minimum(s, cap_r)
        m = jnp.maximum(jnp.max(s, axis=0, keepdims=True), sink)
        e = jnp.exp(s - m)
        return e.astype(BF16), jnp.sum(e, axis=0, keepdims=True) + jnp.exp(sink - m)

    staged = [stage(r) for r in range(n_blk)]
    weights = [[softmax_t(staged[r][0][h], cap_first if r == 0 else cap, sink_ref[h]) for h in range(N_KV)]
               for r in range(n_blk)]
    for r in range(n_blk):
        v_t = staged[r][1]
        outs = [jnp.dot(v_t, e, preferred_element_type=F32) / denom for e, denom in weights[r]]
        o_t = jnp.concatenate([outs[h][h * HEAD_DIM:(h + 1) * HEAD_DIM] for h in range(N_KV)], axis=0)
        for c in range(GRP):
            o_ref[r * WINDOW:(r + 1) * WINDOW, c * KV_W:(c + 1) * KV_W] = (
                o_t[:, c * WINDOW:(c + 1) * WINDOW].T.astype(o_ref.dtype))


def _attn_prompt(q, kv, sink_col):
    tiles = SEQ // ATTN_Q_TILE
    per_tile = ATTN_Q_TILE // WINDOW
    blocks = SEQ // WINDOW
    return pl.pallas_call(
        _attn_prompt_kernel,
        grid=(BATCH, tiles),
        in_specs=[pl.BlockSpec((ATTN_Q_TILE, ATTN_W), lambda b, j: (b * tiles + j, 0)),
                  pl.BlockSpec((WINDOW, 2 * KV_W), lambda b, j: (b * blocks + jnp.maximum(j * per_tile - 1, 0), 0)),
                  pl.BlockSpec((ATTN_Q_TILE, 2 * KV_W), lambda b, j: (b * tiles + j, 0)),
                  pl.BlockSpec((N_KV, 1, GRP * WINDOW), lambda b, j: (0, 0, 0))],
        out_specs=pl.BlockSpec((ATTN_Q_TILE, ATTN_W), lambda b, j: (b * tiles + j, 0)),
        out_shape=jax.ShapeDtypeStruct((N_PROMPT, ATTN_W), BF16),
        compiler_params=pltpu.CompilerParams(dimension_semantics=("parallel", "arbitrary")),
        name="attn_prompt",
    )(q, kv, kv, sink_col)


SAMPLE_SEQ_BLOCK = 8


def _attn_sample_kernel(q_ref, ck_ref, cv_ref, kvn_ref, sink_ref, o_ref, nk_ref, nv_ref):
    rows = GRP * DEC_SEQ
    scale_masks, first_kv = _head_masks()
    cap = _mask_cap(_band_valid(DEC_SEQ, WINDOW + DEC_SEQ, rows)[1])
    q_all = q_ref[...].astype(F32)
    toks = [slice(s * DEC_SEQ, (s + 1) * DEC_SEQ) for s in range(SAMPLE_SEQ_BLOCK)]
    scores, values = [], []
    for s, tok in enumerate(toks):
        ck = ck_ref[s]
        cv = cv_ref[s]
        kn = kvn_ref[tok, :KV_W]
        vn = kvn_ref[tok, KV_W:]
        nk_ref[s, 0:WINDOW - DEC_SEQ, :] = ck[DEC_SEQ:, :]
        nk_ref[s, WINDOW - DEC_SEQ:WINDOW, :] = kn
        nv_ref[s, 0:WINDOW - DEC_SEQ, :] = cv[DEC_SEQ:, :]
        nv_ref[s, WINDOW - DEC_SEQ:WINDOW, :] = vn
        kk = jnp.concatenate([ck, kn], axis=0).astype(BF16)
        values.append(jnp.concatenate([cv, vn], axis=0).astype(BF16))
        scores.append([_scores(q_all[tok, :], scale_masks[h], kk) for h in range(N_KV)])
    weights = [[_sink_softmax(sc[h], cap, sink_ref[h]) for h in range(N_KV)] for sc in scores]
    for s, tok in enumerate(toks):
        outs = [jnp.dot(e, values[s], preferred_element_type=F32) / denom for e, denom in weights[s]]
        for c in range(GRP):
            o_ref[tok, c * KV_W:(c + 1) * KV_W] = jnp.where(
                first_kv, outs[0][c * DEC_SEQ:(c + 1) * DEC_SEQ], outs[1][c * DEC_SEQ:(c + 1) * DEC_SEQ])


def _attn_sample(q, kv, ck, cv, sink_col):
    sb = SAMPLE_SEQ_BLOCK
    tok = sb * DEC_SEQ
    first = N_PROMPT // tok
    c_spec = pl.BlockSpec((sb, WINDOW, KV_W), lambda i: (i, 0, 0))
    return pl.pallas_call(
        _attn_sample_kernel,
        grid=(DEC_BATCH // sb,),
        in_specs=[pl.BlockSpec((tok, ATTN_W), lambda i: (first + i, 0)), c_spec, c_spec,
                  pl.BlockSpec((tok, 2 * KV_W), lambda i: (first + i, 0)),
                  pl.BlockSpec((N_KV, GRP * DEC_SEQ, 1), lambda i: (0, 0, 0))],
        out_specs=[pl.BlockSpec((tok, ATTN_W), lambda i: (i, 0)), c_spec, c_spec],
        out_shape=[jax.ShapeDtypeStruct((N_SAMPLE, ATTN_W), F32),
                   jax.ShapeDtypeStruct((DEC_BATCH, WINDOW, KV_W), F32),
                   jax.ShapeDtypeStruct((DEC_BATCH, WINDOW, KV_W), F32)],
        compiler_params=pltpu.CompilerParams(dimension_semantics=("parallel",)),
        name="attn_sample",
    )(q, ck, cv, kv, sink_col)


def _mix_out(x, y, u, o, d_ref, wglu_ref, gs_ref, ga_ref, wout_s_ref, wout_a_ref):
    y = y + d_ref[...] * u
    y = jax.nn.gelu(y)
    y = y * jax.nn.sigmoid(jnp.dot(y.astype(BF16), wglu_ref[...], preferred_element_type=F32))
    ys = _rms(y, gs_ref[...]).astype(BF16)
    ya = _rms(o, ga_ref[...]).astype(BF16)
    mix = jnp.dot(ys, wout_s_ref[...], preferred_element_type=F32)
    mix = mix + jnp.dot(ya, wout_a_ref[...], preferred_element_type=F32)
    return x + mix


def _tail_kernel(x_ref, yp_ref, ys_ref, u_ref, op_ref, os_ref, pp_ref, ps_ref,
                 d_ref, wglu_ref, gs_ref, ga_ref, wout_s_ref, wout_a_ref,
                 g2_ref, wg_ref, wu_ref, wd_ref, gp_ref, wpg_ref, wpp_ref, gf_ref,
                 *out_refs, final):
    x1 = _mix_out(x_ref[...], _pick(yp_ref, ys_ref), u_ref[...], _pick(op_ref, os_ref),
                  d_ref, wglu_ref, gs_ref, ga_ref, wout_s_ref, wout_a_ref)
    x2 = _half_swiglu(x1, g2_ref, wg_ref, wu_ref, wd_ref)
    h = _rms(x2, gp_ref[...]).astype(BF16)
    gate = jax.nn.sigmoid(jnp.dot(h, wpg_ref[...], preferred_element_type=F32))
    proj = jnp.dot(_pick(pp_ref, ps_ref).astype(BF16), wpp_ref[...], preferred_element_type=F32)
    x3 = x2 + proj * gate
    if not final:
        out_refs[0][...] = x3
        return
    y = _rms(x3, gf_ref[...])
    yp_ref, ys_ref = out_refs

    @pl.when(pl.program_id(0) < PROMPT_TILES)
    def _():
        yp_ref[...] = y

    @pl.when(pl.program_id(0) >= PROMPT_TILES)
    def _():
        ys_ref[...] = y


def _layer_tail(x, y_prompt, y_sample, u, o_prompt, o_sample, p_prompt, p_sample, layer,
                d, wglu, gs, ga, wout_ssm, wout_attn, g2, wg, wu, wd, gp, wpg, wpp, gf, final):
    if final:
        out_specs = _pair_specs(D_MODEL)
        out_shape = [jax.ShapeDtypeStruct((N_PROMPT, D_MODEL), F32), jax.ShapeDtypeStruct((N_SAMPLE, D_MODEL), F32)]
    else:
        out_specs = _row_spec(D_MODEL)
        out_shape = jax.ShapeDtypeStruct((N_TOK, D_MODEL), F32)
    p_specs = [pl.BlockSpec((None, TOKEN_TILE, PLE_DIM), lambda i: (layer, jnp.minimum(i, PROMPT_TILES - 1), 0)),
               pl.BlockSpec((None, TOKEN_TILE, PLE_DIM), lambda i: (layer, jnp.maximum(i - PROMPT_TILES, 0), 0))]
    return pl.pallas_call(
        functools.partial(_tail_kernel, final=final),
        grid=(N_TOK // TOKEN_TILE,),
        in_specs=[_row_spec(D_MODEL)] + _pair_specs(SSM_W) + [_row_spec(SSM_W)] + _pair_specs(ATTN_W) + p_specs + [
            _const_spec((1, SSM_W)), _const_spec((SSM_W, SSM_W)), _const_spec((1, SSM_W)), _const_spec((1, ATTN_W)),
            _const_spec((SSM_W, D_MODEL)), _const_spec((ATTN_W, D_MODEL)),
            _const_spec((1, D_MODEL)),
            _const_spec((D_MODEL, D_FF)), _const_spec((D_MODEL, D_FF)), _const_spec((D_FF, D_MODEL)),
            _const_spec((1, D_MODEL)), _const_spec((D_MODEL, D_MODEL)), _const_spec((PLE_DIM, D_MODEL)),
            _const_spec((1, D_MODEL))],
        out_specs=out_specs,
        out_shape=out_shape,
        compiler_params=pltpu.CompilerParams(dimension_semantics=("arbitrary",), vmem_limit_bytes=VMEM_LIMIT),
        name="layer_tail",
    )(x, y_prompt, y_sample, u, o_prompt, o_sample, p_prompt, p_sample,
      d, wglu, gs, ga, wout_ssm, wout_attn, g2, wg, wu, wd, gp, wpg, wpp, gf)


def _sink_column(sinks, rows_per_head):
    return jnp.repeat(sinks.astype(F32).reshape(N_KV, GRP), rows_per_head, axis=1)[..., None]


def _head_major(a, axis):
    shape = a.shape
    a = a.reshape(shape[:axis] + (N_KV, GRP, HEAD_DIM) + shape[axis + 1:])
    return jnp.swapaxes(a, axis, axis + 1).reshape(shape)


def kernel(x_prompt, x_sample, cache_k, cache_v, state_ssm_re, state_ssm_im, p_prompt, p_sample, ffn1_norm, ffn1_w_gate, ffn1_w_up, ffn1_w_down, mix_norm, w_in, ssm_lam_re, ssm_lam_im, ssm_log_dt, ssm_b_re, ssm_b_im, ssm_c_re, ssm_c_im, ssm_d, ssm_w_glu, ssm_out_norm, attn_sinks, attn_out_norm, w_out, ffn2_norm, ffn2_w_gate, ffn2_w_up, ffn2_w_down, ple_norm, ple_w_gate, ple_w_proj, final_norm):
    wpair, bfull, bhalf, cfull, apw = _ssm_prep(ssm_lam_re, ssm_lam_im, ssm_log_dt, ssm_b_re, ssm_b_im,
                                                ssm_c_re, ssm_c_im)
    row = lambda v: v.reshape(1, -1)
    bf = lambda w: w.astype(BF16)
    xs = (x_prompt.reshape(N_PROMPT, D_MODEL), x_sample.reshape(N_SAMPLE, D_MODEL))
    n_state = SSM_G * SSM_N

    kp_l, vp_l, hpr_l, hpi_l, ks_l, vs_l, hsr_l, hsi_l = [], [], [], [], [], [], [], []
    for i in range(DEPTH):
        x, u, q, kv = _ffn_proj(xs, row(ffn1_norm[i]), bf(ffn1_w_gate[i]), bf(ffn1_w_up[i]), bf(ffn1_w_down[i]),
                                row(mix_norm[i]), bf(w_in[i, :, :SSM_W]),
                                _head_major(bf(w_in[i, :, SSM_W:SSM_W + ATTN_W]), 1), bf(w_in[i, :, SSM_W + ATTN_W:]))

        y_p, hst = _ssm_prompt(u, wpair, bfull, cfull, apw, i)
        hst = hst.reshape(BATCH // SSM_HALF_B, 2, SSM_HALF_B, SSM_G, SSM_N)
        hpr_l.append(hst[:, 0].reshape(BATCH, SSM_G, SSM_N))
        hpi_l.append(hst[:, 1].reshape(BATCH, SSM_G, SSM_N))
        y_s, hsr, hsi = _ssm_sample(u, state_ssm_re[i].reshape(DEC_BATCH, n_state),
                                    state_ssm_im[i].reshape(DEC_BATCH, n_state), wpair, bhalf, cfull, apw, i)
        hsr_l.append(hsr.reshape(DEC_BATCH, SSM_G, SSM_N))
        hsi_l.append(hsi.reshape(DEC_BATCH, SSM_G, SSM_N))

        kv_last = jnp.stack([kv[(b + 1) * SEQ - WINDOW:(b + 1) * SEQ] for b in range(BATCH)])
        kv_last = kv_last.reshape(BATCH, WINDOW, 2, N_KV, HEAD_DIM)
        kp_l.append(kv_last[:, :, 0])
        vp_l.append(kv_last[:, :, 1])
        o_p = _attn_prompt(q, kv, _sink_column(attn_sinks[i], WINDOW).reshape(N_KV, 1, GRP * WINDOW))
        o_s, nk, nv = _attn_sample(q, kv, cache_k[i].reshape(DEC_BATCH, WINDOW, KV_W),
                                   cache_v[i].reshape(DEC_BATCH, WINDOW, KV_W), _sink_column(attn_sinks[i], DEC_SEQ))
        ks_l.append(nk.reshape(DEC_BATCH, WINDOW, N_KV, HEAD_DIM))
        vs_l.append(nv.reshape(DEC_BATCH, WINDOW, N_KV, HEAD_DIM))

        x = _layer_tail(x, y_p, y_s, u, o_p, o_s,
                        p_prompt.reshape(DEPTH, N_PROMPT, PLE_DIM), p_sample.reshape(DEPTH, N_SAMPLE, PLE_DIM), i,
                        row(ssm_d[i]), bf(ssm_w_glu[i]), row(ssm_out_norm[i]), row(_head_major(attn_out_norm[i], 0)),
                        bf(w_out[i, :SSM_W]), _head_major(bf(w_out[i, SSM_W:]), 0),
                        row(ffn2_norm[i]), bf(ffn2_w_gate[i]), bf(ffn2_w_up[i]), bf(ffn2_w_down[i]),
                        row(ple_norm[i]), bf(ple_w_gate[i]), bf(ple_w_proj[i]), row(final_norm),
                        final=(i == DEPTH - 1))
        xs = (x,)

    y_prompt, y_sample = x
    return (y_prompt.reshape(BATCH, SEQ, D_MODEL), y_sample.reshape(DEC_BATCH, DEC_SEQ, D_MODEL),
            jnp.stack(kp_l), jnp.stack(vp_l), jnp.stack(hpr_l), jnp.stack(hpi_l),
            jnp.stack(ks_l), jnp.stack(vs_l), jnp.stack(hsr_l), jnp.stack(hsi_l))
```

```python
import functools

import jax
import jax.numpy as jnp
from jax import lax
from jax.experimental import pallas as pl
from jax.experimental.pallas import tpu as pltpu

F32 = jnp.float32
BF16 = jnp.bfloat16

D_MODEL = 1024
BATCH = 8
SEQ = 2048
DEPTH = 2
DEC_BATCH = 128
DEC_SEQ = 8
SSM_W = 512
SSM_P = 16
SSM_G = 32
SSM_N = 64
ATTN_W = 512
HEAD_DIM = 64
N_HEADS = 8
N_KV = 2
GRP = N_HEADS // N_KV
KV_W = N_KV * HEAD_DIM
IN_W = SSM_W + ATTN_W + 2 * KV_W
WINDOW = 128
D_FF = 2816
PLE_DIM = 256
EPS = 1e-6
NEG_INF = -1e30

N_PROMPT = BATCH * SEQ
N_SAMPLE = DEC_BATCH * DEC_SEQ
N_TOK = N_PROMPT + N_SAMPLE

LANES = 128
SSM_T = 16
N_CHUNK = SEQ // SSM_T
SLAB_G = LANES // SSM_P
N_SLAB = SSM_G // SLAB_G
SLAB_N = SLAB_G * SSM_N
SSM_HALF_B = BATCH // 2

TOKEN_TILE = 512
PROMPT_TILES = N_PROMPT // TOKEN_TILE
FF_CHUNK = 256
VMEM_LIMIT = 56 * 1024 * 1024


def _const_spec(shape):
    nd = len(shape)
    return pl.BlockSpec(shape, lambda *_: (0,) * nd, pipeline_mode=pl.Buffered(1))


def _row_spec(width):
    return pl.BlockSpec((TOKEN_TILE, width), lambda i: (i, 0))


def _pair_specs(width):
    return [pl.BlockSpec((TOKEN_TILE, width), lambda i: (jnp.minimum(i, PROMPT_TILES - 1), 0)),
            pl.BlockSpec((TOKEN_TILE, width), lambda i: (jnp.maximum(i - PROMPT_TILES, 0), 0))]


def _pick(prompt_ref, sample_ref):
    return jnp.where(pl.program_id(0) < PROMPT_TILES, prompt_ref[...].astype(F32), sample_ref[...].astype(F32))


def _rms(x, g):
    return x * lax.rsqrt(jnp.mean(x * x, axis=-1, keepdims=True) + EPS) * g


def _half_swiglu(xf, g_ref, wg_ref, wu_ref, wd_ref):
    h = _rms(xf, g_ref[...]).astype(BF16)
    acc = jnp.zeros(xf.shape, F32)
    for c in range(0, D_FF, FF_CHUNK):
        gate = jnp.dot(h, wg_ref[:, c:c + FF_CHUNK], preferred_element_type=F32)
        up = jnp.dot(h, wu_ref[:, c:c + FF_CHUNK], preferred_element_type=F32)
        act = (gate * jax.nn.sigmoid(gate) * up).astype(BF16)
        acc = acc + jnp.dot(act, wd_ref[c:c + FF_CHUNK, :], preferred_element_type=F32)
    return xf + 0.5 * acc


def _ffn_proj_kernel(*refs, paired):
    if paired:
        xp_ref, xs_ref, *refs = refs
        x = _pick(xp_ref, xs_ref)
    else:
        x_ref, *refs = refs
        x = x_ref[...]
    g1_ref, wg_ref, wu_ref, wd_ref, gm_ref, wu_in_ref, wq_in_ref, wkv_in_ref, xo_ref, u_ref, q_ref, kv_ref = refs
    x1 = _half_swiglu(x, g1_ref, wg_ref, wu_ref, wd_ref)
    xo_ref[...] = x1
    h = _rms(x1, gm_ref[...]).astype(BF16)
    u_ref[...] = jnp.dot(h, wu_in_ref[...], preferred_element_type=F32)
    q_ref[...] = jnp.dot(h, wq_in_ref[...], preferred_element_type=F32).astype(BF16)
    kv_ref[...] = jnp.dot(h, wkv_in_ref[...], preferred_element_type=F32)


def _ffn_proj(xs, g1, wg, wu, wd, gm, win_u, win_q, win_kv):
    paired = len(xs) == 2
    x_specs = _pair_specs(D_MODEL) if paired else [_row_spec(D_MODEL)]
    return pl.pallas_call(
        functools.partial(_ffn_proj_kernel, paired=paired),
        grid=(N_TOK // TOKEN_TILE,),
        in_specs=x_specs + [_const_spec((1, D_MODEL)),
                            _const_spec((D_MODEL, D_FF)), _const_spec((D_MODEL, D_FF)), _const_spec((D_FF, D_MODEL)),
                            _const_spec((1, D_MODEL)), _const_spec((D_MODEL, SSM_W)),
                            _const_spec((D_MODEL, ATTN_W)), _const_spec((D_MODEL, 2 * KV_W))],
        out_specs=[_row_spec(D_MODEL), _row_spec(SSM_W), _row_spec(ATTN_W), _row_spec(2 * KV_W)],
        out_shape=[jax.ShapeDtypeStruct((N_TOK, D_MODEL), F32), jax.ShapeDtypeStruct((N_TOK, SSM_W), F32),
                   jax.ShapeDtypeStruct((N_TOK, ATTN_W), BF16), jax.ShapeDtypeStruct((N_TOK, 2 * KV_W), F32)],
        compiler_params=pltpu.CompilerParams(dimension_semantics=("parallel",), vmem_limit_bytes=VMEM_LIMIT),
        name="ffn1_proj",
    )(*xs, g1, wg, wu, wd, gm, win_u, win_q, win_kv)


def _discretise(lam_re, lam_im, log_dt):
    dt = jnp.exp(log_dt)
    mag = jnp.exp(lam_re * dt)
    ang = lam_im * dt
    a_re = mag * jnp.cos(ang)
    a_im = mag * jnp.sin(ang)
    den = lam_re * lam_re + lam_im * lam_im
    nr = a_re - 1.0
    k_re = (nr * lam_re + a_im * lam_im) / den
    k_im = (a_im * lam_re - nr * lam_im) / den
    return a_re, a_im, k_re, k_im


def _powers(a_re, a_im, n):
    p_re = [jnp.ones_like(a_re)]
    p_im = [jnp.zeros_like(a_re)]
    for _ in range(n):
        r, i = p_re[-1], p_im[-1]
        p_re.append(r * a_re - i * a_im)
        p_im.append(r * a_im + i * a_re)
    return p_re, p_im


def _ssm_disc_kernel(lre_ref, lim_ref, ldt_ref, are_ref, aim_ref, kre_ref, kim_ref):
    a_re, a_im, k_re, k_im = _discretise(lre_ref[...], lim_ref[...], ldt_ref[...])
    are_ref[...] = a_re
    aim_ref[...] = a_im
    kre_ref[...] = k_re
    kim_ref[...] = k_im


def _ssm_disc(lam_re, lam_im, log_dt):
    n = DEPTH * SSM_G
    ldt = jnp.broadcast_to(log_dt[..., None], lam_re.shape)
    outs = pl.pallas_call(
        _ssm_disc_kernel,
        out_shape=[jax.ShapeDtypeStruct((n, SSM_N), F32)] * 4,
        name="ssm_disc",
    )(lam_re.reshape(n, SSM_N), lam_im.reshape(n, SSM_N), ldt.reshape(n, SSM_N))
    return [o.reshape(DEPTH, SSM_G, SSM_N) for o in outs]


def _ssm_prep_kernel(ar_re, ar_im, kr_re, kr_im, btr_re, btr_im, cr_re, cr_im, af_re, af_im,
                     wpair_ref, bfull_ref, bhalf_ref, cfull_ref, apw_ref):
    a_re, a_im, k_re, k_im = ar_re[...], ar_im[...], kr_re[...], kr_im[...]
    bb_re = k_re * btr_re[...] - k_im * btr_im[...]
    bb_im = k_re * btr_im[...] + k_im * btr_re[...]
    p_re, p_im = _powers(a_re, a_im, SSM_T)
    c_re, c_im = cr_re[...], cr_im[...]
    ca = [jnp.concatenate([c_re * p_re[t] - c_im * p_im[t], c_re * p_im[t] + c_im * p_re[t]], axis=1)
          for t in range(SSM_T + 1)]
    ca_all = jnp.concatenate(ca[:SSM_T], axis=0)
    bb2 = jnp.concatenate([bb_re, -bb_im], axis=1)
    drow = lax.dot_general(bb2, ca_all, (((1,), (1,)), ((), ())),
                           precision=lax.Precision.HIGHEST, preferred_element_type=F32)
    row_g = lax.broadcasted_iota(jnp.int32, drow.shape, 0) // SSM_P
    col_g = (lax.broadcasted_iota(jnp.int32, drow.shape, 1) % LANES) // SSM_P
    drow = jnp.where(row_g == col_g, drow, 0.0).astype(BF16)
    wpair_ref[0:LANES, :] = drow
    wpair_ref[LANES:, 0:LANES] = jnp.zeros((LANES, LANES), BF16)
    wpair_ref[LANES:, LANES:] = drow[:, :(SSM_T - 1) * LANES]

    row_grp = lax.broadcasted_iota(jnp.int32, (LANES, LANES), 0) // SSM_P
    lane_half = lax.broadcasted_iota(jnp.int32, (LANES, LANES), 1) // SSM_N
    for s in range(SSM_T):
        w_re, w_im = p_re[SSM_T - 1 - s], p_im[SSM_T - 1 - s]
        parts = [bb_re * w_re - bb_im * w_im, bb_re * w_im + bb_im * w_re]
        parts = [jnp.concatenate([v, v], axis=1) for v in parts]
        blk = jnp.concatenate([jnp.where(row_grp == 2 * k + lane_half, v, 0.0)
                               for v in parts for k in range(SLAB_N // LANES)], axis=1)
        bfull_ref[s * LANES:(s + 1) * LANES, :] = blk.astype(BF16)
        if s >= SSM_T - DEC_SEQ:
            bhalf_ref[(s - SSM_T + DEC_SEQ) * LANES:(s - SSM_T + DEC_SEQ + 1) * LANES, :] = blk

    lane_grp = lax.broadcasted_iota(jnp.int32, (SSM_N, LANES), 1) // SSM_P
    for t in range(SSM_T):
        cat = ca[t + 1].T
        ca_re, ca_im = cat[:SSM_N], cat[SSM_N:]
        cols = slice(t * LANES, (t + 1) * LANES)
        for g in range(SLAB_G):
            cfull_ref[g * SSM_N:(g + 1) * SSM_N, cols] = jnp.where(lane_grp == g, ca_re, 0.0).astype(BF16)
            cfull_ref[SLAB_N + g * SSM_N:SLAB_N + (g + 1) * SSM_N, cols] = (
                jnp.where(lane_grp == g, -ca_im, 0.0).astype(BF16))

    a_re, a_im = af_re[...], af_im[...]
    sq = []
    for _ in range(4):
        a_re, a_im = a_re * a_re - a_im * a_im, 2.0 * a_re * a_im
        sq.append((a_re, a_im))
    apw_ref[0:1, :] = sq[2][0]
    apw_ref[1:2, :] = sq[2][1]
    apw_ref[2:3, :] = sq[3][0]
    apw_ref[3:4, :] = sq[3][1]
    apw_ref[4:8, :] = jnp.zeros((4, SLAB_N), F32)


def _ssm_prep(lam_re, lam_im, log_dt, b_re, b_im, c_re, c_im):
    n = DEPTH * N_SLAB
    a_re, a_im, k_re, k_im = _ssm_disc(lam_re, lam_im, log_dt)
    rows = lambda v: jnp.repeat(v.reshape(n, SLAB_G, 1, SSM_N), SSM_P, axis=2).reshape(n, LANES, SSM_N)
    bt = lambda v: v.transpose(0, 1, 3, 2).reshape(n, LANES, SSM_N)
    flat = lambda v: v.reshape(n, 1, SLAB_N)
    args = ([rows(v) for v in (a_re, a_im, k_re, k_im)]
            + [bt(b_re), bt(b_im), c_re.reshape(n, LANES, SSM_N), c_im.reshape(n, LANES, SSM_N)]
            + [flat(a_re), flat(a_im)])

    def blk(a):
        return pl.BlockSpec((None,) + a.shape[1:], lambda i: (i, 0, 0))

    out_shape = [jax.ShapeDtypeStruct((n, 2 * LANES, SSM_T * LANES), BF16),
                 jax.ShapeDtypeStruct((n, SSM_T * LANES, 2 * SLAB_N), BF16),
                 jax.ShapeDtypeStruct((n, DEC_SEQ * LANES, 2 * SLAB_N), F32),
                 jax.ShapeDtypeStruct((n, 2 * SLAB_N, SSM_T * LANES), BF16),
                 jax.ShapeDtypeStruct((n, 8, SLAB_N), F32)]
    return pl.pallas_call(
        _ssm_prep_kernel,
        grid=(n,),
        in_specs=[blk(a) for a in args],
        out_specs=[blk(s) for s in out_shape],
        out_shape=out_shape,
        compiler_params=pltpu.CompilerParams(dimension_semantics=("parallel",), vmem_limit_bytes=VMEM_LIMIT),
        name="ssm_prep",
    )(*args)


def _ssm_prompt_kernel(u_ref, wpair_ref, bfull_ref, cfull_ref, apw_ref, y_ref, hst_ref,
                       ub_ref, s_ref, yacc_ref, *col_refs):
    rows = SSM_HALF_B * N_CHUNK
    n_col = SLAB_N // LANES
    sp_refs, hp_refs = col_refs[:n_col], col_refs[n_col:]
    for t in range(SSM_T):
        ub_ref[:, t * LANES:(t + 1) * LANES] = u_ref[pl.ds(t, rows, stride=SSM_T), :].astype(BF16)
    s_ref[...] = jnp.dot(ub_ref[...], bfull_ref[...], preferred_element_type=F32)
    for b in range(SSM_HALF_B):
        seq = slice(b * N_CHUNK, (b + 1) * N_CHUNK)
        for k in range(n_col):
            sp_refs[k][pl.ds(b, N_CHUNK, stride=8), :] = s_ref[seq, k * LANES:(k + 1) * LANES]
            sp_refs[k][pl.ds(SSM_HALF_B + b, N_CHUNK, stride=8), :] = (
                s_ref[seq, SLAB_N + k * LANES:SLAB_N + (k + 1) * LANES])
    upper = lax.broadcasted_iota(jnp.int32, (8, LANES), 0) < SSM_HALF_B
    a1 = [jnp.broadcast_to(apw_ref[2:3, k * LANES:(k + 1) * LANES], (8, LANES)) for k in range(n_col)]
    a2 = [jnp.where(upper, -1.0, 1.0) * jnp.broadcast_to(apw_ref[3:4, k * LANES:(k + 1) * LANES], (8, LANES))
          for k in range(n_col)]

    def step(c, h):
        r0 = pl.multiple_of(c * 8, 8)
        new = []
        for k in range(n_col):
            hp_refs[k][pl.ds(r0, 8), :] = h[k]
            new.append(a1[k] * h[k] + a2[k] * pltpu.roll(h[k], SSM_HALF_B, axis=0) + sp_refs[k][pl.ds(r0, 8), :])
        return tuple(new)

    h = lax.fori_loop(0, N_CHUNK, step, tuple(jnp.zeros((8, LANES), F32) for _ in range(n_col)), unroll=8)
    hst_ref[...] = jnp.concatenate(h, axis=1)
    hbc = jnp.concatenate(
        [jnp.concatenate([hp_refs[k][pl.ds(part + b, N_CHUNK, stride=8), :]
                          for part in (0, SSM_HALF_B) for k in range(n_col)], axis=1)
         for b in range(SSM_HALF_B)], axis=0)
    yacc_ref[...] = jnp.dot(hbc.astype(BF16), cfull_ref[...], preferred_element_type=F32)
    for t in range(0, SSM_T, 2):
        yacc_ref[:, t * LANES:] += jnp.dot(ub_ref[:, t * LANES:(t + 2) * LANES],
                                           wpair_ref[:, 0:(SSM_T - t) * LANES], preferred_element_type=F32)
    for t in range(SSM_T):
        y_ref[pl.ds(t, rows, stride=SSM_T), :] = yacc_ref[:, t * LANES:(t + 1) * LANES]


def _ssm_prompt(u, wpair, bfull, cfull, apw, layer):
    rows = SSM_HALF_B * N_CHUNK
    tok = SSM_HALF_B * SEQ

    def wspec(a):
        return pl.BlockSpec((None,) + a.shape[1:], lambda j, h: (layer * N_SLAB + j, 0, 0))

    return pl.pallas_call(
        _ssm_prompt_kernel,
        grid=(N_SLAB, BATCH // SSM_HALF_B),
        in_specs=[pl.BlockSpec((tok, LANES), lambda j, h: (h, j)), wspec(wpair), wspec(bfull), wspec(cfull), wspec(apw)],
        out_specs=[pl.BlockSpec((tok, LANES), lambda j, h: (h, j)),
                   pl.BlockSpec((None, 8, SLAB_N), lambda j, h: (h, 0, j))],
        out_shape=[jax.ShapeDtypeStruct((N_PROMPT, SSM_W), F32),
                   jax.ShapeDtypeStruct((BATCH // SSM_HALF_B, 8, SSM_G * SSM_N), F32)],
        scratch_shapes=[pltpu.VMEM((rows, SSM_T * LANES), BF16), pltpu.VMEM((rows, 2 * SLAB_N), F32),
                        pltpu.VMEM((rows, SSM_T * LANES), F32)]
                       + [pltpu.VMEM((N_CHUNK * 8, LANES), F32)] * (2 * SLAB_N // LANES),
        compiler_params=pltpu.CompilerParams(dimension_semantics=("arbitrary", "arbitrary"),
                                             vmem_limit_bytes=VMEM_LIMIT),
        name="ssm_prompt",
    )(u, wpair, bfull, cfull, apw)


def _ssm_sample_kernel(u_ref, h0r_ref, h0i_ref, wpair_ref, bhalf_ref, cfull_ref, apw_ref,
                       y_ref, hr_ref, hi_ref, uf_ref, yacc_ref):
    for t in range(DEC_SEQ):
        uf_ref[:, t * LANES:(t + 1) * LANES] = u_ref[pl.ds(t, DEC_BATCH, stride=DEC_SEQ), :]
    uf = uf_ref[...]
    h0r = h0r_ref[...]
    h0i = h0i_ref[...]
    ar = apw_ref[0:1, :]
    ai = apw_ref[1:2, :]
    s = jnp.dot(uf, bhalf_ref[...], precision=lax.Precision.HIGHEST, preferred_element_type=F32)
    hr_ref[...] = ar * h0r - ai * h0i + s[:, :SLAB_N]
    hi_ref[...] = ar * h0i + ai * h0r + s[:, SLAB_N:]
    ub = uf.astype(BF16)
    h0 = jnp.concatenate([h0r, h0i], axis=1).astype(BF16)
    yacc_ref[...] = jnp.dot(h0, cfull_ref[...], preferred_element_type=F32)
    for t in range(0, DEC_SEQ, 2):
        yacc_ref[:, t * LANES:] += jnp.dot(ub[:, t * LANES:(t + 2) * LANES],
                                           wpair_ref[:, 0:(DEC_SEQ - t) * LANES], preferred_element_type=F32)
    for t in range(DEC_SEQ):
        y_ref[pl.ds(t, DEC_BATCH, stride=DEC_SEQ), :] = yacc_ref[:, t * LANES:(t + 1) * LANES]


def _ssm_sample(u, h0re, h0im, wpair, bhalf, cfull, apw, layer):
    width = DEC_SEQ * LANES

    def wspec(r, c):
        return pl.BlockSpec((None, r, c), lambda j: (layer * N_SLAB + j, 0, 0))

    st = pl.BlockSpec((DEC_BATCH, SLAB_N), lambda j: (0, j))
    return pl.pallas_call(
        _ssm_sample_kernel,
        grid=(N_SLAB,),
        in_specs=[pl.BlockSpec((N_SAMPLE, LANES), lambda j: (N_PROMPT // N_SAMPLE, j)), st, st,
                  wspec(2 * LANES, width), wspec(width, 2 * SLAB_N), wspec(2 * SLAB_N, width), wspec(8, SLAB_N)],
        out_specs=[pl.BlockSpec((N_SAMPLE, LANES), lambda j: (0, j)), st, st],
        out_shape=[jax.ShapeDtypeStruct((N_SAMPLE, SSM_W), F32),
                   jax.ShapeDtypeStruct((DEC_BATCH, SSM_G * SSM_N), F32),
                   jax.ShapeDtypeStruct((DEC_BATCH, SSM_G * SSM_N), F32)],
        scratch_shapes=[pltpu.VMEM((DEC_BATCH, width), F32), pltpu.VMEM((DEC_BATCH, width), F32)],
        compiler_params=pltpu.CompilerParams(dimension_semantics=("arbitrary",), vmem_limit_bytes=VMEM_LIMIT),
        name="ssm_sample",
    )(u, h0re, h0im, wpair, bhalf, cfull, apw)


def _mask_cap(valid):
    return jnp.where(valid, jnp.inf, NEG_INF).astype(F32)


def _sink_softmax(s, cap, sink):
    s = jnp.minimum(s, cap)
    m = jnp.maximum(jnp.max(s, axis=-1, keepdims=True), sink)
    e = jnp.exp(s - m)
    denom = jnp.sum(e, axis=-1, keepdims=True) + jnp.exp(sink - m)
    return e.astype(BF16), denom


def _scores(q, scale_mask, kk):
    qx = jnp.concatenate([q[:, c * KV_W:(c + 1) * KV_W] * scale_mask for c in range(GRP)], axis=0)
    return lax.dot_general(qx.astype(BF16), kk, (((1,), (1,)), ((), ())), preferred_element_type=F32)


def _band_valid(n_q, n_k, rows):
    qi = lax.broadcasted_iota(jnp.int32, (rows, n_k), 0) & (n_q - 1)
    kj = lax.broadcasted_iota(jnp.int32, (rows, n_k), 1)
    return kj, ((kj - qi - 1).astype(jnp.uint32) < WINDOW)


def _head_masks():
    lane_kv = lax.broadcasted_iota(jnp.int32, (1, KV_W), 1) // HEAD_DIM
    return [jnp.where(lane_kv == h, HEAD_DIM ** -0.5, 0.0) for h in range(N_KV)], lane_kv == 0


ATTN_Q_TILE = 512


def _attn_prompt_kernel(q_ref, kvp_ref, kvc_ref, sink_ref, o_ref):
    cols = GRP * WINDOW
    scale_masks, _ = _head_masks()
    kj = lax.broadcasted_iota(jnp.int32, (2 * WINDOW, cols), 0)
    qi = lax.broadcasted_iota(jnp.int32, (2 * WINDOW, cols), 1) & (WINDOW - 1)
    valid = (kj - qi - 1).astype(jnp.uint32) < WINDOW
    cap_first = _mask_cap(valid & ((kj >= WINDOW) | (pl.program_id(1) > 0)))
    cap = _mask_cap(valid)
    n_blk = ATTN_Q_TILE // WINDOW

    def stage(r):
        cur = kvc_ref[r * WINDOW:(r + 1) * WINDOW, :]
        prev = kvp_ref[...] if r == 0 else kvc_ref[(r - 1) * WINDOW:r * WINDOW, :]
        kk = jnp.concatenate([prev[:, :KV_W], cur[:, :KV_W]], axis=0).astype(BF16)
        v_t = jnp.concatenate([prev[:, KV_W:], cur[:, KV_W:]], axis=0).T.astype(BF16)
        q = q_ref[r * WINDOW:(r + 1) * WINDOW, :].astype(F32)
        s_t = []
        for h in range(N_KV):
            qx = jnp.concatenate([q[:, c * KV_W:(c + 1) * KV_W] * scale_masks[h] for c in range(GRP)], axis=0)
            s_t.append(lax.dot_general(kk, qx.astype(BF16), (((1,), (1,)), ((), ())), preferred_element_type=F32))
        return s_t, v_t

    def softmax_t(s, cap_r, sink):
        s = jnp.minimum(s, cap_r)
        m = jnp.maximum(jnp.max(s, axis=0, keepdims=True), sink)
        e = jnp.exp(s - m)
        return e.astype(BF16), jnp.sum(e, axis=0, keepdims=True) + jnp.exp(sink - m)

    staged = [stage(r) for r in range(n_blk)]
    weights = [[softmax_t(staged[r][0][h], cap_first if r == 0 else cap, sink_ref[h]) for h in range(N_KV)]
               for r in range(n_blk)]
    for r in range(n_blk):
        v_t = staged[r][1]
        outs = [jnp.dot(v_t, e, preferred_element_type=F32) / denom for e, denom in weights[r]]
        o_t = jnp.concatenate([outs[h][h * HEAD_DIM:(h + 1) * HEAD_DIM] for h in range(N_KV)], axis=0)
        for c in range(GRP):
            o_ref[r * WINDOW:(r + 1) * WINDOW, c * KV_W:(c + 1) * KV_W] = (
                o_t[:, c * WINDOW:(c + 1) * WINDOW].T.astype(o_ref.dtype))


def _attn_prompt(q, kv, sink_col):
    tiles = SEQ // ATTN_Q_TILE
    per_tile = ATTN_Q_TILE // WINDOW
    blocks = SEQ // WINDOW
    return pl.pallas_call(
        _attn_prompt_kernel,
        grid=(BATCH, tiles),
        in_specs=[pl.BlockSpec((ATTN_Q_TILE, ATTN_W), lambda b, j: (b * tiles + j, 0)),
                  pl.BlockSpec((WINDOW, 2 * KV_W), lambda b, j: (b * blocks + jnp.maximum(j * per_tile - 1, 0), 0)),
                  pl.BlockSpec((ATTN_Q_TILE, 2 * KV_W), lambda b, j: (b * tiles + j, 0)),
                  pl.BlockSpec((N_KV, 1, GRP * WINDOW), lambda b, j: (0, 0, 0))],
        out_specs=pl.BlockSpec((ATTN_Q_TILE, ATTN_W), lambda b, j: (b * tiles + j, 0)),
        out_shape=jax.ShapeDtypeStruct((N_PROMPT, ATTN_W), BF16),
        compiler_params=pltpu.CompilerParams(dimension_semantics=("parallel", "arbitrary")),
        name="attn_prompt",
    )(q, kv, kv, sink_col)


SAMPLE_SEQ_BLOCK = 8


def _attn_sample_kernel(q_ref, ck_ref, cv_ref, kvn_ref, sink_ref, o_ref, nk_ref, nv_ref):
    rows = GRP * DEC_SEQ
    scale_masks, first_kv = _head_masks()
    cap = _mask_cap(_band_valid(DEC_SEQ, WINDOW + DEC_SEQ, rows)[1])
    q_all = q_ref[...].astype(F32)
    toks = [slice(s * DEC_SEQ, (s + 1) * DEC_SEQ) for s in range(SAMPLE_SEQ_BLOCK)]
    scores, values = [], []
    for s, tok in enumerate(toks):
        kn = kvn_ref[tok, :KV_W]
        vn = kvn_ref[tok, KV_W:]
        nk_ref[s, 0:WINDOW - DEC_SEQ] = ck_ref[s, DEC_SEQ:WINDOW]
        nk_ref[s, WINDOW - DEC_SEQ:WINDOW] = kn.reshape(DEC_SEQ, N_KV, HEAD_DIM)
        nv_ref[s, 0:WINDOW - DEC_SEQ] = cv_ref[s, DEC_SEQ:WINDOW]
        nv_ref[s, WINDOW - DEC_SEQ:WINDOW] = vn.reshape(DEC_SEQ, N_KV, HEAD_DIM)
        ck = ck_ref[s].reshape(WINDOW, KV_W)
        cv = cv_ref[s].reshape(WINDOW, KV_W)
        kk = jnp.concatenate([ck, kn], axis=0).astype(BF16)
        values.append(jnp.concatenate([cv, vn], axis=0).astype(BF16))
        scores.append([_scores(q_all[tok, :], scale_masks[h], kk) for h in range(N_KV)])
    weights = [[_sink_softmax(sc[h], cap, sink_ref[h]) for h in range(N_KV)] for sc in scores]
    for s, tok in enumerate(toks):
        outs = [jnp.dot(e, values[s], preferred_element_type=F32) / denom for e, denom in weights[s]]
        for c in range(GRP):
            o_ref[tok, c * KV_W:(c + 1) * KV_W] = jnp.where(
                first_kv, outs[0][c * DEC_SEQ:(c + 1) * DEC_SEQ], outs[1][c * DEC_SEQ:(c + 1) * DEC_SEQ])


def _attn_sample(q, kv, ck, cv, sink_col, layer):
    sb = SAMPLE_SEQ_BLOCK
    tok = sb * DEC_SEQ
    first = N_PROMPT // tok
    c_in = pl.BlockSpec((None, sb, WINDOW, N_KV, HEAD_DIM), lambda i: (layer, i, 0, 0, 0))
    c_out = pl.BlockSpec((sb, WINDOW, N_KV, HEAD_DIM), lambda i: (i, 0, 0, 0))
    return pl.pallas_call(
        _attn_sample_kernel,
        grid=(DEC_BATCH // sb,),
        in_specs=[pl.BlockSpec((tok, ATTN_W), lambda i: (first + i, 0)), c_in, c_in,
                  pl.BlockSpec((tok, 2 * KV_W), lambda i: (first + i, 0)),
                  pl.BlockSpec((N_KV, GRP * DEC_SEQ, 1), lambda i: (0, 0, 0))],
        out_specs=[pl.BlockSpec((tok, ATTN_W), lambda i: (i, 0)), c_out, c_out],
        out_shape=[jax.ShapeDtypeStruct((N_SAMPLE, ATTN_W), F32),
                   jax.ShapeDtypeStruct((DEC_BATCH, WINDOW, N_KV, HEAD_DIM), F32),
                   jax.ShapeDtypeStruct((DEC_BATCH, WINDOW, N_KV, HEAD_DIM), F32)],
        compiler_params=pltpu.CompilerParams(dimension_semantics=("parallel",)),
        name="attn_sample",
    )(q, ck, cv, kv, sink_col)


def _mix_out(x, y, u, o, d_ref, wglu_ref, gs_ref, ga_ref, wout_s_ref, wout_a_ref):
    y = y + d_ref[...] * u
    y = jax.nn.gelu(y)
    y = y * jax.nn.sigmoid(jnp.dot(y.astype(BF16), wglu_ref[...], preferred_element_type=F32))
    ys = _rms(y, gs_ref[...]).astype(BF16)
    ya = _rms(o, ga_ref[...]).astype(BF16)
    mix = jnp.dot(ys, wout_s_ref[...], preferred_element_type=F32)
    mix = mix + jnp.dot(ya, wout_a_ref[...], preferred_element_type=F32)
    return x + mix


def _tail_kernel(x_ref, yp_ref, ys_ref, u_ref, op_ref, os_ref, pp_ref, ps_ref,
                 d_ref, wglu_ref, gs_ref, ga_ref, wout_s_ref, wout_a_ref,
                 g2_ref, wg_ref, wu_ref, wd_ref, gp_ref, wpg_ref, wpp_ref, gf_ref,
                 *out_refs, final):
    x1 = _mix_out(x_ref[...], _pick(yp_ref, ys_ref), u_ref[...], _pick(op_ref, os_ref),
                  d_ref, wglu_ref, gs_ref, ga_ref, wout_s_ref, wout_a_ref)
    x2 = _half_swiglu(x1, g2_ref, wg_ref, wu_ref, wd_ref)
    h = _rms(x2, gp_ref[...]).astype(BF16)
    gate = jax.nn.sigmoid(jnp.dot(h, wpg_ref[...], preferred_element_type=F32))
    proj = jnp.dot(_pick(pp_ref, ps_ref).astype(BF16), wpp_ref[...], preferred_element_type=F32)
    x3 = x2 + proj * gate
    if not final:
        out_refs[0][...] = x3
        return
    y = _rms(x3, gf_ref[...])
    yp_ref, ys_ref = out_refs

    @pl.when(pl.program_id(0) < PROMPT_TILES)
    def _():
        yp_ref[...] = y

    @pl.when(pl.program_id(0) >= PROMPT_TILES)
    def _():
        ys_ref[...] = y


def _layer_tail(x, y_prompt, y_sample, u, o_prompt, o_sample, p_prompt, p_sample, layer,
                d, wglu, gs, ga, wout_ssm, wout_attn, g2, wg, wu, wd, gp, wpg, wpp, gf, final):
    if final:
        out_specs = _pair_specs(D_MODEL)
        out_shape = [jax.ShapeDtypeStruct((N_PROMPT, D_MODEL), F32), jax.ShapeDtypeStruct((N_SAMPLE, D_MODEL), F32)]
    else:
        out_specs = _row_spec(D_MODEL)
        out_shape = jax.ShapeDtypeStruct((N_TOK, D_MODEL), F32)
    p_specs = [pl.BlockSpec((None, TOKEN_TILE, PLE_DIM), lambda i: (layer, jnp.minimum(i, PROMPT_TILES - 1), 0)),
               pl.BlockSpec((None, TOKEN_TILE, PLE_DIM), lambda i: (layer, jnp.maximum(i - PROMPT_TILES, 0), 0))]
    return pl.pallas_call(
        functools.partial(_tail_kernel, final=final),
        grid=(N_TOK // TOKEN_TILE,),
        in_specs=[_row_spec(D_MODEL)] + _pair_specs(SSM_W) + [_row_spec(SSM_W)] + _pair_specs(ATTN_W) + p_specs + [
            _const_spec((1, SSM_W)), _const_spec((SSM_W, SSM_W)), _const_spec((1, SSM_W)), _const_spec((1, ATTN_W)),
            _const_spec((SSM_W, D_MODEL)), _const_spec((ATTN_W, D_MODEL)),
            _const_spec((1, D_MODEL)),
            _const_spec((D_MODEL, D_FF)), _const_spec((D_MODEL, D_FF)), _const_spec((D_FF, D_MODEL)),
            _const_spec((1, D_MODEL)), _const_spec((D_MODEL, D_MODEL)), _const_spec((PLE_DIM, D_MODEL)),
            _const_spec((1, D_MODEL))],
        out_specs=out_specs,
        out_shape=out_shape,
        compiler_params=pltpu.CompilerParams(dimension_semantics=("arbitrary",), vmem_limit_bytes=VMEM_LIMIT),
        name="layer_tail",
    )(x, y_prompt, y_sample, u, o_prompt, o_sample, p_prompt, p_sample,
      d, wglu, gs, ga, wout_ssm, wout_attn, g2, wg, wu, wd, gp, wpg, wpp, gf)


def _sink_column(sinks, rows_per_head):
    return jnp.repeat(sinks.astype(F32).reshape(N_KV, GRP), rows_per_head, axis=1)[..., None]


def _head_major(a, axis):
    shape = a.shape
    a = a.reshape(shape[:axis] + (N_KV, GRP, HEAD_DIM) + shape[axis + 1:])
    return jnp.swapaxes(a, axis, axis + 1).reshape(shape)


def kernel(x_prompt, x_sample, cache_k, cache_v, state_ssm_re, state_ssm_im, p_prompt, p_sample, ffn1_norm, ffn1_w_gate, ffn1_w_up, ffn1_w_down, mix_norm, w_in, ssm_lam_re, ssm_lam_im, ssm_log_dt, ssm_b_re, ssm_b_im, ssm_c_re, ssm_c_im, ssm_d, ssm_w_glu, ssm_out_norm, attn_sinks, attn_out_norm, w_out, ffn2_norm, ffn2_w_gate, ffn2_w_up, ffn2_w_down, ple_norm, ple_w_gate, ple_w_proj, final_norm):
    wpair, bfull, bhalf, cfull, apw = _ssm_prep(ssm_lam_re, ssm_lam_im, ssm_log_dt, ssm_b_re, ssm_b_im,
                                                ssm_c_re, ssm_c_im)
    row = lambda v: v.reshape(1, -1)
    bf = lambda w: w.astype(BF16)
    xs = (x_prompt.reshape(N_PROMPT, D_MODEL), x_sample.reshape(N_SAMPLE, D_MODEL))
    n_state = SSM_G * SSM_N

    kp_l, vp_l, hpr_l, hpi_l, ks_l, vs_l, hsr_l, hsi_l = [], [], [], [], [], [], [], []
    for i in range(DEPTH):
        x, u, q, kv = _ffn_proj(xs, row(ffn1_norm[i]), bf(ffn1_w_gate[i]), bf(ffn1_w_up[i]), bf(ffn1_w_down[i]),
                                row(mix_norm[i]), bf(w_in[i, :, :SSM_W]),
                                _head_major(bf(w_in[i, :, SSM_W:SSM_W + ATTN_W]), 1), bf(w_in[i, :, SSM_W + ATTN_W:]))

        y_p, hst = _ssm_prompt(u, wpair, bfull, cfull, apw, i)
        hst = hst.reshape(BATCH // SSM_HALF_B, 2, SSM_HALF_B, SSM_G, SSM_N)
        hpr_l.append(hst[:, 0].reshape(BATCH, SSM_G, SSM_N))
        hpi_l.append(hst[:, 1].reshape(BATCH, SSM_G, SSM_N))
        y_s, hsr, hsi = _ssm_sample(u, state_ssm_re[i].reshape(DEC_BATCH, n_state),
                                    state_ssm_im[i].reshape(DEC_BATCH, n_state), wpair, bhalf, cfull, apw, i)
        hsr_l.append(hsr.reshape(DEC_BATCH, SSM_G, SSM_N))
        hsi_l.append(hsi.reshape(DEC_BATCH, SSM_G, SSM_N))

        kv_last = jnp.stack([kv[(b + 1) * SEQ - WINDOW:(b + 1) * SEQ] for b in range(BATCH)])
        kv_last = kv_last.reshape(BATCH, WINDOW, 2, N_KV, HEAD_DIM)
        kp_l.append(kv_last[:, :, 0])
        vp_l.append(kv_last[:, :, 1])
        o_p = _attn_prompt(q, kv, _sink_column(attn_sinks[i], WINDOW).reshape(N_KV, 1, GRP * WINDOW))
        o_s, nk, nv = _attn_sample(q, kv, cache_k, cache_v, _sink_column(attn_sinks[i], DEC_SEQ), i)
        ks_l.append(nk)
        vs_l.append(nv)

        x = _layer_tail(x, y_p, y_s, u, o_p, o_s,
                        p_prompt.reshape(DEPTH, N_PROMPT, PLE_DIM), p_sample.reshape(DEPTH, N_SAMPLE, PLE_DIM), i,
                        row(ssm_d[i]), bf(ssm_w_glu[i]), row(ssm_out_norm[i]), row(_head_major(attn_out_norm[i], 0)),
                        bf(w_out[i, :SSM_W]), _head_major(bf(w_out[i, SSM_W:]), 0),
                        row(ffn2_norm[i]), bf(ffn2_w_gate[i]), bf(ffn2_w_up[i]), bf(ffn2_w_down[i]),
                        row(ple_norm[i]), bf(ple_w_gate[i]), bf(ple_w_proj[i]), row(final_norm),
                        final=(i == DEPTH - 1))
        xs = (x,)

    y_prompt, y_sample = x
    return (y_prompt.reshape(BATCH, SEQ, D_MODEL), y_sample.reshape(DEC_BATCH, DEC_SEQ, D_MODEL),
            jnp.stack(kp_l), jnp.stack(vp_l), jnp.stack(hpr_l), jnp.stack(hpi_l),
            jnp.stack(ks_l), jnp.stack(vs_l), jnp.stack(hsr_l), jnp.stack(hsi_l))
```

```python
import functools

import jax
import jax.numpy as jnp
from jax import lax
from jax.experimental import pallas as pl
from jax.experimental.pallas import tpu as pltpu

F32 = jnp.float32
BF16 = jnp.bfloat16

D_MODEL = 1024
BATCH = 8
SEQ = 2048
DEPTH = 2
DEC_BATCH = 128
DEC_SEQ = 8
SSM_W = 512
SSM_P = 16
SSM_G = 32
SSM_N = 64
ATTN_W = 512
HEAD_DIM = 64
N_HEADS = 8
N_KV = 2
GRP = N_HEADS // N_KV
KV_W = N_KV * HEAD_DIM
IN_W = SSM_W + ATTN_W + 2 * KV_W
WINDOW = 128
D_FF = 2816
PLE_DIM = 256
EPS = 1e-6
NEG_INF = -1e30

N_PROMPT = BATCH * SEQ
N_SAMPLE = DEC_BATCH * DEC_SEQ
N_TOK = N_PROMPT + N_SAMPLE

LANES = 128
SSM_T = 16
N_CHUNK = SEQ // SSM_T
SLAB_G = LANES // SSM_P
N_SLAB = SSM_G // SLAB_G
SLAB_N = SLAB_G * SSM_N
SSM_HALF_B = BATCH // 2

TOKEN_TILE = 512
PROMPT_TILES = N_PROMPT // TOKEN_TILE
FF_CHUNK = 256
FF_STEPS = D_FF // FF_CHUNK
VMEM_LIMIT = 56 * 1024 * 1024


def _const_spec(shape):
    nd = len(shape)
    return pl.BlockSpec(shape, lambda *_: (0,) * nd, pipeline_mode=pl.Buffered(1))


def _row_spec(width):
    return pl.BlockSpec((TOKEN_TILE, width), lambda i: (i, 0))


def _pair_specs(width):
    return [pl.BlockSpec((TOKEN_TILE, width), lambda i: (jnp.minimum(i, PROMPT_TILES - 1), 0)),
            pl.BlockSpec((TOKEN_TILE, width), lambda i: (jnp.maximum(i - PROMPT_TILES, 0), 0))]


def _pick(prompt_ref, sample_ref):
    return jnp.where(pl.program_id(0) < PROMPT_TILES, prompt_ref[...].astype(F32), sample_ref[...].astype(F32))


def _rms(x, g):
    return x * lax.rsqrt(jnp.mean(x * x, axis=-1, keepdims=True) + EPS) * g


def _half_swiglu(xf, g_ref, wg_ref, wu_ref, wd_ref):
    h = _rms(xf, g_ref[...]).astype(BF16)
    acc = jnp.zeros(xf.shape, F32)
    for c in range(0, D_FF, FF_CHUNK):
        gate = jnp.dot(h, wg_ref[:, c:c + FF_CHUNK], preferred_element_type=F32)
        up = jnp.dot(h, wu_ref[:, c:c + FF_CHUNK], preferred_element_type=F32)
        act = (gate * jax.nn.sigmoid(gate) * up).astype(BF16)
        acc = acc + jnp.dot(act, wd_ref[c:c + FF_CHUNK, :], preferred_element_type=F32)
    return xf + 0.5 * acc


def _ffn_proj_kernel(*refs, paired):
    if paired:
        xp_ref, xs_ref, *refs = refs
        x = _pick(xp_ref, xs_ref)
    else:
        x_ref, *refs = refs
        x = x_ref[...]
    (g1_ref, wg_ref, wu_ref, wd_ref, gm_ref, wu_in_ref, wq_in_ref, wkv_in_ref, *cast_in,
     xo_ref, u_ref, q_ref, kv_ref, cg_ref, cu_ref, cd_ref) = refs

    @pl.when(pl.program_id(0) < FF_STEPS)
    def _():
        for src, dst in zip(cast_in, (cg_ref, cu_ref, cd_ref)):
            dst[...] = src[...].astype(BF16)

    x1 = _half_swiglu(x, g1_ref, wg_ref, wu_ref, wd_ref)
    xo_ref[...] = x1
    h = _rms(x1, gm_ref[...]).astype(BF16)
    u_ref[...] = jnp.dot(h, wu_in_ref[...], preferred_element_type=F32)
    q_ref[...] = jnp.dot(h, wq_in_ref[...], preferred_element_type=F32).astype(BF16)
    kv_ref[...] = jnp.dot(h, wkv_in_ref[...], preferred_element_type=F32)


def _ffn_chunk_specs(layer=None):
    chunk = lambda i: jnp.minimum(i, FF_STEPS - 1)
    if layer is None:
        return [pl.BlockSpec((D_MODEL, FF_CHUNK), lambda i: (0, chunk(i))),
                pl.BlockSpec((D_MODEL, FF_CHUNK), lambda i: (0, chunk(i))),
                pl.BlockSpec((FF_CHUNK, D_MODEL), lambda i: (chunk(i), 0))]
    return [pl.BlockSpec((None, D_MODEL, FF_CHUNK), lambda i: (layer, 0, chunk(i))),
            pl.BlockSpec((None, D_MODEL, FF_CHUNK), lambda i: (layer, 0, chunk(i))),
            pl.BlockSpec((None, FF_CHUNK, D_MODEL), lambda i: (layer, chunk(i), 0))]


def _ffn_proj(xs, g1, wg, wu, wd, gm, win_u, win_q, win_kv, cast, layer):
    paired = len(xs) == 2
    x_specs = _pair_specs(D_MODEL) if paired else [_row_spec(D_MODEL)]
    return pl.pallas_call(
        functools.partial(_ffn_proj_kernel, paired=paired),
        grid=(N_TOK // TOKEN_TILE,),
        in_specs=x_specs + [_const_spec((1, D_MODEL)),
                            _const_spec((D_MODEL, D_FF)), _const_spec((D_MODEL, D_FF)), _const_spec((D_FF, D_MODEL)),
                            _const_spec((1, D_MODEL)), _const_spec((D_MODEL, SSM_W)),
                            _const_spec((D_MODEL, ATTN_W)), _const_spec((D_MODEL, 2 * KV_W))]
                 + _ffn_chunk_specs(layer),
        out_specs=[_row_spec(D_MODEL), _row_spec(SSM_W), _row_spec(ATTN_W), _row_spec(2 * KV_W)] + _ffn_chunk_specs(),
        out_shape=[jax.ShapeDtypeStruct((N_TOK, D_MODEL), F32), jax.ShapeDtypeStruct((N_TOK, SSM_W), F32),
                   jax.ShapeDtypeStruct((N_TOK, ATTN_W), BF16), jax.ShapeDtypeStruct((N_TOK, 2 * KV_W), F32),
                   jax.ShapeDtypeStruct((D_MODEL, D_FF), BF16), jax.ShapeDtypeStruct((D_MODEL, D_FF), BF16),
                   jax.ShapeDtypeStruct((D_FF, D_MODEL), BF16)],
        compiler_params=pltpu.CompilerParams(dimension_semantics=("arbitrary",), vmem_limit_bytes=VMEM_LIMIT),
        name="ffn1_proj",
    )(*xs, g1, wg, wu, wd, gm, win_u, win_q, win_kv, *cast)


def _discretise(lam_re, lam_im, log_dt):
    dt = jnp.exp(log_dt)
    mag = jnp.exp(lam_re * dt)
    ang = lam_im * dt
    a_re = mag * jnp.cos(ang)
    a_im = mag * jnp.sin(ang)
    den = lam_re * lam_re + lam_im * lam_im
    nr = a_re - 1.0
    k_re = (nr * lam_re + a_im * lam_im) / den
    k_im = (a_im * lam_re - nr * lam_im) / den
    return a_re, a_im, k_re, k_im


def _powers(a_re, a_im, n):
    p_re = [jnp.ones_like(a_re)]
    p_im = [jnp.zeros_like(a_re)]
    for _ in range(n):
        r, i = p_re[-1], p_im[-1]
        p_re.append(r * a_re - i * a_im)
        p_im.append(r * a_im + i * a_re)
    return p_re, p_im


def _ssm_disc_kernel(lre_ref, lim_ref, ldt_ref, are_ref, aim_ref, kre_ref, kim_ref):
    a_re, a_im, k_re, k_im = _discretise(lre_ref[...], lim_ref[...], ldt_ref[...])
    are_ref[...] = a_re
    aim_ref[...] = a_im
    kre_ref[...] = k_re
    kim_ref[...] = k_im


def _ssm_disc(lam_re, lam_im, log_dt):
    n = DEPTH * SSM_G
    ldt = jnp.broadcast_to(log_dt[..., None], lam_re.shape)
    outs = pl.pallas_call(
        _ssm_disc_kernel,
        out_shape=[jax.ShapeDtypeStruct((n, SSM_N), F32)] * 4,
        name="ssm_disc",
    )(lam_re.reshape(n, SSM_N), lam_im.reshape(n, SSM_N), ldt.reshape(n, SSM_N))
    return [o.reshape(DEPTH, SSM_G, SSM_N) for o in outs]


def _ssm_prep_kernel(ar_re, ar_im, kr_re, kr_im, btr_re, btr_im, cr_re, cr_im, af_re, af_im,
                     wpair_ref, bfull_ref, bhalf_ref, cfull_ref, apw_ref):
    a_re, a_im, k_re, k_im = ar_re[...], ar_im[...], kr_re[...], kr_im[...]
    bb_re = k_re * btr_re[...] - k_im * btr_im[...]
    bb_im = k_re * btr_im[...] + k_im * btr_re[...]
    p_re, p_im = _powers(a_re, a_im, SSM_T)
    c_re, c_im = cr_re[...], cr_im[...]
    ca = [jnp.concatenate([c_re * p_re[t] - c_im * p_im[t], c_re * p_im[t] + c_im * p_re[t]], axis=1)
          for t in range(SSM_T + 1)]
    ca_all = jnp.concatenate(ca[:SSM_T], axis=0)
    bb2 = jnp.concatenate([bb_re, -bb_im], axis=1)
    drow = lax.dot_general(bb2, ca_all, (((1,), (1,)), ((), ())),
                           precision=lax.Precision.HIGHEST, preferred_element_type=F32)
    row_g = lax.broadcasted_iota(jnp.int32, drow.shape, 0) // SSM_P
    col_g = (lax.broadcasted_iota(jnp.int32, drow.shape, 1) % LANES) // SSM_P
    drow = jnp.where(row_g == col_g, drow, 0.0).astype(BF16)
    wpair_ref[0:LANES, :] = drow
    wpair_ref[LANES:, 0:LANES] = jnp.zeros((LANES, LANES), BF16)
    wpair_ref[LANES:, LANES:] = drow[:, :(SSM_T - 1) * LANES]

    row_grp = lax.broadcasted_iota(jnp.int32, (LANES, LANES), 0) // SSM_P
    lane_half = lax.broadcasted_iota(jnp.int32, (LANES, LANES), 1) // SSM_N
    for s in range(SSM_T):
        w_re, w_im = p_re[SSM_T - 1 - s], p_im[SSM_T - 1 - s]
        parts = [bb_re * w_re - bb_im * w_im, bb_re * w_im + bb_im * w_re]
        parts = [jnp.concatenate([v, v], axis=1) for v in parts]
        blk = jnp.concatenate([jnp.where(row_grp == 2 * k + lane_half, v, 0.0)
                               for v in parts for k in range(SLAB_N // LANES)], axis=1)
        bfull_ref[s * LANES:(s + 1) * LANES, :] = blk.astype(BF16)
        if s >= SSM_T - DEC_SEQ:
            bhalf_ref[(s - SSM_T + DEC_SEQ) * LANES:(s - SSM_T + DEC_SEQ + 1) * LANES, :] = blk

    lane_grp = lax.broadcasted_iota(jnp.int32, (SSM_N, LANES), 1) // SSM_P
    for t in range(SSM_T):
        cat = ca[t + 1].T
        ca_re, ca_im = cat[:SSM_N], cat[SSM_N:]
        cols = slice(t * LANES, (t + 1) * LANES)
        for g in range(SLAB_G):
            cfull_ref[g * SSM_N:(g + 1) * SSM_N, cols] = jnp.where(lane_grp == g, ca_re, 0.0).astype(BF16)
            cfull_ref[SLAB_N + g * SSM_N:SLAB_N + (g + 1) * SSM_N, cols] = (
                jnp.where(lane_grp == g, -ca_im, 0.0).astype(BF16))

    a_re, a_im = af_re[...], af_im[...]
    sq = []
    for _ in range(4):
        a_re, a_im = a_re * a_re - a_im * a_im, 2.0 * a_re * a_im
        sq.append((a_re, a_im))
    apw_ref[0:1, :] = sq[2][0]
    apw_ref[1:2, :] = sq[2][1]
    apw_ref[2:3, :] = sq[3][0]
    apw_ref[3:4, :] = sq[3][1]
    apw_ref[4:8, :] = jnp.zeros((4, SLAB_N), F32)


def _ssm_prep(lam_re, lam_im, log_dt, b_re, b_im, c_re, c_im):
    n = DEPTH * N_SLAB
    a_re, a_im, k_re, k_im = _ssm_disc(lam_re, lam_im, log_dt)
    rows = lambda v: jnp.repeat(v.reshape(n, SLAB_G, 1, SSM_N), SSM_P, axis=2).reshape(n, LANES, SSM_N)
    bt = lambda v: v.transpose(0, 1, 3, 2).reshape(n, LANES, SSM_N)
    flat = lambda v: v.reshape(n, 1, SLAB_N)
    args = ([rows(v) for v in (a_re, a_im, k_re, k_im)]
            + [bt(b_re), bt(b_im), c_re.reshape(n, LANES, SSM_N), c_im.reshape(n, LANES, SSM_N)]
            + [flat(a_re), flat(a_im)])

    def blk(a):
        return pl.BlockSpec((None,) + a.shape[1:], lambda i: (i, 0, 0))

    out_shape = [jax.ShapeDtypeStruct((n, 2 * LANES, SSM_T * LANES), BF16),
                 jax.ShapeDtypeStruct((n, SSM_T * LANES, 2 * SLAB_N), BF16),
                 jax.ShapeDtypeStruct((n, DEC_SEQ * LANES, 2 * SLAB_N), F32),
                 jax.ShapeDtypeStruct((n, 2 * SLAB_N, SSM_T * LANES), BF16),
                 jax.ShapeDtypeStruct((n, 8, SLAB_N), F32)]
    return pl.pallas_call(
        _ssm_prep_kernel,
        grid=(n,),
        in_specs=[blk(a) for a in args],
        out_specs=[blk(s) for s in out_shape],
        out_shape=out_shape,
        compiler_params=pltpu.CompilerParams(dimension_semantics=("parallel",), vmem_limit_bytes=VMEM_LIMIT),
        name="ssm_prep",
    )(*args)


def _ssm_prompt_kernel(u_ref, wpair_ref, bfull_ref, cfull_ref, apw_ref, y_ref, hst_ref,
                       ub_ref, s_ref, yacc_ref, *col_refs):
    rows = SSM_HALF_B * N_CHUNK
    n_col = SLAB_N // LANES
    sp_refs, hp_refs = col_refs[:n_col], col_refs[n_col:]
    for t in range(SSM_T):
        ub_ref[:, t * LANES:(t + 1) * LANES] = u_ref[pl.ds(t, rows, stride=SSM_T), :].astype(BF16)
    s_ref[...] = jnp.dot(ub_ref[...], bfull_ref[...], preferred_element_type=F32)
    for b in range(SSM_HALF_B):
        seq = slice(b * N_CHUNK, (b + 1) * N_CHUNK)
        for k in range(n_col):
            sp_refs[k][pl.ds(b, N_CHUNK, stride=8), :] = s_ref[seq, k * LANES:(k + 1) * LANES]
            sp_refs[k][pl.ds(SSM_HALF_B + b, N_CHUNK, stride=8), :] = (
                s_ref[seq, SLAB_N + k * LANES:SLAB_N + (k + 1) * LANES])
    upper = lax.broadcasted_iota(jnp.int32, (8, LANES), 0) < SSM_HALF_B
    a1 = [jnp.broadcast_to(apw_ref[2:3, k * LANES:(k + 1) * LANES], (8, LANES)) for k in range(n_col)]
    a2 = [jnp.where(upper, -1.0, 1.0) * jnp.broadcast_to(apw_ref[3:4, k * LANES:(k + 1) * LANES], (8, LANES))
          for k in range(n_col)]

    def step(c, h):
        r0 = pl.multiple_of(c * 8, 8)
        new = []
        for k in range(n_col):
            hp_refs[k][pl.ds(r0, 8), :] = h[k]
            new.append(a1[k] * h[k] + a2[k] * pltpu.roll(h[k], SSM_HALF_B, axis=0) + sp_refs[k][pl.ds(r0, 8), :])
        return tuple(new)

    h = lax.fori_loop(0, N_CHUNK, step, tuple(jnp.zeros((8, LANES), F32) for _ in range(n_col)), unroll=8)
    hst_ref[...] = jnp.concatenate(h, axis=1)
    hbc = jnp.concatenate(
        [jnp.concatenate([hp_refs[k][pl.ds(part + b, N_CHUNK, stride=8), :]
                          for part in (0, SSM_HALF_B) for k in range(n_col)], axis=1)
         for b in range(SSM_HALF_B)], axis=0)
    yacc_ref[...] = jnp.dot(hbc.astype(BF16), cfull_ref[...], preferred_element_type=F32)
    for t in range(0, SSM_T, 2):
        yacc_ref[:, t * LANES:] += jnp.dot(ub_ref[:, t * LANES:(t + 2) * LANES],
                                           wpair_ref[:, 0:(SSM_T - t) * LANES], preferred_element_type=F32)
    for t in range(SSM_T):
        y_ref[pl.ds(t, rows, stride=SSM_T), :] = yacc_ref[:, t * LANES:(t + 1) * LANES]


def _ssm_prompt(u, wpair, bfull, cfull, apw, layer):
    rows = SSM_HALF_B * N_CHUNK
    tok = SSM_HALF_B * SEQ

    def wspec(a):
        return pl.BlockSpec((None,) + a.shape[1:], lambda j, h: (layer * N_SLAB + j, 0, 0))

    return pl.pallas_call(
        _ssm_prompt_kernel,
        grid=(N_SLAB, BATCH // SSM_HALF_B),
        in_specs=[pl.BlockSpec((tok, LANES), lambda j, h: (h, j)), wspec(wpair), wspec(bfull), wspec(cfull), wspec(apw)],
        out_specs=[pl.BlockSpec((tok, LANES), lambda j, h: (h, j)),
                   pl.BlockSpec((None, 8, SLAB_N), lambda j, h: (h, 0, j))],
        out_shape=[jax.ShapeDtypeStruct((N_PROMPT, SSM_W), F32),
                   jax.ShapeDtypeStruct((BATCH // SSM_HALF_B, 8, SSM_G * SSM_N), F32)],
        scratch_shapes=[pltpu.VMEM((rows, SSM_T * LANES), BF16), pltpu.VMEM((rows, 2 * SLAB_N), F32),
                        pltpu.VMEM((rows, SSM_T * LANES), F32)]
                       + [pltpu.VMEM((N_CHUNK * 8, LANES), F32)] * (2 * SLAB_N // LANES),
        compiler_params=pltpu.CompilerParams(dimension_semantics=("arbitrary", "arbitrary"),
                                             vmem_limit_bytes=VMEM_LIMIT),
        name="ssm_prompt",
    )(u, wpair, bfull, cfull, apw)


def _ssm_sample_kernel(u_ref, h0r_ref, h0i_ref, wpair_ref, bhalf_ref, cfull_ref, apw_ref,
                       y_ref, hr_ref, hi_ref, uf_ref, yacc_ref):
    for t in range(DEC_SEQ):
        uf_ref[:, t * LANES:(t + 1) * LANES] = u_ref[pl.ds(t, DEC_BATCH, stride=DEC_SEQ), :]
    uf = uf_ref[...]
    h0r = h0r_ref[...]
    h0i = h0i_ref[...]
    ar = apw_ref[0:1, :]
    ai = apw_ref[1:2, :]
    s = jnp.dot(uf, bhalf_ref[...], precision=lax.Precision.HIGHEST, preferred_element_type=F32)
    hr_ref[...] = ar * h0r - ai * h0i + s[:, :SLAB_N]
    hi_ref[...] = ar * h0i + ai * h0r + s[:, SLAB_N:]
    ub = uf.astype(BF16)
    h0 = jnp.concatenate([h0r, h0i], axis=1).astype(BF16)
    yacc_ref[...] = jnp.dot(h0, cfull_ref[...], preferred_element_type=F32)
    for t in range(0, DEC_SEQ, 2):
        yacc_ref[:, t * LANES:] += jnp.dot(ub[:, t * LANES:(t + 2) * LANES],
                                           wpair_ref[:, 0:(DEC_SEQ - t) * LANES], preferred_element_type=F32)
    for t in range(DEC_SEQ):
        y_ref[pl.ds(t, DEC_BATCH, stride=DEC_SEQ), :] = yacc_ref[:, t * LANES:(t + 1) * LANES]


def _ssm_sample(u, h0re, h0im, wpair, bhalf, cfull, apw, layer):
    width = DEC_SEQ * LANES

    def wspec(r, c):
        return pl.BlockSpec((None, r, c), lambda j: (layer * N_SLAB + j, 0, 0))

    st = pl.BlockSpec((DEC_BATCH, SLAB_N), lambda j: (0, j))
    return pl.pallas_call(
        _ssm_sample_kernel,
        grid=(N_SLAB,),
        in_specs=[pl.BlockSpec((N_SAMPLE, LANES), lambda j: (N_PROMPT // N_SAMPLE, j)), st, st,
                  wspec(2 * LANES, width), wspec(width, 2 * SLAB_N), wspec(2 * SLAB_N, width), wspec(8, SLAB_N)],
        out_specs=[pl.BlockSpec((N_SAMPLE, LANES), lambda j: (0, j)), st, st],
        out_shape=[jax.ShapeDtypeStruct((N_SAMPLE, SSM_W), F32),
                   jax.ShapeDtypeStruct((DEC_BATCH, SSM_G * SSM_N), F32),
                   jax.ShapeDtypeStruct((DEC_BATCH, SSM_G * SSM_N), F32)],
        scratch_shapes=[pltpu.VMEM((DEC_BATCH, width), F32), pltpu.VMEM((DEC_BATCH, width), F32)],
        compiler_params=pltpu.CompilerParams(dimension_semantics=("arbitrary",), vmem_limit_bytes=VMEM_LIMIT),
        name="ssm_sample",
    )(u, h0re, h0im, wpair, bhalf, cfull, apw)


def _mask_cap(valid):
    return jnp.where(valid, jnp.inf, NEG_INF).astype(F32)


def _sink_softmax(s, cap, sink):
    s = jnp.minimum(s, cap)
    m = jnp.maximum(jnp.max(s, axis=-1, keepdims=True), sink)
    e = jnp.exp(s - m)
    denom = jnp.sum(e, axis=-1, keepdims=True) + jnp.exp(sink - m)
    return e.astype(BF16), denom


def _scores(q, scale_mask, kk):
    qx = jnp.concatenate([q[:, c * KV_W:(c + 1) * KV_W] * scale_mask for c in range(GRP)], axis=0)
    return lax.dot_general(qx.astype(BF16), kk, (((1,), (1,)), ((), ())), preferred_element_type=F32)


def _band_valid(n_q, n_k, rows):
    qi = lax.broadcasted_iota(jnp.int32, (rows, n_k), 0) & (n_q - 1)
    kj = lax.broadcasted_iota(jnp.int32, (rows, n_k), 1)
    return kj, ((kj - qi - 1).astype(jnp.uint32) < WINDOW)


def _head_masks():
    lane_kv = lax.broadcasted_iota(jnp.int32, (1, KV_W), 1) // HEAD_DIM
    return [jnp.where(lane_kv == h, HEAD_DIM ** -0.5, 0.0) for h in range(N_KV)], lane_kv == 0


ATTN_Q_TILE = 512


def _attn_prompt_kernel(q_ref, kvp_ref, kvc_ref, sink_ref, o_ref):
    cols = GRP * WINDOW
    scale_masks, _ = _head_masks()
    kj = lax.broadcasted_iota(jnp.int32, (2 * WINDOW, cols), 0)
    qi = lax.broadcasted_iota(jnp.int32, (2 * WINDOW, cols), 1) & (WINDOW - 1)
    valid = (kj - qi - 1).astype(jnp.uint32) < WINDOW
    cap_first = _mask_cap(valid & ((kj >= WINDOW) | (pl.program_id(1) > 0)))
    cap = _mask_cap(valid)
    n_blk = ATTN_Q_TILE // WINDOW

    def stage(r):
        cur = kvc_ref[r * WINDOW:(r + 1) * WINDOW, :]
        prev = kvp_ref[...] if r == 0 else kvc_ref[(r - 1) * WINDOW:r * WINDOW, :]
        kk = jnp.concatenate([prev[:, :KV_W], cur[:, :KV_W]], axis=0).astype(BF16)
        v_t = jnp.concatenate([prev[:, KV_W:], cur[:, KV_W:]], axis=0).T.astype(BF16)
        q = q_ref[r * WINDOW:(r + 1) * WINDOW, :].astype(F32)
        s_t = []
        for h in range(N_KV):
            qx = jnp.concatenate([q[:, c * KV_W:(c + 1) * KV_W] * scale_masks[h] for c in range(GRP)], axis=0)
            s_t.append(lax.dot_general(kk, qx.astype(BF16), (((1,), (1,)), ((), ())), preferred_element_type=F32))
        return s_t, v_t

    def softmax_t(s, cap_r, sink):
        s = jnp.minimum(s, cap_r)
        m = jnp.maximum(jnp.max(s, axis=0, keepdims=True), sink)
        e = jnp.exp(s - m)
        return e.astype(BF16), jnp.sum(e, axis=0, keepdims=True) + jnp.exp(sink - m)

    staged = [stage(r) for r in range(n_blk)]
    weights = [[softmax_t(staged[r][0][h], cap_first if r == 0 else cap, sink_ref[h]) for h in range(N_KV)]
               for r in range(n_blk)]
    for r in range(n_blk):
        v_t = staged[r][1]
        outs = [jnp.dot(v_t, e, preferred_element_type=F32) / denom for e, denom in weights[r]]
        o_t = jnp.concatenate([outs[h][h * HEAD_DIM:(h + 1) * HEAD_DIM] for h in range(N_KV)], axis=0)
        for c in range(GRP):
            o_ref[r * WINDOW:(r + 1) * WINDOW, c * KV_W:(c + 1) * KV_W] = (
                o_t[:, c * WINDOW:(c + 1) * WINDOW].T.astype(o_ref.dtype))


def _attn_prompt(q, kv, sink_col):
    tiles = SEQ // ATTN_Q_TILE
    per_tile = ATTN_Q_TILE // WINDOW
    blocks = SEQ // WINDOW
    return pl.pallas_call(
        _attn_prompt_kernel,
        grid=(BATCH, tiles),
        in_specs=[pl.BlockSpec((ATTN_Q_TILE, ATTN_W), lambda b, j: (b * tiles + j, 0)),
                  pl.BlockSpec((WINDOW, 2 * KV_W), lambda b, j: (b * blocks + jnp.maximum(j * per_tile - 1, 0), 0)),
                  pl.BlockSpec((ATTN_Q_TILE, 2 * KV_W), lambda b, j: (b * tiles + j, 0)),
                  pl.BlockSpec((N_KV, 1, GRP * WINDOW), lambda b, j: (0, 0, 0))],
        out_specs=pl.BlockSpec((ATTN_Q_TILE, ATTN_W), lambda b, j: (b * tiles + j, 0)),
        out_shape=jax.ShapeDtypeStruct((N_PROMPT, ATTN_W), BF16),
        compiler_params=pltpu.CompilerParams(dimension_semantics=("parallel", "arbitrary")),
        name="attn_prompt",
    )(q, kv, kv, sink_col)


SAMPLE_SEQ_BLOCK = 8


def _attn_sample_kernel(q_ref, ck_ref, cv_ref, kvn_ref, sink_ref, *rest, n_carried):
    o_ref, nk_ref, nv_ref = rest[n_carried:]
    rows = GRP * DEC_SEQ
    scale_masks, first_kv = _head_masks()
    cap = _mask_cap(_band_valid(DEC_SEQ, WINDOW + DEC_SEQ, rows)[1])
    q_all = q_ref[...].astype(F32)
    toks = [slice(s * DEC_SEQ, (s + 1) * DEC_SEQ) for s in range(SAMPLE_SEQ_BLOCK)]
    scores, values = [], []
    for s, tok in enumerate(toks):
        ck = ck_ref[s]
        cv = cv_ref[s]
        kn = kvn_ref[tok, :KV_W]
        vn = kvn_ref[tok, KV_W:]
        nk_ref[s, 0:WINDOW - DEC_SEQ, :] = ck[DEC_SEQ:, :]
        nk_ref[s, WINDOW - DEC_SEQ:WINDOW, :] = kn
        nv_ref[s, 0:WINDOW - DEC_SEQ, :] = cv[DEC_SEQ:, :]
        nv_ref[s, WINDOW - DEC_SEQ:WINDOW, :] = vn
        kk = jnp.concatenate([ck, kn], axis=0).astype(BF16)
        values.append(jnp.concatenate([cv, vn], axis=0).astype(BF16))
        scores.append([_scores(q_all[tok, :], scale_masks[h], kk) for h in range(N_KV)])
    weights = [[_sink_softmax(sc[h], cap, sink_ref[h]) for h in range(N_KV)] for sc in scores]
    for s, tok in enumerate(toks):
        outs = [jnp.dot(e, values[s], preferred_element_type=F32) / denom for e, denom in weights[s]]
        for c in range(GRP):
            o_ref[tok, c * KV_W:(c + 1) * KV_W] = jnp.where(
                first_kv, outs[0][c * DEC_SEQ:(c + 1) * DEC_SEQ], outs[1][c * DEC_SEQ:(c + 1) * DEC_SEQ])


def _attn_sample(q, kv, ck, cv, sink_col, layer, windows):
    sb = SAMPLE_SEQ_BLOCK
    tok = sb * DEC_SEQ
    first = N_PROMPT // tok
    c_spec = pl.BlockSpec((None, sb, WINDOW, KV_W), lambda i: (layer, i, 0, 0))
    in_specs = [pl.BlockSpec((tok, ATTN_W), lambda i: (first + i, 0)), c_spec, c_spec,
                pl.BlockSpec((tok, 2 * KV_W), lambda i: (first + i, 0)),
                pl.BlockSpec((N_KV, GRP * DEC_SEQ, 1), lambda i: (0, 0, 0))]
    args = [q, ck, cv, kv, sink_col]
    aliases = {}
    if windows is not None:
        aliases = {len(args): 1, len(args) + 1: 2}
        in_specs += [pl.BlockSpec(memory_space=pl.ANY)] * 2
        args += list(windows)
    return pl.pallas_call(
        functools.partial(_attn_sample_kernel, n_carried=len(aliases)),
        grid=(DEC_BATCH // sb,),
        in_specs=in_specs,
        out_specs=[pl.BlockSpec((tok, ATTN_W), lambda i: (i, 0)), c_spec, c_spec],
        out_shape=[jax.ShapeDtypeStruct((N_SAMPLE, ATTN_W), F32),
                   jax.ShapeDtypeStruct((DEPTH, DEC_BATCH, WINDOW, KV_W), F32),
                   jax.ShapeDtypeStruct((DEPTH, DEC_BATCH, WINDOW, KV_W), F32)],
        input_output_aliases=aliases,
        compiler_params=pltpu.CompilerParams(dimension_semantics=("parallel",)),
        name="attn_sample",
    )(*args)


def _mix_out(x, y, u, o, d_ref, wglu_ref, gs_ref, ga_ref, wout_s_ref, wout_a_ref):
    y = y + d_ref[...] * u
    y = jax.nn.gelu(y)
    y = y * jax.nn.sigmoid(jnp.dot(y.astype(BF16), wglu_ref[...], preferred_element_type=F32))
    ys = _rms(y, gs_ref[...]).astype(BF16)
    ya = _rms(o, ga_ref[...]).astype(BF16)
    mix = jnp.dot(ys, wout_s_ref[...], preferred_element_type=F32)
    mix = mix + jnp.dot(ya, wout_a_ref[...], preferred_element_type=F32)
    return x + mix


def _tail_kernel(x_ref, yp_ref, ys_ref, u_ref, op_ref, os_ref, pp_ref, ps_ref,
                 d_ref, wglu_ref, gs_ref, ga_ref, wout_s_ref, wout_a_ref,
                 g2_ref, wg_ref, wu_ref, wd_ref, gp_ref, wpg_ref, wpp_ref, gf_ref,
                 *out_refs, final):
    x1 = _mix_out(x_ref[...], _pick(yp_ref, ys_ref), u_ref[...], _pick(op_ref, os_ref),
                  d_ref, wglu_ref, gs_ref, ga_ref, wout_s_ref, wout_a_ref)
    x2 = _half_swiglu(x1, g2_ref, wg_ref, wu_ref, wd_ref)
    h = _rms(x2, gp_ref[...]).astype(BF16)
    gate = jax.nn.sigmoid(jnp.dot(h, wpg_ref[...], preferred_element_type=F32))
    proj = jnp.dot(_pick(pp_ref, ps_ref).astype(BF16), wpp_ref[...], preferred_element_type=F32)
    x3 = x2 + proj * gate
    if not final:
        out_refs[0][...] = x3
        return
    y = _rms(x3, gf_ref[...])
    yp_ref, ys_ref = out_refs

    @pl.when(pl.program_id(0) < PROMPT_TILES)
    def _():
        yp_ref[...] = y

    @pl.when(pl.program_id(0) >= PROMPT_TILES)
    def _():
        ys_ref[...] = y


def _layer_tail(x, y_prompt, y_sample, u, o_prompt, o_sample, p_prompt, p_sample, layer,
                d, wglu, gs, ga, wout_ssm, wout_attn, g2, wg, wu, wd, gp, wpg, wpp, gf, final):
    if final:
        out_specs = _pair_specs(D_MODEL)
        out_shape = [jax.ShapeDtypeStruct((N_PROMPT, D_MODEL), F32), jax.ShapeDtypeStruct((N_SAMPLE, D_MODEL), F32)]
    else:
        out_specs = _row_spec(D_MODEL)
        out_shape = jax.ShapeDtypeStruct((N_TOK, D_MODEL), F32)
    p_specs = [pl.BlockSpec((None, TOKEN_TILE, PLE_DIM), lambda i: (layer, jnp.minimum(i, PROMPT_TILES - 1), 0)),
               pl.BlockSpec((None, TOKEN_TILE, PLE_DIM), lambda i: (layer, jnp.maximum(i - PROMPT_TILES, 0), 0))]
    return pl.pallas_call(
        functools.partial(_tail_kernel, final=final),
        grid=(N_TOK // TOKEN_TILE,),
        in_specs=[_row_spec(D_MODEL)] + _pair_specs(SSM_W) + [_row_spec(SSM_W)] + _pair_specs(ATTN_W) + p_specs + [
            _const_spec((1, SSM_W)), _const_spec((SSM_W, SSM_W)), _const_spec((1, SSM_W)), _const_spec((1, ATTN_W)),
            _const_spec((SSM_W, D_MODEL)), _const_spec((ATTN_W, D_MODEL)),
            _const_spec((1, D_MODEL)),
            _const_spec((D_MODEL, D_FF)), _const_spec((D_MODEL, D_FF)), _const_spec((D_FF, D_MODEL)),
            _const_spec((1, D_MODEL)), _const_spec((D_MODEL, D_MODEL)), _const_spec((PLE_DIM, D_MODEL)),
            _const_spec((1, D_MODEL))],
        out_specs=out_specs,
        out_shape=out_shape,
        compiler_params=pltpu.CompilerParams(dimension_semantics=("arbitrary",), vmem_limit_bytes=VMEM_LIMIT),
        name="layer_tail",
    )(x, y_prompt, y_sample, u, o_prompt, o_sample, p_prompt, p_sample,
      d, wglu, gs, ga, wout_ssm, wout_attn, g2, wg, wu, wd, gp, wpg, wpp, gf)


def _sink_column(sinks, rows_per_head):
    return jnp.repeat(sinks.astype(F32).reshape(N_KV, GRP), rows_per_head, axis=1)[..., None]


def _head_major(a, axis):
    shape = a.shape
    a = a.reshape(shape[:axis] + (N_KV, GRP, HEAD_DIM) + shape[axis + 1:])
    return jnp.swapaxes(a, axis, axis + 1).reshape(shape)


def kernel(x_prompt, x_sample, cache_k, cache_v, state_ssm_re, state_ssm_im, p_prompt, p_sample, ffn1_norm, ffn1_w_gate, ffn1_w_up, ffn1_w_down, mix_norm, w_in, ssm_lam_re, ssm_lam_im, ssm_log_dt, ssm_b_re, ssm_b_im, ssm_c_re, ssm_c_im, ssm_d, ssm_w_glu, ssm_out_norm, attn_sinks, attn_out_norm, w_out, ffn2_norm, ffn2_w_gate, ffn2_w_up, ffn2_w_down, ple_norm, ple_w_gate, ple_w_proj, final_norm):
    wpair, bfull, bhalf, cfull, apw = _ssm_prep(ssm_lam_re, ssm_lam_im, ssm_log_dt, ssm_b_re, ssm_b_im,
                                                ssm_c_re, ssm_c_im)
    row = lambda v: v.reshape(1, -1)
    bf = lambda w: w.astype(BF16)
    xs = (x_prompt.reshape(N_PROMPT, D_MODEL), x_sample.reshape(N_SAMPLE, D_MODEL))
    n_state = SSM_G * SSM_N

    caches = [c.reshape(DEPTH, DEC_BATCH, WINDOW, KV_W) for c in (cache_k, cache_v)]
    windows = None

    kp_l, vp_l, hpr_l, hpi_l, hsr_l, hsi_l = [], [], [], [], [], []
    for i in range(DEPTH):
        x, u, q, kv, *ffn2_w = _ffn_proj(
            xs, row(ffn1_norm[i]), bf(ffn1_w_gate[i]), bf(ffn1_w_up[i]), bf(ffn1_w_down[i]),
            row(mix_norm[i]), bf(w_in[i, :, :SSM_W]),
            _head_major(bf(w_in[i, :, SSM_W:SSM_W + ATTN_W]), 1), bf(w_in[i, :, SSM_W + ATTN_W:]),
            cast=(ffn2_w_gate, ffn2_w_up, ffn2_w_down), layer=i)

        y_p, hst = _ssm_prompt(u, wpair, bfull, cfull, apw, i)
        hst = hst.reshape(BATCH // SSM_HALF_B, 2, SSM_HALF_B, SSM_G, SSM_N)
        hpr_l.append(hst[:, 0].reshape(BATCH, SSM_G, SSM_N))
        hpi_l.append(hst[:, 1].reshape(BATCH, SSM_G, SSM_N))
        y_s, hsr, hsi = _ssm_sample(u, state_ssm_re[i].reshape(DEC_BATCH, n_state),
                                    state_ssm_im[i].reshape(DEC_BATCH, n_state), wpair, bhalf, cfull, apw, i)
        hsr_l.append(hsr.reshape(DEC_BATCH, SSM_G, SSM_N))
        hsi_l.append(hsi.reshape(DEC_BATCH, SSM_G, SSM_N))

        kv_last = jnp.stack([kv[(b + 1) * SEQ - WINDOW:(b + 1) * SEQ] for b in range(BATCH)])
        kv_last = kv_last.reshape(BATCH, WINDOW, 2, N_KV, HEAD_DIM)
        kp_l.append(kv_last[:, :, 0])
        vp_l.append(kv_last[:, :, 1])
        o_p = _attn_prompt(q, kv, _sink_column(attn_sinks[i], WINDOW).reshape(N_KV, 1, GRP * WINDOW))
        o_s, *windows = _attn_sample(q, kv, caches[0], caches[1], _sink_column(attn_sinks[i], DEC_SEQ), i, windows)

        x = _layer_tail(x, y_p, y_s, u, o_p, o_s,
                        p_prompt.reshape(DEPTH, N_PROMPT, PLE_DIM), p_sample.reshape(DEPTH, N_SAMPLE, PLE_DIM), i,
                        row(ssm_d[i]), bf(ssm_w_glu[i]), row(ssm_out_norm[i]), row(_head_major(attn_out_norm[i], 0)),
                        bf(w_out[i, :SSM_W]), _head_major(bf(w_out[i, SSM_W:]), 0),
                        row(ffn2_norm[i]), *ffn2_w,
                        row(ple_norm[i]), bf(ple_w_gate[i]), bf(ple_w_proj[i]), row(final_norm),
                        final=(i == DEPTH - 1))
        xs = (x,)

    y_prompt, y_sample = x
    ks, vs = (w.reshape(DEPTH, DEC_BATCH, WINDOW, N_KV, HEAD_DIM) for w in windows)
    return (y_prompt.reshape(BATCH, SEQ, D_MODEL), y_sample.reshape(DEC_BATCH, DEC_SEQ, D_MODEL),
            jnp.stack(kp_l), jnp.stack(vp_l), jnp.stack(hpr_l), jnp.stack(hpi_l),
            ks, vs, jnp.stack(hsr_l), jnp.stack(hsi_l))
```

```python
import functools

import jax
import jax.numpy as jnp
from jax import lax
from jax.experimental import pallas as pl
from jax.experimental.pallas import tpu as pltpu

F32 = jnp.float32
BF16 = jnp.bfloat16

D_MODEL = 1024
BATCH = 8
SEQ = 2048
DEPTH = 2
DEC_BATCH = 128
DEC_SEQ = 8
SSM_W = 512
SSM_P = 16
SSM_G = 32
SSM_N = 64
ATTN_W = 512
HEAD_DIM = 64
N_HEADS = 8
N_KV = 2
GRP = N_HEADS // N_KV
KV_W = N_KV * HEAD_DIM
IN_W = SSM_W + ATTN_W + 2 * KV_W
WINDOW = 128
D_FF = 2816
PLE_DIM = 256
EPS = 1e-6
NEG_INF = -1e30

N_PROMPT = BATCH * SEQ
N_SAMPLE = DEC_BATCH * DEC_SEQ
N_TOK = N_PROMPT + N_SAMPLE

LANES = 128
SSM_T = 16
N_CHUNK = SEQ // SSM_T
SLAB_G = LANES // SSM_P
N_SLAB = SSM_G // SLAB_G
SLAB_N = SLAB_G * SSM_N
SSM_HALF_B = BATCH // 2

TOKEN_TILE = 512
PROMPT_TILES = N_PROMPT // TOKEN_TILE
FF_CHUNK = 256
FF_STEPS = D_FF // FF_CHUNK
VMEM_LIMIT = 56 * 1024 * 1024


def _const_spec(shape):
    nd = len(shape)
    return pl.BlockSpec(shape, lambda *_: (0,) * nd, pipeline_mode=pl.Buffered(1))


def _row_spec(width):
    return pl.BlockSpec((TOKEN_TILE, width), lambda i: (i, 0))


def _pair_specs(width):
    return [pl.BlockSpec((TOKEN_TILE, width), lambda i: (jnp.minimum(i, PROMPT_TILES - 1), 0)),
            pl.BlockSpec((TOKEN_TILE, width), lambda i: (jnp.maximum(i - PROMPT_TILES, 0), 0))]


def _pick(prompt_ref, sample_ref):
    return jnp.where(pl.program_id(0) < PROMPT_TILES, prompt_ref[...].astype(F32), sample_ref[...].astype(F32))


def _rms(x, g):
    return x * lax.rsqrt(jnp.mean(x * x, axis=-1, keepdims=True) + EPS) * g


def _half_swiglu(xf, g_ref, wg_ref, wu_ref, wd_ref):
    h = _rms(xf, g_ref[...]).astype(BF16)
    acc = jnp.zeros(xf.shape, F32)
    for c in range(0, D_FF, FF_CHUNK):
        gate = jnp.dot(h, wg_ref[:, c:c + FF_CHUNK], preferred_element_type=F32)
        up = jnp.dot(h, wu_ref[:, c:c + FF_CHUNK], preferred_element_type=F32)
        act = (gate * jax.nn.sigmoid(gate) * up).astype(BF16)
        acc = acc + jnp.dot(act, wd_ref[c:c + FF_CHUNK, :], preferred_element_type=F32)
    return xf + 0.5 * acc


def _ffn_proj_kernel(*refs, paired):
    if paired:
        xp_ref, xs_ref, *refs = refs
        x = _pick(xp_ref, xs_ref)
    else:
        x_ref, *refs = refs
        x = x_ref[...]
    (g1_ref, wg_ref, wu_ref, wd_ref, gm_ref, wu_in_ref, wq_in_ref, wkv_in_ref, *cast_in,
     xo_ref, u_ref, q_ref, kv_ref, cg_ref, cu_ref, cd_ref) = refs

    @pl.when(pl.program_id(0) < FF_STEPS)
    def _():
        for src, dst in zip(cast_in, (cg_ref, cu_ref, cd_ref)):
            dst[...] = src[...].astype(BF16)

    x1 = _half_swiglu(x, g1_ref, wg_ref, wu_ref, wd_ref)
    xo_ref[...] = x1
    h = _rms(x1, gm_ref[...]).astype(BF16)
    u_ref[...] = jnp.dot(h, wu_in_ref[...], preferred_element_type=F32)
    q_ref[...] = jnp.dot(h, wq_in_ref[...], preferred_element_type=F32).astype(BF16)
    kv_ref[...] = jnp.dot(h, wkv_in_ref[...], preferred_element_type=F32)


def _ffn_chunk_specs(layer=None):
    chunk = lambda i: jnp.minimum(i, FF_STEPS - 1)
    if layer is None:
        return [pl.BlockSpec((D_MODEL, FF_CHUNK), lambda i: (0, chunk(i))),
                pl.BlockSpec((D_MODEL, FF_CHUNK), lambda i: (0, chunk(i))),
                pl.BlockSpec((FF_CHUNK, D_MODEL), lambda i: (chunk(i), 0))]
    return [pl.BlockSpec((None, D_MODEL, FF_CHUNK), lambda i: (layer, 0, chunk(i))),
            pl.BlockSpec((None, D_MODEL, FF_CHUNK), lambda i: (layer, 0, chunk(i))),
            pl.BlockSpec((None, FF_CHUNK, D_MODEL), lambda i: (layer, chunk(i), 0))]


def _ffn_proj(xs, g1, wg, wu, wd, gm, win_u, win_q, win_kv, cast, layer):
    paired = len(xs) == 2
    x_specs = _pair_specs(D_MODEL) if paired else [_row_spec(D_MODEL)]
    return pl.pallas_call(
        functools.partial(_ffn_proj_kernel, paired=paired),
        grid=(N_TOK // TOKEN_TILE,),
        in_specs=x_specs + [_const_spec((1, D_MODEL)),
                            _const_spec((D_MODEL, D_FF)), _const_spec((D_MODEL, D_FF)), _const_spec((D_FF, D_MODEL)),
                            _const_spec((1, D_MODEL)), _const_spec((D_MODEL, SSM_W)),
                            _const_spec((D_MODEL, ATTN_W)), _const_spec((D_MODEL, 2 * KV_W))]
                 + _ffn_chunk_specs(layer),
        out_specs=[_row_spec(D_MODEL), _row_spec(SSM_W), _row_spec(ATTN_W), _row_spec(2 * KV_W)] + _ffn_chunk_specs(),
        out_shape=[jax.ShapeDtypeStruct((N_TOK, D_MODEL), F32), jax.ShapeDtypeStruct((N_TOK, SSM_W), F32),
                   jax.ShapeDtypeStruct((N_TOK, ATTN_W), BF16), jax.ShapeDtypeStruct((N_TOK, 2 * KV_W), F32),
                   jax.ShapeDtypeStruct((D_MODEL, D_FF), BF16), jax.ShapeDtypeStruct((D_MODEL, D_FF), BF16),
                   jax.ShapeDtypeStruct((D_FF, D_MODEL), BF16)],
        compiler_params=pltpu.CompilerParams(dimension_semantics=("arbitrary",), vmem_limit_bytes=VMEM_LIMIT),
        name="ffn1_proj",
    )(*xs, g1, wg, wu, wd, gm, win_u, win_q, win_kv, *cast)


def _discretise(lam_re, lam_im, log_dt):
    dt = jnp.exp(log_dt)
    mag = jnp.exp(lam_re * dt)
    ang = lam_im * dt
    a_re = mag * jnp.cos(ang)
    a_im = mag * jnp.sin(ang)
    den = lam_re * lam_re + lam_im * lam_im
    nr = a_re - 1.0
    k_re = (nr * lam_re + a_im * lam_im) / den
    k_im = (a_im * lam_re - nr * lam_im) / den
    return a_re, a_im, k_re, k_im


def _powers(a_re, a_im, n):
    p_re = [jnp.ones_like(a_re)]
    p_im = [jnp.zeros_like(a_re)]
    for _ in range(n):
        r, i = p_re[-1], p_im[-1]
        p_re.append(r * a_re - i * a_im)
        p_im.append(r * a_im + i * a_re)
    return p_re, p_im


def _ssm_disc_kernel(lre_ref, lim_ref, ldt_ref, are_ref, aim_ref, kre_ref, kim_ref):
    a_re, a_im, k_re, k_im = _discretise(lre_ref[...], lim_ref[...], ldt_ref[...])
    are_ref[...] = a_re
    aim_ref[...] = a_im
    kre_ref[...] = k_re
    kim_ref[...] = k_im


def _ssm_disc(lam_re, lam_im, log_dt):
    n = DEPTH * SSM_G
    ldt = jnp.broadcast_to(log_dt[..., None], lam_re.shape)
    outs = pl.pallas_call(
        _ssm_disc_kernel,
        out_shape=[jax.ShapeDtypeStruct((n, SSM_N), F32)] * 4,
        name="ssm_disc",
    )(lam_re.reshape(n, SSM_N), lam_im.reshape(n, SSM_N), ldt.reshape(n, SSM_N))
    return [o.reshape(DEPTH, SSM_G, SSM_N) for o in outs]


def _ssm_prep_kernel(ar_re, ar_im, kr_re, kr_im, btr_re, btr_im, cr_re, cr_im, af_re, af_im,
                     wpair_ref, bfull_ref, bhalf_ref, cfull_ref, apw_ref):
    a_re, a_im, k_re, k_im = ar_re[...], ar_im[...], kr_re[...], kr_im[...]
    bb_re = k_re * btr_re[...] - k_im * btr_im[...]
    bb_im = k_re * btr_im[...] + k_im * btr_re[...]
    p_re, p_im = _powers(a_re, a_im, SSM_T)
    c_re, c_im = cr_re[...], cr_im[...]
    ca = [jnp.concatenate([c_re * p_re[t] - c_im * p_im[t], c_re * p_im[t] + c_im * p_re[t]], axis=1)
          for t in range(SSM_T + 1)]
    ca_all = jnp.concatenate(ca[:SSM_T], axis=0)
    bb2 = jnp.concatenate([bb_re, -bb_im], axis=1)
    drow = lax.dot_general(bb2, ca_all, (((1,), (1,)), ((), ())),
                           precision=lax.Precision.HIGHEST, preferred_element_type=F32)
    row_g = lax.broadcasted_iota(jnp.int32, drow.shape, 0) // SSM_P
    col_g = (lax.broadcasted_iota(jnp.int32, drow.shape, 1) % LANES) // SSM_P
    drow = jnp.where(row_g == col_g, drow, 0.0).astype(BF16)
    wpair_ref[0:LANES, :] = drow
    wpair_ref[LANES:, 0:LANES] = jnp.zeros((LANES, LANES), BF16)
    wpair_ref[LANES:, LANES:] = drow[:, :(SSM_T - 1) * LANES]

    row_grp = lax.broadcasted_iota(jnp.int32, (LANES, LANES), 0) // SSM_P
    lane_half = lax.broadcasted_iota(jnp.int32, (LANES, LANES), 1) // SSM_N
    for s in range(SSM_T):
        w_re, w_im = p_re[SSM_T - 1 - s], p_im[SSM_T - 1 - s]
        parts = [bb_re * w_re - bb_im * w_im, bb_re * w_im + bb_im * w_re]
        parts = [jnp.concatenate([v, v], axis=1) for v in parts]
        blk = jnp.concatenate([jnp.where(row_grp == 2 * k + lane_half, v, 0.0)
                               for v in parts for k in range(SLAB_N // LANES)], axis=1)
        bfull_ref[s * LANES:(s + 1) * LANES, :] = blk.astype(BF16)
        if s >= SSM_T - DEC_SEQ:
            bhalf_ref[(s - SSM_T + DEC_SEQ) * LANES:(s - SSM_T + DEC_SEQ + 1) * LANES, :] = blk

    lane_grp = lax.broadcasted_iota(jnp.int32, (SSM_N, LANES), 1) // SSM_P
    for t in range(SSM_T):
        cat = ca[t + 1].T
        ca_re, ca_im = cat[:SSM_N], cat[SSM_N:]
        cols = slice(t * LANES, (t + 1) * LANES)
        for g in range(SLAB_G):
            cfull_ref[g * SSM_N:(g + 1) * SSM_N, cols] = jnp.where(lane_grp == g, ca_re, 0.0).astype(BF16)
            cfull_ref[SLAB_N + g * SSM_N:SLAB_N + (g + 1) * SSM_N, cols] = (
                jnp.where(lane_grp == g, -ca_im, 0.0).astype(BF16))

    a_re, a_im = af_re[...], af_im[...]
    sq = []
    for _ in range(4):
        a_re, a_im = a_re * a_re - a_im * a_im, 2.0 * a_re * a_im
        sq.append((a_re, a_im))
    apw_ref[0:1, :] = sq[2][0]
    apw_ref[1:2, :] = sq[2][1]
    apw_ref[2:3, :] = sq[3][0]
    apw_ref[3:4, :] = sq[3][1]
    apw_ref[4:8, :] = jnp.zeros((4, SLAB_N), F32)


def _ssm_prep(lam_re, lam_im, log_dt, b_re, b_im, c_re, c_im):
    n = DEPTH * N_SLAB
    a_re, a_im, k_re, k_im = _ssm_disc(lam_re, lam_im, log_dt)
    rows = lambda v: jnp.repeat(v.reshape(n, SLAB_G, 1, SSM_N), SSM_P, axis=2).reshape(n, LANES, SSM_N)
    bt = lambda v: v.transpose(0, 1, 3, 2).reshape(n, LANES, SSM_N)
    flat = lambda v: v.reshape(n, 1, SLAB_N)
    args = ([rows(v) for v in (a_re, a_im, k_re, k_im)]
            + [bt(b_re), bt(b_im), c_re.reshape(n, LANES, SSM_N), c_im.reshape(n, LANES, SSM_N)]
            + [flat(a_re), flat(a_im)])

    def blk(a):
        return pl.BlockSpec((None,) + a.shape[1:], lambda i: (i, 0, 0))

    out_shape = [jax.ShapeDtypeStruct((n, 2 * LANES, SSM_T * LANES), BF16),
                 jax.ShapeDtypeStruct((n, SSM_T * LANES, 2 * SLAB_N), BF16),
                 jax.ShapeDtypeStruct((n, DEC_SEQ * LANES, 2 * SLAB_N), F32),
                 jax.ShapeDtypeStruct((n, 2 * SLAB_N, SSM_T * LANES), BF16),
                 jax.ShapeDtypeStruct((n, 8, SLAB_N), F32)]
    return pl.pallas_call(
        _ssm_prep_kernel,
        grid=(n,),
        in_specs=[blk(a) for a in args],
        out_specs=[blk(s) for s in out_shape],
        out_shape=out_shape,
        compiler_params=pltpu.CompilerParams(dimension_semantics=("parallel",), vmem_limit_bytes=VMEM_LIMIT),
        name="ssm_prep",
    )(*args)


def _ssm_prompt_kernel(u_ref, wpair_ref, bfull_ref, cfull_ref, apw_ref, y_ref, hst_ref,
                       ub_ref, s_ref, yacc_ref, *col_refs):
    rows = SSM_HALF_B * N_CHUNK
    n_col = SLAB_N // LANES
    sp_refs, hp_refs = col_refs[:n_col], col_refs[n_col:]
    for t in range(SSM_T):
        ub_ref[:, t * LANES:(t + 1) * LANES] = u_ref[pl.ds(t, rows, stride=SSM_T), :].astype(BF16)
    s_ref[...] = jnp.dot(ub_ref[...], bfull_ref[...], preferred_element_type=F32)
    for b in range(SSM_HALF_B):
        seq = slice(b * N_CHUNK, (b + 1) * N_CHUNK)
        for k in range(n_col):
            sp_refs[k][pl.ds(b, N_CHUNK, stride=8), :] = s_ref[seq, k * LANES:(k + 1) * LANES]
            sp_refs[k][pl.ds(SSM_HALF_B + b, N_CHUNK, stride=8), :] = (
                s_ref[seq, SLAB_N + k * LANES:SLAB_N + (k + 1) * LANES])
    upper = lax.broadcasted_iota(jnp.int32, (8, LANES), 0) < SSM_HALF_B
    a1 = [jnp.broadcast_to(apw_ref[2:3, k * LANES:(k + 1) * LANES], (8, LANES)) for k in range(n_col)]
    a2 = [jnp.where(upper, -1.0, 1.0) * jnp.broadcast_to(apw_ref[3:4, k * LANES:(k + 1) * LANES], (8, LANES))
          for k in range(n_col)]

    def step(c, h):
        r0 = pl.multiple_of(c * 8, 8)
        new = []
        for k in range(n_col):
            hp_refs[k][pl.ds(r0, 8), :] = h[k]
            new.append(a1[k] * h[k] + a2[k] * pltpu.roll(h[k], SSM_HALF_B, axis=0) + sp_refs[k][pl.ds(r0, 8), :])
        return tuple(new)

    h = lax.fori_loop(0, N_CHUNK, step, tuple(jnp.zeros((8, LANES), F32) for _ in range(n_col)), unroll=8)
    hst_ref[...] = jnp.concatenate(h, axis=1)
    hbc = jnp.concatenate(
        [jnp.concatenate([hp_refs[k][pl.ds(part + b, N_CHUNK, stride=8), :]
                          for part in (0, SSM_HALF_B) for k in range(n_col)], axis=1)
         for b in range(SSM_HALF_B)], axis=0)
    yacc_ref[...] = jnp.dot(hbc.astype(BF16), cfull_ref[...], preferred_element_type=F32)
    for t in range(0, SSM_T, 2):
        yacc_ref[:, t * LANES:] += jnp.dot(ub_ref[:, t * LANES:(t + 2) * LANES],
                                           wpair_ref[:, 0:(SSM_T - t) * LANES], preferred_element_type=F32)
    for t in range(SSM_T):
        y_ref[pl.ds(t, rows, stride=SSM_T), :] = yacc_ref[:, t * LANES:(t + 1) * LANES]


def _ssm_prompt(u, wpair, bfull, cfull, apw, layer):
    rows = SSM_HALF_B * N_CHUNK
    tok = SSM_HALF_B * SEQ

    def wspec(a):
        return pl.BlockSpec((None,) + a.shape[1:], lambda j, h: (layer * N_SLAB + j, 0, 0))

    return pl.pallas_call(
        _ssm_prompt_kernel,
        grid=(N_SLAB, BATCH // SSM_HALF_B),
        in_specs=[pl.BlockSpec((tok, LANES), lambda j, h: (h, j)), wspec(wpair), wspec(bfull), wspec(cfull), wspec(apw)],
        out_specs=[pl.BlockSpec((tok, LANES), lambda j, h: (h, j)),
                   pl.BlockSpec((None, 8, SLAB_N), lambda j, h: (h, 0, j))],
        out_shape=[jax.ShapeDtypeStruct((N_PROMPT, SSM_W), F32),
                   jax.ShapeDtypeStruct((BATCH // SSM_HALF_B, 8, SSM_G * SSM_N), F32)],
        scratch_shapes=[pltpu.VMEM((rows, SSM_T * LANES), BF16), pltpu.VMEM((rows, 2 * SLAB_N), F32),
                        pltpu.VMEM((rows, SSM_T * LANES), F32)]
                       + [pltpu.VMEM((N_CHUNK * 8, LANES), F32)] * (2 * SLAB_N // LANES),
        compiler_params=pltpu.CompilerParams(dimension_semantics=("arbitrary", "arbitrary"),
                                             vmem_limit_bytes=VMEM_LIMIT),
        name="ssm_prompt",
    )(u, wpair, bfull, cfull, apw)


def _ssm_sample_kernel(u_ref, h0r_ref, h0i_ref, wpair_ref, bhalf_ref, cfull_ref, apw_ref,
                       y_ref, hr_ref, hi_ref, uf_ref, yacc_ref):
    for t in range(DEC_SEQ):
        uf_ref[:, t * LANES:(t + 1) * LANES] = u_ref[pl.ds(t, DEC_BATCH, stride=DEC_SEQ), :]
    uf = uf_ref[...]
    h0r = h0r_ref[...]
    h0i = h0i_ref[...]
    ar = apw_ref[0:1, :]
    ai = apw_ref[1:2, :]
    s = jnp.dot(uf, bhalf_ref[...], precision=lax.Precision.HIGHEST, preferred_element_type=F32)
    hr_ref[...] = ar * h0r - ai * h0i + s[:, :SLAB_N]
    hi_ref[...] = ar * h0i + ai * h0r + s[:, SLAB_N:]
    ub = uf.astype(BF16)
    h0 = jnp.concatenate([h0r, h0i], axis=1).astype(BF16)
    yacc_ref[...] = jnp.dot(h0, cfull_ref[...], preferred_element_type=F32)
    for t in range(0, DEC_SEQ, 2):
        yacc_ref[:, t * LANES:] += jnp.dot(ub[:, t * LANES:(t + 2) * LANES],
                                           wpair_ref[:, 0:(DEC_SEQ - t) * LANES], preferred_element_type=F32)
    for t in range(DEC_SEQ):
        y_ref[pl.ds(t, DEC_BATCH, stride=DEC_SEQ), :] = yacc_ref[:, t * LANES:(t + 1) * LANES]


def _ssm_sample(u, h0re, h0im, wpair, bhalf, cfull, apw, layer):
    width = DEC_SEQ * LANES

    def wspec(r, c):
        return pl.BlockSpec((None, r, c), lambda j: (layer * N_SLAB + j, 0, 0))

    st = pl.BlockSpec((DEC_BATCH, SLAB_N), lambda j: (0, j))
    return pl.pallas_call(
        _ssm_sample_kernel,
        grid=(N_SLAB,),
        in_specs=[pl.BlockSpec((N_SAMPLE, LANES), lambda j: (N_PROMPT // N_SAMPLE, j)), st, st,
                  wspec(2 * LANES, width), wspec(width, 2 * SLAB_N), wspec(2 * SLAB_N, width), wspec(8, SLAB_N)],
        out_specs=[pl.BlockSpec((N_SAMPLE, LANES), lambda j: (0, j)), st, st],
        out_shape=[jax.ShapeDtypeStruct((N_SAMPLE, SSM_W), F32),
                   jax.ShapeDtypeStruct((DEC_BATCH, SSM_G * SSM_N), F32),
                   jax.ShapeDtypeStruct((DEC_BATCH, SSM_G * SSM_N), F32)],
        scratch_shapes=[pltpu.VMEM((DEC_BATCH, width), F32), pltpu.VMEM((DEC_BATCH, width), F32)],
        compiler_params=pltpu.CompilerParams(dimension_semantics=("arbitrary",), vmem_limit_bytes=VMEM_LIMIT),
        name="ssm_sample",
    )(u, h0re, h0im, wpair, bhalf, cfull, apw)


def _mask_cap(valid):
    return jnp.where(valid, jnp.inf, NEG_INF).astype(F32)


def _sink_softmax(s, cap, sink):
    s = jnp.minimum(s, cap)
    m = jnp.maximum(jnp.max(s, axis=-1, keepdims=True), sink)
    e = jnp.exp(s - m)
    denom = jnp.sum(e, axis=-1, keepdims=True) + jnp.exp(sink - m)
    return e.astype(BF16), denom


def _scores(q, scale_mask, kk):
    qx = jnp.concatenate([q[:, c * KV_W:(c + 1) * KV_W] * scale_mask for c in range(GRP)], axis=0)
    return lax.dot_general(qx.astype(BF16), kk, (((1,), (1,)), ((), ())), preferred_element_type=F32)


def _band_valid(n_q, n_k, rows):
    qi = lax.broadcasted_iota(jnp.int32, (rows, n_k), 0) & (n_q - 1)
    kj = lax.broadcasted_iota(jnp.int32, (rows, n_k), 1)
    return kj, ((kj - qi - 1).astype(jnp.uint32) < WINDOW)


def _head_masks():
    lane_kv = lax.broadcasted_iota(jnp.int32, (1, KV_W), 1) // HEAD_DIM
    return [jnp.where(lane_kv == h, HEAD_DIM ** -0.5, 0.0) for h in range(N_KV)], lane_kv == 0


ATTN_Q_TILE = 512


def _attn_prompt_kernel(q_ref, kvp_ref, kvc_ref, sink_ref, o_ref):
    cols = GRP * WINDOW
    scale_masks, _ = _head_masks()
    kj = lax.broadcasted_iota(jnp.int32, (2 * WINDOW, cols), 0)
    qi = lax.broadcasted_iota(jnp.int32, (2 * WINDOW, cols), 1) & (WINDOW - 1)
    valid = (kj - qi - 1).astype(jnp.uint32) < WINDOW
    cap_first = _mask_cap(valid & ((kj >= WINDOW) | (pl.program_id(1) > 0)))
    cap = _mask_cap(valid)
    n_blk = ATTN_Q_TILE // WINDOW

    def stage(r):
        cur = kvc_ref[r * WINDOW:(r + 1) * WINDOW, :]
        prev = kvp_ref[...] if r == 0 else kvc_ref[(r - 1) * WINDOW:r * WINDOW, :]
        kk = jnp.concatenate([prev[:, :KV_W], cur[:, :KV_W]], axis=0).astype(BF16)
        v_t = jnp.concatenate([prev[:, KV_W:], cur[:, KV_W:]], axis=0).T.astype(BF16)
        q = q_ref[r * WINDOW:(r + 1) * WINDOW, :].astype(F32)
        s_t = []
        for h in range(N_KV):
            qx = jnp.concatenate([q[:, c * KV_W:(c + 1) * KV_W] * scale_masks[h] for c in range(GRP)], axis=0)
            s_t.append(lax.dot_general(kk, qx.astype(BF16), (((1,), (1,)), ((), ())), preferred_element_type=F32))
        return s_t, v_t

    def softmax_t(s, cap_r, sink):
        s = jnp.minimum(s, cap_r)
        m = jnp.maximum(jnp.max(s, axis=0, keepdims=True), sink)
        e = jnp.exp(s - m)
        return e.astype(BF16), jnp.sum(e, axis=0, keepdims=True) + jnp.exp(sink - m)

    staged = [stage(r) for r in range(n_blk)]
    weights = [[softmax_t(staged[r][0][h], cap_first if r == 0 else cap, sink_ref[h]) for h in range(N_KV)]
               for r in range(n_blk)]
    for r in range(n_blk):
        v_t = staged[r][1]
        outs = [jnp.dot(v_t, e, preferred_element_type=F32) / denom for e, denom in weights[r]]
        o_t = jnp.concatenate([outs[h][h * HEAD_DIM:(h + 1) * HEAD_DIM] for h in range(N_KV)], axis=0)
        for c in range(GRP):
            o_ref[r * WINDOW:(r + 1) * WINDOW, c * KV_W:(c + 1) * KV_W] = (
                o_t[:, c * WINDOW:(c + 1) * WINDOW].T.astype(o_ref.dtype))


def _attn_prompt(q, kv, sink_col):
    tiles = SEQ // ATTN_Q_TILE
    per_tile = ATTN_Q_TILE // WINDOW
    blocks = SEQ // WINDOW
    return pl.pallas_call(
        _attn_prompt_kernel,
        grid=(BATCH, tiles),
        in_specs=[pl.BlockSpec((ATTN_Q_TILE, ATTN_W), lambda b, j: (b * tiles + j, 0)),
                  pl.BlockSpec((WINDOW, 2 * KV_W), lambda b, j: (b * blocks + jnp.maximum(j * per_tile - 1, 0), 0)),
                  pl.BlockSpec((ATTN_Q_TILE, 2 * KV_W), lambda b, j: (b * tiles + j, 0)),
                  pl.BlockSpec((N_KV, 1, GRP * WINDOW), lambda b, j: (0, 0, 0))],
        out_specs=pl.BlockSpec((ATTN_Q_TILE, ATTN_W), lambda b, j: (b * tiles + j, 0)),
        out_shape=jax.ShapeDtypeStruct((N_PROMPT, ATTN_W), BF16),
        compiler_params=pltpu.CompilerParams(dimension_semantics=("parallel", "arbitrary")),
        name="attn_prompt",
    )(q, kv, kv, sink_col)


SAMPLE_SEQ_BLOCK = 8


def _attn_sample_kernel(q_ref, ckt_ref, cvt_ref, kvn_ref, sink_ref, *rest, n_carried):
    o_ref, nkt_ref, nvt_ref = rest[n_carried:]
    rows = GRP * DEC_SEQ
    kept = WINDOW - DEC_SEQ
    nt = (((1,), (1,)), ((), ()))
    t = lax.broadcasted_iota(jnp.int32, (rows, WINDOW), 0) & (DEC_SEQ - 1)
    lane = lax.broadcasted_iota(jnp.int32, (rows, WINDOW), 1)
    cap_old = _mask_cap(lane > t)
    cap_new = _mask_cap((lane >= kept) & (lane - kept <= t))
    keep_old = lax.broadcasted_iota(jnp.int32, (HEAD_DIM, WINDOW), 1) < kept
    q_all = q_ref[...].astype(F32) * (HEAD_DIM ** -0.5)
    place = (lax.broadcasted_iota(jnp.int32, (DEC_SEQ, WINDOW), 1)
             == lax.broadcasted_iota(jnp.int32, (DEC_SEQ, WINDOW), 0) + kept).astype(F32)
    toks = [slice(s * DEC_SEQ, (s + 1) * DEC_SEQ) for s in range(SAMPLE_SEQ_BLOCK)]
    new_ts = [lax.dot_general(kvn_ref[tok, :], place, (((0,), (0,)), ((), ())),
                              precision=lax.Precision.HIGHEST, preferred_element_type=F32) for tok in toks]
    scores, values = [], []
    for s, tok in enumerate(toks):
        new_t = new_ts[s]
        for h in range(N_KV):
            ch = slice(h * HEAD_DIM, (h + 1) * HEAD_DIM)
            new_k, new_v = new_t[ch], new_t[KV_W:][ch]
            old_k, old_v = ckt_ref[s, h], cvt_ref[s, h]
            nkt_ref[s, h] = jnp.where(keep_old, pltpu.roll(old_k, kept, axis=1), new_k)
            nvt_ref[s, h] = jnp.where(keep_old, pltpu.roll(old_v, kept, axis=1), new_v)
            q = jnp.concatenate([q_all[tok, c * KV_W + h * HEAD_DIM:c * KV_W + (h + 1) * HEAD_DIM]
                                 for c in range(GRP)], axis=0).astype(BF16)
            scores.append((jnp.dot(q, old_k.astype(BF16), preferred_element_type=F32),
                           jnp.dot(q, new_k.astype(BF16), preferred_element_type=F32)))
            values.append((old_v.astype(BF16), new_v.astype(BF16)))
    weights = []
    for i, (s_old, s_new) in enumerate(scores):
        sink = sink_ref[i % N_KV]
        s_old = jnp.minimum(s_old, cap_old)
        s_new = jnp.minimum(s_new, cap_new)
        m = jnp.maximum(jnp.maximum(jnp.max(s_old, axis=-1, keepdims=True), jnp.max(s_new, axis=-1, keepdims=True)),
                        sink)
        e_old = jnp.exp(s_old - m)
        e_new = jnp.exp(s_new - m)
        denom = (jnp.sum(e_old, axis=-1, keepdims=True) + jnp.sum(e_new, axis=-1, keepdims=True)
                 + jnp.exp(sink - m))
        weights.append((e_old.astype(BF16), e_new.astype(BF16), denom))
    for s, tok in enumerate(toks):
        outs = []
        for h in range(N_KV):
            e_old, e_new, denom = weights[s * N_KV + h]
            v_old, v_new = values[s * N_KV + h]
            outs.append((lax.dot_general(e_old, v_old, nt, preferred_element_type=F32)
                         + lax.dot_general(e_new, v_new, nt, preferred_element_type=F32)) / denom)
        for c in range(GRP):
            o_ref[tok, c * KV_W:(c + 1) * KV_W] = jnp.concatenate(
                [o[c * DEC_SEQ:(c + 1) * DEC_SEQ] for o in outs], axis=1)


def _attn_sample(q, kv, ck, cv, sink_col, layer, windows):
    sb = SAMPLE_SEQ_BLOCK
    tok = sb * DEC_SEQ
    first = N_PROMPT // tok
    c_spec = pl.BlockSpec((None, sb, N_KV, HEAD_DIM, WINDOW), lambda i: (layer, i, 0, 0, 0))
    in_specs = [pl.BlockSpec((tok, ATTN_W), lambda i: (first + i, 0)), c_spec, c_spec,
                pl.BlockSpec((tok, 2 * KV_W), lambda i: (first + i, 0)),
                pl.BlockSpec((N_KV, GRP * DEC_SEQ, 1), lambda i: (0, 0, 0))]
    args = [q, ck, cv, kv, sink_col]
    aliases = {}
    if windows is not None:
        aliases = {len(args): 1, len(args) + 1: 2}
        in_specs += [pl.BlockSpec(memory_space=pl.ANY)] * 2
        args += list(windows)
    return pl.pallas_call(
        functools.partial(_attn_sample_kernel, n_carried=len(aliases)),
        grid=(DEC_BATCH // sb,),
        in_specs=in_specs,
        out_specs=[pl.BlockSpec((tok, ATTN_W), lambda i: (i, 0)), c_spec, c_spec],
        out_shape=[jax.ShapeDtypeStruct((N_SAMPLE, ATTN_W), F32),
                   jax.ShapeDtypeStruct((DEPTH, DEC_BATCH, N_KV, HEAD_DIM, WINDOW), F32),
                   jax.ShapeDtypeStruct((DEPTH, DEC_BATCH, N_KV, HEAD_DIM, WINDOW), F32)],
        input_output_aliases=aliases,
        compiler_params=pltpu.CompilerParams(dimension_semantics=("parallel",)),
        name="attn_sample",
    )(*args)


def _mix_out(x, y, u, o, d_ref, wglu_ref, gs_ref, ga_ref, wout_s_ref, wout_a_ref):
    y = y + d_ref[...] * u
    y = jax.nn.gelu(y)
    y = y * jax.nn.sigmoid(jnp.dot(y.astype(BF16), wglu_ref[...], preferred_element_type=F32))
    ys = _rms(y, gs_ref[...]).astype(BF16)
    ya = _rms(o, ga_ref[...]).astype(BF16)
    mix = jnp.dot(ys, wout_s_ref[...], preferred_element_type=F32)
    mix = mix + jnp.dot(ya, wout_a_ref[...], preferred_element_type=F32)
    return x + mix


def _tail_kernel(x_ref, yp_ref, ys_ref, u_ref, op_ref, os_ref, pp_ref, ps_ref,
                 d_ref, wglu_ref, gs_ref, ga_ref, wout_s_ref, wout_a_ref,
                 g2_ref, wg_ref, wu_ref, wd_ref, gp_ref, wpg_ref, wpp_ref, gf_ref,
                 *out_refs, final):
    x1 = _mix_out(x_ref[...], _pick(yp_ref, ys_ref), u_ref[...], _pick(op_ref, os_ref),
                  d_ref, wglu_ref, gs_ref, ga_ref, wout_s_ref, wout_a_ref)
    x2 = _half_swiglu(x1, g2_ref, wg_ref, wu_ref, wd_ref)
    h = _rms(x2, gp_ref[...]).astype(BF16)
    gate = jax.nn.sigmoid(jnp.dot(h, wpg_ref[...], preferred_element_type=F32))
    proj = jnp.dot(_pick(pp_ref, ps_ref).astype(BF16), wpp_ref[...], preferred_element_type=F32)
    x3 = x2 + proj * gate
    if not final:
        out_refs[0][...] = x3
        return
    y = _rms(x3, gf_ref[...])
    yp_ref, ys_ref = out_refs

    @pl.when(pl.program_id(0) < PROMPT_TILES)
    def _():
        yp_ref[...] = y

    @pl.when(pl.program_id(0) >= PROMPT_TILES)
    def _():
        ys_ref[...] = y


def _layer_tail(x, y_prompt, y_sample, u, o_prompt, o_sample, p_prompt, p_sample, layer,
                d, wglu, gs, ga, wout_ssm, wout_attn, g2, wg, wu, wd, gp, wpg, wpp, gf, final):
    if final:
        out_specs = _pair_specs(D_MODEL)
        out_shape = [jax.ShapeDtypeStruct((N_PROMPT, D_MODEL), F32), jax.ShapeDtypeStruct((N_SAMPLE, D_MODEL), F32)]
    else:
        out_specs = _row_spec(D_MODEL)
        out_shape = jax.ShapeDtypeStruct((N_TOK, D_MODEL), F32)
    p_specs = [pl.BlockSpec((None, TOKEN_TILE, PLE_DIM), lambda i: (layer, jnp.minimum(i, PROMPT_TILES - 1), 0)),
               pl.BlockSpec((None, TOKEN_TILE, PLE_DIM), lambda i: (layer, jnp.maximum(i - PROMPT_TILES, 0), 0))]
    return pl.pallas_call(
        functools.partial(_tail_kernel, final=final),
        grid=(N_TOK // TOKEN_TILE,),
        in_specs=[_row_spec(D_MODEL)] + _pair_specs(SSM_W) + [_row_spec(SSM_W)] + _pair_specs(ATTN_W) + p_specs + [
            _const_spec((1, SSM_W)), _const_spec((SSM_W, SSM_W)), _const_spec((1, SSM_W)), _const_spec((1, ATTN_W)),
            _const_spec((SSM_W, D_MODEL)), _const_spec((ATTN_W, D_MODEL)),
            _const_spec((1, D_MODEL)),
            _const_spec((D_MODEL, D_FF)), _const_spec((D_MODEL, D_FF)), _const_spec((D_FF, D_MODEL)),
            _const_spec((1, D_MODEL)), _const_spec((D_MODEL, D_MODEL)), _const_spec((PLE_DIM, D_MODEL)),
            _const_spec((1, D_MODEL))],
        out_specs=out_specs,
        out_shape=out_shape,
        compiler_params=pltpu.CompilerParams(dimension_semantics=("arbitrary",), vmem_limit_bytes=VMEM_LIMIT),
        name="layer_tail",
    )(x, y_prompt, y_sample, u, o_prompt, o_sample, p_prompt, p_sample,
      d, wglu, gs, ga, wout_ssm, wout_attn, g2, wg, wu, wd, gp, wpg, wpp, gf)


def _sink_column(sinks, rows_per_head):
    return jnp.repeat(sinks.astype(F32).reshape(N_KV, GRP), rows_per_head, axis=1)[..., None]


def _head_major(a, axis):
    shape = a.shape
    a = a.reshape(shape[:axis] + (N_KV, GRP, HEAD_DIM) + shape[axis + 1:])
    return jnp.swapaxes(a, axis, axis + 1).reshape(shape)


def kernel(x_prompt, x_sample, cache_k, cache_v, state_ssm_re, state_ssm_im, p_prompt, p_sample, ffn1_norm, ffn1_w_gate, ffn1_w_up, ffn1_w_down, mix_norm, w_in, ssm_lam_re, ssm_lam_im, ssm_log_dt, ssm_b_re, ssm_b_im, ssm_c_re, ssm_c_im, ssm_d, ssm_w_glu, ssm_out_norm, attn_sinks, attn_out_norm, w_out, ffn2_norm, ffn2_w_gate, ffn2_w_up, ffn2_w_down, ple_norm, ple_w_gate, ple_w_proj, final_norm):
    wpair, bfull, bhalf, cfull, apw = _ssm_prep(ssm_lam_re, ssm_lam_im, ssm_log_dt, ssm_b_re, ssm_b_im,
                                                ssm_c_re, ssm_c_im)
    row = lambda v: v.reshape(1, -1)
    bf = lambda w: w.astype(BF16)
    xs = (x_prompt.reshape(N_PROMPT, D_MODEL), x_sample.reshape(N_SAMPLE, D_MODEL))
    n_state = SSM_G * SSM_N

    caches = [c.transpose(0, 1, 3, 4, 2) for c in (cache_k, cache_v)]
    windows = None

    kp_l, vp_l, hpr_l, hpi_l, hsr_l, hsi_l = [], [], [], [], [], []
    for i in range(DEPTH):
        x, u, q, kv, *ffn2_w = _ffn_proj(
            xs, row(ffn1_norm[i]), bf(ffn1_w_gate[i]), bf(ffn1_w_up[i]), bf(ffn1_w_down[i]),
            row(mix_norm[i]), bf(w_in[i, :, :SSM_W]),
            _head_major(bf(w_in[i, :, SSM_W:SSM_W + ATTN_W]), 1), bf(w_in[i, :, SSM_W + ATTN_W:]),
            cast=(ffn2_w_gate, ffn2_w_up, ffn2_w_down), layer=i)

        y_p, hst = _ssm_prompt(u, wpair, bfull, cfull, apw, i)
        hst = hst.reshape(BATCH // SSM_HALF_B, 2, SSM_HALF_B, SSM_G, SSM_N)
        hpr_l.append(hst[:, 0].reshape(BATCH, SSM_G, SSM_N))
        hpi_l.append(hst[:, 1].reshape(BATCH, SSM_G, SSM_N))
        y_s, hsr, hsi = _ssm_sample(u, state_ssm_re[i].reshape(DEC_BATCH, n_state),
                                    state_ssm_im[i].reshape(DEC_BATCH, n_state), wpair, bhalf, cfull, apw, i)
        hsr_l.append(hsr.reshape(DEC_BATCH, SSM_G, SSM_N))
        hsi_l.append(hsi.reshape(DEC_BATCH, SSM_G, SSM_N))

        kv_last = jnp.stack([kv[(b + 1) * SEQ - WINDOW:(b + 1) * SEQ] for b in range(BATCH)])
        kv_last = kv_last.reshape(BATCH, WINDOW, 2, N_KV, HEAD_DIM)
        kp_l.append(kv_last[:, :, 0])
        vp_l.append(kv_last[:, :, 1])
        o_p = _attn_prompt(q, kv, _sink_column(attn_sinks[i], WINDOW).reshape(N_KV, 1, GRP * WINDOW))
        o_s, *windows = _attn_sample(q, kv, caches[0], caches[1], _sink_column(attn_sinks[i], DEC_SEQ), i, windows)

        x = _layer_tail(x, y_p, y_s, u, o_p, o_s,
                        p_prompt.reshape(DEPTH, N_PROMPT, PLE_DIM), p_sample.reshape(DEPTH, N_SAMPLE, PLE_DIM), i,
                        row(ssm_d[i]), bf(ssm_w_glu[i]), row(ssm_out_norm[i]), row(_head_major(attn_out_norm[i], 0)),
                        bf(w_out[i, :SSM_W]), _head_major(bf(w_out[i, SSM_W:]), 0),
                        row(ffn2_norm[i]), *ffn2_w,
                        row(ple_norm[i]), bf(ple_w_gate[i]), bf(ple_w_proj[i]), row(final_norm),
                        final=(i == DEPTH - 1))
        xs = (x,)

    y_prompt, y_sample = x
    ks, vs = (w.transpose(0, 1, 4, 2, 3) for w in windows)
    return (y_prompt.reshape(BATCH, SEQ, D_MODEL), y_sample.reshape(DEC_BATCH, DEC_SEQ, D_MODEL),
            jnp.stack(kp_l), jnp.stack(vp_l), jnp.stack(hpr_l), jnp.stack(hpi_l),
            ks, vs, jnp.stack(hsr_l), jnp.stack(hsi_l))
```

```python
import functools

import jax
import jax.numpy as jnp
from jax import lax
from jax.experimental import pallas as pl
from jax.experimental.pallas import tpu as pltpu

F32 = jnp.float32
BF16 = jnp.bfloat16

D_MODEL = 1024
BATCH = 8
SEQ = 2048
DEPTH = 2
DEC_BATCH = 128
DEC_SEQ = 8
SSM_W = 512
SSM_P = 16
SSM_G = 32
SSM_N = 64
ATTN_W = 512
HEAD_DIM = 64
N_HEADS = 8
N_KV = 2
GRP = N_HEADS // N_KV
KV_W = N_KV * HEAD_DIM
IN_W = SSM_W + ATTN_W + 2 * KV_W
WINDOW = 128
D_FF = 2816
PLE_DIM = 256
EPS = 1e-6
NEG_INF = -1e30

N_PROMPT = BATCH * SEQ
N_SAMPLE = DEC_BATCH * DEC_SEQ
N_TOK = N_PROMPT + N_SAMPLE

LANES = 128
SSM_T = 16
N_CHUNK = SEQ // SSM_T
SLAB_G = LANES // SSM_P
N_SLAB = SSM_G // SLAB_G
SLAB_N = SLAB_G * SSM_N
SSM_HALF_B = BATCH // 2

TOKEN_TILE = 512
PROMPT_TILES = N_PROMPT // TOKEN_TILE
FF_CHUNK = 256
FF_STEPS = D_FF // FF_CHUNK
VMEM_LIMIT = 56 * 1024 * 1024


def _const_spec(shape):
    nd = len(shape)
    return pl.BlockSpec(shape, lambda *_: (0,) * nd, pipeline_mode=pl.Buffered(1))


def _row_spec(width):
    return pl.BlockSpec((TOKEN_TILE, width), lambda i: (i, 0))


def _pair_specs(width):
    return [pl.BlockSpec((TOKEN_TILE, width), lambda i: (jnp.minimum(i, PROMPT_TILES - 1), 0)),
            pl.BlockSpec((TOKEN_TILE, width), lambda i: (jnp.maximum(i - PROMPT_TILES, 0), 0))]


def _pick(prompt_ref, sample_ref):
    return jnp.where(pl.program_id(0) < PROMPT_TILES, prompt_ref[...].astype(F32), sample_ref[...].astype(F32))


EPILOGUE_SLICES = 2


def _row_slices(n):
    rows = TOKEN_TILE // n
    return [slice(i * rows, (i + 1) * rows) for i in range(n)]


def _rms(x, g):
    return x * lax.rsqrt(jnp.mean(x * x, axis=-1, keepdims=True) + EPS) * g


def _half_swiglu(xf, g_ref, wg_ref, wu_ref, wd_ref):
    h = _rms(xf, g_ref[...]).astype(BF16)
    acc = jnp.zeros(xf.shape, F32)
    for c in range(0, D_FF, FF_CHUNK):
        gate = jnp.dot(h, wg_ref[:, c:c + FF_CHUNK], preferred_element_type=F32)
        up = jnp.dot(h, wu_ref[:, c:c + FF_CHUNK], preferred_element_type=F32)
        act = (gate * jax.nn.sigmoid(gate) * up).astype(BF16)
        acc = acc + jnp.dot(act, wd_ref[c:c + FF_CHUNK, :], preferred_element_type=F32)
    return xf + 0.5 * acc


def _ffn_proj_kernel(*refs, paired):
    if paired:
        xp_ref, xs_ref, *refs = refs
        x = _pick(xp_ref, xs_ref)
    else:
        x_ref, *refs = refs
        x = x_ref[...]
    (g1_ref, wg_ref, wu_ref, wd_ref, gm_ref, wu_in_ref, wq_in_ref, wkv_in_ref, *cast_in,
     xo_ref, u_ref, q_ref, kv_ref, cg_ref, cu_ref, cd_ref) = refs

    @pl.when(pl.program_id(0) < FF_STEPS)
    def _():
        for src, dst in zip(cast_in, (cg_ref, cu_ref, cd_ref)):
            dst[...] = src[...].astype(BF16)

    x1 = _half_swiglu(x, g1_ref, wg_ref, wu_ref, wd_ref)
    xo_ref[...] = x1
    for r in _row_slices(EPILOGUE_SLICES):
        h = _rms(x1[r], gm_ref[...]).astype(BF16)
        u_ref[r, :] = jnp.dot(h, wu_in_ref[...], preferred_element_type=F32)
        q_ref[r, :] = jnp.dot(h, wq_in_ref[...], preferred_element_type=F32).astype(BF16)
        kv_ref[r, :] = jnp.dot(h, wkv_in_ref[...], preferred_element_type=F32)


def _ffn_chunk_specs(layer=None):
    chunk = lambda i: jnp.minimum(i, FF_STEPS - 1)
    if layer is None:
        return [pl.BlockSpec((D_MODEL, FF_CHUNK), lambda i: (0, chunk(i))),
                pl.BlockSpec((D_MODEL, FF_CHUNK), lambda i: (0, chunk(i))),
                pl.BlockSpec((FF_CHUNK, D_MODEL), lambda i: (chunk(i), 0))]
    return [pl.BlockSpec((None, D_MODEL, FF_CHUNK), lambda i: (layer, 0, chunk(i))),
            pl.BlockSpec((None, D_MODEL, FF_CHUNK), lambda i: (layer, 0, chunk(i))),
            pl.BlockSpec((None, FF_CHUNK, D_MODEL), lambda i: (layer, chunk(i), 0))]


def _ffn_proj(xs, g1, wg, wu, wd, gm, win_u, win_q, win_kv, cast, layer):
    paired = len(xs) == 2
    x_specs = _pair_specs(D_MODEL) if paired else [_row_spec(D_MODEL)]
    return pl.pallas_call(
        functools.partial(_ffn_proj_kernel, paired=paired),
        grid=(N_TOK // TOKEN_TILE,),
        in_specs=x_specs + [_const_spec((1, D_MODEL)),
                            _const_spec((D_MODEL, D_FF)), _const_spec((D_MODEL, D_FF)), _const_spec((D_FF, D_MODEL)),
                            _const_spec((1, D_MODEL)), _const_spec((D_MODEL, SSM_W)),
                            _const_spec((D_MODEL, ATTN_W)), _const_spec((D_MODEL, 2 * KV_W))]
                 + _ffn_chunk_specs(layer),
        out_specs=[_row_spec(D_MODEL), _row_spec(SSM_W), _row_spec(ATTN_W), _row_spec(2 * KV_W)] + _ffn_chunk_specs(),
        out_shape=[jax.ShapeDtypeStruct((N_TOK, D_MODEL), F32), jax.ShapeDtypeStruct((N_TOK, SSM_W), F32),
                   jax.ShapeDtypeStruct((N_TOK, ATTN_W), BF16), jax.ShapeDtypeStruct((N_TOK, 2 * KV_W), F32),
                   jax.ShapeDtypeStruct((D_MODEL, D_FF), BF16), jax.ShapeDtypeStruct((D_MODEL, D_FF), BF16),
                   jax.ShapeDtypeStruct((D_FF, D_MODEL), BF16)],
        compiler_params=pltpu.CompilerParams(dimension_semantics=("arbitrary",), vmem_limit_bytes=VMEM_LIMIT),
        name="ffn1_proj",
    )(*xs, g1, wg, wu, wd, gm, win_u, win_q, win_kv, *cast)


def _discretise(lam_re, lam_im, log_dt):
    dt = jnp.exp(log_dt)
    mag = jnp.exp(lam_re * dt)
    ang = lam_im * dt
    a_re = mag * jnp.cos(ang)
    a_im = mag * jnp.sin(ang)
    den = lam_re * lam_re + lam_im * lam_im
    nr = a_re - 1.0
    k_re = (nr * lam_re + a_im * lam_im) / den
    k_im = (a_im * lam_re - nr * lam_im) / den
    return a_re, a_im, k_re, k_im


def _powers(a_re, a_im, n):
    p_re = [jnp.ones_like(a_re)]
    p_im = [jnp.zeros_like(a_re)]
    for _ in range(n):
        r, i = p_re[-1], p_im[-1]
        p_re.append(r * a_re - i * a_im)
        p_im.append(r * a_im + i * a_re)
    return p_re, p_im


def _ssm_disc_kernel(lre_ref, lim_ref, ldt_ref, are_ref, aim_ref, kre_ref, kim_ref):
    a_re, a_im, k_re, k_im = _discretise(lre_ref[...], lim_ref[...], ldt_ref[...])
    are_ref[...] = a_re
    aim_ref[...] = a_im
    kre_ref[...] = k_re
    kim_ref[...] = k_im


def _ssm_disc(lam_re, lam_im, log_dt):
    n = DEPTH * SSM_G
    ldt = jnp.broadcast_to(log_dt[..., None], lam_re.shape)
    outs = pl.pallas_call(
        _ssm_disc_kernel,
        out_shape=[jax.ShapeDtypeStruct((n, SSM_N), F32)] * 4,
        name="ssm_disc",
    )(lam_re.reshape(n, SSM_N), lam_im.reshape(n, SSM_N), ldt.reshape(n, SSM_N))
    return [o.reshape(DEPTH, SSM_G, SSM_N) for o in outs]


def _ssm_prep_kernel(ar_re, ar_im, kr_re, kr_im, btr_re, btr_im, cr_re, cr_im, af_re, af_im,
                     wpair_ref, bfull_ref, bhalf_ref, cfull_ref, apw_ref):
    a_re, a_im, k_re, k_im = ar_re[...], ar_im[...], kr_re[...], kr_im[...]
    bb_re = k_re * btr_re[...] - k_im * btr_im[...]
    bb_im = k_re * btr_im[...] + k_im * btr_re[...]
    p_re, p_im = _powers(a_re, a_im, SSM_T)
    c_re, c_im = cr_re[...], cr_im[...]
    ca = [jnp.concatenate([c_re * p_re[t] - c_im * p_im[t], c_re * p_im[t] + c_im * p_re[t]], axis=1)
          for t in range(SSM_T + 1)]
    ca_all = jnp.concatenate(ca[:SSM_T], axis=0)
    bb2 = jnp.concatenate([bb_re, -bb_im], axis=1)
    drow = lax.dot_general(bb2, ca_all, (((1,), (1,)), ((), ())),
                           precision=lax.Precision.HIGHEST, preferred_element_type=F32)
    row_g = lax.broadcasted_iota(jnp.int32, drow.shape, 0) // SSM_P
    col_g = (lax.broadcasted_iota(jnp.int32, drow.shape, 1) % LANES) // SSM_P
    drow = jnp.where(row_g == col_g, drow, 0.0).astype(BF16)
    wpair_ref[0:LANES, :] = drow
    wpair_ref[LANES:, 0:LANES] = jnp.zeros((LANES, LANES), BF16)
    wpair_ref[LANES:, LANES:] = drow[:, :(SSM_T - 1) * LANES]

    row_grp = lax.broadcasted_iota(jnp.int32, (LANES, LANES), 0) // SSM_P
    lane_half = lax.broadcasted_iota(jnp.int32, (LANES, LANES), 1) // SSM_N
    for s in range(SSM_T):
        w_re, w_im = p_re[SSM_T - 1 - s], p_im[SSM_T - 1 - s]
        parts = [bb_re * w_re - bb_im * w_im, bb_re * w_im + bb_im * w_re]
        parts = [jnp.concatenate([v, v], axis=1) for v in parts]
        blk = jnp.concatenate([jnp.where(row_grp == 2 * k + lane_half, v, 0.0)
                               for v in parts for k in range(SLAB_N // LANES)], axis=1)
        bfull_ref[s * LANES:(s + 1) * LANES, :] = blk.astype(BF16)
        if s >= SSM_T - DEC_SEQ:
            bhalf_ref[(s - SSM_T + DEC_SEQ) * LANES:(s - SSM_T + DEC_SEQ + 1) * LANES, :] = blk

    lane_grp = lax.broadcasted_iota(jnp.int32, (SSM_N, LANES), 1) // SSM_P
    for t in range(SSM_T):
        cat = ca[t + 1].T
        ca_re, ca_im = cat[:SSM_N], cat[SSM_N:]
        cols = slice(t * LANES, (t + 1) * LANES)
        for g in range(SLAB_G):
            cfull_ref[g * SSM_N:(g + 1) * SSM_N, cols] = jnp.where(lane_grp == g, ca_re, 0.0).astype(BF16)
            cfull_ref[SLAB_N + g * SSM_N:SLAB_N + (g + 1) * SSM_N, cols] = (
                jnp.where(lane_grp == g, -ca_im, 0.0).astype(BF16))

    a_re, a_im = af_re[...], af_im[...]
    sq = []
    for _ in range(4):
        a_re, a_im = a_re * a_re - a_im * a_im, 2.0 * a_re * a_im
        sq.append((a_re, a_im))
    apw_ref[0:1, :] = sq[2][0]
    apw_ref[1:2, :] = sq[2][1]
    apw_ref[2:3, :] = sq[3][0]
    apw_ref[3:4, :] = sq[3][1]
    apw_ref[4:8, :] = jnp.zeros((4, SLAB_N), F32)


def _ssm_prep(lam_re, lam_im, log_dt, b_re, b_im, c_re, c_im):
    n = DEPTH * N_SLAB
    a_re, a_im, k_re, k_im = _ssm_disc(lam_re, lam_im, log_dt)
    rows = lambda v: jnp.repeat(v.reshape(n, SLAB_G, 1, SSM_N), SSM_P, axis=2).reshape(n, LANES, SSM_N)
    bt = lambda v: v.transpose(0, 1, 3, 2).reshape(n, LANES, SSM_N)
    flat = lambda v: v.reshape(n, 1, SLAB_N)
    args = ([rows(v) for v in (a_re, a_im, k_re, k_im)]
            + [bt(b_re), bt(b_im), c_re.reshape(n, LANES, SSM_N), c_im.reshape(n, LANES, SSM_N)]
            + [flat(a_re), flat(a_im)])

    def blk(a):
        return pl.BlockSpec((None,) + a.shape[1:], lambda i: (i, 0, 0))

    out_shape = [jax.ShapeDtypeStruct((n, 2 * LANES, SSM_T * LANES), BF16),
                 jax.ShapeDtypeStruct((n, SSM_T * LANES, 2 * SLAB_N), BF16),
                 jax.ShapeDtypeStruct((n, DEC_SEQ * LANES, 2 * SLAB_N), F32),
                 jax.ShapeDtypeStruct((n, 2 * SLAB_N, SSM_T * LANES), BF16),
                 jax.ShapeDtypeStruct((n, 8, SLAB_N), F32)]
    return pl.pallas_call(
        _ssm_prep_kernel,
        grid=(n,),
        in_specs=[blk(a) for a in args],
        out_specs=[blk(s) for s in out_shape],
        out_shape=out_shape,
        compiler_params=pltpu.CompilerParams(dimension_semantics=("parallel",), vmem_limit_bytes=VMEM_LIMIT),
        name="ssm_prep",
    )(*args)


def _ssm_prompt_kernel(u_ref, wpair_ref, bfull_ref, cfull_ref, apw_ref, y_ref, hst_ref,
                       ub_ref, s_ref, yacc_ref, *col_refs):
    rows = SSM_HALF_B * N_CHUNK
    n_col = SLAB_N // LANES
    sp_refs, hp_refs = col_refs[:n_col], col_refs[n_col:]
    for t in range(SSM_T):
        ub_ref[:, t * LANES:(t + 1) * LANES] = u_ref[pl.ds(t, rows, stride=SSM_T), :].astype(BF16)
    s_ref[...] = jnp.dot(ub_ref[...], bfull_ref[...], preferred_element_type=F32)
    for b in range(SSM_HALF_B):
        seq = slice(b * N_CHUNK, (b + 1) * N_CHUNK)
        for k in range(n_col):
            sp_refs[k][pl.ds(b, N_CHUNK, stride=8), :] = s_ref[seq, k * LANES:(k + 1) * LANES]
            sp_refs[k][pl.ds(SSM_HALF_B + b, N_CHUNK, stride=8), :] = (
                s_ref[seq, SLAB_N + k * LANES:SLAB_N + (k + 1) * LANES])
    upper = lax.broadcasted_iota(jnp.int32, (8, LANES), 0) < SSM_HALF_B
    a1 = [jnp.broadcast_to(apw_ref[2:3, k * LANES:(k + 1) * LANES], (8, LANES)) for k in range(n_col)]
    a2 = [jnp.where(upper, -1.0, 1.0) * jnp.broadcast_to(apw_ref[3:4, k * LANES:(k + 1) * LANES], (8, LANES))
          for k in range(n_col)]

    def step(c, h):
        r0 = pl.multiple_of(c * 8, 8)
        new = []
        for k in range(n_col):
            hp_refs[k][pl.ds(r0, 8), :] = h[k]
            new.append(a1[k] * h[k] + a2[k] * pltpu.roll(h[k], SSM_HALF_B, axis=0) + sp_refs[k][pl.ds(r0, 8), :])
        return tuple(new)

    h = lax.fori_loop(0, N_CHUNK, step, tuple(jnp.zeros((8, LANES), F32) for _ in range(n_col)), unroll=8)
    hst_ref[...] = jnp.concatenate(h, axis=1)
    hbc = jnp.concatenate(
        [jnp.concatenate([hp_refs[k][pl.ds(part + b, N_CHUNK, stride=8), :]
                          for part in (0, SSM_HALF_B) for k in range(n_col)], axis=1)
         for b in range(SSM_HALF_B)], axis=0)
    yacc_ref[...] = jnp.dot(hbc.astype(BF16), cfull_ref[...], preferred_element_type=F32)
    for t in range(0, SSM_T, 2):
        yacc_ref[:, t * LANES:] += jnp.dot(ub_ref[:, t * LANES:(t + 2) * LANES],
                                           wpair_ref[:, 0:(SSM_T - t) * LANES], preferred_element_type=F32)
    for t in range(SSM_T):
        y_ref[pl.ds(t, rows, stride=SSM_T), :] = yacc_ref[:, t * LANES:(t + 1) * LANES]


def _ssm_prompt(u, wpair, bfull, cfull, apw, layer):
    rows = SSM_HALF_B * N_CHUNK
    tok = SSM_HALF_B * SEQ

    def wspec(a):
        return pl.BlockSpec((None,) + a.shape[1:], lambda j, h: (layer * N_SLAB + j, 0, 0))

    return pl.pallas_call(
        _ssm_prompt_kernel,
        grid=(N_SLAB, BATCH // SSM_HALF_B),
        in_specs=[pl.BlockSpec((tok, LANES), lambda j, h: (h, j)), wspec(wpair), wspec(bfull), wspec(cfull), wspec(apw)],
        out_specs=[pl.BlockSpec((tok, LANES), lambda j, h: (h, j)),
                   pl.BlockSpec((None, 8, SLAB_N), lambda j, h: (h, 0, j))],
        out_shape=[jax.ShapeDtypeStruct((N_PROMPT, SSM_W), F32),
                   jax.ShapeDtypeStruct((BATCH // SSM_HALF_B, 8, SSM_G * SSM_N), F32)],
        scratch_shapes=[pltpu.VMEM((rows, SSM_T * LANES), BF16), pltpu.VMEM((rows, 2 * SLAB_N), F32),
                        pltpu.VMEM((rows, SSM_T * LANES), F32)]
                       + [pltpu.VMEM((N_CHUNK * 8, LANES), F32)] * (2 * SLAB_N // LANES),
        compiler_params=pltpu.CompilerParams(dimension_semantics=("arbitrary", "arbitrary"),
                                             vmem_limit_bytes=VMEM_LIMIT),
        name="ssm_prompt",
    )(u, wpair, bfull, cfull, apw)


def _ssm_sample_kernel(u_ref, h0r_ref, h0i_ref, wpair_ref, bhalf_ref, cfull_ref, apw_ref,
                       y_ref, hr_ref, hi_ref, uf_ref, yacc_ref):
    for t in range(DEC_SEQ):
        uf_ref[:, t * LANES:(t + 1) * LANES] = u_ref[pl.ds(t, DEC_BATCH, stride=DEC_SEQ), :]
    uf = uf_ref[...]
    h0r = h0r_ref[...]
    h0i = h0i_ref[...]
    ar = apw_ref[0:1, :]
    ai = apw_ref[1:2, :]
    s = jnp.dot(uf, bhalf_ref[...], precision=lax.Precision.HIGHEST, preferred_element_type=F32)
    hr_ref[...] = ar * h0r - ai * h0i + s[:, :SLAB_N]
    hi_ref[...] = ar * h0i + ai * h0r + s[:, SLAB_N:]
    ub = uf.astype(BF16)
    h0 = jnp.concatenate([h0r, h0i], axis=1).astype(BF16)
    yacc_ref[...] = jnp.dot(h0, cfull_ref[...], preferred_element_type=F32)
    for t in range(0, DEC_SEQ, 2):
        yacc_ref[:, t * LANES:] += jnp.dot(ub[:, t * LANES:(t + 2) * LANES],
                                           wpair_ref[:, 0:(DEC_SEQ - t) * LANES], preferred_element_type=F32)
    for t in range(DEC_SEQ):
        y_ref[pl.ds(t, DEC_BATCH, stride=DEC_SEQ), :] = yacc_ref[:, t * LANES:(t + 1) * LANES]


def _ssm_sample(u, h0re, h0im, wpair, bhalf, cfull, apw, layer):
    width = DEC_SEQ * LANES

    def wspec(r, c):
        return pl.BlockSpec((None, r, c), lambda j: (layer * N_SLAB + j, 0, 0))

    st = pl.BlockSpec((DEC_BATCH, SLAB_N), lambda j: (0, j))
    return pl.pallas_call(
        _ssm_sample_kernel,
        grid=(N_SLAB,),
        in_specs=[pl.BlockSpec((N_SAMPLE, LANES), lambda j: (N_PROMPT // N_SAMPLE, j)), st, st,
                  wspec(2 * LANES, width), wspec(width, 2 * SLAB_N), wspec(2 * SLAB_N, width), wspec(8, SLAB_N)],
        out_specs=[pl.BlockSpec((N_SAMPLE, LANES), lambda j: (0, j)), st, st],
        out_shape=[jax.ShapeDtypeStruct((N_SAMPLE, SSM_W), F32),
                   jax.ShapeDtypeStruct((DEC_BATCH, SSM_G * SSM_N), F32),
                   jax.ShapeDtypeStruct((DEC_BATCH, SSM_G * SSM_N), F32)],
        scratch_shapes=[pltpu.VMEM((DEC_BATCH, width), F32), pltpu.VMEM((DEC_BATCH, width), F32)],
        compiler_params=pltpu.CompilerParams(dimension_semantics=("arbitrary",), vmem_limit_bytes=VMEM_LIMIT),
        name="ssm_sample",
    )(u, h0re, h0im, wpair, bhalf, cfull, apw)


def _mask_cap(valid):
    return jnp.where(valid, jnp.inf, NEG_INF).astype(F32)


def _sink_softmax(s, cap, sink):
    s = jnp.minimum(s, cap)
    m = jnp.maximum(jnp.max(s, axis=-1, keepdims=True), sink)
    e = jnp.exp(s - m)
    denom = jnp.sum(e, axis=-1, keepdims=True) + jnp.exp(sink - m)
    return e.astype(BF16), denom


def _scores(q, scale_mask, kk):
    qx = jnp.concatenate([q[:, c * KV_W:(c + 1) * KV_W] * scale_mask for c in range(GRP)], axis=0)
    return lax.dot_general(qx.astype(BF16), kk, (((1,), (1,)), ((), ())), preferred_element_type=F32)


def _band_valid(n_q, n_k, rows):
    qi = lax.broadcasted_iota(jnp.int32, (rows, n_k), 0) & (n_q - 1)
    kj = lax.broadcasted_iota(jnp.int32, (rows, n_k), 1)
    return kj, ((kj - qi - 1).astype(jnp.uint32) < WINDOW)


def _head_masks():
    lane_kv = lax.broadcasted_iota(jnp.int32, (1, KV_W), 1) // HEAD_DIM
    return [jnp.where(lane_kv == h, HEAD_DIM ** -0.5, 0.0) for h in range(N_KV)], lane_kv == 0


ATTN_Q_TILE = 1024


def _attn_prompt_kernel(q_ref, kvp_ref, kvc_ref, sink_ref, o_ref):
    cols = GRP * WINDOW
    scale_masks, _ = _head_masks()
    kj = lax.broadcasted_iota(jnp.int32, (2 * WINDOW, cols), 0)
    qi = lax.broadcasted_iota(jnp.int32, (2 * WINDOW, cols), 1) & (WINDOW - 1)
    valid = (kj - qi - 1).astype(jnp.uint32) < WINDOW
    cap_first = _mask_cap(valid & ((kj >= WINDOW) | (pl.program_id(1) > 0)))
    cap = _mask_cap(valid)
    n_blk = ATTN_Q_TILE // WINDOW

    def stage(r):
        cur = kvc_ref[r * WINDOW:(r + 1) * WINDOW, :]
        prev = kvp_ref[...] if r == 0 else kvc_ref[(r - 1) * WINDOW:r * WINDOW, :]
        kk = jnp.concatenate([prev[:, :KV_W], cur[:, :KV_W]], axis=0).astype(BF16)
        v_t = jnp.concatenate([prev[:, KV_W:], cur[:, KV_W:]], axis=0).T.astype(BF16)
        q = q_ref[r * WINDOW:(r + 1) * WINDOW, :].astype(F32)
        s_t = []
        for h in range(N_KV):
            qx = jnp.concatenate([q[:, c * KV_W:(c + 1) * KV_W] * scale_masks[h] for c in range(GRP)], axis=0)
            s_t.append(lax.dot_general(kk, qx.astype(BF16), (((1,), (1,)), ((), ())), preferred_element_type=F32))
        return s_t, v_t

    def softmax_t(s, cap_r, sink):
        s = jnp.minimum(s, cap_r)
        m = jnp.maximum(jnp.max(s, axis=0, keepdims=True), sink)
        e = jnp.exp(s - m)
        return e.astype(BF16), jnp.sum(e, axis=0, keepdims=True) + jnp.exp(sink - m)

    staged = [stage(r) for r in range(n_blk)]
    weights = [[softmax_t(staged[r][0][h], cap_first if r == 0 else cap, sink_ref[h]) for h in range(N_KV)]
               for r in range(n_blk)]
    for r in range(n_blk):
        v_t = staged[r][1]
        outs = [jnp.dot(v_t, e, preferred_element_type=F32) / denom for e, denom in weights[r]]
        o_t = jnp.concatenate([outs[h][h * HEAD_DIM:(h + 1) * HEAD_DIM] for h in range(N_KV)], axis=0)
        for c in range(GRP):
            o_ref[r * WINDOW:(r + 1) * WINDOW, c * KV_W:(c + 1) * KV_W] = (
                o_t[:, c * WINDOW:(c + 1) * WINDOW].T.astype(o_ref.dtype))


def _attn_prompt(q, kv, sink_col):
    tiles = SEQ // ATTN_Q_TILE
    per_tile = ATTN_Q_TILE // WINDOW
    blocks = SEQ // WINDOW
    return pl.pallas_call(
        _attn_prompt_kernel,
        grid=(BATCH, tiles),
        in_specs=[pl.BlockSpec((ATTN_Q_TILE, ATTN_W), lambda b, j: (b * tiles + j, 0)),
                  pl.BlockSpec((WINDOW, 2 * KV_W), lambda b, j: (b * blocks + jnp.maximum(j * per_tile - 1, 0), 0)),
                  pl.BlockSpec((ATTN_Q_TILE, 2 * KV_W), lambda b, j: (b * tiles + j, 0)),
                  pl.BlockSpec((N_KV, 1, GRP * WINDOW), lambda b, j: (0, 0, 0))],
        out_specs=pl.BlockSpec((ATTN_Q_TILE, ATTN_W), lambda b, j: (b * tiles + j, 0)),
        out_shape=jax.ShapeDtypeStruct((N_PROMPT, ATTN_W), BF16),
        compiler_params=pltpu.CompilerParams(dimension_semantics=("parallel", "arbitrary")),
        name="attn_prompt",
    )(q, kv, kv, sink_col)


SAMPLE_SEQ_BLOCK = 8


def _attn_sample_kernel(q_ref, ckt_ref, cvt_ref, kvn_ref, sink_ref, *rest, n_carried):
    o_ref, nkt_ref, nvt_ref = rest[n_carried:]
    rows = GRP * DEC_SEQ
    kept = WINDOW - DEC_SEQ
    nt = (((1,), (1,)), ((), ()))
    t = lax.broadcasted_iota(jnp.int32, (rows, WINDOW), 0) & (DEC_SEQ - 1)
    lane = lax.broadcasted_iota(jnp.int32, (rows, WINDOW), 1)
    cap_old = _mask_cap(lane > t)
    cap_new = _mask_cap((lane >= kept) & (lane - kept <= t))
    keep_old = lax.broadcasted_iota(jnp.int32, (HEAD_DIM, WINDOW), 1) < kept
    q_all = q_ref[...].astype(F32) * (HEAD_DIM ** -0.5)
    place = (lax.broadcasted_iota(jnp.int32, (DEC_SEQ, WINDOW), 1)
             == lax.broadcasted_iota(jnp.int32, (DEC_SEQ, WINDOW), 0) + kept).astype(F32)
    toks = [slice(s * DEC_SEQ, (s + 1) * DEC_SEQ) for s in range(SAMPLE_SEQ_BLOCK)]
    new_ts = [lax.dot_general(kvn_ref[tok, :], place, (((0,), (0,)), ((), ())),
                              precision=lax.Precision.HIGHEST, preferred_element_type=F32) for tok in toks]
    scores, values = [], []
    for s, tok in enumerate(toks):
        new_t = new_ts[s]
        for h in range(N_KV):
            ch = slice(h * HEAD_DIM, (h + 1) * HEAD_DIM)
            new_k, new_v = new_t[ch], new_t[KV_W:][ch]
            old_k, old_v = ckt_ref[s, h], cvt_ref[s, h]
            nkt_ref[s, h] = jnp.where(keep_old, pltpu.roll(old_k, kept, axis=1), new_k)
            nvt_ref[s, h] = jnp.where(keep_old, pltpu.roll(old_v, kept, axis=1), new_v)
            q = jnp.concatenate([q_all[tok, c * KV_W + h * HEAD_DIM:c * KV_W + (h + 1) * HEAD_DIM]
                                 for c in range(GRP)], axis=0).astype(BF16)
            scores.append((jnp.dot(q, old_k.astype(BF16), preferred_element_type=F32),
                           jnp.dot(q, new_k.astype(BF16), preferred_element_type=F32)))
            values.append((old_v.astype(BF16), new_v.astype(BF16)))
    weights = []
    for i, (s_old, s_new) in enumerate(scores):
        sink = sink_ref[i % N_KV]
        s_old = jnp.minimum(s_old, cap_old)
        s_new = jnp.minimum(s_new, cap_new)
        m = jnp.maximum(jnp.maximum(jnp.max(s_old, axis=-1, keepdims=True), jnp.max(s_new, axis=-1, keepdims=True)),
                        sink)
        e_old = jnp.exp(s_old - m)
        e_new = jnp.exp(s_new - m)
        denom = (jnp.sum(e_old, axis=-1, keepdims=True) + jnp.sum(e_new, axis=-1, keepdims=True)
                 + jnp.exp(sink - m))
        weights.append((e_old.astype(BF16), e_new.astype(BF16), denom))
    for s, tok in enumerate(toks):
        outs = []
        for h in range(N_KV):
            e_old, e_new, denom = weights[s * N_KV + h]
            v_old, v_new = values[s * N_KV + h]
            outs.append((lax.dot_general(e_old, v_old, nt, preferred_element_type=F32)
                         + lax.dot_general(e_new, v_new, nt, preferred_element_type=F32)) / denom)
        for c in range(GRP):
            o_ref[tok, c * KV_W:(c + 1) * KV_W] = jnp.concatenate(
                [o[c * DEC_SEQ:(c + 1) * DEC_SEQ] for o in outs], axis=1)


def _attn_sample(q, kv, ck, cv, sink_col, layer, windows):
    sb = SAMPLE_SEQ_BLOCK
    tok = sb * DEC_SEQ
    first = N_PROMPT // tok
    c_spec = pl.BlockSpec((None, sb, N_KV, HEAD_DIM, WINDOW), lambda i: (layer, i, 0, 0, 0))
    in_specs = [pl.BlockSpec((tok, ATTN_W), lambda i: (first + i, 0)), c_spec, c_spec,
                pl.BlockSpec((tok, 2 * KV_W), lambda i: (first + i, 0)),
                pl.BlockSpec((N_KV, GRP * DEC_SEQ, 1), lambda i: (0, 0, 0))]
    args = [q, ck, cv, kv, sink_col]
    aliases = {}
    if windows is not None:
        aliases = {len(args): 1, len(args) + 1: 2}
        in_specs += [pl.BlockSpec(memory_space=pl.ANY)] * 2
        args += list(windows)
    return pl.pallas_call(
        functools.partial(_attn_sample_kernel, n_carried=len(aliases)),
        grid=(DEC_BATCH // sb,),
        in_specs=in_specs,
        out_specs=[pl.BlockSpec((tok, ATTN_W), lambda i: (i, 0)), c_spec, c_spec],
        out_shape=[jax.ShapeDtypeStruct((N_SAMPLE, ATTN_W), F32),
                   jax.ShapeDtypeStruct((DEPTH, DEC_BATCH, N_KV, HEAD_DIM, WINDOW), F32),
                   jax.ShapeDtypeStruct((DEPTH, DEC_BATCH, N_KV, HEAD_DIM, WINDOW), F32)],
        input_output_aliases=aliases,
        compiler_params=pltpu.CompilerParams(dimension_semantics=("parallel",)),
        name="attn_sample",
    )(*args)


def _mix_out(x, y, u, o, d_ref, wglu_ref, gs_ref, ga_ref, wout_s_ref, wout_a_ref):
    y = y + d_ref[...] * u
    y = jax.nn.gelu(y)
    y = y * jax.nn.sigmoid(jnp.dot(y.astype(BF16), wglu_ref[...], preferred_element_type=F32))
    ys = _rms(y, gs_ref[...]).astype(BF16)
    ya = _rms(o, ga_ref[...]).astype(BF16)
    mix = jnp.dot(ys, wout_s_ref[...], preferred_element_type=F32)
    mix = mix + jnp.dot(ya, wout_a_ref[...], preferred_element_type=F32)
    return x + mix


def _tail_kernel(x_ref, yp_ref, ys_ref, u_ref, op_ref, os_ref, pp_ref, ps_ref,
                 d_ref, wglu_ref, gs_ref, ga_ref, wout_s_ref, wout_a_ref,
                 g2_ref, wg_ref, wu_ref, wd_ref, gp_ref, wpg_ref, wpp_ref, gf_ref,
                 *out_refs, final):
    x, y, u, o = x_ref[...], _pick(yp_ref, ys_ref), u_ref[...], _pick(op_ref, os_ref)
    x1 = jnp.concatenate([_mix_out(x[r], y[r], u[r], o[r], d_ref, wglu_ref, gs_ref, ga_ref, wout_s_ref, wout_a_ref)
                          for r in _row_slices(EPILOGUE_SLICES)], axis=0)
    x2 = _half_swiglu(x1, g2_ref, wg_ref, wu_ref, wd_ref)
    p = _pick(pp_ref, ps_ref).astype(BF16)
    x3 = []
    for r in _row_slices(EPILOGUE_SLICES):
        h = _rms(x2[r], gp_ref[...]).astype(BF16)
        gate = jax.nn.sigmoid(jnp.dot(h, wpg_ref[...], preferred_element_type=F32))
        proj = jnp.dot(p[r], wpp_ref[...], preferred_element_type=F32)
        x3.append(x2[r] + proj * gate)
    x3 = jnp.concatenate(x3, axis=0)
    if not final:
        out_refs[0][...] = x3
        return
    y = _rms(x3, gf_ref[...])
    yp_ref, ys_ref = out_refs

    @pl.when(pl.program_id(0) < PROMPT_TILES)
    def _():
        yp_ref[...] = y

    @pl.when(pl.program_id(0) >= PROMPT_TILES)
    def _():
        ys_ref[...] = y


def _layer_tail(x, y_prompt, y_sample, u, o_prompt, o_sample, p_prompt, p_sample, layer,
                d, wglu, gs, ga, wout_ssm, wout_attn, g2, wg, wu, wd, gp, wpg, wpp, gf, final):
    if final:
        out_specs = _pair_specs(D_MODEL)
        out_shape = [jax.ShapeDtypeStruct((N_PROMPT, D_MODEL), F32), jax.ShapeDtypeStruct((N_SAMPLE, D_MODEL), F32)]
    else:
        out_specs = _row_spec(D_MODEL)
        out_shape = jax.ShapeDtypeStruct((N_TOK, D_MODEL), F32)
    p_specs = [pl.BlockSpec((None, TOKEN_TILE, PLE_DIM), lambda i: (layer, jnp.minimum(i, PROMPT_TILES - 1), 0)),
               pl.BlockSpec((None, TOKEN_TILE, PLE_DIM), lambda i: (layer, jnp.maximum(i - PROMPT_TILES, 0), 0))]
    return pl.pallas_call(
        functools.partial(_tail_kernel, final=final),
        grid=(N_TOK // TOKEN_TILE,),
        in_specs=[_row_spec(D_MODEL)] + _pair_specs(SSM_W) + [_row_spec(SSM_W)] + _pair_specs(ATTN_W) + p_specs + [
            _const_spec((1, SSM_W)), _const_spec((SSM_W, SSM_W)), _const_spec((1, SSM_W)), _const_spec((1, ATTN_W)),
            _const_spec((SSM_W, D_MODEL)), _const_spec((ATTN_W, D_MODEL)),
            _const_spec((1, D_MODEL)),
            _const_spec((D_MODEL, D_FF)), _const_spec((D_MODEL, D_FF)), _const_spec((D_FF, D_MODEL)),
            _const_spec((1, D_MODEL)), _const_spec((D_MODEL, D_MODEL)), _const_spec((PLE_DIM, D_MODEL)),
            _const_spec((1, D_MODEL))],
        out_specs=out_specs,
        out_shape=out_shape,
        compiler_params=pltpu.CompilerParams(dimension_semantics=("arbitrary",), vmem_limit_bytes=VMEM_LIMIT),
        name="layer_tail",
    )(x, y_prompt, y_sample, u, o_prompt, o_sample, p_prompt, p_sample,
      d, wglu, gs, ga, wout_ssm, wout_attn, g2, wg, wu, wd, gp, wpg, wpp, gf)


def _sink_column(sinks, rows_per_head):
    return jnp.repeat(sinks.astype(F32).reshape(N_KV, GRP), rows_per_head, axis=1)[..., None]


def _head_major(a, axis):
    shape = a.shape
    a = a.reshape(shape[:axis] + (N_KV, GRP, HEAD_DIM) + shape[axis + 1:])
    return jnp.swapaxes(a, axis, axis + 1).reshape(shape)


def kernel(x_prompt, x_sample, cache_k, cache_v, state_ssm_re, state_ssm_im, p_prompt, p_sample, ffn1_norm, ffn1_w_gate, ffn1_w_up, ffn1_w_down, mix_norm, w_in, ssm_lam_re, ssm_lam_im, ssm_log_dt, ssm_b_re, ssm_b_im, ssm_c_re, ssm_c_im, ssm_d, ssm_w_glu, ssm_out_norm, attn_sinks, attn_out_norm, w_out, ffn2_norm, ffn2_w_gate, ffn2_w_up, ffn2_w_down, ple_norm, ple_w_gate, ple_w_proj, final_norm):
    wpair, bfull, bhalf, cfull, apw = _ssm_prep(ssm_lam_re, ssm_lam_im, ssm_log_dt, ssm_b_re, ssm_b_im,
                                                ssm_c_re, ssm_c_im)
    row = lambda v: v.reshape(1, -1)
    bf = lambda w: w.astype(BF16)
    xs = (x_prompt.reshape(N_PROMPT, D_MODEL), x_sample.reshape(N_SAMPLE, D_MODEL))
    n_state = SSM_G * SSM_N

    caches = [c.transpose(0, 1, 3, 4, 2) for c in (cache_k, cache_v)]
    windows = None

    kp_l, vp_l, hpr_l, hpi_l, hsr_l, hsi_l = [], [], [], [], [], []
    for i in range(DEPTH):
        x, u, q, kv, *ffn2_w = _ffn_proj(
            xs, row(ffn1_norm[i]), bf(ffn1_w_gate[i]), bf(ffn1_w_up[i]), bf(ffn1_w_down[i]),
            row(mix_norm[i]), bf(w_in[i, :, :SSM_W]),
            _head_major(bf(w_in[i, :, SSM_W:SSM_W + ATTN_W]), 1), bf(w_in[i, :, SSM_W + ATTN_W:]),
            cast=(ffn2_w_gate, ffn2_w_up, ffn2_w_down), layer=i)

        y_p, hst = _ssm_prompt(u, wpair, bfull, cfull, apw, i)
        hst = hst.reshape(BATCH // SSM_HALF_B, 2, SSM_HALF_B, SSM_G, SSM_N)
        hpr_l.append(hst[:, 0].reshape(BATCH, SSM_G, SSM_N))
        hpi_l.append(hst[:, 1].reshape(BATCH, SSM_G, SSM_N))
        y_s, hsr, hsi = _ssm_sample(u, state_ssm_re[i].reshape(DEC_BATCH, n_state),
                                    state_ssm_im[i].reshape(DEC_BATCH, n_state), wpair, bhalf, cfull, apw, i)
        hsr_l.append(hsr.reshape(DEC_BATCH, SSM_G, SSM_N))
        hsi_l.append(hsi.reshape(DEC_BATCH, SSM_G, SSM_N))

        kv_last = jnp.stack([kv[(b + 1) * SEQ - WINDOW:(b + 1) * SEQ] for b in range(BATCH)])
        kv_last = kv_last.reshape(BATCH, WINDOW, 2, N_KV, HEAD_DIM)
        kp_l.append(kv_last[:, :, 0])
        vp_l.append(kv_last[:, :, 1])
        o_p = _attn_prompt(q, kv, _sink_column(attn_sinks[i], WINDOW).reshape(N_KV, 1, GRP * WINDOW))
        o_s, *windows = _attn_sample(q, kv, caches[0], caches[1], _sink_column(attn_sinks[i], DEC_SEQ), i, windows)

        x = _layer_tail(x, y_p, y_s, u, o_p, o_s,
                        p_prompt.reshape(DEPTH, N_PROMPT, PLE_DIM), p_sample.reshape(DEPTH, N_SAMPLE, PLE_DIM), i,
                        row(ssm_d[i]), bf(ssm_w_glu[i]), row(ssm_out_norm[i]), row(_head_major(attn_out_norm[i], 0)),
                        bf(w_out[i, :SSM_W]), _head_major(bf(w_out[i, SSM_W:]), 0),
                        row(ffn2_norm[i]), *ffn2_w,
                        row(ple_norm[i]), bf(ple_w_gate[i]), bf(ple_w_proj[i]), row(final_norm),
                        final=(i == DEPTH - 1))
        xs = (x,)

    y_prompt, y_sample = x
    ks, vs = (w.transpose(0, 1, 4, 2, 3) for w in windows)
    return (y_prompt.reshape(BATCH, SEQ, D_MODEL), y_sample.reshape(DEC_BATCH, DEC_SEQ, D_MODEL),
            jnp.stack(kp_l), jnp.stack(vp_l), jnp.stack(hpr_l), jnp.stack(hpi_l),
            ks, vs, jnp.stack(hsr_l), jnp.stack(hsi_l))
```

```python
import functools

import jax
import jax.numpy as jnp
from jax import lax
from jax.experimental import pallas as pl
from jax.experimental.pallas import tpu as pltpu

F32 = jnp.float32
BF16 = jnp.bfloat16

D_MODEL = 1024
BATCH = 8
SEQ = 2048
DEPTH = 2
DEC_BATCH = 128
DEC_SEQ = 8
SSM_W = 512
SSM_P = 16
SSM_G = 32
SSM_N = 64
ATTN_W = 512
HEAD_DIM = 64
N_HEADS = 8
N_KV = 2
GRP = N_HEADS // N_KV
KV_W = N_KV * HEAD_DIM
IN_W = SSM_W + ATTN_W + 2 * KV_W
WINDOW = 128
D_FF = 2816
PLE_DIM = 256
EPS = 1e-6
NEG_INF = -1e30

N_PROMPT = BATCH * SEQ
N_SAMPLE = DEC_BATCH * DEC_SEQ
N_TOK = N_PROMPT + N_SAMPLE

LANES = 128
SSM_T = 16
N_CHUNK = SEQ // SSM_T
SLAB_G = LANES // SSM_P
N_SLAB = SSM_G // SLAB_G
SLAB_N = SLAB_G * SSM_N
SSM_HALF_B = BATCH // 2

TOKEN_TILE = 512
PROMPT_TILES = N_PROMPT // TOKEN_TILE
TILES_PER_SEQ = SEQ // TOKEN_TILE
FF_CHUNK = 256
FF_STEPS = D_FF // FF_CHUNK
VMEM_LIMIT = 56 * 1024 * 1024


def _const_spec(shape):
    nd = len(shape)
    return pl.BlockSpec(shape, lambda *_: (0,) * nd, pipeline_mode=pl.Buffered(1))


def _layer_spec(shape, layer, at=None):
    at = at or (0,) * len(shape)
    return pl.BlockSpec((None,) + shape, lambda *_: (layer,) + at, pipeline_mode=pl.Buffered(1))


def _row_spec(width):
    return pl.BlockSpec((TOKEN_TILE, width), lambda i: (i, 0))


def _pair_specs(width):
    return [pl.BlockSpec((TOKEN_TILE, width), lambda i: (jnp.minimum(i, PROMPT_TILES - 1), 0)),
            pl.BlockSpec((TOKEN_TILE, width), lambda i: (jnp.maximum(i - PROMPT_TILES, 0), 0))]


def _pick(prompt_ref, sample_ref):
    return jnp.where(pl.program_id(0) < PROMPT_TILES, prompt_ref[...].astype(F32), sample_ref[...].astype(F32))


EPILOGUE_SLICES = 2


def _row_slices(n):
    rows = TOKEN_TILE // n
    return [slice(i * rows, (i + 1) * rows) for i in range(n)]


def _rms(x, g):
    return x * lax.rsqrt(jnp.mean(x * x, axis=-1, keepdims=True) + EPS) * g


def _half_swiglu(xf, g_ref, wg_ref, wu_ref, wd_ref):
    h = _rms(xf, g_ref[...]).astype(BF16)
    acc = jnp.zeros(xf.shape, F32)
    for c in range(0, D_FF, FF_CHUNK):
        gate = jnp.dot(h, wg_ref[:, c:c + FF_CHUNK], preferred_element_type=F32)
        up = jnp.dot(h, wu_ref[:, c:c + FF_CHUNK], preferred_element_type=F32)
        act = (gate * jax.nn.sigmoid(gate) * up).astype(BF16)
        acc = acc + jnp.dot(act, wd_ref[c:c + FF_CHUNK, :], preferred_element_type=F32)
    return xf + 0.5 * acc


def _ffn_proj_kernel(*refs, paired):
    if paired:
        xp_ref, xs_ref, *refs = refs
        x = _pick(xp_ref, xs_ref)
    else:
        x_ref, *refs = refs
        x = x_ref[...]
    (g1_ref, wg_ref, wu_ref, wd_ref, gm_ref, wu_in_ref, wq_in_ref, wkv_in_ref, *cast_in,
     xo_ref, u_ref, q_ref, kv_ref, kv_tail_ref, cg_ref, cu_ref, cd_ref) = refs

    @pl.when(pl.program_id(0) < FF_STEPS)
    def _():
        for src, dst in zip(cast_in, (cg_ref, cu_ref, cd_ref)):
            dst[...] = src[...].astype(BF16)

    x1 = _half_swiglu(x, g1_ref, wg_ref, wu_ref, wd_ref)
    xo_ref[...] = x1
    for r in _row_slices(EPILOGUE_SLICES):
        h = _rms(x1[r], gm_ref[...]).astype(BF16)
        u_ref[r, :] = jnp.dot(h, wu_in_ref[...], preferred_element_type=F32)
        q_ref[r, :] = jnp.dot(h, wq_in_ref[...], preferred_element_type=F32).astype(BF16)
        kv_ref[r, :] = jnp.dot(h, wkv_in_ref[...], preferred_element_type=F32)

    step = pl.program_id(0)

    @pl.when((step < PROMPT_TILES) & (step % TILES_PER_SEQ == TILES_PER_SEQ - 1))
    def _():
        kv_tail_ref[...] = kv_ref[TOKEN_TILE - WINDOW:, :]


def _ffn_chunk_specs(layer=None):
    chunk = lambda i: jnp.minimum(i, FF_STEPS - 1)
    if layer is None:
        return [pl.BlockSpec((D_MODEL, FF_CHUNK), lambda i: (0, chunk(i))),
                pl.BlockSpec((D_MODEL, FF_CHUNK), lambda i: (0, chunk(i))),
                pl.BlockSpec((FF_CHUNK, D_MODEL), lambda i: (chunk(i), 0))]
    return [pl.BlockSpec((None, D_MODEL, FF_CHUNK), lambda i: (layer, 0, chunk(i))),
            pl.BlockSpec((None, D_MODEL, FF_CHUNK), lambda i: (layer, 0, chunk(i))),
            pl.BlockSpec((None, FF_CHUNK, D_MODEL), lambda i: (layer, chunk(i), 0))]


def _ffn_proj(xs, g1, wg, wu, wd, gm, w_in, w_in_q, cast, layer):
    paired = len(xs) == 2
    x_specs = _pair_specs(D_MODEL) if paired else [_row_spec(D_MODEL)]
    kv_col = (SSM_W + ATTN_W) // (2 * KV_W)
    return pl.pallas_call(
        functools.partial(_ffn_proj_kernel, paired=paired),
        grid=(N_TOK // TOKEN_TILE,),
        in_specs=x_specs + [_const_spec((1, D_MODEL)),
                            _layer_spec((D_MODEL, D_FF), layer), _layer_spec((D_MODEL, D_FF), layer),
                            _layer_spec((D_FF, D_MODEL), layer),
                            _const_spec((1, D_MODEL)), _layer_spec((D_MODEL, SSM_W), layer),
                            _layer_spec((D_MODEL, ATTN_W), layer), _layer_spec((D_MODEL, 2 * KV_W), layer, (0, kv_col))]
                 + _ffn_chunk_specs(layer),
        out_specs=[_row_spec(D_MODEL), _row_spec(SSM_W), _row_spec(ATTN_W), _row_spec(2 * KV_W),
                   pl.BlockSpec((WINDOW, 2 * KV_W), lambda i: (jnp.minimum(i // TILES_PER_SEQ, BATCH - 1), 0))]
                  + _ffn_chunk_specs(),
        out_shape=[jax.ShapeDtypeStruct((N_TOK, D_MODEL), F32), jax.ShapeDtypeStruct((N_TOK, SSM_W), F32),
                   jax.ShapeDtypeStruct((N_TOK, ATTN_W), BF16), jax.ShapeDtypeStruct((N_TOK, 2 * KV_W), F32),
                   jax.ShapeDtypeStruct((BATCH * WINDOW, 2 * KV_W), F32),
                   jax.ShapeDtypeStruct((D_MODEL, D_FF), BF16), jax.ShapeDtypeStruct((D_MODEL, D_FF), BF16),
                   jax.ShapeDtypeStruct((D_FF, D_MODEL), BF16)],
        compiler_params=pltpu.CompilerParams(dimension_semantics=("arbitrary",), vmem_limit_bytes=VMEM_LIMIT),
        name="ffn1_proj",
    )(*xs, g1, wg, wu, wd, gm, w_in, w_in_q, w_in, *cast)


def _discretise(lam_re, lam_im, log_dt):
    dt = jnp.exp(log_dt)
    mag = jnp.exp(lam_re * dt)
    ang = lam_im * dt
    a_re = mag * jnp.cos(ang)
    a_im = mag * jnp.sin(ang)
    den = lam_re * lam_re + lam_im * lam_im
    nr = a_re - 1.0
    k_re = (nr * lam_re + a_im * lam_im) / den
    k_im = (a_im * lam_re - nr * lam_im) / den
    return a_re, a_im, k_re, k_im


def _powers(a_re, a_im, n):
    p_re = [jnp.ones_like(a_re)]
    p_im = [jnp.zeros_like(a_re)]
    for _ in range(n):
        r, i = p_re[-1], p_im[-1]
        p_re.append(r * a_re - i * a_im)
        p_im.append(r * a_im + i * a_re)
    return p_re, p_im


def _ssm_disc_kernel(lre_ref, lim_ref, ldt_ref, are_ref, aim_ref, kre_ref, kim_ref):
    a_re, a_im, k_re, k_im = _discretise(lre_ref[...], lim_ref[...], ldt_ref[...])
    are_ref[...] = a_re
    aim_ref[...] = a_im
    kre_ref[...] = k_re
    kim_ref[...] = k_im


def _ssm_disc(lam_re, lam_im, log_dt):
    n = DEPTH * SSM_G
    ldt = jnp.broadcast_to(log_dt[..., None], lam_re.shape)
    outs = pl.pallas_call(
        _ssm_disc_kernel,
        out_shape=[jax.ShapeDtypeStruct((n, SSM_N), F32)] * 4,
        name="ssm_disc",
    )(lam_re.reshape(n, SSM_N), lam_im.reshape(n, SSM_N), ldt.reshape(n, SSM_N))
    return [o.reshape(DEPTH, SSM_G, SSM_N) for o in outs]


def _ssm_prep_kernel(ar_re, ar_im, kr_re, kr_im, btr_re, btr_im, cr_re, cr_im, af_re, af_im,
                     wpair_ref, bfull_ref, bhalf_ref, cfull_ref, apw_ref):
    a_re, a_im, k_re, k_im = ar_re[...], ar_im[...], kr_re[...], kr_im[...]
    bb_re = k_re * btr_re[...] - k_im * btr_im[...]
    bb_im = k_re * btr_im[...] + k_im * btr_re[...]
    p_re, p_im = _powers(a_re, a_im, SSM_T)
    c_re, c_im = cr_re[...], cr_im[...]
    ca = [jnp.concatenate([c_re * p_re[t] - c_im * p_im[t], c_re * p_im[t] + c_im * p_re[t]], axis=1)
          for t in range(SSM_T + 1)]
    ca_all = jnp.concatenate(ca[:SSM_T], axis=0)
    bb2 = jnp.concatenate([bb_re, -bb_im], axis=1)
    drow = lax.dot_general(bb2, ca_all, (((1,), (1,)), ((), ())),
                           precision=lax.Precision.HIGHEST, preferred_element_type=F32)
    row_g = lax.broadcasted_iota(jnp.int32, drow.shape, 0) // SSM_P
    col_g = (lax.broadcasted_iota(jnp.int32, drow.shape, 1) % LANES) // SSM_P
    drow = jnp.where(row_g == col_g, drow, 0.0).astype(BF16)
    wpair_ref[0:LANES, :] = drow
    wpair_ref[LANES:, 0:LANES] = jnp.zeros((LANES, LANES), BF16)
    wpair_ref[LANES:, LANES:] = drow[:, :(SSM_T - 1) * LANES]

    row_grp = lax.broadcasted_iota(jnp.int32, (LANES, LANES), 0) // SSM_P
    lane_half = lax.broadcasted_iota(jnp.int32, (LANES, LANES), 1) // SSM_N
    for s in range(SSM_T):
        w_re, w_im = p_re[SSM_T - 1 - s], p_im[SSM_T - 1 - s]
        parts = [bb_re * w_re - bb_im * w_im, bb_re * w_im + bb_im * w_re]
        parts = [jnp.concatenate([v, v], axis=1) for v in parts]
        blk = jnp.concatenate([jnp.where(row_grp == 2 * k + lane_half, v, 0.0)
                               for v in parts for k in range(SLAB_N // LANES)], axis=1)
        bfull_ref[s * LANES:(s + 1) * LANES, :] = blk.astype(BF16)
        if s >= SSM_T - DEC_SEQ:
            bhalf_ref[(s - SSM_T + DEC_SEQ) * LANES:(s - SSM_T + DEC_SEQ + 1) * LANES, :] = blk

    lane_grp = lax.broadcasted_iota(jnp.int32, (SSM_N, LANES), 1) // SSM_P
    for t in range(SSM_T):
        cat = ca[t + 1].T
        ca_re, ca_im = cat[:SSM_N], cat[SSM_N:]
        cols = slice(t * LANES, (t + 1) * LANES)
        for g in range(SLAB_G):
            cfull_ref[g * SSM_N:(g + 1) * SSM_N, cols] = jnp.where(lane_grp == g, ca_re, 0.0).astype(BF16)
            cfull_ref[SLAB_N + g * SSM_N:SLAB_N + (g + 1) * SSM_N, cols] = (
                jnp.where(lane_grp == g, -ca_im, 0.0).astype(BF16))

    a_re, a_im = af_re[...], af_im[...]
    sq = []
    for _ in range(4):
        a_re, a_im = a_re * a_re - a_im * a_im, 2.0 * a_re * a_im
        sq.append((a_re, a_im))
    apw_ref[0:1, :] = sq[2][0]
    apw_ref[1:2, :] = sq[2][1]
    apw_ref[2:3, :] = sq[3][0]
    apw_ref[3:4, :] = sq[3][1]
    apw_ref[4:8, :] = jnp.zeros((4, SLAB_N), F32)


def _ssm_prep(lam_re, lam_im, log_dt, b_re, b_im, c_re, c_im):
    n = DEPTH * N_SLAB
    a_re, a_im, k_re, k_im = _ssm_disc(lam_re, lam_im, log_dt)
    rows = lambda v: jnp.repeat(v.reshape(n, SLAB_G, 1, SSM_N), SSM_P, axis=2).reshape(n, LANES, SSM_N)
    bt = lambda v: v.transpose(0, 1, 3, 2).reshape(n, LANES, SSM_N)
    flat = lambda v: v.reshape(n, 1, SLAB_N)
    args = ([rows(v) for v in (a_re, a_im, k_re, k_im)]
            + [bt(b_re), bt(b_im), c_re.reshape(n, LANES, SSM_N), c_im.reshape(n, LANES, SSM_N)]
            + [flat(a_re), flat(a_im)])

    def blk(a):
        return pl.BlockSpec((None,) + a.shape[1:], lambda i: (i, 0, 0))

    out_shape = [jax.ShapeDtypeStruct((n, 2 * LANES, SSM_T * LANES), BF16),
                 jax.ShapeDtypeStruct((n, SSM_T * LANES, 2 * SLAB_N), BF16),
                 jax.ShapeDtypeStruct((n, DEC_SEQ * LANES, 2 * SLAB_N), F32),
                 jax.ShapeDtypeStruct((n, 2 * SLAB_N, SSM_T * LANES), BF16),
                 jax.ShapeDtypeStruct((n, 8, SLAB_N), F32)]
    return pl.pallas_call(
        _ssm_prep_kernel,
        grid=(n,),
        in_specs=[blk(a) for a in args],
        out_specs=[blk(s) for s in out_shape],
        out_shape=out_shape,
        compiler_params=pltpu.CompilerParams(dimension_semantics=("parallel",), vmem_limit_bytes=VMEM_LIMIT),
        name="ssm_prep",
    )(*args)


def _ssm_prompt_kernel(u_ref, wpair_ref, bfull_ref, cfull_ref, apw_ref, y_ref, hst_ref,
                       ub_ref, s_ref, yacc_ref, *col_refs):
    rows = SSM_HALF_B * N_CHUNK
    n_col = SLAB_N // LANES
    sp_refs, hp_refs = col_refs[:n_col], col_refs[n_col:]
    for t in range(SSM_T):
        ub_ref[:, t * LANES:(t + 1) * LANES] = u_ref[pl.ds(t, rows, stride=SSM_T), :].astype(BF16)
    s_ref[...] = jnp.dot(ub_ref[...], bfull_ref[...], preferred_element_type=F32)
    for b in range(SSM_HALF_B):
        seq = slice(b * N_CHUNK, (b + 1) * N_CHUNK)
        for k in range(n_col):
            sp_refs[k][pl.ds(b, N_CHUNK, stride=8), :] = s_ref[seq, k * LANES:(k + 1) * LANES]
            sp_refs[k][pl.ds(SSM_HALF_B + b, N_CHUNK, stride=8), :] = (
                s_ref[seq, SLAB_N + k * LANES:SLAB_N + (k + 1) * LANES])
    upper = lax.broadcasted_iota(jnp.int32, (8, LANES), 0) < SSM_HALF_B
    a1 = [jnp.broadcast_to(apw_ref[2:3, k * LANES:(k + 1) * LANES], (8, LANES)) for k in range(n_col)]
    a2 = [jnp.where(upper, -1.0, 1.0) * jnp.broadcast_to(apw_ref[3:4, k * LANES:(k + 1) * LANES], (8, LANES))
          for k in range(n_col)]

    def step(c, h):
        r0 = pl.multiple_of(c * 8, 8)
        new = []
        for k in range(n_col):
            hp_refs[k][pl.ds(r0, 8), :] = h[k]
            new.append(a1[k] * h[k] + a2[k] * pltpu.roll(h[k], SSM_HALF_B, axis=0) + sp_refs[k][pl.ds(r0, 8), :])
        return tuple(new)

    h = lax.fori_loop(0, N_CHUNK, step, tuple(jnp.zeros((8, LANES), F32) for _ in range(n_col)), unroll=8)
    hst_ref[...] = jnp.concatenate(h, axis=1)
    hbc = jnp.concatenate(
        [jnp.concatenate([hp_refs[k][pl.ds(part + b, N_CHUNK, stride=8), :]
                          for part in (0, SSM_HALF_B) for k in range(n_col)], axis=1)
         for b in range(SSM_HALF_B)], axis=0)
    yacc_ref[...] = jnp.dot(hbc.astype(BF16), cfull_ref[...], preferred_element_type=F32)
    for t in range(0, SSM_T, 2):
        yacc_ref[:, t * LANES:] += jnp.dot(ub_ref[:, t * LANES:(t + 2) * LANES],
                                           wpair_ref[:, 0:(SSM_T - t) * LANES], preferred_element_type=F32)
    for t in range(SSM_T):
        y_ref[pl.ds(t, rows, stride=SSM_T), :] = yacc_ref[:, t * LANES:(t + 1) * LANES]


def _ssm_prompt(u, wpair, bfull, cfull, apw, layer):
    rows = SSM_HALF_B * N_CHUNK
    tok = SSM_HALF_B * SEQ

    def wspec(a):
        return pl.BlockSpec((None,) + a.shape[1:], lambda j, h: (layer * N_SLAB + j, 0, 0))

    return pl.pallas_call(
        _ssm_prompt_kernel,
        grid=(N_SLAB, BATCH // SSM_HALF_B),
        in_specs=[pl.BlockSpec((tok, LANES), lambda j, h: (h, j)), wspec(wpair), wspec(bfull), wspec(cfull), wspec(apw)],
        out_specs=[pl.BlockSpec((tok, LANES), lambda j, h: (h, j)),
                   pl.BlockSpec((None, 8, SLAB_N), lambda j, h: (h, 0, j))],
        out_shape=[jax.ShapeDtypeStruct((N_PROMPT, SSM_W), F32),
                   jax.ShapeDtypeStruct((BATCH // SSM_HALF_B, 8, SSM_G * SSM_N), F32)],
        scratch_shapes=[pltpu.VMEM((rows, SSM_T * LANES), BF16), pltpu.VMEM((rows, 2 * SLAB_N), F32),
                        pltpu.VMEM((rows, SSM_T * LANES), F32)]
                       + [pltpu.VMEM((N_CHUNK * 8, LANES), F32)] * (2 * SLAB_N // LANES),
        compiler_params=pltpu.CompilerParams(dimension_semantics=("arbitrary", "arbitrary"),
                                             vmem_limit_bytes=VMEM_LIMIT),
        name="ssm_prompt",
    )(u, wpair, bfull, cfull, apw)


def _ssm_sample_kernel(u_ref, h0r_ref, h0i_ref, wpair_ref, bhalf_ref, cfull_ref, apw_ref,
                       y_ref, hr_ref, hi_ref, uf_ref, yacc_ref):
    for t in range(DEC_SEQ):
        uf_ref[:, t * LANES:(t + 1) * LANES] = u_ref[pl.ds(t, DEC_BATCH, stride=DEC_SEQ), :]
    uf = uf_ref[...]
    h0r = h0r_ref[...]
    h0i = h0i_ref[...]
    ar = apw_ref[0:1, :]
    ai = apw_ref[1:2, :]
    s = jnp.dot(uf, bhalf_ref[...], precision=lax.Precision.HIGHEST, preferred_element_type=F32)
    hr_ref[...] = ar * h0r - ai * h0i + s[:, :SLAB_N]
    hi_ref[...] = ar * h0i + ai * h0r + s[:, SLAB_N:]
    ub = uf.astype(BF16)
    h0 = jnp.concatenate([h0r, h0i], axis=1).astype(BF16)
    yacc_ref[...] = jnp.dot(h0, cfull_ref[...], preferred_element_type=F32)
    for t in range(0, DEC_SEQ, 2):
        yacc_ref[:, t * LANES:] += jnp.dot(ub[:, t * LANES:(t + 2) * LANES],
                                           wpair_ref[:, 0:(DEC_SEQ - t) * LANES], preferred_element_type=F32)
    for t in range(DEC_SEQ):
        y_ref[pl.ds(t, DEC_BATCH, stride=DEC_SEQ), :] = yacc_ref[:, t * LANES:(t + 1) * LANES]


def _ssm_sample(u, h0re, h0im, wpair, bhalf, cfull, apw, layer):
    width = DEC_SEQ * LANES

    def wspec(r, c):
        return pl.BlockSpec((None, r, c), lambda j: (layer * N_SLAB + j, 0, 0))

    st = pl.BlockSpec((DEC_BATCH, SLAB_N), lambda j: (0, j))
    return pl.pallas_call(
        _ssm_sample_kernel,
        grid=(N_SLAB,),
        in_specs=[pl.BlockSpec((N_SAMPLE, LANES), lambda j: (N_PROMPT // N_SAMPLE, j)), st, st,
                  wspec(2 * LANES, width), wspec(width, 2 * SLAB_N), wspec(2 * SLAB_N, width), wspec(8, SLAB_N)],
        out_specs=[pl.BlockSpec((N_SAMPLE, LANES), lambda j: (0, j)), st, st],
        out_shape=[jax.ShapeDtypeStruct((N_SAMPLE, SSM_W), F32),
                   jax.ShapeDtypeStruct((DEC_BATCH, SSM_G * SSM_N), F32),
                   jax.ShapeDtypeStruct((DEC_BATCH, SSM_G * SSM_N), F32)],
        scratch_shapes=[pltpu.VMEM((DEC_BATCH, width), F32), pltpu.VMEM((DEC_BATCH, width), F32)],
        compiler_params=pltpu.CompilerParams(dimension_semantics=("arbitrary",), vmem_limit_bytes=VMEM_LIMIT),
        name="ssm_sample",
    )(u, h0re, h0im, wpair, bhalf, cfull, apw)


def _mask_cap(valid):
    return jnp.where(valid, jnp.inf, NEG_INF).astype(F32)


def _sink_softmax(s, cap, sink):
    s = jnp.minimum(s, cap)
    m = jnp.maximum(jnp.max(s, axis=-1, keepdims=True), sink)
    e = jnp.exp(s - m)
    denom = jnp.sum(e, axis=-1, keepdims=True) + jnp.exp(sink - m)
    return e.astype(BF16), denom


def _scores(q, scale_mask, kk):
    qx = jnp.concatenate([q[:, c * KV_W:(c + 1) * KV_W] * scale_mask for c in range(GRP)], axis=0)
    return lax.dot_general(qx.astype(BF16), kk, (((1,), (1,)), ((), ())), preferred_element_type=F32)


def _band_valid(n_q, n_k, rows):
    qi = lax.broadcasted_iota(jnp.int32, (rows, n_k), 0) & (n_q - 1)
    kj = lax.broadcasted_iota(jnp.int32, (rows, n_k), 1)
    return kj, ((kj - qi - 1).astype(jnp.uint32) < WINDOW)


def _head_masks():
    lane_kv = lax.broadcasted_iota(jnp.int32, (1, KV_W), 1) // HEAD_DIM
    return [jnp.where(lane_kv == h, HEAD_DIM ** -0.5, 0.0) for h in range(N_KV)], lane_kv == 0


ATTN_Q_TILE = 1024


def _attn_prompt_kernel(q_ref, kvp_ref, kvc_ref, sink_ref, o_ref):
    cols = GRP * WINDOW
    scale_masks, _ = _head_masks()
    kj = lax.broadcasted_iota(jnp.int32, (2 * WINDOW, cols), 0)
    qi = lax.broadcasted_iota(jnp.int32, (2 * WINDOW, cols), 1) & (WINDOW - 1)
    valid = (kj - qi - 1).astype(jnp.uint32) < WINDOW
    cap_first = _mask_cap(valid & ((kj >= WINDOW) | (pl.program_id(1) > 0)))
    cap = _mask_cap(valid)
    n_blk = ATTN_Q_TILE // WINDOW

    def stage(r):
        cur = kvc_ref[r * WINDOW:(r + 1) * WINDOW, :]
        prev = kvp_ref[...] if r == 0 else kvc_ref[(r - 1) * WINDOW:r * WINDOW, :]
        kk = jnp.concatenate([prev[:, :KV_W], cur[:, :KV_W]], axis=0).astype(BF16)
        v_t = jnp.concatenate([prev[:, KV_W:], cur[:, KV_W:]], axis=0).T.astype(BF16)
        q = q_ref[r * WINDOW:(r + 1) * WINDOW, :].astype(F32)
        s_t = []
        for h in range(N_KV):
            qx = jnp.concatenate([q[:, c * KV_W:(c + 1) * KV_W] * scale_masks[h] for c in range(GRP)], axis=0)
            s_t.append(lax.dot_general(kk, qx.astype(BF16), (((1,), (1,)), ((), ())), preferred_element_type=F32))
        return s_t, v_t

    def softmax_t(s, cap_r, sink):
        s = jnp.minimum(s, cap_r)
        m = jnp.maximum(jnp.max(s, axis=0, keepdims=True), sink)
        e = jnp.exp(s - m)
        return e.astype(BF16), jnp.sum(e, axis=0, keepdims=True) + jnp.exp(sink - m)

    staged = [stage(r) for r in range(n_blk)]
    weights = [[softmax_t(staged[r][0][h], cap_first if r == 0 else cap, sink_ref[h]) for h in range(N_KV)]
               for r in range(n_blk)]
    for r in range(n_blk):
        v_t = staged[r][1]
        outs = [jnp.dot(v_t, e, preferred_element_type=F32) / denom for e, denom in weights[r]]
        o_t = jnp.concatenate([outs[h][h * HEAD_DIM:(h + 1) * HEAD_DIM] for h in range(N_KV)], axis=0)
        for c in range(GRP):
            o_ref[r * WINDOW:(r + 1) * WINDOW, c * KV_W:(c + 1) * KV_W] = (
                o_t[:, c * WINDOW:(c + 1) * WINDOW].T.astype(o_ref.dtype))


def _attn_prompt(q, kv, sink_col):
    tiles = SEQ // ATTN_Q_TILE
    per_tile = ATTN_Q_TILE // WINDOW
    blocks = SEQ // WINDOW
    return pl.pallas_call(
        _attn_prompt_kernel,
        grid=(BATCH, tiles),
        in_specs=[pl.BlockSpec((ATTN_Q_TILE, ATTN_W), lambda b, j: (b * tiles + j, 0)),
                  pl.BlockSpec((WINDOW, 2 * KV_W), lambda b, j: (b * blocks + jnp.maximum(j * per_tile - 1, 0), 0)),
                  pl.BlockSpec((ATTN_Q_TILE, 2 * KV_W), lambda b, j: (b * tiles + j, 0)),
                  pl.BlockSpec((N_KV, 1, GRP * WINDOW), lambda b, j: (0, 0, 0))],
        out_specs=pl.BlockSpec((ATTN_Q_TILE, ATTN_W), lambda b, j: (b * tiles + j, 0)),
        out_shape=jax.ShapeDtypeStruct((N_PROMPT, ATTN_W), BF16),
        compiler_params=pltpu.CompilerParams(dimension_semantics=("parallel", "arbitrary")),
        name="attn_prompt",
    )(q, kv, kv, sink_col)


SAMPLE_SEQ_BLOCK = 8


def _attn_sample_kernel(q_ref, ckt_ref, cvt_ref, kvn_ref, sink_ref, *rest, n_carried):
    o_ref, nkt_ref, nvt_ref = rest[n_carried:]
    rows = GRP * DEC_SEQ
    kept = WINDOW - DEC_SEQ
    nt = (((1,), (1,)), ((), ()))
    t = lax.broadcasted_iota(jnp.int32, (rows, WINDOW), 0) & (DEC_SEQ - 1)
    lane = lax.broadcasted_iota(jnp.int32, (rows, WINDOW), 1)
    cap_old = _mask_cap(lane > t)
    cap_new = _mask_cap((lane >= kept) & (lane - kept <= t))
    keep_old = lax.broadcasted_iota(jnp.int32, (HEAD_DIM, WINDOW), 1) < kept
    q_all = q_ref[...].astype(F32) * (HEAD_DIM ** -0.5)
    place = (lax.broadcasted_iota(jnp.int32, (DEC_SEQ, WINDOW), 1)
             == lax.broadcasted_iota(jnp.int32, (DEC_SEQ, WINDOW), 0) + kept).astype(F32)
    toks = [slice(s * DEC_SEQ, (s + 1) * DEC_SEQ) for s in range(SAMPLE_SEQ_BLOCK)]
    new_ts = [lax.dot_general(kvn_ref[tok, :], place, (((0,), (0,)), ((), ())),
                              precision=lax.Precision.HIGHEST, preferred_element_type=F32) for tok in toks]
    scores, values = [], []
    for s, tok in enumerate(toks):
        new_t = new_ts[s]
        for h in range(N_KV):
            ch = slice(h * HEAD_DIM, (h + 1) * HEAD_DIM)
            new_k, new_v = new_t[ch], new_t[KV_W:][ch]
            old_k, old_v = ckt_ref[s, h], cvt_ref[s, h]
            nkt_ref[s, h] = jnp.where(keep_old, pltpu.roll(old_k, kept, axis=1), new_k)
            nvt_ref[s, h] = jnp.where(keep_old, pltpu.roll(old_v, kept, axis=1), new_v)
            q = jnp.concatenate([q_all[tok, c * KV_W + h * HEAD_DIM:c * KV_W + (h + 1) * HEAD_DIM]
                                 for c in range(GRP)], axis=0).astype(BF16)
            scores.append((jnp.dot(q, old_k.astype(BF16), preferred_element_type=F32),
                           jnp.dot(q, new_k.astype(BF16), preferred_element_type=F32)))
            values.append((old_v.astype(BF16), new_v.astype(BF16)))
    weights = []
    for i, (s_old, s_new) in enumerate(scores):
        sink = sink_ref[i % N_KV]
        s_old = jnp.minimum(s_old, cap_old)
        s_new = jnp.minimum(s_new, cap_new)
        m = jnp.maximum(jnp.maximum(jnp.max(s_old, axis=-1, keepdims=True), jnp.max(s_new, axis=-1, keepdims=True)),
                        sink)
        e_old = jnp.exp(s_old - m)
        e_new = jnp.exp(s_new - m)
        denom = (jnp.sum(e_old, axis=-1, keepdims=True) + jnp.sum(e_new, axis=-1, keepdims=True)
                 + jnp.exp(sink - m))
        weights.append((e_old.astype(BF16), e_new.astype(BF16), denom))
    for s, tok in enumerate(toks):
        outs = []
        for h in range(N_KV):
            e_old, e_new, denom = weights[s * N_KV + h]
            v_old, v_new = values[s * N_KV + h]
            outs.append((lax.dot_general(e_old, v_old, nt, preferred_element_type=F32)
                         + lax.dot_general(e_new, v_new, nt, preferred_element_type=F32)) / denom)
        for c in range(GRP):
            o_ref[tok, c * KV_W:(c + 1) * KV_W] = jnp.concatenate(
                [o[c * DEC_SEQ:(c + 1) * DEC_SEQ] for o in outs], axis=1)


def _attn_sample(q, kv, ck, cv, sink_col, layer, windows):
    sb = SAMPLE_SEQ_BLOCK
    tok = sb * DEC_SEQ
    first = N_PROMPT // tok
    c_spec = pl.BlockSpec((None, sb, N_KV, HEAD_DIM, WINDOW), lambda i: (layer, i, 0, 0, 0))
    in_specs = [pl.BlockSpec((tok, ATTN_W), lambda i: (first + i, 0)), c_spec, c_spec,
                pl.BlockSpec((tok, 2 * KV_W), lambda i: (first + i, 0)),
                pl.BlockSpec((N_KV, GRP * DEC_SEQ, 1), lambda i: (0, 0, 0))]
    args = [q, ck, cv, kv, sink_col]
    aliases = {}
    if windows is not None:
        aliases = {len(args): 1, len(args) + 1: 2}
        in_specs += [pl.BlockSpec(memory_space=pl.ANY)] * 2
        args += list(windows)
    return pl.pallas_call(
        functools.partial(_attn_sample_kernel, n_carried=len(aliases)),
        grid=(DEC_BATCH // sb,),
        in_specs=in_specs,
        out_specs=[pl.BlockSpec((tok, ATTN_W), lambda i: (i, 0)), c_spec, c_spec],
        out_shape=[jax.ShapeDtypeStruct((N_SAMPLE, ATTN_W), F32),
                   jax.ShapeDtypeStruct((DEPTH, DEC_BATCH, N_KV, HEAD_DIM, WINDOW), F32),
                   jax.ShapeDtypeStruct((DEPTH, DEC_BATCH, N_KV, HEAD_DIM, WINDOW), F32)],
        input_output_aliases=aliases,
        compiler_params=pltpu.CompilerParams(dimension_semantics=("parallel",)),
        name="attn_sample",
    )(*args)


def _mix_out(x, y, u, o, d_ref, wglu_ref, gs_ref, ga_ref, wout_s_ref, wout_a_ref):
    y = y + d_ref[...] * u
    y = jax.nn.gelu(y)
    y = y * jax.nn.sigmoid(jnp.dot(y.astype(BF16), wglu_ref[...], preferred_element_type=F32))
    ys = _rms(y, gs_ref[...]).astype(BF16)
    ya = _rms(o, ga_ref[...]).astype(BF16)
    mix = jnp.dot(ys, wout_s_ref[...], preferred_element_type=F32)
    mix = mix + jnp.dot(ya, wout_a_ref[...], preferred_element_type=F32)
    return x + mix


def _tail_kernel(x_ref, yp_ref, ys_ref, u_ref, op_ref, os_ref, pp_ref, ps_ref,
                 d_ref, wglu_ref, gs_ref, ga_ref, wout_s_ref, wout_a_ref,
                 g2_ref, wg_ref, wu_ref, wd_ref, gp_ref, wpg_ref, wpp_ref, gf_ref,
                 *out_refs, final):
    x, y, u, o = x_ref[...], _pick(yp_ref, ys_ref), u_ref[...], _pick(op_ref, os_ref)
    x1 = jnp.concatenate([_mix_out(x[r], y[r], u[r], o[r], d_ref, wglu_ref, gs_ref, ga_ref, wout_s_ref, wout_a_ref)
                          for r in _row_slices(EPILOGUE_SLICES)], axis=0)
    x2 = _half_swiglu(x1, g2_ref, wg_ref, wu_ref, wd_ref)
    p = _pick(pp_ref, ps_ref).astype(BF16)
    x3 = []
    for r in _row_slices(EPILOGUE_SLICES):
        h = _rms(x2[r], gp_ref[...]).astype(BF16)
        gate = jax.nn.sigmoid(jnp.dot(h, wpg_ref[...], preferred_element_type=F32))
        proj = jnp.dot(p[r], wpp_ref[...], preferred_element_type=F32)
        x3.append(x2[r] + proj * gate)
    x3 = jnp.concatenate(x3, axis=0)
    if not final:
        out_refs[0][...] = x3
        return
    y = _rms(x3, gf_ref[...])
    yp_ref, ys_ref = out_refs

    @pl.when(pl.program_id(0) < PROMPT_TILES)
    def _():
        yp_ref[...] = y

    @pl.when(pl.program_id(0) >= PROMPT_TILES)
    def _():
        ys_ref[...] = y


def _layer_tail(x, y_prompt, y_sample, u, o_prompt, o_sample, p_prompt, p_sample, layer,
                d, wglu, gs, ga, wout_ssm, wout_attn, g2, wg, wu, wd, gp, wpg, wpp, gf, final):
    if final:
        out_specs = _pair_specs(D_MODEL)
        out_shape = [jax.ShapeDtypeStruct((N_PROMPT, D_MODEL), F32), jax.ShapeDtypeStruct((N_SAMPLE, D_MODEL), F32)]
    else:
        out_specs = _row_spec(D_MODEL)
        out_shape = jax.ShapeDtypeStruct((N_TOK, D_MODEL), F32)
    p_specs = [pl.BlockSpec((None, TOKEN_TILE, PLE_DIM), lambda i: (layer, jnp.minimum(i, PROMPT_TILES - 1), 0)),
               pl.BlockSpec((None, TOKEN_TILE, PLE_DIM), lambda i: (layer, jnp.maximum(i - PROMPT_TILES, 0), 0))]
    return pl.pallas_call(
        functools.partial(_tail_kernel, final=final),
        grid=(N_TOK // TOKEN_TILE,),
        in_specs=[_row_spec(D_MODEL)] + _pair_specs(SSM_W) + [_row_spec(SSM_W)] + _pair_specs(ATTN_W) + p_specs + [
            _const_spec((1, SSM_W)), _layer_spec((SSM_W, SSM_W), layer), _const_spec((1, SSM_W)),
            _const_spec((1, ATTN_W)),
            _layer_spec((SSM_W, D_MODEL), layer), _layer_spec((ATTN_W, D_MODEL), layer),
            _const_spec((1, D_MODEL)),
            _const_spec((D_MODEL, D_FF)), _const_spec((D_MODEL, D_FF)), _const_spec((D_FF, D_MODEL)),
            _const_spec((1, D_MODEL)), _layer_spec((D_MODEL, D_MODEL), layer), _layer_spec((PLE_DIM, D_MODEL), layer),
            _const_spec((1, D_MODEL))],
        out_specs=out_specs,
        out_shape=out_shape,
        compiler_params=pltpu.CompilerParams(dimension_semantics=("arbitrary",), vmem_limit_bytes=VMEM_LIMIT),
        name="layer_tail",
    )(x, y_prompt, y_sample, u, o_prompt, o_sample, p_prompt, p_sample,
      d, wglu, gs, ga, wout_ssm, wout_attn, g2, wg, wu, wd, gp, wpg, wpp, gf)


def _sink_column(sinks, rows_per_head):
    return jnp.repeat(sinks.astype(F32).reshape(N_KV, GRP), rows_per_head, axis=1)[..., None]


def _head_major(a, axis):
    shape = a.shape
    a = a.reshape(shape[:axis] + (N_KV, GRP, HEAD_DIM) + shape[axis + 1:])
    return jnp.swapaxes(a, axis, axis + 1).reshape(shape)


def kernel(x_prompt, x_sample, cache_k, cache_v, state_ssm_re, state_ssm_im, p_prompt, p_sample, ffn1_norm, ffn1_w_gate, ffn1_w_up, ffn1_w_down, mix_norm, w_in, ssm_lam_re, ssm_lam_im, ssm_log_dt, ssm_b_re, ssm_b_im, ssm_c_re, ssm_c_im, ssm_d, ssm_w_glu, ssm_out_norm, attn_sinks, attn_out_norm, w_out, ffn2_norm, ffn2_w_gate, ffn2_w_up, ffn2_w_down, ple_norm, ple_w_gate, ple_w_proj, final_norm):
    wpair, bfull, bhalf, cfull, apw = _ssm_prep(ssm_lam_re, ssm_lam_im, ssm_log_dt, ssm_b_re, ssm_b_im,
                                                ssm_c_re, ssm_c_im)
    row = lambda v: v.reshape(1, -1)
    bf = lambda w: w.astype(BF16)
    xs = (x_prompt.reshape(N_PROMPT, D_MODEL), x_sample.reshape(N_SAMPLE, D_MODEL))
    n_state = SSM_G * SSM_N

    caches = [c.transpose(0, 1, 3, 4, 2) for c in (cache_k, cache_v)]
    windows = None
    ffn1_w = (bf(ffn1_w_gate), bf(ffn1_w_up), bf(ffn1_w_down))
    w_in_bf, w_out_bf, w_glu_bf = bf(w_in), bf(w_out), bf(ssm_w_glu)
    ple_gate_bf, ple_proj_bf = bf(ple_w_gate), bf(ple_w_proj)
    w_in_q = _head_major(w_in_bf[:, :, SSM_W:SSM_W + ATTN_W], 2)
    w_out_attn = _head_major(w_out_bf[:, SSM_W:], 1)

    kp_l, vp_l, hpr_l, hpi_l, hsr_l, hsi_l = [], [], [], [], [], []
    for i in range(DEPTH):
        x, u, q, kv, kv_last, *ffn2_w = _ffn_proj(
            xs, row(ffn1_norm[i]), *ffn1_w, row(mix_norm[i]), w_in_bf, w_in_q,
            cast=(ffn2_w_gate, ffn2_w_up, ffn2_w_down), layer=i)

        y_p, hst = _ssm_prompt(u, wpair, bfull, cfull, apw, i)
        hst = hst.reshape(BATCH // SSM_HALF_B, 2, SSM_HALF_B, SSM_G, SSM_N)
        hpr_l.append(hst[:, 0].reshape(BATCH, SSM_G, SSM_N))
        hpi_l.append(hst[:, 1].reshape(BATCH, SSM_G, SSM_N))
        y_s, hsr, hsi = _ssm_sample(u, state_ssm_re[i].reshape(DEC_BATCH, n_state),
                                    state_ssm_im[i].reshape(DEC_BATCH, n_state), wpair, bhalf, cfull, apw, i)
        hsr_l.append(hsr.reshape(DEC_BATCH, SSM_G, SSM_N))
        hsi_l.append(hsi.reshape(DEC_BATCH, SSM_G, SSM_N))

        kv_last = kv_last.reshape(BATCH, WINDOW, 2, N_KV, HEAD_DIM)
        kp_l.append(kv_last[:, :, 0])
        vp_l.append(kv_last[:, :, 1])
        o_p = _attn_prompt(q, kv, _sink_column(attn_sinks[i], WINDOW).reshape(N_KV, 1, GRP * WINDOW))
        o_s, *windows = _attn_sample(q, kv, caches[0], caches[1], _sink_column(attn_sinks[i], DEC_SEQ), i, windows)

        x = _layer_tail(x, y_p, y_s, u, o_p, o_s,
                        p_prompt.reshape(DEPTH, N_PROMPT, PLE_DIM), p_sample.reshape(DEPTH, N_SAMPLE, PLE_DIM), i,
                        row(ssm_d[i]), w_glu_bf, row(ssm_out_norm[i]), row(_head_major(attn_out_norm[i], 0)),
                        w_out_bf, w_out_attn,
                        row(ffn2_norm[i]), *ffn2_w,
                        row(ple_norm[i]), ple_gate_bf, ple_proj_bf, row(final_norm),
                        final=(i == DEPTH - 1))
        xs = (x,)

    y_prompt, y_sample = x
    ks, vs = (w.transpose(0, 1, 4, 2, 3) for w in windows)
    return (y_prompt.reshape(BATCH, SEQ, D_MODEL), y_sample.reshape(DEC_BATCH, DEC_SEQ, D_MODEL),
            jnp.stack(kp_l), jnp.stack(vp_l), jnp.stack(hpr_l), jnp.stack(hpi_l),
            ks, vs, jnp.stack(hsr_l), jnp.stack(hsi_l))
```

```python
import functools

import jax
import jax.numpy as jnp
from jax import lax
from jax.experimental import pallas as pl
from jax.experimental.pallas import tpu as pltpu

F32 = jnp.float32
BF16 = jnp.bfloat16

D_MODEL = 1024
BATCH = 8
SEQ = 2048
DEPTH = 2
DEC_BATCH = 128
DEC_SEQ = 8
SSM_W = 512
SSM_P = 16
SSM_G = 32
SSM_N = 64
ATTN_W = 512
HEAD_DIM = 64
N_HEADS = 8
N_KV = 2
GRP = N_HEADS // N_KV
KV_W = N_KV * HEAD_DIM
IN_W = SSM_W + ATTN_W + 2 * KV_W
WINDOW = 128
D_FF = 2816
PLE_DIM = 256
EPS = 1e-6
NEG_INF = -1e30

N_PROMPT = BATCH * SEQ
N_SAMPLE = DEC_BATCH * DEC_SEQ
N_TOK = N_PROMPT + N_SAMPLE

LANES = 128
SSM_T = 16
N_CHUNK = SEQ // SSM_T
SLAB_G = LANES // SSM_P
N_SLAB = SSM_G // SLAB_G
SLAB_N = SLAB_G * SSM_N
SSM_HALF_B = BATCH // 2

TOKEN_TILE = 512
PROMPT_TILES = N_PROMPT // TOKEN_TILE
TILES_PER_SEQ = SEQ // TOKEN_TILE
FF_CHUNK = 256
FF_STEPS = D_FF // FF_CHUNK
VMEM_LIMIT = 56 * 1024 * 1024


def _const_spec(shape):
    nd = len(shape)
    return pl.BlockSpec(shape, lambda *_: (0,) * nd, pipeline_mode=pl.Buffered(1))


def _layer_spec(shape, layer, at=None):
    at = at or (0,) * len(shape)
    return pl.BlockSpec((None,) + shape, lambda *_: (layer,) + at, pipeline_mode=pl.Buffered(1))


def _row_spec(width):
    return pl.BlockSpec((TOKEN_TILE, width), lambda i: (i, 0))


def _pair_specs(width):
    return [pl.BlockSpec((TOKEN_TILE, width), lambda i: (jnp.minimum(i, PROMPT_TILES - 1), 0)),
            pl.BlockSpec((TOKEN_TILE, width), lambda i: (jnp.maximum(i - PROMPT_TILES, 0), 0))]


def _pick(prompt_ref, sample_ref):
    return jnp.where(pl.program_id(0) < PROMPT_TILES, prompt_ref[...].astype(F32), sample_ref[...].astype(F32))


EPILOGUE_SLICES = 2


def _row_slices(n):
    rows = TOKEN_TILE // n
    return [slice(i * rows, (i + 1) * rows) for i in range(n)]


def _rms(x, g):
    return x * lax.rsqrt(jnp.mean(x * x, axis=-1, keepdims=True) + EPS) * g


def _half_swiglu(xf, g_ref, wg_ref, wu_ref, wd_ref):
    h = _rms(xf, g_ref[...]).astype(BF16)
    acc = jnp.zeros(xf.shape, F32)
    for c in range(0, D_FF, FF_CHUNK):
        gate = jnp.dot(h, wg_ref[:, c:c + FF_CHUNK], preferred_element_type=F32)
        up = jnp.dot(h, wu_ref[:, c:c + FF_CHUNK], preferred_element_type=F32)
        act = (gate * jax.nn.sigmoid(gate) * up).astype(BF16)
        acc = acc + jnp.dot(act, wd_ref[c:c + FF_CHUNK, :], preferred_element_type=F32)
    return xf + 0.5 * acc


def _ffn_proj_kernel(*refs, paired):
    if paired:
        xp_ref, xs_ref, *refs = refs
        x = _pick(xp_ref, xs_ref)
    else:
        x_ref, *refs = refs
        x = x_ref[...]
    (g1_ref, wg_ref, wu_ref, wd_ref, gm_ref, wu_in_ref, wq_in_ref, wkv_in_ref, *cast_in,
     xo_ref, u_ref, q_ref, kv_ref, kv_tail_ref, cg_ref, cu_ref, cd_ref) = refs

    @pl.when(pl.program_id(0) < FF_STEPS)
    def _():
        for src, dst in zip(cast_in, (cg_ref, cu_ref, cd_ref)):
            dst[...] = src[...].astype(BF16)

    x1 = _half_swiglu(x, g1_ref, wg_ref, wu_ref, wd_ref)
    xo_ref[...] = x1
    for r in _row_slices(EPILOGUE_SLICES):
        h = _rms(x1[r], gm_ref[...]).astype(BF16)
        u_ref[r, :] = jnp.dot(h, wu_in_ref[...], preferred_element_type=F32)
        q_ref[r, :] = jnp.dot(h, wq_in_ref[...], preferred_element_type=F32).astype(BF16)
        kv_ref[r, :] = jnp.dot(h, wkv_in_ref[...], preferred_element_type=F32)

    step = pl.program_id(0)

    @pl.when((step < PROMPT_TILES) & (step % TILES_PER_SEQ == TILES_PER_SEQ - 1))
    def _():
        kv_tail_ref[...] = kv_ref[TOKEN_TILE - WINDOW:, :]


def _ffn_chunk_specs(layer=None):
    chunk = lambda i: jnp.minimum(i, FF_STEPS - 1)
    if layer is None:
        return [pl.BlockSpec((D_MODEL, FF_CHUNK), lambda i: (0, chunk(i))),
                pl.BlockSpec((D_MODEL, FF_CHUNK), lambda i: (0, chunk(i))),
                pl.BlockSpec((FF_CHUNK, D_MODEL), lambda i: (chunk(i), 0))]
    return [pl.BlockSpec((None, D_MODEL, FF_CHUNK), lambda i: (layer, 0, chunk(i))),
            pl.BlockSpec((None, D_MODEL, FF_CHUNK), lambda i: (layer, 0, chunk(i))),
            pl.BlockSpec((None, FF_CHUNK, D_MODEL), lambda i: (layer, chunk(i), 0))]


def _ffn_proj(xs, g1, wg, wu, wd, gm, w_in, w_in_q, cast, layer):
    paired = len(xs) == 2
    x_specs = _pair_specs(D_MODEL) if paired else [_row_spec(D_MODEL)]
    kv_col = (SSM_W + ATTN_W) // (2 * KV_W)
    return pl.pallas_call(
        functools.partial(_ffn_proj_kernel, paired=paired),
        grid=(N_TOK // TOKEN_TILE,),
        in_specs=x_specs + [_const_spec((1, D_MODEL)),
                            _layer_spec((D_MODEL, D_FF), layer), _layer_spec((D_MODEL, D_FF), layer),
                            _layer_spec((D_FF, D_MODEL), layer),
                            _const_spec((1, D_MODEL)), _layer_spec((D_MODEL, SSM_W), layer),
                            _layer_spec((D_MODEL, ATTN_W), layer), _layer_spec((D_MODEL, 2 * KV_W), layer, (0, kv_col))]
                 + _ffn_chunk_specs(layer),
        out_specs=[_row_spec(D_MODEL), _row_spec(SSM_W), _row_spec(ATTN_W), _row_spec(2 * KV_W),
                   pl.BlockSpec((WINDOW, 2 * KV_W), lambda i: (jnp.minimum(i // TILES_PER_SEQ, BATCH - 1), 0))]
                  + _ffn_chunk_specs(),
        out_shape=[jax.ShapeDtypeStruct((N_TOK, D_MODEL), F32), jax.ShapeDtypeStruct((N_TOK, SSM_W), F32),
                   jax.ShapeDtypeStruct((N_TOK, ATTN_W), BF16), jax.ShapeDtypeStruct((N_TOK, 2 * KV_W), F32),
                   jax.ShapeDtypeStruct((BATCH * WINDOW, 2 * KV_W), F32),
                   jax.ShapeDtypeStruct((D_MODEL, D_FF), BF16), jax.ShapeDtypeStruct((D_MODEL, D_FF), BF16),
                   jax.ShapeDtypeStruct((D_FF, D_MODEL), BF16)],
        compiler_params=pltpu.CompilerParams(dimension_semantics=("arbitrary",), vmem_limit_bytes=VMEM_LIMIT),
        name="ffn1_proj",
    )(*xs, g1, wg, wu, wd, gm, w_in, w_in_q, w_in, *cast)


def _discretise(lam_re, lam_im, log_dt):
    dt = jnp.exp(log_dt)
    mag = jnp.exp(lam_re * dt)
    ang = lam_im * dt
    a_re = mag * jnp.cos(ang)
    a_im = mag * jnp.sin(ang)
    den = lam_re * lam_re + lam_im * lam_im
    nr = a_re - 1.0
    k_re = (nr * lam_re + a_im * lam_im) / den
    k_im = (a_im * lam_re - nr * lam_im) / den
    return a_re, a_im, k_re, k_im


def _powers(a_re, a_im, n):
    p_re = [jnp.ones_like(a_re)]
    p_im = [jnp.zeros_like(a_re)]
    for _ in range(n):
        r, i = p_re[-1], p_im[-1]
        p_re.append(r * a_re - i * a_im)
        p_im.append(r * a_im + i * a_re)
    return p_re, p_im


def _ssm_disc_kernel(lre_ref, lim_ref, ldt_ref, are_ref, aim_ref, kre_ref, kim_ref):
    a_re, a_im, k_re, k_im = _discretise(lre_ref[...], lim_ref[...], ldt_ref[...])
    are_ref[...] = a_re
    aim_ref[...] = a_im
    kre_ref[...] = k_re
    kim_ref[...] = k_im


def _ssm_disc(lam_re, lam_im, log_dt):
    n = DEPTH * SSM_G
    ldt = jnp.broadcast_to(log_dt[..., None], lam_re.shape)
    outs = pl.pallas_call(
        _ssm_disc_kernel,
        out_shape=[jax.ShapeDtypeStruct((n, SSM_N), F32)] * 4,
        name="ssm_disc",
    )(lam_re.reshape(n, SSM_N), lam_im.reshape(n, SSM_N), ldt.reshape(n, SSM_N))
    return [o.reshape(DEPTH, SSM_G, SSM_N) for o in outs]


def _ssm_prep_kernel(ar_re, ar_im, kr_re, kr_im, btr_re, btr_im, cr_re, cr_im, af_re, af_im,
                     wpair_ref, bcomp_ref, ccomp_ref, apw_ref):
    a_re, a_im, k_re, k_im = ar_re[...], ar_im[...], kr_re[...], kr_im[...]
    bb_re = k_re * btr_re[...] - k_im * btr_im[...]
    bb_im = k_re * btr_im[...] + k_im * btr_re[...]
    p_re, p_im = _powers(a_re, a_im, SSM_T)
    c_re, c_im = cr_re[...], cr_im[...]
    ca = [jnp.concatenate([c_re * p_re[t] - c_im * p_im[t], c_re * p_im[t] + c_im * p_re[t]], axis=1)
          for t in range(SSM_T + 1)]
    ca_all = jnp.concatenate(ca[:SSM_T], axis=0)
    bb2 = jnp.concatenate([bb_re, -bb_im], axis=1)
    drow = lax.dot_general(bb2, ca_all, (((1,), (1,)), ((), ())),
                           precision=lax.Precision.HIGHEST, preferred_element_type=F32)
    row_g = lax.broadcasted_iota(jnp.int32, drow.shape, 0) // SSM_P
    col_g = (lax.broadcasted_iota(jnp.int32, drow.shape, 1) % LANES) // SSM_P
    drow = jnp.where(row_g == col_g, drow, 0.0).astype(BF16)
    wpair_ref[0:LANES, :] = drow
    wpair_ref[LANES:, 0:LANES] = jnp.zeros((LANES, LANES), BF16)
    wpair_ref[LANES:, LANES:] = drow[:, :(SSM_T - 1) * LANES]

    for s in range(SSM_T):
        w_re, w_im = p_re[SSM_T - 1 - s], p_im[SSM_T - 1 - s]
        for part, v in enumerate((bb_re * w_re - bb_im * w_im, bb_re * w_im + bb_im * w_re)):
            bcomp_ref[(2 * s + part) * LANES:(2 * s + part + 1) * LANES, :] = jnp.concatenate([v, v], axis=1)

    for t in range(SSM_T):
        cat = ca[t + 1].T
        ccomp_ref[t * LANES:(t + 1) * LANES, :] = jnp.concatenate([cat[:SSM_N], -cat[SSM_N:]], axis=0)

    a_re, a_im = af_re[...], af_im[...]
    sq = []
    for _ in range(4):
        a_re, a_im = a_re * a_re - a_im * a_im, 2.0 * a_re * a_im
        sq.append((a_re, a_im))
    apw_ref[0:1, :] = sq[2][0]
    apw_ref[1:2, :] = sq[2][1]
    apw_ref[2:3, :] = sq[3][0]
    apw_ref[3:4, :] = sq[3][1]
    apw_ref[4:8, :] = jnp.zeros((4, SLAB_N), F32)


def _ssm_prep(lam_re, lam_im, log_dt, b_re, b_im, c_re, c_im):
    n = DEPTH * N_SLAB
    a_re, a_im, k_re, k_im = _ssm_disc(lam_re, lam_im, log_dt)
    rows = lambda v: jnp.repeat(v.reshape(n, SLAB_G, 1, SSM_N), SSM_P, axis=2).reshape(n, LANES, SSM_N)
    bt = lambda v: v.transpose(0, 1, 3, 2).reshape(n, LANES, SSM_N)
    flat = lambda v: v.reshape(n, 1, SLAB_N)
    args = ([rows(v) for v in (a_re, a_im, k_re, k_im)]
            + [bt(b_re), bt(b_im), c_re.reshape(n, LANES, SSM_N), c_im.reshape(n, LANES, SSM_N)]
            + [flat(a_re), flat(a_im)])

    def blk(a):
        return pl.BlockSpec((None,) + a.shape[1:], lambda i: (i, 0, 0))

    out_shape = [jax.ShapeDtypeStruct((n, 2 * LANES, SSM_T * LANES), BF16),
                 jax.ShapeDtypeStruct((n, 2 * SSM_T * LANES, LANES), F32),
                 jax.ShapeDtypeStruct((n, SSM_T * LANES, LANES), F32),
                 jax.ShapeDtypeStruct((n, 8, SLAB_N), F32)]
    return pl.pallas_call(
        _ssm_prep_kernel,
        grid=(n,),
        in_specs=[blk(a) for a in args],
        out_specs=[blk(s) for s in out_shape],
        out_shape=out_shape,
        compiler_params=pltpu.CompilerParams(dimension_semantics=("parallel",), vmem_limit_bytes=VMEM_LIMIT),
        name="ssm_prep",
    )(*args)


def _expand_state_in(bcomp_ref, dst_ref, first, count):
    row_grp = lax.broadcasted_iota(jnp.int32, (LANES, LANES), 0) // SSM_P
    lane_half = lax.broadcasted_iota(jnp.int32, (LANES, LANES), 1) // SSM_N
    for j in range(count):
        for part in range(2):
            tile = bcomp_ref[(2 * (first + j) + part) * LANES:(2 * (first + j) + part + 1) * LANES, :]
            for k in range(SLAB_N // LANES):
                dst_ref[j * LANES:(j + 1) * LANES, part * SLAB_N + k * LANES:part * SLAB_N + (k + 1) * LANES] = (
                    jnp.where(row_grp == 2 * k + lane_half, tile, 0.0).astype(dst_ref.dtype))


def _expand_state_out(ccomp_ref, dst_ref, count):
    lane_grp = lax.broadcasted_iota(jnp.int32, (SSM_N, LANES), 1) // SSM_P
    for t in range(count):
        for part in range(2):
            tile = ccomp_ref[t * LANES + part * SSM_N:t * LANES + (part + 1) * SSM_N, :]
            for g in range(SLAB_G):
                dst_ref[part * SLAB_N + g * SSM_N:part * SLAB_N + (g + 1) * SSM_N, t * LANES:(t + 1) * LANES] = (
                    jnp.where(lane_grp == g, tile, 0.0).astype(dst_ref.dtype))


def _ssm_prompt_kernel(u_ref, wpair_ref, bcomp_ref, ccomp_ref, apw_ref, y_ref, hst_ref,
                       bfull_ref, cfull_ref, ub_ref, s_ref, yacc_ref, *col_refs):
    rows = SSM_HALF_B * N_CHUNK
    n_col = SLAB_N // LANES

    @pl.when(pl.program_id(1) == 0)
    def _():
        _expand_state_in(bcomp_ref, bfull_ref, 0, SSM_T)
        _expand_state_out(ccomp_ref, cfull_ref, SSM_T)

    sp_refs, hp_refs = col_refs[:n_col], col_refs[n_col:]
    for t in range(SSM_T):
        ub_ref[:, t * LANES:(t + 1) * LANES] = u_ref[pl.ds(t, rows, stride=SSM_T), :].astype(BF16)
    s_ref[...] = jnp.dot(ub_ref[...], bfull_ref[...], preferred_element_type=F32)
    for b in range(SSM_HALF_B):
        seq = slice(b * N_CHUNK, (b + 1) * N_CHUNK)
        for k in range(n_col):
            sp_refs[k][pl.ds(b, N_CHUNK, stride=8), :] = s_ref[seq, k * LANES:(k + 1) * LANES]
            sp_refs[k][pl.ds(SSM_HALF_B + b, N_CHUNK, stride=8), :] = (
                s_ref[seq, SLAB_N + k * LANES:SLAB_N + (k + 1) * LANES])
    upper = lax.broadcasted_iota(jnp.int32, (8, LANES), 0) < SSM_HALF_B
    a1 = [jnp.broadcast_to(apw_ref[2:3, k * LANES:(k + 1) * LANES], (8, LANES)) for k in range(n_col)]
    a2 = [jnp.where(upper, -1.0, 1.0) * jnp.broadcast_to(apw_ref[3:4, k * LANES:(k + 1) * LANES], (8, LANES))
          for k in range(n_col)]

    def step(c, h):
        r0 = pl.multiple_of(c * 8, 8)
        new = []
        for k in range(n_col):
            hp_refs[k][pl.ds(r0, 8), :] = h[k]
            new.append(a1[k] * h[k] + a2[k] * pltpu.roll(h[k], SSM_HALF_B, axis=0) + sp_refs[k][pl.ds(r0, 8), :])
        return tuple(new)

    h = lax.fori_loop(0, N_CHUNK, step, tuple(jnp.zeros((8, LANES), F32) for _ in range(n_col)), unroll=8)
    hst_ref[...] = jnp.concatenate(h, axis=1)
    hbc = jnp.concatenate(
        [jnp.concatenate([hp_refs[k][pl.ds(part + b, N_CHUNK, stride=8), :]
                          for part in (0, SSM_HALF_B) for k in range(n_col)], axis=1)
         for b in range(SSM_HALF_B)], axis=0)
    yacc_ref[...] = jnp.dot(hbc.astype(BF16), cfull_ref[...], preferred_element_type=F32)
    for t in range(0, SSM_T, 2):
        yacc_ref[:, t * LANES:] += jnp.dot(ub_ref[:, t * LANES:(t + 2) * LANES],
                                           wpair_ref[:, 0:(SSM_T - t) * LANES], preferred_element_type=F32)
    for t in range(SSM_T):
        y_ref[pl.ds(t, rows, stride=SSM_T), :] = yacc_ref[:, t * LANES:(t + 1) * LANES]


def _ssm_prompt(u, wpair, bcomp, ccomp, apw, layer):
    rows = SSM_HALF_B * N_CHUNK
    tok = SSM_HALF_B * SEQ

    def wspec(a):
        return pl.BlockSpec((None,) + a.shape[1:], lambda j, h: (layer * N_SLAB + j, 0, 0))

    return pl.pallas_call(
        _ssm_prompt_kernel,
        grid=(N_SLAB, BATCH // SSM_HALF_B),
        in_specs=[pl.BlockSpec((tok, LANES), lambda j, h: (h, j)), wspec(wpair), wspec(bcomp), wspec(ccomp), wspec(apw)],
        out_specs=[pl.BlockSpec((tok, LANES), lambda j, h: (h, j)),
                   pl.BlockSpec((None, 8, SLAB_N), lambda j, h: (h, 0, j))],
        out_shape=[jax.ShapeDtypeStruct((N_PROMPT, SSM_W), F32),
                   jax.ShapeDtypeStruct((BATCH // SSM_HALF_B, 8, SSM_G * SSM_N), F32)],
        scratch_shapes=[pltpu.VMEM((SSM_T * LANES, 2 * SLAB_N), BF16), pltpu.VMEM((2 * SLAB_N, SSM_T * LANES), BF16),
                        pltpu.VMEM((rows, SSM_T * LANES), BF16), pltpu.VMEM((rows, 2 * SLAB_N), F32),
                        pltpu.VMEM((rows, SSM_T * LANES), F32)]
                       + [pltpu.VMEM((N_CHUNK * 8, LANES), F32)] * (2 * SLAB_N // LANES),
        compiler_params=pltpu.CompilerParams(dimension_semantics=("arbitrary", "arbitrary"),
                                             vmem_limit_bytes=VMEM_LIMIT),
        name="ssm_prompt",
    )(u, wpair, bcomp, ccomp, apw)


def _ssm_sample_kernel(u_ref, h0r_ref, h0i_ref, wpair_ref, bcomp_ref, ccomp_ref, apw_ref,
                       y_ref, hr_ref, hi_ref, bhalf_ref, cfull_ref, uf_ref, yacc_ref):
    _expand_state_in(bcomp_ref, bhalf_ref, 0, DEC_SEQ)
    _expand_state_out(ccomp_ref, cfull_ref, DEC_SEQ)
    for t in range(DEC_SEQ):
        uf_ref[:, t * LANES:(t + 1) * LANES] = u_ref[pl.ds(t, DEC_BATCH, stride=DEC_SEQ), :]
    uf = uf_ref[...]
    h0r = h0r_ref[...]
    h0i = h0i_ref[...]
    ar = apw_ref[0:1, :]
    ai = apw_ref[1:2, :]
    s = jnp.dot(uf, bhalf_ref[...], precision=lax.Precision.HIGHEST, preferred_element_type=F32)
    hr_ref[...] = ar * h0r - ai * h0i + s[:, :SLAB_N]
    hi_ref[...] = ar * h0i + ai * h0r + s[:, SLAB_N:]
    ub = uf.astype(BF16)
    h0 = jnp.concatenate([h0r, h0i], axis=1).astype(BF16)
    yacc_ref[...] = jnp.dot(h0, cfull_ref[...], preferred_element_type=F32)
    for t in range(0, DEC_SEQ, 2):
        yacc_ref[:, t * LANES:] += jnp.dot(ub[:, t * LANES:(t + 2) * LANES],
                                           wpair_ref[:, 0:(DEC_SEQ - t) * LANES], preferred_element_type=F32)
    for t in range(DEC_SEQ):
        y_ref[pl.ds(t, DEC_BATCH, stride=DEC_SEQ), :] = yacc_ref[:, t * LANES:(t + 1) * LANES]


def _ssm_sample(u, h0re, h0im, wpair, bcomp, ccomp, apw, layer):
    width = DEC_SEQ * LANES

    def wspec(r, c):
        return pl.BlockSpec((None, r, c), lambda j: (layer * N_SLAB + j, 0, 0))

    st = pl.BlockSpec((DEC_BATCH, SLAB_N), lambda j: (0, j))
    return pl.pallas_call(
        _ssm_sample_kernel,
        grid=(N_SLAB,),
        in_specs=[pl.BlockSpec((N_SAMPLE, LANES), lambda j: (N_PROMPT // N_SAMPLE, j)), st, st,
                  wspec(2 * LANES, width),
                  pl.BlockSpec((None, 2 * DEC_SEQ * LANES, LANES),
                               lambda j: (layer * N_SLAB + j, SSM_T // DEC_SEQ - 1, 0)),
                  wspec(DEC_SEQ * LANES, LANES), wspec(8, SLAB_N)],
        out_specs=[pl.BlockSpec((N_SAMPLE, LANES), lambda j: (0, j)), st, st],
        out_shape=[jax.ShapeDtypeStruct((N_SAMPLE, SSM_W), F32),
                   jax.ShapeDtypeStruct((DEC_BATCH, SSM_G * SSM_N), F32),
                   jax.ShapeDtypeStruct((DEC_BATCH, SSM_G * SSM_N), F32)],
        scratch_shapes=[pltpu.VMEM((width, 2 * SLAB_N), F32), pltpu.VMEM((2 * SLAB_N, width), BF16),
                        pltpu.VMEM((DEC_BATCH, width), F32), pltpu.VMEM((DEC_BATCH, width), F32)],
        compiler_params=pltpu.CompilerParams(dimension_semantics=("arbitrary",), vmem_limit_bytes=VMEM_LIMIT),
        name="ssm_sample",
    )(u, h0re, h0im, wpair, bcomp, ccomp, apw)


def _mask_cap(valid):
    return jnp.where(valid, jnp.inf, NEG_INF).astype(F32)


def _sink_softmax(s, cap, sink):
    s = jnp.minimum(s, cap)
    m = jnp.maximum(jnp.max(s, axis=-1, keepdims=True), sink)
    e = jnp.exp(s - m)
    denom = jnp.sum(e, axis=-1, keepdims=True) + jnp.exp(sink - m)
    return e.astype(BF16), denom


def _scores(q, scale_mask, kk):
    qx = jnp.concatenate([q[:, c * KV_W:(c + 1) * KV_W] * scale_mask for c in range(GRP)], axis=0)
    return lax.dot_general(qx.astype(BF16), kk, (((1,), (1,)), ((), ())), preferred_element_type=F32)


def _band_valid(n_q, n_k, rows):
    qi = lax.broadcasted_iota(jnp.int32, (rows, n_k), 0) & (n_q - 1)
    kj = lax.broadcasted_iota(jnp.int32, (rows, n_k), 1)
    return kj, ((kj - qi - 1).astype(jnp.uint32) < WINDOW)


def _head_masks():
    lane_kv = lax.broadcasted_iota(jnp.int32, (1, KV_W), 1) // HEAD_DIM
    return [jnp.where(lane_kv == h, HEAD_DIM ** -0.5, 0.0) for h in range(N_KV)], lane_kv == 0


ATTN_Q_TILE = 1024


def _attn_prompt_kernel(q_ref, kvp_ref, kvc_ref, sink_ref, o_ref):
    cols = GRP * WINDOW
    scale_masks, _ = _head_masks()
    kj = lax.broadcasted_iota(jnp.int32, (2 * WINDOW, cols), 0)
    qi = lax.broadcasted_iota(jnp.int32, (2 * WINDOW, cols), 1) & (WINDOW - 1)
    valid = (kj - qi - 1).astype(jnp.uint32) < WINDOW
    cap_first = _mask_cap(valid & ((kj >= WINDOW) | (pl.program_id(1) > 0)))
    cap = _mask_cap(valid)
    n_blk = ATTN_Q_TILE // WINDOW

    def stage(r):
        cur = kvc_ref[r * WINDOW:(r + 1) * WINDOW, :]
        prev = kvp_ref[...] if r == 0 else kvc_ref[(r - 1) * WINDOW:r * WINDOW, :]
        kk = jnp.concatenate([prev[:, :KV_W], cur[:, :KV_W]], axis=0).astype(BF16)
        v_t = jnp.concatenate([prev[:, KV_W:], cur[:, KV_W:]], axis=0).T.astype(BF16)
        q = q_ref[r * WINDOW:(r + 1) * WINDOW, :].astype(F32)
        s_t = []
        for h in range(N_KV):
            qx = jnp.concatenate([q[:, c * KV_W:(c + 1) * KV_W] * scale_masks[h] for c in range(GRP)], axis=0)
            s_t.append(lax.dot_general(kk, qx.astype(BF16), (((1,), (1,)), ((), ())), preferred_element_type=F32))
        return s_t, v_t

    def softmax_t(s, cap_r, sink):
        s = jnp.minimum(s, cap_r)
        m = jnp.maximum(jnp.max(s, axis=0, keepdims=True), sink)
        e = jnp.exp(s - m)
        return e.astype(BF16), jnp.sum(e, axis=0, keepdims=True) + jnp.exp(sink - m)

    staged = [stage(r) for r in range(n_blk)]
    weights = [[softmax_t(staged[r][0][h], cap_first if r == 0 else cap, sink_ref[h]) for h in range(N_KV)]
               for r in range(n_blk)]
    for r in range(n_blk):
        v_t = staged[r][1]
        outs = [jnp.dot(v_t, e, preferred_element_type=F32) / denom for e, denom in weights[r]]
        o_t = jnp.concatenate([outs[h][h * HEAD_DIM:(h + 1) * HEAD_DIM] for h in range(N_KV)], axis=0)
        for c in range(GRP):
            o_ref[r * WINDOW:(r + 1) * WINDOW, c * KV_W:(c + 1) * KV_W] = (
                o_t[:, c * WINDOW:(c + 1) * WINDOW].T.astype(o_ref.dtype))


def _attn_prompt(q, kv, sink_col):
    tiles = SEQ // ATTN_Q_TILE
    per_tile = ATTN_Q_TILE // WINDOW
    blocks = SEQ // WINDOW
    return pl.pallas_call(
        _attn_prompt_kernel,
        grid=(BATCH, tiles),
        in_specs=[pl.BlockSpec((ATTN_Q_TILE, ATTN_W), lambda b, j: (b * tiles + j, 0)),
                  pl.BlockSpec((WINDOW, 2 * KV_W), lambda b, j: (b * blocks + jnp.maximum(j * per_tile - 1, 0), 0)),
                  pl.BlockSpec((ATTN_Q_TILE, 2 * KV_W), lambda b, j: (b * tiles + j, 0)),
                  pl.BlockSpec((N_KV, 1, GRP * WINDOW), lambda b, j: (0, 0, 0))],
        out_specs=pl.BlockSpec((ATTN_Q_TILE, ATTN_W), lambda b, j: (b * tiles + j, 0)),
        out_shape=jax.ShapeDtypeStruct((N_PROMPT, ATTN_W), BF16),
        compiler_params=pltpu.CompilerParams(dimension_semantics=("parallel", "arbitrary")),
        name="attn_prompt",
    )(q, kv, kv, sink_col)


SAMPLE_SEQ_BLOCK = 8


def _attn_sample_kernel(q_ref, ckt_ref, cvt_ref, kvn_ref, sink_ref, *rest, n_carried):
    o_ref, nkt_ref, nvt_ref = rest[n_carried:]
    rows = GRP * DEC_SEQ
    kept = WINDOW - DEC_SEQ
    nt = (((1,), (1,)), ((), ()))
    t = lax.broadcasted_iota(jnp.int32, (rows, WINDOW), 0) & (DEC_SEQ - 1)
    lane = lax.broadcasted_iota(jnp.int32, (rows, WINDOW), 1)
    cap_old = _mask_cap(lane > t)
    cap_new = _mask_cap((lane >= kept) & (lane - kept <= t))
    keep_old = lax.broadcasted_iota(jnp.int32, (HEAD_DIM, WINDOW), 1) < kept
    q_all = q_ref[...].astype(F32) * (HEAD_DIM ** -0.5)
    place = (lax.broadcasted_iota(jnp.int32, (DEC_SEQ, WINDOW), 1)
             == lax.broadcasted_iota(jnp.int32, (DEC_SEQ, WINDOW), 0) + kept).astype(F32)
    toks = [slice(s * DEC_SEQ, (s + 1) * DEC_SEQ) for s in range(SAMPLE_SEQ_BLOCK)]
    new_ts = [lax.dot_general(kvn_ref[tok, :], place, (((0,), (0,)), ((), ())),
                              precision=lax.Precision.HIGHEST, preferred_element_type=F32) for tok in toks]
    scores, values = [], []
    for s, tok in enumerate(toks):
        new_t = new_ts[s]
        for h in range(N_KV):
            ch = slice(h * HEAD_DIM, (h + 1) * HEAD_DIM)
            new_k, new_v = new_t[ch], new_t[KV_W:][ch]
            old_k, old_v = ckt_ref[s, h], cvt_ref[s, h]
            nkt_ref[s, h] = jnp.where(keep_old, pltpu.roll(old_k, kept, axis=1), new_k)
            nvt_ref[s, h] = jnp.where(keep_old, pltpu.roll(old_v, kept, axis=1), new_v)
            q = jnp.concatenate([q_all[tok, c * KV_W + h * HEAD_DIM:c * KV_W + (h + 1) * HEAD_DIM]
                                 for c in range(GRP)], axis=0).astype(BF16)
            scores.append((jnp.dot(q, old_k.astype(BF16), preferred_element_type=F32),
                           jnp.dot(q, new_k.astype(BF16), preferred_element_type=F32)))
            values.append((old_v.astype(BF16), new_v.astype(BF16)))
    weights = []
    for i, (s_old, s_new) in enumerate(scores):
        sink = sink_ref[i % N_KV]
        s_old = jnp.minimum(s_old, cap_old)
        s_new = jnp.minimum(s_new, cap_new)
        m = jnp.maximum(jnp.maximum(jnp.max(s_old, axis=-1, keepdims=True), jnp.max(s_new, axis=-1, keepdims=True)),
                        sink)
        e_old = jnp.exp(s_old - m)
        e_new = jnp.exp(s_new - m)
        denom = (jnp.sum(e_old, axis=-1, keepdims=True) + jnp.sum(e_new, axis=-1, keepdims=True)
                 + jnp.exp(sink - m))
        weights.append((e_old.astype(BF16), e_new.astype(BF16), denom))
    for s, tok in enumerate(toks):
        outs = []
        for h in range(N_KV):
            e_old, e_new, denom = weights[s * N_KV + h]
            v_old, v_new = values[s * N_KV + h]
            outs.append((lax.dot_general(e_old, v_old, nt, preferred_element_type=F32)
                         + lax.dot_general(e_new, v_new, nt, preferred_element_type=F32)) / denom)
        for c in range(GRP):
            o_ref[tok, c * KV_W:(c + 1) * KV_W] = jnp.concatenate(
                [o[c * DEC_SEQ:(c + 1) * DEC_SEQ] for o in outs], axis=1)


def _attn_sample(q, kv, ck, cv, sink_col, layer, windows):
    sb = SAMPLE_SEQ_BLOCK
    tok = sb * DEC_SEQ
    first = N_PROMPT // tok
    c_spec = pl.BlockSpec((None, sb, N_KV, HEAD_DIM, WINDOW), lambda i: (layer, i, 0, 0, 0))
    in_specs = [pl.BlockSpec((tok, ATTN_W), lambda i: (first + i, 0)), c_spec, c_spec,
                pl.BlockSpec((tok, 2 * KV_W), lambda i: (first + i, 0)),
                pl.BlockSpec((N_KV, GRP * DEC_SEQ, 1), lambda i: (0, 0, 0))]
    args = [q, ck, cv, kv, sink_col]
    aliases = {}
    if windows is not None:
        aliases = {len(args): 1, len(args) + 1: 2}
        in_specs += [pl.BlockSpec(memory_space=pl.ANY)] * 2
        args += list(windows)
    return pl.pallas_call(
        functools.partial(_attn_sample_kernel, n_carried=len(aliases)),
        grid=(DEC_BATCH // sb,),
        in_specs=in_specs,
        out_specs=[pl.BlockSpec((tok, ATTN_W), lambda i: (i, 0)), c_spec, c_spec],
        out_shape=[jax.ShapeDtypeStruct((N_SAMPLE, ATTN_W), F32),
                   jax.ShapeDtypeStruct((DEPTH, DEC_BATCH, N_KV, HEAD_DIM, WINDOW), F32),
                   jax.ShapeDtypeStruct((DEPTH, DEC_BATCH, N_KV, HEAD_DIM, WINDOW), F32)],
        input_output_aliases=aliases,
        compiler_params=pltpu.CompilerParams(dimension_semantics=("parallel",)),
        name="attn_sample",
    )(*args)


def _mix_out(x, y, u, o, d_ref, wglu_ref, gs_ref, ga_ref, wout_s_ref, wout_a_ref):
    y = y + d_ref[...] * u
    y = jax.nn.gelu(y)
    y = y * jax.nn.sigmoid(jnp.dot(y.astype(BF16), wglu_ref[...], preferred_element_type=F32))
    ys = _rms(y, gs_ref[...]).astype(BF16)
    ya = _rms(o, ga_ref[...]).astype(BF16)
    mix = jnp.dot(ys, wout_s_ref[...], preferred_element_type=F32)
    mix = mix + jnp.dot(ya, wout_a_ref[...], preferred_element_type=F32)
    return x + mix


def _tail_kernel(x_ref, yp_ref, ys_ref, u_ref, op_ref, os_ref, pp_ref, ps_ref,
                 d_ref, wglu_ref, gs_ref, ga_ref, wout_s_ref, wout_a_ref,
                 g2_ref, wg_ref, wu_ref, wd_ref, gp_ref, wpg_ref, wpp_ref, gf_ref,
                 *out_refs, final):
    x, y, u, o = x_ref[...], _pick(yp_ref, ys_ref), u_ref[...], _pick(op_ref, os_ref)
    x1 = jnp.concatenate([_mix_out(x[r], y[r], u[r], o[r], d_ref, wglu_ref, gs_ref, ga_ref, wout_s_ref, wout_a_ref)
                          for r in _row_slices(EPILOGUE_SLICES)], axis=0)
    x2 = _half_swiglu(x1, g2_ref, wg_ref, wu_ref, wd_ref)
    p = _pick(pp_ref, ps_ref).astype(BF16)
    x3 = []
    for r in _row_slices(EPILOGUE_SLICES):
        h = _rms(x2[r], gp_ref[...]).astype(BF16)
        gate = jax.nn.sigmoid(jnp.dot(h, wpg_ref[...], preferred_element_type=F32))
        proj = jnp.dot(p[r], wpp_ref[...], preferred_element_type=F32)
        x3.append(x2[r] + proj * gate)
    x3 = jnp.concatenate(x3, axis=0)
    if not final:
        out_refs[0][...] = x3
        return
    y = _rms(x3, gf_ref[...])
    yp_ref, ys_ref = out_refs

    @pl.when(pl.program_id(0) < PROMPT_TILES)
    def _():
        yp_ref[...] = y

    @pl.when(pl.program_id(0) >= PROMPT_TILES)
    def _():
        ys_ref[...] = y


def _layer_tail(x, y_prompt, y_sample, u, o_prompt, o_sample, p_prompt, p_sample, layer,
                d, wglu, gs, ga, wout_ssm, wout_attn, g2, wg, wu, wd, gp, wpg, wpp, gf, final):
    if final:
        out_specs = _pair_specs(D_MODEL)
        out_shape = [jax.ShapeDtypeStruct((N_PROMPT, D_MODEL), F32), jax.ShapeDtypeStruct((N_SAMPLE, D_MODEL), F32)]
    else:
        out_specs = _row_spec(D_MODEL)
        out_shape = jax.ShapeDtypeStruct((N_TOK, D_MODEL), F32)
    p_specs = [pl.BlockSpec((None, TOKEN_TILE, PLE_DIM), lambda i: (layer, jnp.minimum(i, PROMPT_TILES - 1), 0)),
               pl.BlockSpec((None, TOKEN_TILE, PLE_DIM), lambda i: (layer, jnp.maximum(i - PROMPT_TILES, 0), 0))]
    return pl.pallas_call(
        functools.partial(_tail_kernel, final=final),
        grid=(N_TOK // TOKEN_TILE,),
        in_specs=[_row_spec(D_MODEL)] + _pair_specs(SSM_W) + [_row_spec(SSM_W)] + _pair_specs(ATTN_W) + p_specs + [
            _const_spec((1, SSM_W)), _layer_spec((SSM_W, SSM_W), layer), _const_spec((1, SSM_W)),
            _const_spec((1, ATTN_W)),
            _layer_spec((SSM_W, D_MODEL), layer), _layer_spec((ATTN_W, D_MODEL), layer),
            _const_spec((1, D_MODEL)),
            _const_spec((D_MODEL, D_FF)), _const_spec((D_MODEL, D_FF)), _const_spec((D_FF, D_MODEL)),
            _const_spec((1, D_MODEL)), _layer_spec((D_MODEL, D_MODEL), layer), _layer_spec((PLE_DIM, D_MODEL), layer),
            _const_spec((1, D_MODEL))],
        out_specs=out_specs,
        out_shape=out_shape,
        compiler_params=pltpu.CompilerParams(dimension_semantics=("arbitrary",), vmem_limit_bytes=VMEM_LIMIT),
        name="layer_tail",
    )(x, y_prompt, y_sample, u, o_prompt, o_sample, p_prompt, p_sample,
      d, wglu, gs, ga, wout_ssm, wout_attn, g2, wg, wu, wd, gp, wpg, wpp, gf)


def _sink_column(sinks, rows_per_head):
    return jnp.repeat(sinks.astype(F32).reshape(N_KV, GRP), rows_per_head, axis=1)[..., None]


def _head_major(a, axis):
    shape = a.shape
    a = a.reshape(shape[:axis] + (N_KV, GRP, HEAD_DIM) + shape[axis + 1:])
    return jnp.swapaxes(a, axis, axis + 1).reshape(shape)


def kernel(x_prompt, x_sample, cache_k, cache_v, state_ssm_re, state_ssm_im, p_prompt, p_sample, ffn1_norm, ffn1_w_gate, ffn1_w_up, ffn1_w_down, mix_norm, w_in, ssm_lam_re, ssm_lam_im, ssm_log_dt, ssm_b_re, ssm_b_im, ssm_c_re, ssm_c_im, ssm_d, ssm_w_glu, ssm_out_norm, attn_sinks, attn_out_norm, w_out, ffn2_norm, ffn2_w_gate, ffn2_w_up, ffn2_w_down, ple_norm, ple_w_gate, ple_w_proj, final_norm):
    wpair, bcomp, ccomp, apw = _ssm_prep(ssm_lam_re, ssm_lam_im, ssm_log_dt, ssm_b_re, ssm_b_im, ssm_c_re, ssm_c_im)
    row = lambda v: v.reshape(1, -1)
    bf = lambda w: w.astype(BF16)
    xs = (x_prompt.reshape(N_PROMPT, D_MODEL), x_sample.reshape(N_SAMPLE, D_MODEL))
    n_state = SSM_G * SSM_N

    caches = [c.transpose(0, 1, 3, 4, 2) for c in (cache_k, cache_v)]
    windows = None
    ffn1_w = (bf(ffn1_w_gate), bf(ffn1_w_up), bf(ffn1_w_down))
    w_in_bf, w_out_bf, w_glu_bf = bf(w_in), bf(w_out), bf(ssm_w_glu)
    ple_gate_bf, ple_proj_bf = bf(ple_w_gate), bf(ple_w_proj)
    w_in_q = _head_major(w_in_bf[:, :, SSM_W:SSM_W + ATTN_W], 2)
    w_out_attn = _head_major(w_out_bf[:, SSM_W:], 1)

    kp_l, vp_l, hpr_l, hpi_l, hsr_l, hsi_l = [], [], [], [], [], []
    for i in range(DEPTH):
        x, u, q, kv, kv_last, *ffn2_w = _ffn_proj(
            xs, row(ffn1_norm[i]), *ffn1_w, row(mix_norm[i]), w_in_bf, w_in_q,
            cast=(ffn2_w_gate, ffn2_w_up, ffn2_w_down), layer=i)

        y_p, hst = _ssm_prompt(u, wpair, bcomp, ccomp, apw, i)
        hst = hst.reshape(BATCH // SSM_HALF_B, 2, SSM_HALF_B, SSM_G, SSM_N)
        hpr_l.append(hst[:, 0].reshape(BATCH, SSM_G, SSM_N))
        hpi_l.append(hst[:, 1].reshape(BATCH, SSM_G, SSM_N))
        y_s, hsr, hsi = _ssm_sample(u, state_ssm_re[i].reshape(DEC_BATCH, n_state),
                                    state_ssm_im[i].reshape(DEC_BATCH, n_state), wpair, bcomp, ccomp, apw, i)
        hsr_l.append(hsr.reshape(DEC_BATCH, SSM_G, SSM_N))
        hsi_l.append(hsi.reshape(DEC_BATCH, SSM_G, SSM_N))

        kv_last = kv_last.reshape(BATCH, WINDOW, 2, N_KV, HEAD_DIM)
        kp_l.append(kv_last[:, :, 0])
        vp_l.append(kv_last[:, :, 1])
        o_p = _attn_prompt(q, kv, _sink_column(attn_sinks[i], WINDOW).reshape(N_KV, 1, GRP * WINDOW))
        o_s, *windows = _attn_sample(q, kv, caches[0], caches[1], _sink_column(attn_sinks[i], DEC_SEQ), i, windows)

        x = _layer_tail(x, y_p, y_s, u, o_p, o_s,
                        p_prompt.reshape(DEPTH, N_PROMPT, PLE_DIM), p_sample.reshape(DEPTH, N_SAMPLE, PLE_DIM), i,
                        row(ssm_d[i]), w_glu_bf, row(ssm_out_norm[i]), row(_head_major(attn_out_norm[i], 0)),
                        w_out_bf, w_out_attn,
                        row(ffn2_norm[i]), *ffn2_w,
                        row(ple_norm[i]), ple_gate_bf, ple_proj_bf, row(final_norm),
                        final=(i == DEPTH - 1))
        xs = (x,)

    y_prompt, y_sample = x
    ks, vs = (w.transpose(0, 1, 4, 2, 3) for w in windows)
    return (y_prompt.reshape(BATCH, SEQ, D_MODEL), y_sample.reshape(DEC_BATCH, DEC_SEQ, D_MODEL),
            jnp.stack(kp_l), jnp.stack(vp_l), jnp.stack(hpr_l), jnp.stack(hpi_l),
            ks, vs, jnp.stack(hsr_l), jnp.stack(hsi_l))
```

```python
import functools

import jax
import jax.numpy as jnp
from jax import lax
from jax.experimental import pallas as pl
from jax.experimental.pallas import tpu as pltpu

F32 = jnp.float32
BF16 = jnp.bfloat16

D_MODEL = 1024
BATCH = 8
SEQ = 2048
DEPTH = 2
DEC_BATCH = 128
DEC_SEQ = 8
SSM_W = 512
SSM_P = 16
SSM_G = 32
SSM_N = 64
ATTN_W = 512
HEAD_DIM = 64
N_HEADS = 8
N_KV = 2
GRP = N_HEADS // N_KV
KV_W = N_KV * HEAD_DIM
IN_W = SSM_W + ATTN_W + 2 * KV_W
WINDOW = 128
D_FF = 2816
PLE_DIM = 256
EPS = 1e-6
NEG_INF = -1e30

N_PROMPT = BATCH * SEQ
N_SAMPLE = DEC_BATCH * DEC_SEQ
N_TOK = N_PROMPT + N_SAMPLE

LANES = 128
SSM_T = 16
N_CHUNK = SEQ // SSM_T
SLAB_G = LANES // SSM_P
N_SLAB = SSM_G // SLAB_G
SLAB_N = SLAB_G * SSM_N
SSM_HALF_B = BATCH // 2

TOKEN_TILE = 512
PROMPT_TILES = N_PROMPT // TOKEN_TILE
TILES_PER_SEQ = SEQ // TOKEN_TILE
FF_CHUNK = 256
FF_STEPS = D_FF // FF_CHUNK
VMEM_LIMIT = 56 * 1024 * 1024


def _const_spec(shape):
    nd = len(shape)
    return pl.BlockSpec(shape, lambda *_: (0,) * nd, pipeline_mode=pl.Buffered(1))


def _layer_spec(shape, layer, at=None):
    at = at or (0,) * len(shape)
    return pl.BlockSpec((None,) + shape, lambda *_: (layer,) + at, pipeline_mode=pl.Buffered(1))


def _row_spec(width):
    return pl.BlockSpec((TOKEN_TILE, width), lambda i: (i, 0))


def _pair_specs(width):
    return [pl.BlockSpec((TOKEN_TILE, width), lambda i: (jnp.minimum(i, PROMPT_TILES - 1), 0)),
            pl.BlockSpec((TOKEN_TILE, width), lambda i: (jnp.maximum(i - PROMPT_TILES, 0), 0))]


def _pick(prompt_ref, sample_ref):
    return jnp.where(pl.program_id(0) < PROMPT_TILES, prompt_ref[...].astype(F32), sample_ref[...].astype(F32))


EPILOGUE_SLICES = 2


def _row_slices(n):
    rows = TOKEN_TILE // n
    return [slice(i * rows, (i + 1) * rows) for i in range(n)]


def _rms(x, g):
    return x * lax.rsqrt(jnp.mean(x * x, axis=-1, keepdims=True) + EPS) * g


def _half_swiglu(xf, g_ref, wg_ref, wu_ref, wd_ref):
    h = _rms(xf, g_ref[...]).astype(BF16)
    acc = jnp.zeros(xf.shape, F32)
    for c in range(0, D_FF, FF_CHUNK):
        gate = jnp.dot(h, wg_ref[:, c:c + FF_CHUNK], preferred_element_type=F32)
        up = jnp.dot(h, wu_ref[:, c:c + FF_CHUNK], preferred_element_type=F32)
        act = (gate * jax.nn.sigmoid(gate) * up).astype(BF16)
        acc = acc + jnp.dot(act, wd_ref[c:c + FF_CHUNK, :], preferred_element_type=F32)
    return xf + 0.5 * acc


def _ffn_proj_kernel(*refs, paired):
    if paired:
        xp_ref, xs_ref, *refs = refs
        x = _pick(xp_ref, xs_ref)
    else:
        x_ref, *refs = refs
        x = x_ref[...]
    (g1_ref, wg_ref, wu_ref, wd_ref, gm_ref, wu_in_ref, wq_in_ref, wkv_in_ref, *cast_in,
     xo_ref, u_ref, q_ref, kv_ref, kv_tail_ref, cg_ref, cu_ref, cd_ref) = refs

    @pl.when(pl.program_id(0) < FF_STEPS)
    def _():
        for src, dst in zip(cast_in, (cg_ref, cu_ref, cd_ref)):
            dst[...] = src[...].astype(BF16)

    x1 = _half_swiglu(x, g1_ref, wg_ref, wu_ref, wd_ref)
    xo_ref[...] = x1
    for r in _row_slices(EPILOGUE_SLICES):
        h = _rms(x1[r], gm_ref[...]).astype(BF16)
        u_ref[r, :] = jnp.dot(h, wu_in_ref[...], preferred_element_type=F32)
        q_ref[r, :] = jnp.dot(h, wq_in_ref[...], preferred_element_type=F32).astype(BF16)
        kv_ref[r, :] = jnp.dot(h, wkv_in_ref[...], preferred_element_type=F32)

    step = pl.program_id(0)

    @pl.when((step < PROMPT_TILES) & (step % TILES_PER_SEQ == TILES_PER_SEQ - 1))
    def _():
        kv_tail_ref[...] = kv_ref[TOKEN_TILE - WINDOW:, :]


def _ffn_chunk_specs(layer=None):
    chunk = lambda i: jnp.minimum(i, FF_STEPS - 1)
    if layer is None:
        return [pl.BlockSpec((D_MODEL, FF_CHUNK), lambda i: (0, chunk(i))),
                pl.BlockSpec((D_MODEL, FF_CHUNK), lambda i: (0, chunk(i))),
                pl.BlockSpec((FF_CHUNK, D_MODEL), lambda i: (chunk(i), 0))]
    return [pl.BlockSpec((None, D_MODEL, FF_CHUNK), lambda i: (layer, 0, chunk(i))),
            pl.BlockSpec((None, D_MODEL, FF_CHUNK), lambda i: (layer, 0, chunk(i))),
            pl.BlockSpec((None, FF_CHUNK, D_MODEL), lambda i: (layer, chunk(i), 0))]


def _ffn_proj(xs, g1, wg, wu, wd, gm, w_in, w_in_q, cast, layer):
    paired = len(xs) == 2
    x_specs = _pair_specs(D_MODEL) if paired else [_row_spec(D_MODEL)]
    kv_col = (SSM_W + ATTN_W) // (2 * KV_W)
    return pl.pallas_call(
        functools.partial(_ffn_proj_kernel, paired=paired),
        grid=(N_TOK // TOKEN_TILE,),
        in_specs=x_specs + [_const_spec((1, D_MODEL)),
                            _layer_spec((D_MODEL, D_FF), layer), _layer_spec((D_MODEL, D_FF), layer),
                            _layer_spec((D_FF, D_MODEL), layer),
                            _const_spec((1, D_MODEL)), _layer_spec((D_MODEL, SSM_W), layer),
                            _layer_spec((D_MODEL, ATTN_W), layer), _layer_spec((D_MODEL, 2 * KV_W), layer, (0, kv_col))]
                 + _ffn_chunk_specs(layer),
        out_specs=[_row_spec(D_MODEL), _row_spec(SSM_W), _row_spec(ATTN_W), _row_spec(2 * KV_W),
                   pl.BlockSpec((WINDOW, 2 * KV_W), lambda i: (jnp.minimum(i // TILES_PER_SEQ, BATCH - 1), 0))]
                  + _ffn_chunk_specs(),
        out_shape=[jax.ShapeDtypeStruct((N_TOK, D_MODEL), F32), jax.ShapeDtypeStruct((N_TOK, SSM_W), F32),
                   jax.ShapeDtypeStruct((N_TOK, ATTN_W), BF16), jax.ShapeDtypeStruct((N_TOK, 2 * KV_W), F32),
                   jax.ShapeDtypeStruct((BATCH * WINDOW, 2 * KV_W), F32),
                   jax.ShapeDtypeStruct((D_MODEL, D_FF), BF16), jax.ShapeDtypeStruct((D_MODEL, D_FF), BF16),
                   jax.ShapeDtypeStruct((D_FF, D_MODEL), BF16)],
        compiler_params=pltpu.CompilerParams(dimension_semantics=("arbitrary",), vmem_limit_bytes=VMEM_LIMIT),
        name="ffn1_proj",
    )(*xs, g1, wg, wu, wd, gm, w_in, w_in_q, w_in, *cast)


def _discretise(lam_re, lam_im, log_dt):
    dt = jnp.exp(log_dt)
    mag = jnp.exp(lam_re * dt)
    ang = lam_im * dt
    a_re = mag * jnp.cos(ang)
    a_im = mag * jnp.sin(ang)
    den = lam_re * lam_re + lam_im * lam_im
    nr = a_re - 1.0
    k_re = (nr * lam_re + a_im * lam_im) / den
    k_im = (a_im * lam_re - nr * lam_im) / den
    return a_re, a_im, k_re, k_im


def _powers(a_re, a_im, n):
    p_re = [jnp.ones_like(a_re)]
    p_im = [jnp.zeros_like(a_re)]
    for _ in range(n):
        r, i = p_re[-1], p_im[-1]
        p_re.append(r * a_re - i * a_im)
        p_im.append(r * a_im + i * a_re)
    return p_re, p_im


def _ssm_disc_kernel(lre_ref, lim_ref, ldt_ref, are_ref, aim_ref, kre_ref, kim_ref):
    a_re, a_im, k_re, k_im = _discretise(lre_ref[...], lim_ref[...], ldt_ref[...])
    are_ref[...] = a_re
    aim_ref[...] = a_im
    kre_ref[...] = k_re
    kim_ref[...] = k_im


def _ssm_disc(lam_re, lam_im, log_dt):
    n = DEPTH * SSM_G
    ldt = jnp.broadcast_to(log_dt[..., None], lam_re.shape)
    outs = pl.pallas_call(
        _ssm_disc_kernel,
        out_shape=[jax.ShapeDtypeStruct((n, SSM_N), F32)] * 4,
        name="ssm_disc",
    )(lam_re.reshape(n, SSM_N), lam_im.reshape(n, SSM_N), ldt.reshape(n, SSM_N))
    return [o.reshape(DEPTH, SSM_G, SSM_N) for o in outs]


def _ssm_prep_kernel(ar_re, ar_im, kr_re, kr_im, btr_re, btr_im, cr_re, cr_im, af_re, af_im,
                     wpair_ref, bcomp_ref, ccomp_ref, apw_ref):
    a_re, a_im, k_re, k_im = ar_re[...], ar_im[...], kr_re[...], kr_im[...]
    bb_re = k_re * btr_re[...] - k_im * btr_im[...]
    bb_im = k_re * btr_im[...] + k_im * btr_re[...]
    p_re, p_im = _powers(a_re, a_im, SSM_T)
    c_re, c_im = cr_re[...], cr_im[...]
    ca = [jnp.concatenate([c_re * p_re[t] - c_im * p_im[t], c_re * p_im[t] + c_im * p_re[t]], axis=1)
          for t in range(SSM_T + 1)]
    ca_all = jnp.concatenate(ca[:SSM_T], axis=0)
    bb2 = jnp.concatenate([bb_re, -bb_im], axis=1)
    drow = lax.dot_general(bb2, ca_all, (((1,), (1,)), ((), ())),
                           precision=lax.Precision.HIGHEST, preferred_element_type=F32)
    row_g = lax.broadcasted_iota(jnp.int32, drow.shape, 0) // SSM_P
    col_g = (lax.broadcasted_iota(jnp.int32, drow.shape, 1) % LANES) // SSM_P
    drow = jnp.where(row_g == col_g, drow, 0.0).astype(BF16)
    wpair_ref[0:LANES, :] = drow
    wpair_ref[LANES:, 0:LANES] = jnp.zeros((LANES, LANES), BF16)
    wpair_ref[LANES:, LANES:] = drow[:, :(SSM_T - 1) * LANES]

    for s in range(SSM_T):
        w_re, w_im = p_re[SSM_T - 1 - s], p_im[SSM_T - 1 - s]
        for part, v in enumerate((bb_re * w_re - bb_im * w_im, bb_re * w_im + bb_im * w_re)):
            bcomp_ref[(2 * s + part) * LANES:(2 * s + part + 1) * LANES, :] = jnp.concatenate([v, v], axis=1)

    for t in range(SSM_T):
        cat = ca[t + 1].T
        ccomp_ref[t * LANES:(t + 1) * LANES, :] = jnp.concatenate([cat[:SSM_N], -cat[SSM_N:]], axis=0)

    a_re, a_im = af_re[...], af_im[...]
    sq = []
    for _ in range(4):
        a_re, a_im = a_re * a_re - a_im * a_im, 2.0 * a_re * a_im
        sq.append((a_re, a_im))
    apw_ref[0:1, :] = sq[2][0]
    apw_ref[1:2, :] = sq[2][1]
    apw_ref[2:3, :] = sq[3][0]
    apw_ref[3:4, :] = sq[3][1]
    apw_ref[4:8, :] = jnp.zeros((4, SLAB_N), F32)


def _ssm_prep(lam_re, lam_im, log_dt, b_re, b_im, c_re, c_im):
    n = DEPTH * N_SLAB
    a_re, a_im, k_re, k_im = _ssm_disc(lam_re, lam_im, log_dt)
    rows = lambda v: jnp.repeat(v.reshape(n, SLAB_G, 1, SSM_N), SSM_P, axis=2).reshape(n, LANES, SSM_N)
    bt = lambda v: v.transpose(0, 1, 3, 2).reshape(n, LANES, SSM_N)
    flat = lambda v: v.reshape(n, 1, SLAB_N)
    args = ([rows(v) for v in (a_re, a_im, k_re, k_im)]
            + [bt(b_re), bt(b_im), c_re.reshape(n, LANES, SSM_N), c_im.reshape(n, LANES, SSM_N)]
            + [flat(a_re), flat(a_im)])

    def blk(a):
        return pl.BlockSpec((None,) + a.shape[1:], lambda i: (i, 0, 0))

    out_shape = [jax.ShapeDtypeStruct((n, 2 * LANES, SSM_T * LANES), BF16),
                 jax.ShapeDtypeStruct((n, 2 * SSM_T * LANES, LANES), F32),
                 jax.ShapeDtypeStruct((n, SSM_T * LANES, LANES), F32),
                 jax.ShapeDtypeStruct((n, 8, SLAB_N), F32)]
    return pl.pallas_call(
        _ssm_prep_kernel,
        grid=(n,),
        in_specs=[blk(a) for a in args],
        out_specs=[blk(s) for s in out_shape],
        out_shape=out_shape,
        compiler_params=pltpu.CompilerParams(dimension_semantics=("parallel",), vmem_limit_bytes=VMEM_LIMIT),
        name="ssm_prep",
    )(*args)


def _expand_state_in(bcomp_ref, dst_ref, first, count):
    row_grp = lax.broadcasted_iota(jnp.int32, (LANES, LANES), 0) // SSM_P
    lane_half = lax.broadcasted_iota(jnp.int32, (LANES, LANES), 1) // SSM_N
    for j in range(count):
        for part in range(2):
            tile = bcomp_ref[(2 * (first + j) + part) * LANES:(2 * (first + j) + part + 1) * LANES, :]
            for k in range(SLAB_N // LANES):
                dst_ref[j * LANES:(j + 1) * LANES, part * SLAB_N + k * LANES:part * SLAB_N + (k + 1) * LANES] = (
                    jnp.where(row_grp == 2 * k + lane_half, tile, 0.0).astype(dst_ref.dtype))


def _expand_state_out(ccomp_ref, dst_ref, count):
    lane_grp = lax.broadcasted_iota(jnp.int32, (SSM_N, LANES), 1) // SSM_P
    for t in range(count):
        for part in range(2):
            tile = ccomp_ref[t * LANES + part * SSM_N:t * LANES + (part + 1) * SSM_N, :]
            for g in range(SLAB_G):
                dst_ref[part * SLAB_N + g * SSM_N:part * SLAB_N + (g + 1) * SSM_N, t * LANES:(t + 1) * LANES] = (
                    jnp.where(lane_grp == g, tile, 0.0).astype(dst_ref.dtype))


def _ssm_prompt_kernel(u_ref, wpair_ref, bcomp_ref, ccomp_ref, apw_ref, y_ref, hst_ref,
                       bfull_ref, cfull_ref, ub_ref, s_ref, yacc_ref, *col_refs):
    rows = SSM_HALF_B * N_CHUNK
    n_col = SLAB_N // LANES

    @pl.when(pl.program_id(1) == 0)
    def _():
        _expand_state_in(bcomp_ref, bfull_ref, 0, SSM_T)
        _expand_state_out(ccomp_ref, cfull_ref, SSM_T)

    sp_refs, hp_refs = col_refs[:n_col], col_refs[n_col:]
    for t in range(SSM_T):
        ub_ref[:, t * LANES:(t + 1) * LANES] = u_ref[pl.ds(t, rows, stride=SSM_T), :].astype(BF16)
    s_ref[...] = jnp.dot(ub_ref[...], bfull_ref[...], preferred_element_type=F32)
    for b in range(SSM_HALF_B):
        seq = slice(b * N_CHUNK, (b + 1) * N_CHUNK)
        for k in range(n_col):
            sp_refs[k][pl.ds(b, N_CHUNK, stride=8), :] = s_ref[seq, k * LANES:(k + 1) * LANES]
            sp_refs[k][pl.ds(SSM_HALF_B + b, N_CHUNK, stride=8), :] = (
                s_ref[seq, SLAB_N + k * LANES:SLAB_N + (k + 1) * LANES])
    upper = lax.broadcasted_iota(jnp.int32, (8, LANES), 0) < SSM_HALF_B
    a1 = [jnp.broadcast_to(apw_ref[2:3, k * LANES:(k + 1) * LANES], (8, LANES)) for k in range(n_col)]
    a2 = [jnp.where(upper, -1.0, 1.0) * jnp.broadcast_to(apw_ref[3:4, k * LANES:(k + 1) * LANES], (8, LANES))
          for k in range(n_col)]

    def step(c, h):
        r0 = pl.multiple_of(c * 8, 8)
        new = []
        for k in range(n_col):
            hp_refs[k][pl.ds(r0, 8), :] = h[k]
            new.append(a1[k] * h[k] + a2[k] * pltpu.roll(h[k], SSM_HALF_B, axis=0) + sp_refs[k][pl.ds(r0, 8), :])
        return tuple(new)

    h = lax.fori_loop(0, N_CHUNK, step, tuple(jnp.zeros((8, LANES), F32) for _ in range(n_col)), unroll=8)
    hst_ref[...] = jnp.concatenate(h, axis=1)
    hbc = jnp.concatenate(
        [jnp.concatenate([hp_refs[k][pl.ds(part + b, N_CHUNK, stride=8), :]
                          for part in (0, SSM_HALF_B) for k in range(n_col)], axis=1)
         for b in range(SSM_HALF_B)], axis=0)
    yacc_ref[...] = jnp.dot(hbc.astype(BF16), cfull_ref[...], preferred_element_type=F32)
    for t in range(0, SSM_T, 2):
        yacc_ref[:, t * LANES:] += jnp.dot(ub_ref[:, t * LANES:(t + 2) * LANES],
                                           wpair_ref[:, 0:(SSM_T - t) * LANES], preferred_element_type=F32)
    for t in range(SSM_T):
        y_ref[pl.ds(t, rows, stride=SSM_T), :] = yacc_ref[:, t * LANES:(t + 1) * LANES]


def _ssm_prompt(u, wpair, bcomp, ccomp, apw, layer):
    rows = SSM_HALF_B * N_CHUNK
    tok = SSM_HALF_B * SEQ

    def wspec(a):
        return pl.BlockSpec((None,) + a.shape[1:], lambda j, h: (layer * N_SLAB + j, 0, 0))

    return pl.pallas_call(
        _ssm_prompt_kernel,
        grid=(N_SLAB, BATCH // SSM_HALF_B),
        in_specs=[pl.BlockSpec((tok, LANES), lambda j, h: (h, j)), wspec(wpair), wspec(bcomp), wspec(ccomp), wspec(apw)],
        out_specs=[pl.BlockSpec((tok, LANES), lambda j, h: (h, j)),
                   pl.BlockSpec((None, 8, SLAB_N), lambda j, h: (h, 0, j))],
        out_shape=[jax.ShapeDtypeStruct((N_PROMPT, SSM_W), F32),
                   jax.ShapeDtypeStruct((BATCH // SSM_HALF_B, 8, SSM_G * SSM_N), F32)],
        scratch_shapes=[pltpu.VMEM((SSM_T * LANES, 2 * SLAB_N), BF16), pltpu.VMEM((2 * SLAB_N, SSM_T * LANES), BF16),
                        pltpu.VMEM((rows, SSM_T * LANES), BF16), pltpu.VMEM((rows, 2 * SLAB_N), F32),
                        pltpu.VMEM((rows, SSM_T * LANES), F32)]
                       + [pltpu.VMEM((N_CHUNK * 8, LANES), F32)] * (2 * SLAB_N // LANES),
        compiler_params=pltpu.CompilerParams(dimension_semantics=("arbitrary", "arbitrary"),
                                             vmem_limit_bytes=VMEM_LIMIT),
        name="ssm_prompt",
    )(u, wpair, bcomp, ccomp, apw)


def _ssm_sample_kernel(u_ref, h0r_ref, h0i_ref, wpair_ref, bcomp_ref, ccomp_ref, apw_ref,
                       y_ref, hr_ref, hi_ref, bhalf_ref, cfull_ref, uf_ref, yacc_ref):
    _expand_state_in(bcomp_ref, bhalf_ref, 0, DEC_SEQ)
    _expand_state_out(ccomp_ref, cfull_ref, DEC_SEQ)
    for t in range(DEC_SEQ):
        uf_ref[:, t * LANES:(t + 1) * LANES] = u_ref[pl.ds(t, DEC_BATCH, stride=DEC_SEQ), :]
    uf = uf_ref[...]
    h0r = h0r_ref[...]
    h0i = h0i_ref[...]
    ar = apw_ref[0:1, :]
    ai = apw_ref[1:2, :]
    s = jnp.dot(uf, bhalf_ref[...], precision=lax.Precision.HIGHEST, preferred_element_type=F32)
    hr_ref[...] = ar * h0r - ai * h0i + s[:, :SLAB_N]
    hi_ref[...] = ar * h0i + ai * h0r + s[:, SLAB_N:]
    ub = uf.astype(BF16)
    h0 = jnp.concatenate([h0r, h0i], axis=1).astype(BF16)
    yacc_ref[...] = jnp.dot(h0, cfull_ref[...], preferred_element_type=F32)
    for t in range(0, DEC_SEQ, 2):
        yacc_ref[:, t * LANES:] += jnp.dot(ub[:, t * LANES:(t + 2) * LANES],
                                           wpair_ref[:, 0:(DEC_SEQ - t) * LANES], preferred_element_type=F32)
    for t in range(DEC_SEQ):
        y_ref[pl.ds(t, DEC_BATCH, stride=DEC_SEQ), :] = yacc_ref[:, t * LANES:(t + 1) * LANES]


def _ssm_sample(u, h0re, h0im, wpair, bcomp, ccomp, apw, layer):
    width = DEC_SEQ * LANES

    def wspec(r, c):
        return pl.BlockSpec((None, r, c), lambda j: (layer * N_SLAB + j, 0, 0))

    st = pl.BlockSpec((DEC_BATCH, SLAB_N), lambda j: (0, j))
    return pl.pallas_call(
        _ssm_sample_kernel,
        grid=(N_SLAB,),
        in_specs=[pl.BlockSpec((N_SAMPLE, LANES), lambda j: (N_PROMPT // N_SAMPLE, j)), st, st,
                  wspec(2 * LANES, width),
                  pl.BlockSpec((None, 2 * DEC_SEQ * LANES, LANES),
                               lambda j: (layer * N_SLAB + j, SSM_T // DEC_SEQ - 1, 0)),
                  wspec(DEC_SEQ * LANES, LANES), wspec(8, SLAB_N)],
        out_specs=[pl.BlockSpec((N_SAMPLE, LANES), lambda j: (0, j)), st, st],
        out_shape=[jax.ShapeDtypeStruct((N_SAMPLE, SSM_W), F32),
                   jax.ShapeDtypeStruct((DEC_BATCH, SSM_G * SSM_N), F32),
                   jax.ShapeDtypeStruct((DEC_BATCH, SSM_G * SSM_N), F32)],
        scratch_shapes=[pltpu.VMEM((width, 2 * SLAB_N), F32), pltpu.VMEM((2 * SLAB_N, width), BF16),
                        pltpu.VMEM((DEC_BATCH, width), F32), pltpu.VMEM((DEC_BATCH, width), F32)],
        compiler_params=pltpu.CompilerParams(dimension_semantics=("arbitrary",), vmem_limit_bytes=VMEM_LIMIT),
        name="ssm_sample",
    )(u, h0re, h0im, wpair, bcomp, ccomp, apw)


def _mask_cap(valid):
    return jnp.where(valid, jnp.inf, NEG_INF).astype(F32)


def _sink_softmax(s, cap, sink):
    s = jnp.minimum(s, cap)
    m = jnp.maximum(jnp.max(s, axis=-1, keepdims=True), sink)
    e = jnp.exp(s - m)
    denom = jnp.sum(e, axis=-1, keepdims=True) + jnp.exp(sink - m)
    return e.astype(BF16), denom


def _scores(q, scale_mask, kk):
    qx = jnp.concatenate([q[:, c * KV_W:(c + 1) * KV_W] * scale_mask for c in range(GRP)], axis=0)
    return lax.dot_general(qx.astype(BF16), kk, (((1,), (1,)), ((), ())), preferred_element_type=F32)


def _band_valid(n_q, n_k, rows):
    qi = lax.broadcasted_iota(jnp.int32, (rows, n_k), 0) & (n_q - 1)
    kj = lax.broadcasted_iota(jnp.int32, (rows, n_k), 1)
    return kj, ((kj - qi - 1).astype(jnp.uint32) < WINDOW)


def _head_masks():
    lane_kv = lax.broadcasted_iota(jnp.int32, (1, KV_W), 1) // HEAD_DIM
    return [jnp.where(lane_kv == h, HEAD_DIM ** -0.5, 0.0) for h in range(N_KV)], lane_kv == 0


ATTN_Q_TILE = 1024


def _attn_prompt_kernel(q_ref, kvp_ref, kvc_ref, sink_ref, o_ref):
    cols = GRP * WINDOW
    scale_masks, _ = _head_masks()
    kj = lax.broadcasted_iota(jnp.int32, (WINDOW, cols), 0)
    qi = lax.broadcasted_iota(jnp.int32, (WINDOW, cols), 1) & (WINDOW - 1)
    from_prev = kj > qi
    cap_first = _mask_cap(~(from_prev & (pl.program_id(1) == 0)))
    n_blk = ATTN_Q_TILE // WINDOW

    def stage(r):
        cur = kvc_ref[r * WINDOW:(r + 1) * WINDOW, :]
        prev = kvp_ref[...] if r == 0 else kvc_ref[(r - 1) * WINDOW:r * WINDOW, :]
        kk = jnp.concatenate([prev[:, :KV_W], cur[:, :KV_W]], axis=0).astype(BF16)
        v_t = jnp.concatenate([prev[:, KV_W:], cur[:, KV_W:]], axis=0).T.astype(BF16)
        q = q_ref[r * WINDOW:(r + 1) * WINDOW, :].astype(F32)
        s_t = []
        for h in range(N_KV):
            qx = jnp.concatenate([q[:, c * KV_W:(c + 1) * KV_W] * scale_masks[h] for c in range(GRP)], axis=0)
            s_all = lax.dot_general(kk, qx.astype(BF16), (((1,), (1,)), ((), ())), preferred_element_type=F32)
            s_t.append(jnp.where(from_prev, s_all[:WINDOW], s_all[WINDOW:]))
        return s_t, v_t

    def softmax_t(s, first, sink):
        if first:
            s = jnp.minimum(s, cap_first)
        m = jnp.maximum(jnp.max(s, axis=0, keepdims=True), sink)
        e = jnp.exp(s - m)
        denom = jnp.sum(e, axis=0, keepdims=True) + jnp.exp(sink - m)
        e_t = jnp.concatenate([jnp.where(from_prev, e, 0.0), jnp.where(from_prev, 0.0, e)], axis=0)
        return e_t.astype(BF16), denom

    staged = [stage(r) for r in range(n_blk)]
    weights = [[softmax_t(staged[r][0][h], r == 0, sink_ref[h]) for h in range(N_KV)] for r in range(n_blk)]
    for r in range(n_blk):
        v_t = staged[r][1]
        outs = [jnp.dot(v_t, e, preferred_element_type=F32) / denom for e, denom in weights[r]]
        o_t = jnp.concatenate([outs[h][h * HEAD_DIM:(h + 1) * HEAD_DIM] for h in range(N_KV)], axis=0)
        for c in range(GRP):
            o_ref[r * WINDOW:(r + 1) * WINDOW, c * KV_W:(c + 1) * KV_W] = (
                o_t[:, c * WINDOW:(c + 1) * WINDOW].T.astype(o_ref.dtype))


def _attn_prompt(q, kv, sink_col):
    tiles = SEQ // ATTN_Q_TILE
    per_tile = ATTN_Q_TILE // WINDOW
    blocks = SEQ // WINDOW
    return pl.pallas_call(
        _attn_prompt_kernel,
        grid=(BATCH, tiles),
        in_specs=[pl.BlockSpec((ATTN_Q_TILE, ATTN_W), lambda b, j: (b * tiles + j, 0)),
                  pl.BlockSpec((WINDOW, 2 * KV_W), lambda b, j: (b * blocks + jnp.maximum(j * per_tile - 1, 0), 0)),
                  pl.BlockSpec((ATTN_Q_TILE, 2 * KV_W), lambda b, j: (b * tiles + j, 0)),
                  pl.BlockSpec((N_KV, 1, GRP * WINDOW), lambda b, j: (0, 0, 0))],
        out_specs=pl.BlockSpec((ATTN_Q_TILE, ATTN_W), lambda b, j: (b * tiles + j, 0)),
        out_shape=jax.ShapeDtypeStruct((N_PROMPT, ATTN_W), BF16),
        compiler_params=pltpu.CompilerParams(dimension_semantics=("parallel", "arbitrary")),
        name="attn_prompt",
    )(q, kv, kv, sink_col)


SAMPLE_SEQ_BLOCK = 8


def _attn_sample_kernel(q_ref, ckt_ref, cvt_ref, kvn_ref, sink_ref, *rest, n_carried):
    o_ref, nkt_ref, nvt_ref = rest[n_carried:]
    rows = GRP * DEC_SEQ
    kept = WINDOW - DEC_SEQ
    nt = (((1,), (1,)), ((), ()))
    t = lax.broadcasted_iota(jnp.int32, (rows, WINDOW), 0) & (DEC_SEQ - 1)
    lane = lax.broadcasted_iota(jnp.int32, (rows, WINDOW), 1)
    cap_old = _mask_cap(lane > t)
    cap_new = _mask_cap((lane >= kept) & (lane - kept <= t))
    keep_old = lax.broadcasted_iota(jnp.int32, (HEAD_DIM, WINDOW), 1) < kept
    q_all = q_ref[...].astype(F32) * (HEAD_DIM ** -0.5)
    place = (lax.broadcasted_iota(jnp.int32, (DEC_SEQ, WINDOW), 1)
             == lax.broadcasted_iota(jnp.int32, (DEC_SEQ, WINDOW), 0) + kept).astype(F32)
    toks = [slice(s * DEC_SEQ, (s + 1) * DEC_SEQ) for s in range(SAMPLE_SEQ_BLOCK)]
    new_ts = [lax.dot_general(kvn_ref[tok, :], place, (((0,), (0,)), ((), ())),
                              precision=lax.Precision.HIGHEST, preferred_element_type=F32) for tok in toks]
    scores, values = [], []
    for s, tok in enumerate(toks):
        new_t = new_ts[s]
        for h in range(N_KV):
            ch = slice(h * HEAD_DIM, (h + 1) * HEAD_DIM)
            new_k, new_v = new_t[ch], new_t[KV_W:][ch]
            old_k, old_v = ckt_ref[s, h], cvt_ref[s, h]
            nkt_ref[s, h] = jnp.where(keep_old, pltpu.roll(old_k, kept, axis=1), new_k)
            nvt_ref[s, h] = jnp.where(keep_old, pltpu.roll(old_v, kept, axis=1), new_v)
            q = jnp.concatenate([q_all[tok, c * KV_W + h * HEAD_DIM:c * KV_W + (h + 1) * HEAD_DIM]
                                 for c in range(GRP)], axis=0).astype(BF16)
            scores.append((jnp.dot(q, old_k.astype(BF16), preferred_element_type=F32),
                           jnp.dot(q, new_k.astype(BF16), preferred_element_type=F32)))
            values.append((old_v.astype(BF16), new_v.astype(BF16)))
    weights = []
    for i, (s_old, s_new) in enumerate(scores):
        sink = sink_ref[i % N_KV]
        s_old = jnp.minimum(s_old, cap_old)
        s_new = jnp.minimum(s_new, cap_new)
        m = jnp.maximum(jnp.maximum(jnp.max(s_old, axis=-1, keepdims=True), jnp.max(s_new, axis=-1, keepdims=True)),
                        sink)
        e_old = jnp.exp(s_old - m)
        e_new = jnp.exp(s_new - m)
        denom = (jnp.sum(e_old, axis=-1, keepdims=True) + jnp.sum(e_new, axis=-1, keepdims=True)
                 + jnp.exp(sink - m))
        weights.append((e_old.astype(BF16), e_new.astype(BF16), denom))
    for s, tok in enumerate(toks):
        outs = []
        for h in range(N_KV):
            e_old, e_new, denom = weights[s * N_KV + h]
            v_old, v_new = values[s * N_KV + h]
            outs.append((lax.dot_general(e_old, v_old, nt, preferred_element_type=F32)
                         + lax.dot_general(e_new, v_new, nt, preferred_element_type=F32)) / denom)
        for c in range(GRP):
            o_ref[tok, c * KV_W:(c + 1) * KV_W] = jnp.concatenate(
                [o[c * DEC_SEQ:(c + 1) * DEC_SEQ] for o in outs], axis=1)


def _attn_sample(q, kv, ck, cv, sink_col, layer, windows):
    sb = SAMPLE_SEQ_BLOCK
    tok = sb * DEC_SEQ
    first = N_PROMPT // tok
    c_spec = pl.BlockSpec((None, sb, N_KV, HEAD_DIM, WINDOW), lambda i: (layer, i, 0, 0, 0))
    in_specs = [pl.BlockSpec((tok, ATTN_W), lambda i: (first + i, 0)), c_spec, c_spec,
                pl.BlockSpec((tok, 2 * KV_W), lambda i: (first + i, 0)),
                pl.BlockSpec((N_KV, GRP * DEC_SEQ, 1), lambda i: (0, 0, 0))]
    args = [q, ck, cv, kv, sink_col]
    aliases = {}
    if windows is not None:
        aliases = {len(args): 1, len(args) + 1: 2}
        in_specs += [pl.BlockSpec(memory_space=pl.ANY)] * 2
        args += list(windows)
    return pl.pallas_call(
        functools.partial(_attn_sample_kernel, n_carried=len(aliases)),
        grid=(DEC_BATCH // sb,),
        in_specs=in_specs,
        out_specs=[pl.BlockSpec((tok, ATTN_W), lambda i: (i, 0)), c_spec, c_spec],
        out_shape=[jax.ShapeDtypeStruct((N_SAMPLE, ATTN_W), F32),
                   jax.ShapeDtypeStruct((DEPTH, DEC_BATCH, N_KV, HEAD_DIM, WINDOW), F32),
                   jax.ShapeDtypeStruct((DEPTH, DEC_BATCH, N_KV, HEAD_DIM, WINDOW), F32)],
        input_output_aliases=aliases,
        compiler_params=pltpu.CompilerParams(dimension_semantics=("parallel",)),
        name="attn_sample",
    )(*args)


def _mix_out(x, y, u, o, d_ref, wglu_ref, gs_ref, ga_ref, wout_s_ref, wout_a_ref):
    y = y + d_ref[...] * u
    y = jax.nn.gelu(y)
    y = y * jax.nn.sigmoid(jnp.dot(y.astype(BF16), wglu_ref[...], preferred_element_type=F32))
    ys = _rms(y, gs_ref[...]).astype(BF16)
    ya = _rms(o, ga_ref[...]).astype(BF16)
    mix = jnp.dot(ys, wout_s_ref[...], preferred_element_type=F32)
    mix = mix + jnp.dot(ya, wout_a_ref[...], preferred_element_type=F32)
    return x + mix


def _tail_kernel(x_ref, yp_ref, ys_ref, u_ref, op_ref, os_ref, pp_ref, ps_ref,
                 d_ref, wglu_ref, gs_ref, ga_ref, wout_s_ref, wout_a_ref,
                 g2_ref, wg_ref, wu_ref, wd_ref, gp_ref, wpg_ref, wpp_ref, gf_ref,
                 *out_refs, final):
    x, y, u, o = x_ref[...], _pick(yp_ref, ys_ref), u_ref[...], _pick(op_ref, os_ref)
    x1 = jnp.concatenate([_mix_out(x[r], y[r], u[r], o[r], d_ref, wglu_ref, gs_ref, ga_ref, wout_s_ref, wout_a_ref)
                          for r in _row_slices(EPILOGUE_SLICES)], axis=0)
    x2 = _half_swiglu(x1, g2_ref, wg_ref, wu_ref, wd_ref)
    p = _pick(pp_ref, ps_ref).astype(BF16)
    x3 = []
    for r in _row_slices(EPILOGUE_SLICES):
        h = _rms(x2[r], gp_ref[...]).astype(BF16)
        gate = jax.nn.sigmoid(jnp.dot(h, wpg_ref[...], preferred_element_type=F32))
        proj = jnp.dot(p[r], wpp_ref[...], preferred_element_type=F32)
        x3.append(x2[r] + proj * gate)
    x3 = jnp.concatenate(x3, axis=0)
    if not final:
        out_refs[0][...] = x3
        return
    y = _rms(x3, gf_ref[...])
    yp_ref, ys_ref = out_refs

    @pl.when(pl.program_id(0) < PROMPT_TILES)
    def _():
        yp_ref[...] = y

    @pl.when(pl.program_id(0) >= PROMPT_TILES)
    def _():
        ys_ref[...] = y


def _layer_tail(x, y_prompt, y_sample, u, o_prompt, o_sample, p_prompt, p_sample, layer,
                d, wglu, gs, ga, wout_ssm, wout_attn, g2, wg, wu, wd, gp, wpg, wpp, gf, final):
    if final:
        out_specs = _pair_specs(D_MODEL)
        out_shape = [jax.ShapeDtypeStruct((N_PROMPT, D_MODEL), F32), jax.ShapeDtypeStruct((N_SAMPLE, D_MODEL), F32)]
    else:
        out_specs = _row_spec(D_MODEL)
        out_shape = jax.ShapeDtypeStruct((N_TOK, D_MODEL), F32)
    p_specs = [pl.BlockSpec((None, TOKEN_TILE, PLE_DIM), lambda i: (layer, jnp.minimum(i, PROMPT_TILES - 1), 0)),
               pl.BlockSpec((None, TOKEN_TILE, PLE_DIM), lambda i: (layer, jnp.maximum(i - PROMPT_TILES, 0), 0))]
    return pl.pallas_call(
        functools.partial(_tail_kernel, final=final),
        grid=(N_TOK // TOKEN_TILE,),
        in_specs=[_row_spec(D_MODEL)] + _pair_specs(SSM_W) + [_row_spec(SSM_W)] + _pair_specs(ATTN_W) + p_specs + [
            _const_spec((1, SSM_W)), _layer_spec((SSM_W, SSM_W), layer), _const_spec((1, SSM_W)),
            _const_spec((1, ATTN_W)),
            _layer_spec((SSM_W, D_MODEL), layer), _layer_spec((ATTN_W, D_MODEL), layer),
            _const_spec((1, D_MODEL)),
            _const_spec((D_MODEL, D_FF)), _const_spec((D_MODEL, D_FF)), _const_spec((D_FF, D_MODEL)),
            _const_spec((1, D_MODEL)), _layer_spec((D_MODEL, D_MODEL), layer), _layer_spec((PLE_DIM, D_MODEL), layer),
            _const_spec((1, D_MODEL))],
        out_specs=out_specs,
        out_shape=out_shape,
        compiler_params=pltpu.CompilerParams(dimension_semantics=("arbitrary",), vmem_limit_bytes=VMEM_LIMIT),
        name="layer_tail",
    )(x, y_prompt, y_sample, u, o_prompt, o_sample, p_prompt, p_sample,
      d, wglu, gs, ga, wout_ssm, wout_attn, g2, wg, wu, wd, gp, wpg, wpp, gf)


def _sink_column(sinks, rows_per_head):
    return jnp.repeat(sinks.astype(F32).reshape(N_KV, GRP), rows_per_head, axis=1)[..., None]


def _head_major(a, axis):
    shape = a.shape
    a = a.reshape(shape[:axis] + (N_KV, GRP, HEAD_DIM) + shape[axis + 1:])
    return jnp.swapaxes(a, axis, axis + 1).reshape(shape)


def kernel(x_prompt, x_sample, cache_k, cache_v, state_ssm_re, state_ssm_im, p_prompt, p_sample, ffn1_norm, ffn1_w_gate, ffn1_w_up, ffn1_w_down, mix_norm, w_in, ssm_lam_re, ssm_lam_im, ssm_log_dt, ssm_b_re, ssm_b_im, ssm_c_re, ssm_c_im, ssm_d, ssm_w_glu, ssm_out_norm, attn_sinks, attn_out_norm, w_out, ffn2_norm, ffn2_w_gate, ffn2_w_up, ffn2_w_down, ple_norm, ple_w_gate, ple_w_proj, final_norm):
    wpair, bcomp, ccomp, apw = _ssm_prep(ssm_lam_re, ssm_lam_im, ssm_log_dt, ssm_b_re, ssm_b_im, ssm_c_re, ssm_c_im)
    row = lambda v: v.reshape(1, -1)
    bf = lambda w: w.astype(BF16)
    xs = (x_prompt.reshape(N_PROMPT, D_MODEL), x_sample.reshape(N_SAMPLE, D_MODEL))
    n_state = SSM_G * SSM_N

    caches = [c.transpose(0, 1, 3, 4, 2) for c in (cache_k, cache_v)]
    windows = None
    ffn1_w = (bf(ffn1_w_gate), bf(ffn1_w_up), bf(ffn1_w_down))
    w_in_bf, w_out_bf, w_glu_bf = bf(w_in), bf(w_out), bf(ssm_w_glu)
    ple_gate_bf, ple_proj_bf = bf(ple_w_gate), bf(ple_w_proj)
    w_in_q = _head_major(w_in_bf[:, :, SSM_W:SSM_W + ATTN_W], 2)
    w_out_attn = _head_major(w_out_bf[:, SSM_W:], 1)

    kp_l, vp_l, hpr_l, hpi_l, hsr_l, hsi_l = [], [], [], [], [], []
    for i in range(DEPTH):
        x, u, q, kv, kv_last, *ffn2_w = _ffn_proj(
            xs, row(ffn1_norm[i]), *ffn1_w, row(mix_norm[i]), w_in_bf, w_in_q,
            cast=(ffn2_w_gate, ffn2_w_up, ffn2_w_down), layer=i)

        y_p, hst = _ssm_prompt(u, wpair, bcomp, ccomp, apw, i)
        hst = hst.reshape(BATCH // SSM_HALF_B, 2, SSM_HALF_B, SSM_G, SSM_N)
        hpr_l.append(hst[:, 0].reshape(BATCH, SSM_G, SSM_N))
        hpi_l.append(hst[:, 1].reshape(BATCH, SSM_G, SSM_N))
        y_s, hsr, hsi = _ssm_sample(u, state_ssm_re[i].reshape(DEC_BATCH, n_state),
                                    state_ssm_im[i].reshape(DEC_BATCH, n_state), wpair, bcomp, ccomp, apw, i)
        hsr_l.append(hsr.reshape(DEC_BATCH, SSM_G, SSM_N))
        hsi_l.append(hsi.reshape(DEC_BATCH, SSM_G, SSM_N))

        kv_last = kv_last.reshape(BATCH, WINDOW, 2, N_KV, HEAD_DIM)
        kp_l.append(kv_last[:, :, 0])
        vp_l.append(kv_last[:, :, 1])
        o_p = _attn_prompt(q, kv, _sink_column(attn_sinks[i], WINDOW).reshape(N_KV, 1, GRP * WINDOW))
        o_s, *windows = _attn_sample(q, kv, caches[0], caches[1], _sink_column(attn_sinks[i], DEC_SEQ), i, windows)

        x = _layer_tail(x, y_p, y_s, u, o_p, o_s,
                        p_prompt.reshape(DEPTH, N_PROMPT, PLE_DIM), p_sample.reshape(DEPTH, N_SAMPLE, PLE_DIM), i,
                        row(ssm_d[i]), w_glu_bf, row(ssm_out_norm[i]), row(_head_major(attn_out_norm[i], 0)),
                        w_out_bf, w_out_attn,
                        row(ffn2_norm[i]), *ffn2_w,
                        row(ple_norm[i]), ple_gate_bf, ple_proj_bf, row(final_norm),
                        final=(i == DEPTH - 1))
        xs = (x,)

    y_prompt, y_sample = x
    ks, vs = (w.transpose(0, 1, 4, 2, 3) for w in windows)
    return (y_prompt.reshape(BATCH, SEQ, D_MODEL), y_sample.reshape(DEC_BATCH, DEC_SEQ, D_MODEL),
            jnp.stack(kp_l), jnp.stack(vp_l), jnp.stack(hpr_l), jnp.stack(hpi_l),
            ks, vs, jnp.stack(hsr_l), jnp.stack(hsi_l))
```

```python
import functools

import jax
import jax.numpy as jnp
from jax import lax
from jax.experimental import pallas as pl
from jax.experimental.pallas import tpu as pltpu

F32 = jnp.float32
BF16 = jnp.bfloat16

D_MODEL = 1024
BATCH = 8
SEQ = 2048
DEPTH = 2
DEC_BATCH = 128
DEC_SEQ = 8
SSM_W = 512
SSM_P = 16
SSM_G = 32
SSM_N = 64
ATTN_W = 512
HEAD_DIM = 64
N_HEADS = 8
N_KV = 2
GRP = N_HEADS // N_KV
KV_W = N_KV * HEAD_DIM
IN_W = SSM_W + ATTN_W + 2 * KV_W
WINDOW = 128
D_FF = 2816
PLE_DIM = 256
EPS = 1e-6
NEG_INF = -1e30

N_PROMPT = BATCH * SEQ
N_SAMPLE = DEC_BATCH * DEC_SEQ
N_TOK = N_PROMPT + N_SAMPLE

LANES = 128
SSM_T = 16
N_CHUNK = SEQ // SSM_T
SLAB_G = LANES // SSM_P
N_SLAB = SSM_G // SLAB_G
SLAB_N = SLAB_G * SSM_N
SSM_HALF_B = BATCH // 2

TOKEN_TILE = 512
PROMPT_TILES = N_PROMPT // TOKEN_TILE
TILES_PER_SEQ = SEQ // TOKEN_TILE
FF_CHUNK = 256
FF_STEPS = D_FF // FF_CHUNK
VMEM_LIMIT = 56 * 1024 * 1024


def _const_spec(shape):
    nd = len(shape)
    return pl.BlockSpec(shape, lambda *_: (0,) * nd, pipeline_mode=pl.Buffered(1))


def _layer_spec(shape, layer, at=None):
    at = at or (0,) * len(shape)
    return pl.BlockSpec((None,) + shape, lambda *_: (layer,) + at, pipeline_mode=pl.Buffered(1))


def _row_spec(width):
    return pl.BlockSpec((TOKEN_TILE, width), lambda i: (i, 0))


def _pair_specs(width):
    return [pl.BlockSpec((TOKEN_TILE, width), lambda i: (jnp.minimum(i, PROMPT_TILES - 1), 0)),
            pl.BlockSpec((TOKEN_TILE, width), lambda i: (jnp.maximum(i - PROMPT_TILES, 0), 0))]


def _pick(prompt_ref, sample_ref):
    return jnp.where(pl.program_id(0) < PROMPT_TILES, prompt_ref[...].astype(F32), sample_ref[...].astype(F32))


EPILOGUE_SLICES = 2


def _row_slices(n):
    rows = TOKEN_TILE // n
    return [slice(i * rows, (i + 1) * rows) for i in range(n)]


def _rms(x, g):
    return x * lax.rsqrt(jnp.mean(x * x, axis=-1, keepdims=True) + EPS) * g


def _half_swiglu(xf, g_ref, wg_ref, wu_ref, wd_ref):
    h = _rms(xf, g_ref[...]).astype(BF16)
    acc = jnp.zeros(xf.shape, F32)
    for c in range(0, D_FF, FF_CHUNK):
        gate = jnp.dot(h, wg_ref[:, c:c + FF_CHUNK], preferred_element_type=F32)
        up = jnp.dot(h, wu_ref[:, c:c + FF_CHUNK], preferred_element_type=F32)
        act = (gate * jax.nn.sigmoid(gate) * up).astype(BF16)
        acc = acc + jnp.dot(act, wd_ref[c:c + FF_CHUNK, :], preferred_element_type=F32)
    return xf + 0.5 * acc


def _ffn_proj_kernel(*refs, paired):
    if paired:
        xp_ref, xs_ref, *refs = refs
        x = _pick(xp_ref, xs_ref)
    else:
        x_ref, *refs = refs
        x = x_ref[...]
    (g1_ref, wg_ref, wu_ref, wd_ref, gm_ref, wu_in_ref, wq_in_ref, wkv_in_ref, *cast_in,
     xo_ref, u_ref, q_ref, kv_ref, kv_tail_ref, cg_ref, cu_ref, cd_ref) = refs

    @pl.when(pl.program_id(0) < FF_STEPS)
    def _():
        for src, dst in zip(cast_in, (cg_ref, cu_ref, cd_ref)):
            dst[...] = src[...].astype(BF16)

    x1 = _half_swiglu(x, g1_ref, wg_ref, wu_ref, wd_ref)
    xo_ref[...] = x1
    for r in _row_slices(EPILOGUE_SLICES):
        h = _rms(x1[r], gm_ref[...]).astype(BF16)
        u_ref[r, :] = jnp.dot(h, wu_in_ref[...], preferred_element_type=F32)
        q_ref[r, :] = jnp.dot(h, wq_in_ref[...], preferred_element_type=F32).astype(BF16)
        kv_ref[r, :] = jnp.dot(h, wkv_in_ref[...], preferred_element_type=F32)

    step = pl.program_id(0)

    @pl.when((step < PROMPT_TILES) & (step % TILES_PER_SEQ == TILES_PER_SEQ - 1))
    def _():
        kv_tail_ref[...] = kv_ref[TOKEN_TILE - WINDOW:, :]


def _ffn_chunk_specs(layer=None):
    chunk = lambda i: jnp.minimum(i, FF_STEPS - 1)
    if layer is None:
        return [pl.BlockSpec((D_MODEL, FF_CHUNK), lambda i: (0, chunk(i))),
                pl.BlockSpec((D_MODEL, FF_CHUNK), lambda i: (0, chunk(i))),
                pl.BlockSpec((FF_CHUNK, D_MODEL), lambda i: (chunk(i), 0))]
    return [pl.BlockSpec((None, D_MODEL, FF_CHUNK), lambda i: (layer, 0, chunk(i))),
            pl.BlockSpec((None, D_MODEL, FF_CHUNK), lambda i: (layer, 0, chunk(i))),
            pl.BlockSpec((None, FF_CHUNK, D_MODEL), lambda i: (layer, chunk(i), 0))]


def _ffn_proj(xs, g1, wg, wu, wd, gm, w_in, w_in_q, cast, layer):
    paired = len(xs) == 2
    x_specs = _pair_specs(D_MODEL) if paired else [_row_spec(D_MODEL)]
    kv_col = (SSM_W + ATTN_W) // (2 * KV_W)
    return pl.pallas_call(
        functools.partial(_ffn_proj_kernel, paired=paired),
        grid=(N_TOK // TOKEN_TILE,),
        in_specs=x_specs + [_const_spec((1, D_MODEL)),
                            _layer_spec((D_MODEL, D_FF), layer), _layer_spec((D_MODEL, D_FF), layer),
                            _layer_spec((D_FF, D_MODEL), layer),
                            _const_spec((1, D_MODEL)), _layer_spec((D_MODEL, SSM_W), layer),
                            _layer_spec((D_MODEL, ATTN_W), layer), _layer_spec((D_MODEL, 2 * KV_W), layer, (0, kv_col))]
                 + _ffn_chunk_specs(layer),
        out_specs=[_row_spec(D_MODEL), _row_spec(SSM_W), _row_spec(ATTN_W), _row_spec(2 * KV_W),
                   pl.BlockSpec((WINDOW, 2 * KV_W), lambda i: (jnp.minimum(i // TILES_PER_SEQ, BATCH - 1), 0))]
                  + _ffn_chunk_specs(),
        out_shape=[jax.ShapeDtypeStruct((N_TOK, D_MODEL), F32), jax.ShapeDtypeStruct((N_TOK, SSM_W), F32),
                   jax.ShapeDtypeStruct((N_TOK, ATTN_W), BF16), jax.ShapeDtypeStruct((N_TOK, 2 * KV_W), F32),
                   jax.ShapeDtypeStruct((BATCH * WINDOW, 2 * KV_W), F32),
                   jax.ShapeDtypeStruct((D_MODEL, D_FF), BF16), jax.ShapeDtypeStruct((D_MODEL, D_FF), BF16),
                   jax.ShapeDtypeStruct((D_FF, D_MODEL), BF16)],
        compiler_params=pltpu.CompilerParams(dimension_semantics=("arbitrary",), vmem_limit_bytes=VMEM_LIMIT),
        name="ffn1_proj",
    )(*xs, g1, wg, wu, wd, gm, w_in, w_in_q, w_in, *cast)


def _discretise(lam_re, lam_im, log_dt):
    dt = jnp.exp(log_dt)
    mag = jnp.exp(lam_re * dt)
    ang = lam_im * dt
    a_re = mag * jnp.cos(ang)
    a_im = mag * jnp.sin(ang)
    den = lam_re * lam_re + lam_im * lam_im
    nr = a_re - 1.0
    k_re = (nr * lam_re + a_im * lam_im) / den
    k_im = (a_im * lam_re - nr * lam_im) / den
    return a_re, a_im, k_re, k_im


def _powers(a_re, a_im, n):
    p_re = [jnp.ones_like(a_re)]
    p_im = [jnp.zeros_like(a_re)]
    for _ in range(n):
        r, i = p_re[-1], p_im[-1]
        p_re.append(r * a_re - i * a_im)
        p_im.append(r * a_im + i * a_re)
    return p_re, p_im


def _ssm_disc_kernel(lre_ref, lim_ref, ldt_ref, are_ref, aim_ref, kre_ref, kim_ref):
    a_re, a_im, k_re, k_im = _discretise(lre_ref[...], lim_ref[...], ldt_ref[...])
    are_ref[...] = a_re
    aim_ref[...] = a_im
    kre_ref[...] = k_re
    kim_ref[...] = k_im


def _ssm_disc(lam_re, lam_im, log_dt):
    n = DEPTH * SSM_G
    ldt = jnp.broadcast_to(log_dt[..., None], lam_re.shape)
    outs = pl.pallas_call(
        _ssm_disc_kernel,
        out_shape=[jax.ShapeDtypeStruct((n, SSM_N), F32)] * 4,
        name="ssm_disc",
    )(lam_re.reshape(n, SSM_N), lam_im.reshape(n, SSM_N), ldt.reshape(n, SSM_N))
    return [o.reshape(DEPTH, SSM_G, SSM_N) for o in outs]


def _ssm_prep_kernel(ar_re, ar_im, kr_re, kr_im, btr_re, btr_im, cr_re, cr_im, af_re, af_im,
                     wpair_ref, bcomp_ref, ccomp_ref, apw_ref):
    a_re, a_im, k_re, k_im = ar_re[...], ar_im[...], kr_re[...], kr_im[...]
    bb_re = k_re * btr_re[...] - k_im * btr_im[...]
    bb_im = k_re * btr_im[...] + k_im * btr_re[...]
    p_re, p_im = _powers(a_re, a_im, SSM_T)
    c_re, c_im = cr_re[...], cr_im[...]
    ca = [jnp.concatenate([c_re * p_re[t] - c_im * p_im[t], c_re * p_im[t] + c_im * p_re[t]], axis=1)
          for t in range(SSM_T + 1)]
    ca_all = jnp.concatenate(ca[:SSM_T], axis=0)
    bb2 = jnp.concatenate([bb_re, -bb_im], axis=1)
    drow = lax.dot_general(bb2, ca_all, (((1,), (1,)), ((), ())),
                           precision=lax.Precision.HIGHEST, preferred_element_type=F32)
    row_g = lax.broadcasted_iota(jnp.int32, drow.shape, 0) // SSM_P
    col_g = (lax.broadcasted_iota(jnp.int32, drow.shape, 1) % LANES) // SSM_P
    drow = jnp.where(row_g == col_g, drow, 0.0).astype(BF16)
    wpair_ref[0:LANES, :] = drow
    wpair_ref[LANES:, 0:LANES] = jnp.zeros((LANES, LANES), BF16)
    wpair_ref[LANES:, LANES:] = drow[:, :(SSM_T - 1) * LANES]

    for s in range(SSM_T):
        w_re, w_im = p_re[SSM_T - 1 - s], p_im[SSM_T - 1 - s]
        for part, v in enumerate((bb_re * w_re - bb_im * w_im, bb_re * w_im + bb_im * w_re)):
            bcomp_ref[(2 * s + part) * LANES:(2 * s + part + 1) * LANES, :] = jnp.concatenate([v, v], axis=1)

    for t in range(SSM_T):
        cat = ca[t + 1].T
        ccomp_ref[t * LANES:(t + 1) * LANES, :] = jnp.concatenate([cat[:SSM_N], -cat[SSM_N:]], axis=0)

    a_re, a_im = af_re[...], af_im[...]
    sq = []
    for _ in range(4):
        a_re, a_im = a_re * a_re - a_im * a_im, 2.0 * a_re * a_im
        sq.append((a_re, a_im))
    apw_ref[0:1, :] = sq[2][0]
    apw_ref[1:2, :] = sq[2][1]
    apw_ref[2:3, :] = sq[3][0]
    apw_ref[3:4, :] = sq[3][1]
    apw_ref[4:8, :] = jnp.zeros((4, SLAB_N), F32)


def _ssm_prep(lam_re, lam_im, log_dt, b_re, b_im, c_re, c_im):
    n = DEPTH * N_SLAB
    a_re, a_im, k_re, k_im = _ssm_disc(lam_re, lam_im, log_dt)
    rows = lambda v: jnp.repeat(v.reshape(n, SLAB_G, 1, SSM_N), SSM_P, axis=2).reshape(n, LANES, SSM_N)
    bt = lambda v: v.transpose(0, 1, 3, 2).reshape(n, LANES, SSM_N)
    flat = lambda v: v.reshape(n, 1, SLAB_N)
    args = ([rows(v) for v in (a_re, a_im, k_re, k_im)]
            + [bt(b_re), bt(b_im), c_re.reshape(n, LANES, SSM_N), c_im.reshape(n, LANES, SSM_N)]
            + [flat(a_re), flat(a_im)])

    def blk(a):
        return pl.BlockSpec((None,) + a.shape[1:], lambda i: (i, 0, 0))

    out_shape = [jax.ShapeDtypeStruct((n, 2 * LANES, SSM_T * LANES), BF16),
                 jax.ShapeDtypeStruct((n, 2 * SSM_T * LANES, LANES), F32),
                 jax.ShapeDtypeStruct((n, SSM_T * LANES, LANES), F32),
                 jax.ShapeDtypeStruct((n, 8, SLAB_N), F32)]
    return pl.pallas_call(
        _ssm_prep_kernel,
        grid=(n,),
        in_specs=[blk(a) for a in args],
        out_specs=[blk(s) for s in out_shape],
        out_shape=out_shape,
        compiler_params=pltpu.CompilerParams(dimension_semantics=("parallel",), vmem_limit_bytes=VMEM_LIMIT),
        name="ssm_prep",
    )(*args)


def _expand_state_in(bcomp_ref, dst_ref, first, count):
    row_grp = lax.broadcasted_iota(jnp.int32, (LANES, LANES), 0) // SSM_P
    lane_half = lax.broadcasted_iota(jnp.int32, (LANES, LANES), 1) // SSM_N
    for j in range(count):
        for part in range(2):
            tile = bcomp_ref[(2 * (first + j) + part) * LANES:(2 * (first + j) + part + 1) * LANES, :]
            for k in range(SLAB_N // LANES):
                dst_ref[j * LANES:(j + 1) * LANES, part * SLAB_N + k * LANES:part * SLAB_N + (k + 1) * LANES] = (
                    jnp.where(row_grp == 2 * k + lane_half, tile, 0.0).astype(dst_ref.dtype))


def _expand_state_out(ccomp_ref, dst_ref, count):
    lane_grp = lax.broadcasted_iota(jnp.int32, (SSM_N, LANES), 1) // SSM_P
    for t in range(count):
        for part in range(2):
            tile = ccomp_ref[t * LANES + part * SSM_N:t * LANES + (part + 1) * SSM_N, :]
            for g in range(SLAB_G):
                dst_ref[part * SLAB_N + g * SSM_N:part * SLAB_N + (g + 1) * SSM_N, t * LANES:(t + 1) * LANES] = (
                    jnp.where(lane_grp == g, tile, 0.0).astype(dst_ref.dtype))


def _ssm_prompt_kernel(u_ref, wpair_ref, bcomp_ref, ccomp_ref, apw_ref, y_ref, hst_ref,
                       bfull_ref, cfull_ref, ub_ref, s_ref, yacc_ref, *col_refs):
    rows = SSM_HALF_B * N_CHUNK
    n_col = SLAB_N // LANES

    @pl.when(pl.program_id(1) == 0)
    def _():
        _expand_state_in(bcomp_ref, bfull_ref, 0, SSM_T)
        _expand_state_out(ccomp_ref, cfull_ref, SSM_T)

    sp_refs, hp_refs = col_refs[:n_col], col_refs[n_col:]
    for t in range(SSM_T):
        ub_ref[:, t * LANES:(t + 1) * LANES] = u_ref[pl.ds(t, rows, stride=SSM_T), :].astype(BF16)
    s_ref[...] = jnp.dot(ub_ref[...], bfull_ref[...], preferred_element_type=F32)
    for b in range(SSM_HALF_B):
        seq = slice(b * N_CHUNK, (b + 1) * N_CHUNK)
        for k in range(n_col):
            sp_refs[k][pl.ds(b, N_CHUNK, stride=8), :] = s_ref[seq, k * LANES:(k + 1) * LANES]
            sp_refs[k][pl.ds(SSM_HALF_B + b, N_CHUNK, stride=8), :] = (
                s_ref[seq, SLAB_N + k * LANES:SLAB_N + (k + 1) * LANES])
    upper = lax.broadcasted_iota(jnp.int32, (8, LANES), 0) < SSM_HALF_B
    a1 = [jnp.broadcast_to(apw_ref[2:3, k * LANES:(k + 1) * LANES], (8, LANES)) for k in range(n_col)]
    a2 = [jnp.where(upper, -1.0, 1.0) * jnp.broadcast_to(apw_ref[3:4, k * LANES:(k + 1) * LANES], (8, LANES))
          for k in range(n_col)]

    def step(c, h):
        r0 = pl.multiple_of(c * 8, 8)
        new = []
        for k in range(n_col):
            hp_refs[k][pl.ds(r0, 8), :] = h[k]
            new.append(a1[k] * h[k] + a2[k] * pltpu.roll(h[k], SSM_HALF_B, axis=0) + sp_refs[k][pl.ds(r0, 8), :])
        return tuple(new)

    h = lax.fori_loop(0, N_CHUNK, step, tuple(jnp.zeros((8, LANES), F32) for _ in range(n_col)), unroll=8)
    hst_ref[...] = jnp.concatenate(h, axis=1)
    hbc = jnp.concatenate(
        [jnp.concatenate([hp_refs[k][pl.ds(part + b, N_CHUNK, stride=8), :]
                          for part in (0, SSM_HALF_B) for k in range(n_col)], axis=1)
         for b in range(SSM_HALF_B)], axis=0)
    yacc_ref[...] = jnp.dot(hbc.astype(BF16), cfull_ref[...], preferred_element_type=F32)
    for t in range(0, SSM_T, 2):
        yacc_ref[:, t * LANES:] += jnp.dot(ub_ref[:, t * LANES:(t + 2) * LANES],
                                           wpair_ref[:, 0:(SSM_T - t) * LANES], preferred_element_type=F32)
    for t in range(SSM_T):
        y_ref[pl.ds(t, rows, stride=SSM_T), :] = yacc_ref[:, t * LANES:(t + 1) * LANES]


def _ssm_prompt(u, wpair, bcomp, ccomp, apw, layer):
    rows = SSM_HALF_B * N_CHUNK
    tok = SSM_HALF_B * SEQ

    def wspec(a):
        return pl.BlockSpec((None,) + a.shape[1:], lambda j, h: (layer * N_SLAB + j, 0, 0))

    return pl.pallas_call(
        _ssm_prompt_kernel,
        grid=(N_SLAB, BATCH // SSM_HALF_B),
        in_specs=[pl.BlockSpec((tok, LANES), lambda j, h: (h, j)), wspec(wpair), wspec(bcomp), wspec(ccomp), wspec(apw)],
        out_specs=[pl.BlockSpec((tok, LANES), lambda j, h: (h, j)),
                   pl.BlockSpec((None, 8, SLAB_N), lambda j, h: (h, 0, j))],
        out_shape=[jax.ShapeDtypeStruct((N_PROMPT, SSM_W), F32),
                   jax.ShapeDtypeStruct((BATCH // SSM_HALF_B, 8, SSM_G * SSM_N), F32)],
        scratch_shapes=[pltpu.VMEM((SSM_T * LANES, 2 * SLAB_N), BF16), pltpu.VMEM((2 * SLAB_N, SSM_T * LANES), BF16),
                        pltpu.VMEM((rows, SSM_T * LANES), BF16), pltpu.VMEM((rows, 2 * SLAB_N), F32),
                        pltpu.VMEM((rows, SSM_T * LANES), F32)]
                       + [pltpu.VMEM((N_CHUNK * 8, LANES), F32)] * (2 * SLAB_N // LANES),
        compiler_params=pltpu.CompilerParams(dimension_semantics=("arbitrary", "arbitrary"),
                                             vmem_limit_bytes=VMEM_LIMIT),
        name="ssm_prompt",
    )(u, wpair, bcomp, ccomp, apw)


def _ssm_sample_kernel(u_ref, h0r_ref, h0i_ref, wpair_ref, bcomp_ref, ccomp_ref, apw_ref,
                       y_ref, hr_ref, hi_ref, bhalf_ref, cfull_ref, uf_ref, yacc_ref):
    _expand_state_in(bcomp_ref, bhalf_ref, 0, DEC_SEQ)
    _expand_state_out(ccomp_ref, cfull_ref, DEC_SEQ)
    for t in range(DEC_SEQ):
        uf_ref[:, t * LANES:(t + 1) * LANES] = u_ref[pl.ds(t, DEC_BATCH, stride=DEC_SEQ), :]
    uf = uf_ref[...]
    h0r = h0r_ref[...]
    h0i = h0i_ref[...]
    ar = apw_ref[0:1, :]
    ai = apw_ref[1:2, :]
    s = jnp.dot(uf, bhalf_ref[...], precision=lax.Precision.HIGHEST, preferred_element_type=F32)
    hr_ref[...] = ar * h0r - ai * h0i + s[:, :SLAB_N]
    hi_ref[...] = ar * h0i + ai * h0r + s[:, SLAB_N:]
    ub = uf.astype(BF16)
    h0 = jnp.concatenate([h0r, h0i], axis=1).astype(BF16)
    yacc_ref[...] = jnp.dot(h0, cfull_ref[...], preferred_element_type=F32)
    for t in range(0, DEC_SEQ, 2):
        yacc_ref[:, t * LANES:] += jnp.dot(ub[:, t * LANES:(t + 2) * LANES],
                                           wpair_ref[:, 0:(DEC_SEQ - t) * LANES], preferred_element_type=F32)
    for t in range(DEC_SEQ):
        y_ref[pl.ds(t, DEC_BATCH, stride=DEC_SEQ), :] = yacc_ref[:, t * LANES:(t + 1) * LANES]


def _ssm_sample(u, h0re, h0im, wpair, bcomp, ccomp, apw, layer):
    width = DEC_SEQ * LANES

    def wspec(r, c):
        return pl.BlockSpec((None, r, c), lambda j: (layer * N_SLAB + j, 0, 0))

    st = pl.BlockSpec((DEC_BATCH, SLAB_N), lambda j: (0, j))
    return pl.pallas_call(
        _ssm_sample_kernel,
        grid=(N_SLAB,),
        in_specs=[pl.BlockSpec((N_SAMPLE, LANES), lambda j: (N_PROMPT // N_SAMPLE, j)), st, st,
                  wspec(2 * LANES, width),
                  pl.BlockSpec((None, 2 * DEC_SEQ * LANES, LANES),
                               lambda j: (layer * N_SLAB + j, SSM_T // DEC_SEQ - 1, 0)),
                  wspec(DEC_SEQ * LANES, LANES), wspec(8, SLAB_N)],
        out_specs=[pl.BlockSpec((N_SAMPLE, LANES), lambda j: (0, j)), st, st],
        out_shape=[jax.ShapeDtypeStruct((N_SAMPLE, SSM_W), F32),
                   jax.ShapeDtypeStruct((DEC_BATCH, SSM_G * SSM_N), F32),
                   jax.ShapeDtypeStruct((DEC_BATCH, SSM_G * SSM_N), F32)],
        scratch_shapes=[pltpu.VMEM((width, 2 * SLAB_N), F32), pltpu.VMEM((2 * SLAB_N, width), BF16),
                        pltpu.VMEM((DEC_BATCH, width), F32), pltpu.VMEM((DEC_BATCH, width), F32)],
        compiler_params=pltpu.CompilerParams(dimension_semantics=("arbitrary",), vmem_limit_bytes=VMEM_LIMIT),
        name="ssm_sample",
    )(u, h0re, h0im, wpair, bcomp, ccomp, apw)


def _mask_cap(valid):
    return jnp.where(valid, jnp.inf, NEG_INF).astype(F32)


def _sink_softmax(s, cap, sink):
    s = jnp.minimum(s, cap)
    m = jnp.maximum(jnp.max(s, axis=-1, keepdims=True), sink)
    e = jnp.exp(s - m)
    denom = jnp.sum(e, axis=-1, keepdims=True) + jnp.exp(sink - m)
    return e.astype(BF16), denom


def _scores(q, scale_mask, kk):
    qx = jnp.concatenate([q[:, c * KV_W:(c + 1) * KV_W] * scale_mask for c in range(GRP)], axis=0)
    return lax.dot_general(qx.astype(BF16), kk, (((1,), (1,)), ((), ())), preferred_element_type=F32)


def _band_valid(n_q, n_k, rows):
    qi = lax.broadcasted_iota(jnp.int32, (rows, n_k), 0) & (n_q - 1)
    kj = lax.broadcasted_iota(jnp.int32, (rows, n_k), 1)
    return kj, ((kj - qi - 1).astype(jnp.uint32) < WINDOW)


def _head_masks():
    lane_kv = lax.broadcasted_iota(jnp.int32, (1, KV_W), 1) // HEAD_DIM
    return [jnp.where(lane_kv == h, HEAD_DIM ** -0.5, 0.0) for h in range(N_KV)], lane_kv == 0


ATTN_Q_TILE = 1024


def _attn_prompt_kernel(q_ref, kvp_ref, kvc_ref, sink_ref, o_ref):
    cols = GRP * WINDOW
    scale_masks, _ = _head_masks()
    kj = lax.broadcasted_iota(jnp.int32, (WINDOW, cols), 0)
    qi = lax.broadcasted_iota(jnp.int32, (WINDOW, cols), 1) & (WINDOW - 1)
    from_prev = kj > qi
    cap_first = _mask_cap(~(from_prev & (pl.program_id(1) == 0)))
    n_blk = ATTN_Q_TILE // WINDOW

    def stage(r):
        cur = kvc_ref[r * WINDOW:(r + 1) * WINDOW, :]
        prev = kvp_ref[...] if r == 0 else kvc_ref[(r - 1) * WINDOW:r * WINDOW, :]
        kk = jnp.concatenate([prev[:, :KV_W], cur[:, :KV_W]], axis=0).astype(BF16)
        v_t = jnp.concatenate([prev[:, KV_W:], cur[:, KV_W:]], axis=0).T.astype(BF16)
        q = q_ref[r * WINDOW:(r + 1) * WINDOW, :].astype(F32)
        s_t = []
        for h in range(N_KV):
            qx = jnp.concatenate([q[:, c * KV_W:(c + 1) * KV_W] * scale_masks[h] for c in range(GRP)], axis=0)
            s_all = lax.dot_general(kk, qx.astype(BF16), (((1,), (1,)), ((), ())), preferred_element_type=F32)
            s_t.append(jnp.where(from_prev, s_all[:WINDOW], s_all[WINDOW:]))
        return s_t, v_t

    def softmax_t(s, first, sink):
        if first:
            s = jnp.minimum(s, cap_first)
        m = jnp.maximum(jnp.max(s, axis=0, keepdims=True), sink)
        e = jnp.exp(s - m)
        denom = jnp.sum(e, axis=0, keepdims=True) + jnp.exp(sink - m)
        e_t = jnp.concatenate([jnp.where(from_prev, e, 0.0), jnp.where(from_prev, 0.0, e)], axis=0)
        return e_t.astype(BF16), denom

    staged = [stage(r) for r in range(n_blk)]
    weights = [[softmax_t(staged[r][0][h], r == 0, sink_ref[h]) for h in range(N_KV)] for r in range(n_blk)]
    for r in range(n_blk):
        v_t = staged[r][1]
        outs = [jnp.dot(v_t, e, preferred_element_type=F32) / denom for e, denom in weights[r]]
        o_t = jnp.concatenate([outs[h][h * HEAD_DIM:(h + 1) * HEAD_DIM] for h in range(N_KV)], axis=0)
        for c in range(GRP):
            o_ref[r * WINDOW:(r + 1) * WINDOW, c * KV_W:(c + 1) * KV_W] = (
                o_t[:, c * WINDOW:(c + 1) * WINDOW].T.astype(o_ref.dtype))


def _attn_prompt(q, kv, sink_col):
    tiles = SEQ // ATTN_Q_TILE
    per_tile = ATTN_Q_TILE // WINDOW
    blocks = SEQ // WINDOW
    return pl.pallas_call(
        _attn_prompt_kernel,
        grid=(BATCH, tiles),
        in_specs=[pl.BlockSpec((ATTN_Q_TILE, ATTN_W), lambda b, j: (b * tiles + j, 0)),
                  pl.BlockSpec((WINDOW, 2 * KV_W), lambda b, j: (b * blocks + jnp.maximum(j * per_tile - 1, 0), 0)),
                  pl.BlockSpec((ATTN_Q_TILE, 2 * KV_W), lambda b, j: (b * tiles + j, 0)),
                  pl.BlockSpec((N_KV, 1, GRP * WINDOW), lambda b, j: (0, 0, 0))],
        out_specs=pl.BlockSpec((ATTN_Q_TILE, ATTN_W), lambda b, j: (b * tiles + j, 0)),
        out_shape=jax.ShapeDtypeStruct((N_PROMPT, ATTN_W), BF16),
        compiler_params=pltpu.CompilerParams(dimension_semantics=("parallel", "arbitrary")),
        name="attn_prompt",
    )(q, kv, kv, sink_col)


SAMPLE_SEQ_BLOCK = 8


def _attn_sample_kernel(q_ref, ckt_ref, cvt_ref, kvn_ref, sink_ref, *rest, n_carried):
    o_ref, nkt_ref, nvt_ref = rest[n_carried:]
    rows = N_HEADS * DEC_SEQ
    scale_masks, first_kv = _head_masks()
    kept = WINDOW - DEC_SEQ
    nt = (((1,), (1,)), ((), ()))
    t = lax.broadcasted_iota(jnp.int32, (rows, WINDOW), 0) & (DEC_SEQ - 1)
    lane = lax.broadcasted_iota(jnp.int32, (rows, WINDOW), 1)
    cap_old = _mask_cap(lane > t)
    cap_new = _mask_cap((lane >= kept) & (lane - kept <= t))
    keep_old = lax.broadcasted_iota(jnp.int32, (HEAD_DIM, WINDOW), 1) < kept
    q_all = q_ref[...].astype(F32)
    sink = jnp.concatenate([sink_ref[h] for h in range(N_KV)], axis=0)
    place = (lax.broadcasted_iota(jnp.int32, (DEC_SEQ, WINDOW), 1)
             == lax.broadcasted_iota(jnp.int32, (DEC_SEQ, WINDOW), 0) + kept).astype(F32)
    toks = [slice(s * DEC_SEQ, (s + 1) * DEC_SEQ) for s in range(SAMPLE_SEQ_BLOCK)]
    new_ts = [lax.dot_general(kvn_ref[tok, :], place, (((0,), (0,)), ((), ())),
                              precision=lax.Precision.HIGHEST, preferred_element_type=F32) for tok in toks]
    scores, values = [], []
    for s, tok in enumerate(toks):
        new_t = new_ts[s]
        for h in range(N_KV):
            ch = slice(h * HEAD_DIM, (h + 1) * HEAD_DIM)
            nkt_ref[s, h] = jnp.where(keep_old, pltpu.roll(ckt_ref[s, h], kept, axis=1), new_t[ch])
            nvt_ref[s, h] = jnp.where(keep_old, pltpu.roll(cvt_ref[s, h], kept, axis=1), new_t[KV_W:][ch])
        q = jnp.concatenate([q_all[tok, c * KV_W:(c + 1) * KV_W] * scale_masks[h]
                             for h in range(N_KV) for c in range(GRP)], axis=0).astype(BF16)
        old_k = ckt_ref[s].reshape(KV_W, WINDOW).astype(BF16)
        old_v = cvt_ref[s].reshape(KV_W, WINDOW).astype(BF16)
        new_k, new_v = new_t[:KV_W].astype(BF16), new_t[KV_W:].astype(BF16)
        scores.append((jnp.dot(q, old_k, preferred_element_type=F32), jnp.dot(q, new_k, preferred_element_type=F32)))
        values.append((old_v, new_v))
    weights = []
    for s_old, s_new in scores:
        s_old = jnp.minimum(s_old, cap_old)
        s_new = jnp.minimum(s_new, cap_new)
        m = jnp.maximum(jnp.maximum(jnp.max(s_old, axis=-1, keepdims=True), jnp.max(s_new, axis=-1, keepdims=True)),
                        sink)
        e_old = jnp.exp(s_old - m)
        e_new = jnp.exp(s_new - m)
        denom = (jnp.sum(e_old, axis=-1, keepdims=True) + jnp.sum(e_new, axis=-1, keepdims=True)
                 + jnp.exp(sink - m))
        weights.append((e_old.astype(BF16), e_new.astype(BF16), denom))
    half = GRP * DEC_SEQ
    for s, tok in enumerate(toks):
        e_old, e_new, denom = weights[s]
        v_old, v_new = values[s]
        o = (lax.dot_general(e_old, v_old, nt, preferred_element_type=F32)
             + lax.dot_general(e_new, v_new, nt, preferred_element_type=F32)) / denom
        for c in range(GRP):
            o_ref[tok, c * KV_W:(c + 1) * KV_W] = jnp.where(
                first_kv, o[c * DEC_SEQ:(c + 1) * DEC_SEQ], o[half + c * DEC_SEQ:half + (c + 1) * DEC_SEQ])


def _attn_sample(q, kv, ck, cv, sink_col, layer, windows):
    sb = SAMPLE_SEQ_BLOCK
    tok = sb * DEC_SEQ
    first = N_PROMPT // tok
    c_spec = pl.BlockSpec((None, sb, N_KV, HEAD_DIM, WINDOW), lambda i: (layer, i, 0, 0, 0))
    in_specs = [pl.BlockSpec((tok, ATTN_W), lambda i: (first + i, 0)), c_spec, c_spec,
                pl.BlockSpec((tok, 2 * KV_W), lambda i: (first + i, 0)),
                pl.BlockSpec((N_KV, GRP * DEC_SEQ, 1), lambda i: (0, 0, 0))]
    args = [q, ck, cv, kv, sink_col]
    aliases = {}
    if windows is not None:
        aliases = {len(args): 1, len(args) + 1: 2}
        in_specs += [pl.BlockSpec(memory_space=pl.ANY)] * 2
        args += list(windows)
    return pl.pallas_call(
        functools.partial(_attn_sample_kernel, n_carried=len(aliases)),
        grid=(DEC_BATCH // sb,),
        in_specs=in_specs,
        out_specs=[pl.BlockSpec((tok, ATTN_W), lambda i: (i, 0)), c_spec, c_spec],
        out_shape=[jax.ShapeDtypeStruct((N_SAMPLE, ATTN_W), F32),
                   jax.ShapeDtypeStruct((DEPTH, DEC_BATCH, N_KV, HEAD_DIM, WINDOW), F32),
                   jax.ShapeDtypeStruct((DEPTH, DEC_BATCH, N_KV, HEAD_DIM, WINDOW), F32)],
        input_output_aliases=aliases,
        compiler_params=pltpu.CompilerParams(dimension_semantics=("parallel",)),
        name="attn_sample",
    )(*args)


def _mix_out(x, y, u, o, d_ref, wglu_ref, gs_ref, ga_ref, wout_s_ref, wout_a_ref):
    y = y + d_ref[...] * u
    y = jax.nn.gelu(y)
    y = y * jax.nn.sigmoid(jnp.dot(y.astype(BF16), wglu_ref[...], preferred_element_type=F32))
    ys = _rms(y, gs_ref[...]).astype(BF16)
    ya = _rms(o, ga_ref[...]).astype(BF16)
    mix = jnp.dot(ys, wout_s_ref[...], preferred_element_type=F32)
    mix = mix + jnp.dot(ya, wout_a_ref[...], preferred_element_type=F32)
    return x + mix


def _tail_kernel(x_ref, yp_ref, ys_ref, u_ref, op_ref, os_ref, pp_ref, ps_ref,
                 d_ref, wglu_ref, gs_ref, ga_ref, wout_s_ref, wout_a_ref,
                 g2_ref, wg_ref, wu_ref, wd_ref, gp_ref, wpg_ref, wpp_ref, gf_ref,
                 *out_refs, final):
    x, y, u, o = x_ref[...], _pick(yp_ref, ys_ref), u_ref[...], _pick(op_ref, os_ref)
    x1 = jnp.concatenate([_mix_out(x[r], y[r], u[r], o[r], d_ref, wglu_ref, gs_ref, ga_ref, wout_s_ref, wout_a_ref)
                          for r in _row_slices(EPILOGUE_SLICES)], axis=0)
    x2 = _half_swiglu(x1, g2_ref, wg_ref, wu_ref, wd_ref)
    p = _pick(pp_ref, ps_ref).astype(BF16)
    x3 = []
    for r in _row_slices(EPILOGUE_SLICES):
        h = _rms(x2[r], gp_ref[...]).astype(BF16)
        gate = jax.nn.sigmoid(jnp.dot(h, wpg_ref[...], preferred_element_type=F32))
        proj = jnp.dot(p[r], wpp_ref[...], preferred_element_type=F32)
        x3.append(x2[r] + proj * gate)
    x3 = jnp.concatenate(x3, axis=0)
    if not final:
        out_refs[0][...] = x3
        return
    y = _rms(x3, gf_ref[...])
    yp_ref, ys_ref = out_refs

    @pl.when(pl.program_id(0) < PROMPT_TILES)
    def _():
        yp_ref[...] = y

    @pl.when(pl.program_id(0) >= PROMPT_TILES)
    def _():
        ys_ref[...] = y


def _layer_tail(x, y_prompt, y_sample, u, o_prompt, o_sample, p_prompt, p_sample, layer,
                d, wglu, gs, ga, wout_ssm, wout_attn, g2, wg, wu, wd, gp, wpg, wpp, gf, final):
    if final:
        out_specs = _pair_specs(D_MODEL)
        out_shape = [jax.ShapeDtypeStruct((N_PROMPT, D_MODEL), F32), jax.ShapeDtypeStruct((N_SAMPLE, D_MODEL), F32)]
    else:
        out_specs = _row_spec(D_MODEL)
        out_shape = jax.ShapeDtypeStruct((N_TOK, D_MODEL), F32)
    p_specs = [pl.BlockSpec((None, TOKEN_TILE, PLE_DIM), lambda i: (layer, jnp.minimum(i, PROMPT_TILES - 1), 0)),
               pl.BlockSpec((None, TOKEN_TILE, PLE_DIM), lambda i: (layer, jnp.maximum(i - PROMPT_TILES, 0), 0))]
    return pl.pallas_call(
        functools.partial(_tail_kernel, final=final),
        grid=(N_TOK // TOKEN_TILE,),
        in_specs=[_row_spec(D_MODEL)] + _pair_specs(SSM_W) + [_row_spec(SSM_W)] + _pair_specs(ATTN_W) + p_specs + [
            _const_spec((1, SSM_W)), _layer_spec((SSM_W, SSM_W), layer), _const_spec((1, SSM_W)),
            _const_spec((1, ATTN_W)),
            _layer_spec((SSM_W, D_MODEL), layer), _layer_spec((ATTN_W, D_MODEL), layer),
            _const_spec((1, D_MODEL)),
            _const_spec((D_MODEL, D_FF)), _const_spec((D_MODEL, D_FF)), _const_spec((D_FF, D_MODEL)),
            _const_spec((1, D_MODEL)), _layer_spec((D_MODEL, D_MODEL), layer), _layer_spec((PLE_DIM, D_MODEL), layer),
            _const_spec((1, D_MODEL))],
        out_specs=out_specs,
        out_shape=out_shape,
        compiler_params=pltpu.CompilerParams(dimension_semantics=("arbitrary",), vmem_limit_bytes=VMEM_LIMIT),
        name="layer_tail",
    )(x, y_prompt, y_sample, u, o_prompt, o_sample, p_prompt, p_sample,
      d, wglu, gs, ga, wout_ssm, wout_attn, g2, wg, wu, wd, gp, wpg, wpp, gf)


def _sink_column(sinks, rows_per_head):
    return jnp.repeat(sinks.astype(F32).reshape(N_KV, GRP), rows_per_head, axis=1)[..., None]


def _head_major(a, axis):
    shape = a.shape
    a = a.reshape(shape[:axis] + (N_KV, GRP, HEAD_DIM) + shape[axis + 1:])
    return jnp.swapaxes(a, axis, axis + 1).reshape(shape)


def kernel(x_prompt, x_sample, cache_k, cache_v, state_ssm_re, state_ssm_im, p_prompt, p_sample, ffn1_norm, ffn1_w_gate, ffn1_w_up, ffn1_w_down, mix_norm, w_in, ssm_lam_re, ssm_lam_im, ssm_log_dt, ssm_b_re, ssm_b_im, ssm_c_re, ssm_c_im, ssm_d, ssm_w_glu, ssm_out_norm, attn_sinks, attn_out_norm, w_out, ffn2_norm, ffn2_w_gate, ffn2_w_up, ffn2_w_down, ple_norm, ple_w_gate, ple_w_proj, final_norm):
    wpair, bcomp, ccomp, apw = _ssm_prep(ssm_lam_re, ssm_lam_im, ssm_log_dt, ssm_b_re, ssm_b_im, ssm_c_re, ssm_c_im)
    row = lambda v: v.reshape(1, -1)
    bf = lambda w: w.astype(BF16)
    xs = (x_prompt.reshape(N_PROMPT, D_MODEL), x_sample.reshape(N_SAMPLE, D_MODEL))
    n_state = SSM_G * SSM_N

    caches = [c.transpose(0, 1, 3, 4, 2) for c in (cache_k, cache_v)]
    windows = None
    ffn1_w = (bf(ffn1_w_gate), bf(ffn1_w_up), bf(ffn1_w_down))
    w_in_bf, w_out_bf, w_glu_bf = bf(w_in), bf(w_out), bf(ssm_w_glu)
    ple_gate_bf, ple_proj_bf = bf(ple_w_gate), bf(ple_w_proj)
    w_in_q = _head_major(w_in_bf[:, :, SSM_W:SSM_W + ATTN_W], 2)
    w_out_attn = _head_major(w_out_bf[:, SSM_W:], 1)

    kp_l, vp_l, hpr_l, hpi_l, hsr_l, hsi_l = [], [], [], [], [], []
    for i in range(DEPTH):
        x, u, q, kv, kv_last, *ffn2_w = _ffn_proj(
            xs, row(ffn1_norm[i]), *ffn1_w, row(mix_norm[i]), w_in_bf, w_in_q,
            cast=(ffn2_w_gate, ffn2_w_up, ffn2_w_down), layer=i)

        y_p, hst = _ssm_prompt(u, wpair, bcomp, ccomp, apw, i)
        hst = hst.reshape(BATCH // SSM_HALF_B, 2, SSM_HALF_B, SSM_G, SSM_N)
        hpr_l.append(hst[:, 0].reshape(BATCH, SSM_G, SSM_N))
        hpi_l.append(hst[:, 1].reshape(BATCH, SSM_G, SSM_N))
        y_s, hsr, hsi = _ssm_sample(u, state_ssm_re[i].reshape(DEC_BATCH, n_state),
                                    state_ssm_im[i].reshape(DEC_BATCH, n_state), wpair, bcomp, ccomp, apw, i)
        hsr_l.append(hsr.reshape(DEC_BATCH, SSM_G, SSM_N))
        hsi_l.append(hsi.reshape(DEC_BATCH, SSM_G, SSM_N))

        kv_last = kv_last.reshape(BATCH, WINDOW, 2, N_KV, HEAD_DIM)
        kp_l.append(kv_last[:, :, 0])
        vp_l.append(kv_last[:, :, 1])
        o_p = _attn_prompt(q, kv, _sink_column(attn_sinks[i], WINDOW).reshape(N_KV, 1, GRP * WINDOW))
        o_s, *windows = _attn_sample(q, kv, caches[0], caches[1], _sink_column(attn_sinks[i], DEC_SEQ), i, windows)

        x = _layer_tail(x, y_p, y_s, u, o_p, o_s,
                        p_prompt.reshape(DEPTH, N_PROMPT, PLE_DIM), p_sample.reshape(DEPTH, N_SAMPLE, PLE_DIM), i,
                        row(ssm_d[i]), w_glu_bf, row(ssm_out_norm[i]), row(_head_major(attn_out_norm[i], 0)),
                        w_out_bf, w_out_attn,
                        row(ffn2_norm[i]), *ffn2_w,
                        row(ple_norm[i]), ple_gate_bf, ple_proj_bf, row(final_norm),
                        final=(i == DEPTH - 1))
        xs = (x,)

    y_prompt, y_sample = x
    ks, vs = (w.transpose(0, 1, 4, 2, 3) for w in windows)
    return (y_prompt.reshape(BATCH, SEQ, D_MODEL), y_sample.reshape(DEC_BATCH, DEC_SEQ, D_MODEL),
            jnp.stack(kp_l), jnp.stack(vp_l), jnp.stack(hpr_l), jnp.stack(hpi_l),
            ks, vs, jnp.stack(hsr_l), jnp.stack(hsi_l))
```

```python
import functools

import jax
import jax.numpy as jnp
from jax import lax
from jax.experimental import pallas as pl
from jax.experimental.pallas import tpu as pltpu

F32 = jnp.float32
BF16 = jnp.bfloat16

D_MODEL = 1024
BATCH = 8
SEQ = 2048
DEPTH = 2
DEC_BATCH = 128
DEC_SEQ = 8
SSM_W = 512
SSM_P = 16
SSM_G = 32
SSM_N = 64
ATTN_W = 512
HEAD_DIM = 64
N_HEADS = 8
N_KV = 2
GRP = N_HEADS // N_KV
KV_W = N_KV * HEAD_DIM
IN_W = SSM_W + ATTN_W + 2 * KV_W
WINDOW = 128
D_FF = 2816
PLE_DIM = 256
EPS = 1e-6
NEG_INF = -1e30

N_PROMPT = BATCH * SEQ
N_SAMPLE = DEC_BATCH * DEC_SEQ
N_TOK = N_PROMPT + N_SAMPLE

LANES = 128
SSM_T = 16
N_CHUNK = SEQ // SSM_T
SLAB_G = LANES // SSM_P
N_SLAB = SSM_G // SLAB_G
SLAB_N = SLAB_G * SSM_N
SSM_HALF_B = BATCH // 2

TOKEN_TILE = 512
PROMPT_TILES = N_PROMPT // TOKEN_TILE
TILES_PER_SEQ = SEQ // TOKEN_TILE
FF_CHUNK = 256
FF_STEPS = D_FF // FF_CHUNK
VMEM_LIMIT = 56 * 1024 * 1024


def _const_spec(shape):
    nd = len(shape)
    return pl.BlockSpec(shape, lambda *_: (0,) * nd, pipeline_mode=pl.Buffered(1))


def _layer_spec(shape, layer, at=None):
    at = at or (0,) * len(shape)
    return pl.BlockSpec((None,) + shape, lambda *_: (layer,) + at, pipeline_mode=pl.Buffered(1))


def _row_spec(width):
    return pl.BlockSpec((TOKEN_TILE, width), lambda i: (i, 0))


def _pair_specs(width):
    return [pl.BlockSpec((TOKEN_TILE, width), lambda i: (jnp.minimum(i, PROMPT_TILES - 1), 0)),
            pl.BlockSpec((TOKEN_TILE, width), lambda i: (jnp.maximum(i - PROMPT_TILES, 0), 0))]


def _pick(prompt_ref, sample_ref):
    return jnp.where(pl.program_id(0) < PROMPT_TILES, prompt_ref[...].astype(F32), sample_ref[...].astype(F32))


EPILOGUE_SLICES = 2


def _row_slices(n):
    rows = TOKEN_TILE // n
    return [slice(i * rows, (i + 1) * rows) for i in range(n)]


def _rms(x, g):
    return x * lax.rsqrt(jnp.mean(x * x, axis=-1, keepdims=True) + EPS) * g


def _half_swiglu(xf, g_ref, wg_ref, wu_ref, wd_ref):
    h = _rms(xf, g_ref[...]).astype(BF16)
    acc = jnp.zeros(xf.shape, F32)
    for c in range(0, D_FF, FF_CHUNK):
        gate = jnp.dot(h, wg_ref[:, c:c + FF_CHUNK], preferred_element_type=F32)
        up = jnp.dot(h, wu_ref[:, c:c + FF_CHUNK], preferred_element_type=F32)
        act = (gate * jax.nn.sigmoid(gate) * up).astype(BF16)
        acc = acc + jnp.dot(act, wd_ref[c:c + FF_CHUNK, :], preferred_element_type=F32)
    return xf + 0.5 * acc


def _ffn_proj_kernel(*refs, paired):
    if paired:
        xp_ref, xs_ref, *refs = refs
        x = _pick(xp_ref, xs_ref)
    else:
        x_ref, *refs = refs
        x = x_ref[...]
    (g1_ref, wg_ref, wu_ref, wd_ref, gm_ref, wu_in_ref, wq_in_ref, wkv_in_ref, *cast_in,
     xo_ref, u_ref, q_ref, kv_ref, kv_tail_ref, cg_ref, cu_ref, cd_ref) = refs

    @pl.when(pl.program_id(0) < FF_STEPS)
    def _():
        for src, dst in zip(cast_in, (cg_ref, cu_ref, cd_ref)):
            dst[...] = src[...].astype(BF16)

    x1 = _half_swiglu(x, g1_ref, wg_ref, wu_ref, wd_ref)
    xo_ref[...] = x1
    for r in _row_slices(EPILOGUE_SLICES):
        h = _rms(x1[r], gm_ref[...]).astype(BF16)
        u_ref[r, :] = jnp.dot(h, wu_in_ref[...], preferred_element_type=F32)
        q_ref[r, :] = jnp.dot(h, wq_in_ref[...], preferred_element_type=F32).astype(BF16)
        kv_ref[r, :] = jnp.dot(h, wkv_in_ref[...], preferred_element_type=F32)

    step = pl.program_id(0)

    @pl.when((step < PROMPT_TILES) & (step % TILES_PER_SEQ == TILES_PER_SEQ - 1))
    def _():
        kv_tail_ref[...] = kv_ref[TOKEN_TILE - WINDOW:, :]


def _ffn_chunk_specs(layer=None):
    chunk = lambda i: jnp.minimum(i, FF_STEPS - 1)
    if layer is None:
        return [pl.BlockSpec((D_MODEL, FF_CHUNK), lambda i: (0, chunk(i))),
                pl.BlockSpec((D_MODEL, FF_CHUNK), lambda i: (0, chunk(i))),
                pl.BlockSpec((FF_CHUNK, D_MODEL), lambda i: (chunk(i), 0))]
    return [pl.BlockSpec((None, D_MODEL, FF_CHUNK), lambda i: (layer, 0, chunk(i))),
            pl.BlockSpec((None, D_MODEL, FF_CHUNK), lambda i: (layer, 0, chunk(i))),
            pl.BlockSpec((None, FF_CHUNK, D_MODEL), lambda i: (layer, chunk(i), 0))]


def _ffn_proj(xs, g1, wg, wu, wd, gm, w_in, w_in_q, cast, layer):
    paired = len(xs) == 2
    x_specs = _pair_specs(D_MODEL) if paired else [_row_spec(D_MODEL)]
    kv_col = (SSM_W + ATTN_W) // (2 * KV_W)
    return pl.pallas_call(
        functools.partial(_ffn_proj_kernel, paired=paired),
        grid=(N_TOK // TOKEN_TILE,),
        in_specs=x_specs + [_const_spec((1, D_MODEL)),
                            _layer_spec((D_MODEL, D_FF), layer), _layer_spec((D_MODEL, D_FF), layer),
                            _layer_spec((D_FF, D_MODEL), layer),
                            _const_spec((1, D_MODEL)), _layer_spec((D_MODEL, SSM_W), layer),
                            _layer_spec((D_MODEL, ATTN_W), layer), _layer_spec((D_MODEL, 2 * KV_W), layer, (0, kv_col))]
                 + _ffn_chunk_specs(layer),
        out_specs=[_row_spec(D_MODEL), _row_spec(SSM_W), _row_spec(ATTN_W), _row_spec(2 * KV_W),
                   pl.BlockSpec((WINDOW, 2 * KV_W), lambda i: (jnp.minimum(i // TILES_PER_SEQ, BATCH - 1), 0))]
                  + _ffn_chunk_specs(),
        out_shape=[jax.ShapeDtypeStruct((N_TOK, D_MODEL), F32), jax.ShapeDtypeStruct((N_TOK, SSM_W), F32),
                   jax.ShapeDtypeStruct((N_TOK, ATTN_W), BF16), jax.ShapeDtypeStruct((N_TOK, 2 * KV_W), F32),
                   jax.ShapeDtypeStruct((BATCH * WINDOW, 2 * KV_W), F32),
                   jax.ShapeDtypeStruct((D_MODEL, D_FF), BF16), jax.ShapeDtypeStruct((D_MODEL, D_FF), BF16),
                   jax.ShapeDtypeStruct((D_FF, D_MODEL), BF16)],
        compiler_params=pltpu.CompilerParams(dimension_semantics=("arbitrary",), vmem_limit_bytes=VMEM_LIMIT),
        name="ffn1_proj",
    )(*xs, g1, wg, wu, wd, gm, w_in, w_in_q, w_in, *cast)


def _discretise(lam_re, lam_im, log_dt):
    dt = jnp.exp(log_dt)
    mag = jnp.exp(lam_re * dt)
    ang = lam_im * dt
    a_re = mag * jnp.cos(ang)
    a_im = mag * jnp.sin(ang)
    den = lam_re * lam_re + lam_im * lam_im
    nr = a_re - 1.0
    k_re = (nr * lam_re + a_im * lam_im) / den
    k_im = (a_im * lam_re - nr * lam_im) / den
    return a_re, a_im, k_re, k_im


def _powers(a_re, a_im, n):
    p_re = [jnp.ones_like(a_re)]
    p_im = [jnp.zeros_like(a_re)]
    for _ in range(n):
        r, i = p_re[-1], p_im[-1]
        p_re.append(r * a_re - i * a_im)
        p_im.append(r * a_im + i * a_re)
    return p_re, p_im


def _ssm_disc_kernel(lre_ref, lim_ref, ldt_ref, are_ref, aim_ref, kre_ref, kim_ref):
    a_re, a_im, k_re, k_im = _discretise(lre_ref[...], lim_ref[...], ldt_ref[...])
    are_ref[...] = a_re
    aim_ref[...] = a_im
    kre_ref[...] = k_re
    kim_ref[...] = k_im


def _ssm_disc(lam_re, lam_im, log_dt):
    n = DEPTH * SSM_G
    ldt = jnp.broadcast_to(log_dt[..., None], lam_re.shape)
    outs = pl.pallas_call(
        _ssm_disc_kernel,
        out_shape=[jax.ShapeDtypeStruct((n, SSM_N), F32)] * 4,
        name="ssm_disc",
    )(lam_re.reshape(n, SSM_N), lam_im.reshape(n, SSM_N), ldt.reshape(n, SSM_N))
    return [o.reshape(DEPTH, SSM_G, SSM_N) for o in outs]


def _ssm_prep_kernel(ar_re, ar_im, kr_re, kr_im, btr_re, btr_im, cr_re, cr_im, af_re, af_im,
                     wpair_ref, bcomp_ref, ccomp_ref, apw_ref):
    a_re, a_im, k_re, k_im = ar_re[...], ar_im[...], kr_re[...], kr_im[...]
    bb_re = k_re * btr_re[...] - k_im * btr_im[...]
    bb_im = k_re * btr_im[...] + k_im * btr_re[...]
    p_re, p_im = _powers(a_re, a_im, SSM_T)
    c_re, c_im = cr_re[...], cr_im[...]
    ca = [jnp.concatenate([c_re * p_re[t] - c_im * p_im[t], c_re * p_im[t] + c_im * p_re[t]], axis=1)
          for t in range(SSM_T + 1)]
    ca_all = jnp.concatenate(ca[:SSM_T], axis=0)
    bb2 = jnp.concatenate([bb_re, -bb_im], axis=1)
    drow = lax.dot_general(bb2, ca_all, (((1,), (1,)), ((), ())),
                           precision=lax.Precision.HIGHEST, preferred_element_type=F32)
    row_g = lax.broadcasted_iota(jnp.int32, drow.shape, 0) // SSM_P
    col_g = (lax.broadcasted_iota(jnp.int32, drow.shape, 1) % LANES) // SSM_P
    drow = jnp.where(row_g == col_g, drow, 0.0).astype(BF16)
    wpair_ref[0:LANES, :] = drow
    wpair_ref[LANES:, 0:LANES] = jnp.zeros((LANES, LANES), BF16)
    wpair_ref[LANES:, LANES:] = drow[:, :(SSM_T - 1) * LANES]

    for s in range(SSM_T):
        w_re, w_im = p_re[SSM_T - 1 - s], p_im[SSM_T - 1 - s]
        for part, v in enumerate((bb_re * w_re - bb_im * w_im, bb_re * w_im + bb_im * w_re)):
            bcomp_ref[(2 * s + part) * LANES:(2 * s + part + 1) * LANES, :] = jnp.concatenate([v, v], axis=1)

    for t in range(SSM_T):
        cat = ca[t + 1].T
        ccomp_ref[t * LANES:(t + 1) * LANES, :] = jnp.concatenate([cat[:SSM_N], -cat[SSM_N:]], axis=0)

    a_re, a_im = af_re[...], af_im[...]
    sq = []
    for _ in range(4):
        a_re, a_im = a_re * a_re - a_im * a_im, 2.0 * a_re * a_im
        sq.append((a_re, a_im))
    apw_ref[0:1, :] = sq[2][0]
    apw_ref[1:2, :] = sq[2][1]
    apw_ref[2:3, :] = sq[3][0]
    apw_ref[3:4, :] = sq[3][1]
    apw_ref[4:8, :] = jnp.zeros((4, SLAB_N), F32)


def _ssm_prep(lam_re, lam_im, log_dt, b_re, b_im, c_re, c_im):
    n = DEPTH * N_SLAB
    a_re, a_im, k_re, k_im = _ssm_disc(lam_re, lam_im, log_dt)
    rows = lambda v: jnp.repeat(v.reshape(n, SLAB_G, 1, SSM_N), SSM_P, axis=2).reshape(n, LANES, SSM_N)
    bt = lambda v: v.transpose(0, 1, 3, 2).reshape(n, LANES, SSM_N)
    flat = lambda v: v.reshape(n, 1, SLAB_N)
    args = ([rows(v) for v in (a_re, a_im, k_re, k_im)]
            + [bt(b_re), bt(b_im), c_re.reshape(n, LANES, SSM_N), c_im.reshape(n, LANES, SSM_N)]
            + [flat(a_re), flat(a_im)])

    def blk(a):
        return pl.BlockSpec((None,) + a.shape[1:], lambda i: (i, 0, 0))

    out_shape = [jax.ShapeDtypeStruct((n, 2 * LANES, SSM_T * LANES), BF16),
                 jax.ShapeDtypeStruct((n, 2 * SSM_T * LANES, LANES), F32),
                 jax.ShapeDtypeStruct((n, SSM_T * LANES, LANES), F32),
                 jax.ShapeDtypeStruct((n, 8, SLAB_N), F32)]
    return pl.pallas_call(
        _ssm_prep_kernel,
        grid=(n,),
        in_specs=[blk(a) for a in args],
        out_specs=[blk(s) for s in out_shape],
        out_shape=out_shape,
        compiler_params=pltpu.CompilerParams(dimension_semantics=("parallel",), vmem_limit_bytes=VMEM_LIMIT),
        name="ssm_prep",
    )(*args)


def _expand_state_in(bcomp_ref, dst_ref, first, count):
    row_grp = lax.broadcasted_iota(jnp.int32, (LANES, LANES), 0) // SSM_P
    lane_half = lax.broadcasted_iota(jnp.int32, (LANES, LANES), 1) // SSM_N
    for j in range(count):
        for part in range(2):
            tile = bcomp_ref[(2 * (first + j) + part) * LANES:(2 * (first + j) + part + 1) * LANES, :]
            for k in range(SLAB_N // LANES):
                dst_ref[j * LANES:(j + 1) * LANES, part * SLAB_N + k * LANES:part * SLAB_N + (k + 1) * LANES] = (
                    jnp.where(row_grp == 2 * k + lane_half, tile, 0.0).astype(dst_ref.dtype))


def _expand_state_out(ccomp_ref, dst_ref, count):
    lane_grp = lax.broadcasted_iota(jnp.int32, (SSM_N, LANES), 1) // SSM_P
    for t in range(count):
        for part in range(2):
            tile = ccomp_ref[t * LANES + part * SSM_N:t * LANES + (part + 1) * SSM_N, :]
            for g in range(SLAB_G):
                dst_ref[part * SLAB_N + g * SSM_N:part * SLAB_N + (g + 1) * SSM_N, t * LANES:(t + 1) * LANES] = (
                    jnp.where(lane_grp == g, tile, 0.0).astype(dst_ref.dtype))


def _ssm_prompt_kernel(u_ref, wpair_ref, bcomp_ref, ccomp_ref, apw_ref, y_ref, hst_ref,
                       bfull_ref, cfull_ref, ub_ref, s_ref, yacc_ref, *col_refs):
    rows = SSM_HALF_B * N_CHUNK
    n_col = SLAB_N // LANES

    @pl.when(pl.program_id(1) == 0)
    def _():
        _expand_state_in(bcomp_ref, bfull_ref, 0, SSM_T)
        _expand_state_out(ccomp_ref, cfull_ref, SSM_T)

    sp_refs, hp_refs = col_refs[:n_col], col_refs[n_col:]
    for t in range(SSM_T):
        ub_ref[:, t * LANES:(t + 1) * LANES] = u_ref[pl.ds(t, rows, stride=SSM_T), :].astype(BF16)
    s_ref[...] = jnp.dot(ub_ref[...], bfull_ref[...], preferred_element_type=F32)
    for b in range(SSM_HALF_B):
        seq = slice(b * N_CHUNK, (b + 1) * N_CHUNK)
        for k in range(n_col):
            sp_refs[k][pl.ds(b, N_CHUNK, stride=8), :] = s_ref[seq, k * LANES:(k + 1) * LANES]
            sp_refs[k][pl.ds(SSM_HALF_B + b, N_CHUNK, stride=8), :] = (
                s_ref[seq, SLAB_N + k * LANES:SLAB_N + (k + 1) * LANES])
    upper = lax.broadcasted_iota(jnp.int32, (8, LANES), 0) < SSM_HALF_B
    a1 = [jnp.broadcast_to(apw_ref[2:3, k * LANES:(k + 1) * LANES], (8, LANES)) for k in range(n_col)]
    a2 = [jnp.where(upper, -1.0, 1.0) * jnp.broadcast_to(apw_ref[3:4, k * LANES:(k + 1) * LANES], (8, LANES))
          for k in range(n_col)]

    def step(c, h):
        r0 = pl.multiple_of(c * 8, 8)
        new = []
        for k in range(n_col):
            hp_refs[k][pl.ds(r0, 8), :] = h[k]
            new.append(a1[k] * h[k] + a2[k] * pltpu.roll(h[k], SSM_HALF_B, axis=0) + sp_refs[k][pl.ds(r0, 8), :])
        return tuple(new)

    h = lax.fori_loop(0, N_CHUNK, step, tuple(jnp.zeros((8, LANES), F32) for _ in range(n_col)), unroll=8)
    hst_ref[...] = jnp.concatenate(h, axis=1)
    hbc = jnp.concatenate(
        [jnp.concatenate([hp_refs[k][pl.ds(part + b, N_CHUNK, stride=8), :]
                          for part in (0, SSM_HALF_B) for k in range(n_col)], axis=1)
         for b in range(SSM_HALF_B)], axis=0)
    yacc_ref[...] = jnp.dot(hbc.astype(BF16), cfull_ref[...], preferred_element_type=F32)
    for t in range(0, SSM_T, 2):
        yacc_ref[:, t * LANES:] += jnp.dot(ub_ref[:, t * LANES:(t + 2) * LANES],
                                           wpair_ref[:, 0:(SSM_T - t) * LANES], preferred_element_type=F32)
    for t in range(SSM_T):
        y_ref[pl.ds(t, rows, stride=SSM_T), :] = yacc_ref[:, t * LANES:(t + 1) * LANES]


def _ssm_prompt(u, wpair, bcomp, ccomp, apw, layer):
    rows = SSM_HALF_B * N_CHUNK
    tok = SSM_HALF_B * SEQ

    def wspec(a):
        return pl.BlockSpec((None,) + a.shape[1:], lambda j, h: (layer * N_SLAB + j, 0, 0))

    return pl.pallas_call(
        _ssm_prompt_kernel,
        grid=(N_SLAB, BATCH // SSM_HALF_B),
        in_specs=[pl.BlockSpec((tok, LANES), lambda j, h: (h, j)), wspec(wpair), wspec(bcomp), wspec(ccomp), wspec(apw)],
        out_specs=[pl.BlockSpec((tok, LANES), lambda j, h: (h, j)),
                   pl.BlockSpec((None, 8, SLAB_N), lambda j, h: (h, 0, j))],
        out_shape=[jax.ShapeDtypeStruct((N_PROMPT, SSM_W), F32),
                   jax.ShapeDtypeStruct((BATCH // SSM_HALF_B, 8, SSM_G * SSM_N), F32)],
        scratch_shapes=[pltpu.VMEM((SSM_T * LANES, 2 * SLAB_N), BF16), pltpu.VMEM((2 * SLAB_N, SSM_T * LANES), BF16),
                        pltpu.VMEM((rows, SSM_T * LANES), BF16), pltpu.VMEM((rows, 2 * SLAB_N), F32),
                        pltpu.VMEM((rows, SSM_T * LANES), F32)]
                       + [pltpu.VMEM((N_CHUNK * 8, LANES), F32)] * (2 * SLAB_N // LANES),
        compiler_params=pltpu.CompilerParams(dimension_semantics=("arbitrary", "arbitrary"),
                                             vmem_limit_bytes=VMEM_LIMIT),
        name="ssm_prompt",
    )(u, wpair, bcomp, ccomp, apw)


def _ssm_sample_kernel(u_ref, h0r_ref, h0i_ref, wpair_ref, bcomp_ref, ccomp_ref, apw_ref,
                       y_ref, hr_ref, hi_ref, bhalf_ref, cfull_ref, uf_ref, yacc_ref):
    _expand_state_in(bcomp_ref, bhalf_ref, 0, DEC_SEQ)
    _expand_state_out(ccomp_ref, cfull_ref, DEC_SEQ)
    for t in range(DEC_SEQ):
        uf_ref[:, t * LANES:(t + 1) * LANES] = u_ref[pl.ds(t, DEC_BATCH, stride=DEC_SEQ), :]
    uf = uf_ref[...]
    h0r = h0r_ref[...]
    h0i = h0i_ref[...]
    ar = apw_ref[0:1, :]
    ai = apw_ref[1:2, :]
    s = jnp.dot(uf, bhalf_ref[...], precision=lax.Precision.HIGHEST, preferred_element_type=F32)
    hr_ref[...] = ar * h0r - ai * h0i + s[:, :SLAB_N]
    hi_ref[...] = ar * h0i + ai * h0r + s[:, SLAB_N:]
    ub = uf.astype(BF16)
    h0 = jnp.concatenate([h0r, h0i], axis=1).astype(BF16)
    yacc_ref[...] = jnp.dot(h0, cfull_ref[...], preferred_element_type=F32)
    for t in range(0, DEC_SEQ, 2):
        yacc_ref[:, t * LANES:] += jnp.dot(ub[:, t * LANES:(t + 2) * LANES],
                                           wpair_ref[:, 0:(DEC_SEQ - t) * LANES], preferred_element_type=F32)
    for t in range(DEC_SEQ):
        y_ref[pl.ds(t, DEC_BATCH, stride=DEC_SEQ), :] = yacc_ref[:, t * LANES:(t + 1) * LANES]


def _ssm_sample(u, h0re, h0im, wpair, bcomp, ccomp, apw, layer):
    width = DEC_SEQ * LANES

    def wspec(r, c):
        return pl.BlockSpec((None, r, c), lambda j: (layer * N_SLAB + j, 0, 0))

    st = pl.BlockSpec((DEC_BATCH, SLAB_N), lambda j: (0, j))
    return pl.pallas_call(
        _ssm_sample_kernel,
        grid=(N_SLAB,),
        in_specs=[pl.BlockSpec((N_SAMPLE, LANES), lambda j: (N_PROMPT // N_SAMPLE, j)), st, st,
                  wspec(2 * LANES, width),
                  pl.BlockSpec((None, 2 * DEC_SEQ * LANES, LANES),
                               lambda j: (layer * N_SLAB + j, SSM_T // DEC_SEQ - 1, 0)),
                  wspec(DEC_SEQ * LANES, LANES), wspec(8, SLAB_N)],
        out_specs=[pl.BlockSpec((N_SAMPLE, LANES), lambda j: (0, j)), st, st],
        out_shape=[jax.ShapeDtypeStruct((N_SAMPLE, SSM_W), F32),
                   jax.ShapeDtypeStruct((DEC_BATCH, SSM_G * SSM_N), F32),
                   jax.ShapeDtypeStruct((DEC_BATCH, SSM_G * SSM_N), F32)],
        scratch_shapes=[pltpu.VMEM((width, 2 * SLAB_N), F32), pltpu.VMEM((2 * SLAB_N, width), BF16),
                        pltpu.VMEM((DEC_BATCH, width), F32), pltpu.VMEM((DEC_BATCH, width), F32)],
        compiler_params=pltpu.CompilerParams(dimension_semantics=("arbitrary",), vmem_limit_bytes=VMEM_LIMIT),
        name="ssm_sample",
    )(u, h0re, h0im, wpair, bcomp, ccomp, apw)


def _mask_cap(valid):
    return jnp.where(valid, jnp.inf, NEG_INF).astype(F32)


def _head_masks():
    lane_kv = lax.broadcasted_iota(jnp.int32, (1, KV_W), 1) // HEAD_DIM
    return [jnp.where(lane_kv == h, HEAD_DIM ** -0.5, 0.0) for h in range(N_KV)], lane_kv == 0


ATTN_Q_TILE = 1024


def _attn_prompt_kernel(q_ref, kvp_ref, kvc_ref, sink_ref, o_ref):
    cols = GRP * WINDOW
    scale_masks, _ = _head_masks()
    kj = lax.broadcasted_iota(jnp.int32, (WINDOW, cols), 0)
    qi = lax.broadcasted_iota(jnp.int32, (WINDOW, cols), 1) & (WINDOW - 1)
    from_prev = kj > qi
    cap_first = _mask_cap(~(from_prev & (pl.program_id(1) == 0)))
    n_blk = ATTN_Q_TILE // WINDOW

    def stage(r):
        cur = kvc_ref[r * WINDOW:(r + 1) * WINDOW, :]
        prev = kvp_ref[...] if r == 0 else kvc_ref[(r - 1) * WINDOW:r * WINDOW, :]
        kk = jnp.concatenate([prev[:, :KV_W], cur[:, :KV_W]], axis=0).astype(BF16)
        v_t = jnp.concatenate([prev[:, KV_W:], cur[:, KV_W:]], axis=0).T.astype(BF16)
        q = q_ref[r * WINDOW:(r + 1) * WINDOW, :].astype(F32)
        s_t = []
        for h in range(N_KV):
            qx = jnp.concatenate([q[:, c * KV_W:(c + 1) * KV_W] * scale_masks[h] for c in range(GRP)], axis=0)
            s_all = lax.dot_general(kk, qx.astype(BF16), (((1,), (1,)), ((), ())), preferred_element_type=F32)
            s_t.append(jnp.where(from_prev, s_all[:WINDOW], s_all[WINDOW:]))
        return s_t, v_t

    def softmax_t(s, first, sink):
        if first:
            s = jnp.minimum(s, cap_first)
        m = jnp.maximum(jnp.max(s, axis=0, keepdims=True), sink)
        e = jnp.exp(s - m)
        denom = jnp.sum(e, axis=0, keepdims=True) + jnp.exp(sink - m)
        e_t = jnp.concatenate([jnp.where(from_prev, e, 0.0), jnp.where(from_prev, 0.0, e)], axis=0)
        return e_t.astype(BF16), denom

    staged = [stage(r) for r in range(n_blk)]
    weights = [[softmax_t(staged[r][0][h], r == 0, sink_ref[h]) for h in range(N_KV)] for r in range(n_blk)]
    for r in range(n_blk):
        v_t = staged[r][1]
        outs = [jnp.dot(v_t, e, preferred_element_type=F32) / denom for e, denom in weights[r]]
        o_t = jnp.concatenate([outs[h][h * HEAD_DIM:(h + 1) * HEAD_DIM] for h in range(N_KV)], axis=0)
        for c in range(GRP):
            o_ref[r * WINDOW:(r + 1) * WINDOW, c * KV_W:(c + 1) * KV_W] = (
                o_t[:, c * WINDOW:(c + 1) * WINDOW].T.astype(o_ref.dtype))


def _attn_prompt(q, kv, sink_col):
    tiles = SEQ // ATTN_Q_TILE
    per_tile = ATTN_Q_TILE // WINDOW
    blocks = SEQ // WINDOW
    return pl.pallas_call(
        _attn_prompt_kernel,
        grid=(BATCH, tiles),
        in_specs=[pl.BlockSpec((ATTN_Q_TILE, ATTN_W), lambda b, j: (b * tiles + j, 0)),
                  pl.BlockSpec((WINDOW, 2 * KV_W), lambda b, j: (b * blocks + jnp.maximum(j * per_tile - 1, 0), 0)),
                  pl.BlockSpec((ATTN_Q_TILE, 2 * KV_W), lambda b, j: (b * tiles + j, 0)),
                  pl.BlockSpec((N_KV, 1, GRP * WINDOW), lambda b, j: (0, 0, 0))],
        out_specs=pl.BlockSpec((ATTN_Q_TILE, ATTN_W), lambda b, j: (b * tiles + j, 0)),
        out_shape=jax.ShapeDtypeStruct((N_PROMPT, ATTN_W), BF16),
        compiler_params=pltpu.CompilerParams(dimension_semantics=("parallel", "arbitrary")),
        name="attn_prompt",
    )(q, kv, kv, sink_col)


SAMPLE_SEQ_BLOCK = 8


def _attn_sample_kernel(q_ref, ckt_ref, cvt_ref, kvn_ref, sink_ref, *rest, n_carried):
    o_ref, nkt_ref, nvt_ref = rest[n_carried:]
    rows = N_HEADS * DEC_SEQ
    scale_masks, first_kv = _head_masks()
    kept = WINDOW - DEC_SEQ
    nt = (((1,), (1,)), ((), ()))
    t = lax.broadcasted_iota(jnp.int32, (rows, WINDOW), 0) & (DEC_SEQ - 1)
    lane = lax.broadcasted_iota(jnp.int32, (rows, WINDOW), 1)
    cap_old = _mask_cap(lane > t)
    cap_new = _mask_cap((lane >= kept) & (lane - kept <= t))
    keep_old = lax.broadcasted_iota(jnp.int32, (HEAD_DIM, WINDOW), 1) < kept
    q_all = q_ref[...].astype(F32)
    sink = jnp.concatenate([sink_ref[h] for h in range(N_KV)], axis=0)
    place = (lax.broadcasted_iota(jnp.int32, (DEC_SEQ, WINDOW), 1)
             == lax.broadcasted_iota(jnp.int32, (DEC_SEQ, WINDOW), 0) + kept).astype(F32)
    toks = [slice(s * DEC_SEQ, (s + 1) * DEC_SEQ) for s in range(SAMPLE_SEQ_BLOCK)]
    new_ts = [lax.dot_general(kvn_ref[tok, :], place, (((0,), (0,)), ((), ())),
                              precision=lax.Precision.HIGHEST, preferred_element_type=F32) for tok in toks]
    scores, values = [], []
    for s, tok in enumerate(toks):
        new_t = new_ts[s]
        for h in range(N_KV):
            ch = slice(h * HEAD_DIM, (h + 1) * HEAD_DIM)
            nkt_ref[s, h] = jnp.where(keep_old, pltpu.roll(ckt_ref[s, h], kept, axis=1), new_t[ch])
            nvt_ref[s, h] = jnp.where(keep_old, pltpu.roll(cvt_ref[s, h], kept, axis=1), new_t[KV_W:][ch])
        q = jnp.concatenate([q_all[tok, c * KV_W:(c + 1) * KV_W] * scale_masks[h]
                             for h in range(N_KV) for c in range(GRP)], axis=0).astype(BF16)
        old_k = ckt_ref[s].reshape(KV_W, WINDOW).astype(BF16)
        old_v = cvt_ref[s].reshape(KV_W, WINDOW).astype(BF16)
        new_k, new_v = new_t[:KV_W].astype(BF16), new_t[KV_W:].astype(BF16)
        scores.append((jnp.dot(q, old_k, preferred_element_type=F32), jnp.dot(q, new_k, preferred_element_type=F32)))
        values.append((old_v, new_v))
    weights = []
    for s_old, s_new in scores:
        s_old = jnp.minimum(s_old, cap_old)
        s_new = jnp.minimum(s_new, cap_new)
        m = jnp.maximum(jnp.maximum(jnp.max(s_old, axis=-1, keepdims=True), jnp.max(s_new, axis=-1, keepdims=True)),
                        sink)
        e_old = jnp.exp(s_old - m)
        e_new = jnp.exp(s_new - m)
        denom = (jnp.sum(e_old, axis=-1, keepdims=True) + jnp.sum(e_new, axis=-1, keepdims=True)
                 + jnp.exp(sink - m))
        weights.append((e_old.astype(BF16), e_new.astype(BF16), denom))
    half = GRP * DEC_SEQ
    for s, tok in enumerate(toks):
        e_old, e_new, denom = weights[s]
        v_old, v_new = values[s]
        o = (lax.dot_general(e_old, v_old, nt, preferred_element_type=F32)
             + lax.dot_general(e_new, v_new, nt, preferred_element_type=F32)) / denom
        for c in range(GRP):
            o_ref[tok, c * KV_W:(c + 1) * KV_W] = jnp.where(
                first_kv, o[c * DEC_SEQ:(c + 1) * DEC_SEQ], o[half + c * DEC_SEQ:half + (c + 1) * DEC_SEQ])


def _attn_sample(q, kv, ck, cv, sink_col, layer, windows):
    sb = SAMPLE_SEQ_BLOCK
    tok = sb * DEC_SEQ
    first = N_PROMPT // tok
    c_spec = pl.BlockSpec((None, sb, N_KV, HEAD_DIM, WINDOW), lambda i: (layer, i, 0, 0, 0))
    in_specs = [pl.BlockSpec((tok, ATTN_W), lambda i: (first + i, 0)), c_spec, c_spec,
                pl.BlockSpec((tok, 2 * KV_W), lambda i: (first + i, 0)),
                pl.BlockSpec((N_KV, GRP * DEC_SEQ, 1), lambda i: (0, 0, 0))]
    args = [q, ck, cv, kv, sink_col]
    aliases = {}
    if windows is not None:
        aliases = {len(args): 1, len(args) + 1: 2}
        in_specs += [pl.BlockSpec(memory_space=pl.ANY)] * 2
        args += list(windows)
    return pl.pallas_call(
        functools.partial(_attn_sample_kernel, n_carried=len(aliases)),
        grid=(DEC_BATCH // sb,),
        in_specs=in_specs,
        out_specs=[pl.BlockSpec((tok, ATTN_W), lambda i: (i, 0)), c_spec, c_spec],
        out_shape=[jax.ShapeDtypeStruct((N_SAMPLE, ATTN_W), F32),
                   jax.ShapeDtypeStruct((DEPTH, DEC_BATCH, N_KV, HEAD_DIM, WINDOW), F32),
                   jax.ShapeDtypeStruct((DEPTH, DEC_BATCH, N_KV, HEAD_DIM, WINDOW), F32)],
        input_output_aliases=aliases,
        compiler_params=pltpu.CompilerParams(dimension_semantics=("parallel",)),
        name="attn_sample",
    )(*args)


def _mix_out(x, y, u, o, d_ref, wglu_ref, gs_ref, ga_ref, wout_s_ref, wout_a_ref):
    y = y + d_ref[...] * u
    y = jax.nn.gelu(y)
    y = y * jax.nn.sigmoid(jnp.dot(y.astype(BF16), wglu_ref[...], preferred_element_type=F32))
    ys = _rms(y, gs_ref[...]).astype(BF16)
    ya = _rms(o, ga_ref[...]).astype(BF16)
    mix = jnp.dot(ys, wout_s_ref[...], preferred_element_type=F32)
    mix = mix + jnp.dot(ya, wout_a_ref[...], preferred_element_type=F32)
    return x + mix


def _tail_kernel(x_ref, yp_ref, ys_ref, u_ref, op_ref, os_ref, pp_ref, ps_ref,
                 d_ref, wglu_ref, gs_ref, ga_ref, wout_s_ref, wout_a_ref,
                 g2_ref, wg_ref, wu_ref, wd_ref, gp_ref, wpg_ref, wpp_ref, gf_ref,
                 *out_refs, final):
    x, y, u, o = x_ref[...], _pick(yp_ref, ys_ref), u_ref[...], _pick(op_ref, os_ref)
    x1 = jnp.concatenate([_mix_out(x[r], y[r], u[r], o[r], d_ref, wglu_ref, gs_ref, ga_ref, wout_s_ref, wout_a_ref)
                          for r in _row_slices(EPILOGUE_SLICES)], axis=0)
    x2 = _half_swiglu(x1, g2_ref, wg_ref, wu_ref, wd_ref)
    p = _pick(pp_ref, ps_ref).astype(BF16)
    x3 = []
    for r in _row_slices(EPILOGUE_SLICES):
        h = _rms(x2[r], gp_ref[...]).astype(BF16)
        gate = jax.nn.sigmoid(jnp.dot(h, wpg_ref[...], preferred_element_type=F32))
        proj = jnp.dot(p[r], wpp_ref[...], preferred_element_type=F32)
        x3.append(x2[r] + proj * gate)
    x3 = jnp.concatenate(x3, axis=0)
    if not final:
        out_refs[0][...] = x3
        return
    y = _rms(x3, gf_ref[...])
    yp_ref, ys_ref = out_refs

    @pl.when(pl.program_id(0) < PROMPT_TILES)
    def _():
        yp_ref[...] = y

    @pl.when(pl.program_id(0) >= PROMPT_TILES)
    def _():
        ys_ref[...] = y


def _layer_tail(x, y_prompt, y_sample, u, o_prompt, o_sample, p_prompt, p_sample, layer,
                d, wglu, gs, ga, wout_ssm, wout_attn, g2, wg, wu, wd, gp, wpg, wpp, gf, final):
    if final:
        out_specs = _pair_specs(D_MODEL)
        out_shape = [jax.ShapeDtypeStruct((N_PROMPT, D_MODEL), F32), jax.ShapeDtypeStruct((N_SAMPLE, D_MODEL), F32)]
    else:
        out_specs = _row_spec(D_MODEL)
        out_shape = jax.ShapeDtypeStruct((N_TOK, D_MODEL), F32)
    p_specs = [pl.BlockSpec((None, TOKEN_TILE, PLE_DIM), lambda i: (layer, jnp.minimum(i, PROMPT_TILES - 1), 0)),
               pl.BlockSpec((None, TOKEN_TILE, PLE_DIM), lambda i: (layer, jnp.maximum(i - PROMPT_TILES, 0), 0))]
    return pl.pallas_call(
        functools.partial(_tail_kernel, final=final),
        grid=(N_TOK // TOKEN_TILE,),
        in_specs=[_row_spec(D_MODEL)] + _pair_specs(SSM_W) + [_row_spec(SSM_W)] + _pair_specs(ATTN_W) + p_specs + [
            _const_spec((1, SSM_W)), _layer_spec((SSM_W, SSM_W), layer), _const_spec((1, SSM_W)),
            _const_spec((1, ATTN_W)),
            _layer_spec((SSM_W, D_MODEL), layer), _layer_spec((ATTN_W, D_MODEL), layer),
            _const_spec((1, D_MODEL)),
            _const_spec((D_MODEL, D_FF)), _const_spec((D_MODEL, D_FF)), _const_spec((D_FF, D_MODEL)),
            _const_spec((1, D_MODEL)), _layer_spec((D_MODEL, D_MODEL), layer), _layer_spec((PLE_DIM, D_MODEL), layer),
            _const_spec((1, D_MODEL))],
        out_specs=out_specs,
        out_shape=out_shape,
        compiler_params=pltpu.CompilerParams(dimension_semantics=("arbitrary",), vmem_limit_bytes=VMEM_LIMIT),
        name="layer_tail",
    )(x, y_prompt, y_sample, u, o_prompt, o_sample, p_prompt, p_sample,
      d, wglu, gs, ga, wout_ssm, wout_attn, g2, wg, wu, wd, gp, wpg, wpp, gf)


def _sink_column(sinks, rows_per_head):
    return jnp.repeat(sinks.astype(F32).reshape(N_KV, GRP), rows_per_head, axis=1)[..., None]


def _head_major(a, axis):
    shape = a.shape
    a = a.reshape(shape[:axis] + (N_KV, GRP, HEAD_DIM) + shape[axis + 1:])
    return jnp.swapaxes(a, axis, axis + 1).reshape(shape)


def kernel(x_prompt, x_sample, cache_k, cache_v, state_ssm_re, state_ssm_im, p_prompt, p_sample, ffn1_norm, ffn1_w_gate, ffn1_w_up, ffn1_w_down, mix_norm, w_in, ssm_lam_re, ssm_lam_im, ssm_log_dt, ssm_b_re, ssm_b_im, ssm_c_re, ssm_c_im, ssm_d, ssm_w_glu, ssm_out_norm, attn_sinks, attn_out_norm, w_out, ffn2_norm, ffn2_w_gate, ffn2_w_up, ffn2_w_down, ple_norm, ple_w_gate, ple_w_proj, final_norm):
    wpair, bcomp, ccomp, apw = _ssm_prep(ssm_lam_re, ssm_lam_im, ssm_log_dt, ssm_b_re, ssm_b_im, ssm_c_re, ssm_c_im)
    row = lambda v: v.reshape(1, -1)
    bf = lambda w: w.astype(BF16)
    xs = (x_prompt.reshape(N_PROMPT, D_MODEL), x_sample.reshape(N_SAMPLE, D_MODEL))
    n_state = SSM_G * SSM_N

    caches = [c.transpose(0, 1, 3, 4, 2) for c in (cache_k, cache_v)]
    windows = None
    ffn1_w = (bf(ffn1_w_gate), bf(ffn1_w_up), bf(ffn1_w_down))
    w_in_bf, w_out_bf, w_glu_bf = bf(w_in), bf(w_out), bf(ssm_w_glu)
    ple_gate_bf, ple_proj_bf = bf(ple_w_gate), bf(ple_w_proj)
    w_in_q = _head_major(w_in_bf[:, :, SSM_W:SSM_W + ATTN_W], 2)
    w_out_attn = _head_major(w_out_bf[:, SSM_W:], 1)

    kp_l, vp_l, hpr_l, hpi_l, hsr_l, hsi_l = [], [], [], [], [], []
    for i in range(DEPTH):
        x, u, q, kv, kv_last, *ffn2_w = _ffn_proj(
            xs, row(ffn1_norm[i]), *ffn1_w, row(mix_norm[i]), w_in_bf, w_in_q,
            cast=(ffn2_w_gate, ffn2_w_up, ffn2_w_down), layer=i)

        y_p, hst = _ssm_prompt(u, wpair, bcomp, ccomp, apw, i)
        hst = hst.reshape(BATCH // SSM_HALF_B, 2, SSM_HALF_B, SSM_G, SSM_N)
        hpr_l.append(hst[:, 0].reshape(BATCH, SSM_G, SSM_N))
        hpi_l.append(hst[:, 1].reshape(BATCH, SSM_G, SSM_N))
        y_s, hsr, hsi = _ssm_sample(u, state_ssm_re[i].reshape(DEC_BATCH, n_state),
                                    state_ssm_im[i].reshape(DEC_BATCH, n_state), wpair, bcomp, ccomp, apw, i)
        hsr_l.append(hsr.reshape(DEC_BATCH, SSM_G, SSM_N))
        hsi_l.append(hsi.reshape(DEC_BATCH, SSM_G, SSM_N))

        kv_last = kv_last.reshape(BATCH, WINDOW, 2, N_KV, HEAD_DIM)
        kp_l.append(kv_last[:, :, 0])
        vp_l.append(kv_last[:, :, 1])
        o_p = _attn_prompt(q, kv, _sink_column(attn_sinks[i], WINDOW).reshape(N_KV, 1, GRP * WINDOW))
        o_s, *windows = _attn_sample(q, kv, caches[0], caches[1], _sink_column(attn_sinks[i], DEC_SEQ), i, windows)

        x = _layer_tail(x, y_p, y_s, u, o_p, o_s,
                        p_prompt.reshape(DEPTH, N_PROMPT, PLE_DIM), p_sample.reshape(DEPTH, N_SAMPLE, PLE_DIM), i,
                        row(ssm_d[i]), w_glu_bf, row(ssm_out_norm[i]), row(_head_major(attn_out_norm[i], 0)),
                        w_out_bf, w_out_attn,
                        row(ffn2_norm[i]), *ffn2_w,
                        row(ple_norm[i]), ple_gate_bf, ple_proj_bf, row(final_norm),
                        final=(i == DEPTH - 1))
        xs = (x,)

    y_prompt, y_sample = x
    ks, vs = (w.transpose(0, 1, 4, 2, 3) for w in windows)
    return (y_prompt.reshape(BATCH, SEQ, D_MODEL), y_sample.reshape(DEC_BATCH, DEC_SEQ, D_MODEL),
            jnp.stack(kp_l), jnp.stack(vp_l), jnp.stack(hpr_l), jnp.stack(hpi_l),
            ks, vs, jnp.stack(hsr_l), jnp.stack(hsi_l))
```

```python
import functools

import jax
import jax.numpy as jnp
from jax import lax
from jax.experimental import pallas as pl
from jax.experimental.pallas import tpu as pltpu

F32 = jnp.float32
BF16 = jnp.bfloat16

D_MODEL = 1024
BATCH = 8
SEQ = 2048
DEPTH = 2
DEC_BATCH = 128
DEC_SEQ = 8
SSM_W = 512
SSM_P = 16
SSM_G = 32
SSM_N = 64
ATTN_W = 512
HEAD_DIM = 64
N_HEADS = 8
N_KV = 2
GRP = N_HEADS // N_KV
KV_W = N_KV * HEAD_DIM
IN_W = SSM_W + ATTN_W + 2 * KV_W
WINDOW = 128
D_FF = 2816
PLE_DIM = 256
EPS = 1e-6
NEG_INF = -1e30

N_PROMPT = BATCH * SEQ
N_SAMPLE = DEC_BATCH * DEC_SEQ
N_TOK = N_PROMPT + N_SAMPLE

LANES = 128
SSM_T = 16
N_CHUNK = SEQ // SSM_T
SLAB_G = LANES // SSM_P
N_SLAB = SSM_G // SLAB_G
SLAB_N = SLAB_G * SSM_N
SSM_HALF_B = BATCH // 2

TOKEN_TILE = 512
PROMPT_TILES = N_PROMPT // TOKEN_TILE
TILES_PER_SEQ = SEQ // TOKEN_TILE
FF_CHUNK = 256
FF_STEPS = D_FF // FF_CHUNK
VMEM_LIMIT = 56 * 1024 * 1024


def _const_spec(shape):
    nd = len(shape)
    return pl.BlockSpec(shape, lambda *_: (0,) * nd, pipeline_mode=pl.Buffered(1))


def _layer_spec(shape, layer, at=None):
    at = at or (0,) * len(shape)
    return pl.BlockSpec((None,) + shape, lambda *_: (layer,) + at, pipeline_mode=pl.Buffered(1))


def _row_spec(width):
    return pl.BlockSpec((TOKEN_TILE, width), lambda i: (i, 0))


def _pair_specs(width):
    return [pl.BlockSpec((TOKEN_TILE, width), lambda i: (jnp.minimum(i, PROMPT_TILES - 1), 0)),
            pl.BlockSpec((TOKEN_TILE, width), lambda i: (jnp.maximum(i - PROMPT_TILES, 0), 0))]


def _pick(prompt_ref, sample_ref):
    return jnp.where(pl.program_id(0) < PROMPT_TILES, prompt_ref[...].astype(F32), sample_ref[...].astype(F32))


EPILOGUE_SLICES = 2


def _row_slices(n):
    rows = TOKEN_TILE // n
    return [slice(i * rows, (i + 1) * rows) for i in range(n)]


def _rms(x, g):
    return x * lax.rsqrt(jnp.mean(x * x, axis=-1, keepdims=True) + EPS) * g


def _half_swiglu(xf, g_ref, wg_ref, wu_ref, wd_ref):
    h = _rms(xf, g_ref[...]).astype(BF16)
    acc = jnp.zeros(xf.shape, F32)
    for c in range(0, D_FF, FF_CHUNK):
        gate = jnp.dot(h, wg_ref[:, c:c + FF_CHUNK], preferred_element_type=F32)
        up = jnp.dot(h, wu_ref[:, c:c + FF_CHUNK], preferred_element_type=F32)
        act = (gate * jax.nn.sigmoid(gate) * up).astype(BF16)
        acc = acc + jnp.dot(act, wd_ref[c:c + FF_CHUNK, :], preferred_element_type=F32)
    return xf + 0.5 * acc


def _ffn_proj_kernel(*refs, paired):
    if paired:
        xp_ref, xs_ref, *refs = refs
        x = _pick(xp_ref, xs_ref)
    else:
        x_ref, *refs = refs
        x = x_ref[...]
    (g1_ref, wg_ref, wu_ref, wd_ref, gm_ref, wu_in_ref, wq_in_ref, wkv_in_ref, *cast_in,
     xo_ref, u_ref, q_ref, kv_ref, kv_tail_ref, cg_ref, cu_ref, cd_ref) = refs

    @pl.when(pl.program_id(0) < FF_STEPS)
    def _():
        for src, dst in zip(cast_in, (cg_ref, cu_ref, cd_ref)):
            dst[...] = src[...].astype(BF16)

    x1 = _half_swiglu(x, g1_ref, wg_ref, wu_ref, wd_ref)
    xo_ref[...] = x1
    for r in _row_slices(EPILOGUE_SLICES):
        h = _rms(x1[r], gm_ref[...]).astype(BF16)
        u_ref[r, :] = jnp.dot(h, wu_in_ref[...], preferred_element_type=F32)
        q_ref[r, :] = jnp.dot(h, wq_in_ref[...], preferred_element_type=F32).astype(BF16)
        kv_ref[r, :] = jnp.dot(h, wkv_in_ref[...], preferred_element_type=F32)

    step = pl.program_id(0)

    @pl.when((step < PROMPT_TILES) & (step % TILES_PER_SEQ == TILES_PER_SEQ - 1))
    def _():
        kv_tail_ref[...] = kv_ref[TOKEN_TILE - WINDOW:, :]


def _ffn_chunk_specs(layer=None):
    chunk = lambda i: jnp.minimum(i, FF_STEPS - 1)
    if layer is None:
        return [pl.BlockSpec((D_MODEL, FF_CHUNK), lambda i: (0, chunk(i))),
                pl.BlockSpec((D_MODEL, FF_CHUNK), lambda i: (0, chunk(i))),
                pl.BlockSpec((FF_CHUNK, D_MODEL), lambda i: (chunk(i), 0))]
    return [pl.BlockSpec((None, D_MODEL, FF_CHUNK), lambda i: (layer, 0, chunk(i))),
            pl.BlockSpec((None, D_MODEL, FF_CHUNK), lambda i: (layer, 0, chunk(i))),
            pl.BlockSpec((None, FF_CHUNK, D_MODEL), lambda i: (layer, chunk(i), 0))]


def _ffn_proj(xs, g1, wg, wu, wd, gm, w_in, w_in_q, cast, layer):
    paired = len(xs) == 2
    x_specs = _pair_specs(D_MODEL) if paired else [_row_spec(D_MODEL)]
    kv_col = (SSM_W + ATTN_W) // (2 * KV_W)
    return pl.pallas_call(
        functools.partial(_ffn_proj_kernel, paired=paired),
        grid=(N_TOK // TOKEN_TILE,),
        in_specs=x_specs + [_const_spec((1, D_MODEL)),
                            _layer_spec((D_MODEL, D_FF), layer), _layer_spec((D_MODEL, D_FF), layer),
                            _layer_spec((D_FF, D_MODEL), layer),
                            _const_spec((1, D_MODEL)), _layer_spec((D_MODEL, SSM_W), layer),
                            _layer_spec((D_MODEL, ATTN_W), layer), _layer_spec((D_MODEL, 2 * KV_W), layer, (0, kv_col))]
                 + _ffn_chunk_specs(layer),
        out_specs=[_row_spec(D_MODEL), _row_spec(SSM_W), _row_spec(ATTN_W), _row_spec(2 * KV_W),
                   pl.BlockSpec((WINDOW, 2 * KV_W), lambda i: (jnp.minimum(i // TILES_PER_SEQ, BATCH - 1), 0))]
                  + _ffn_chunk_specs(),
        out_shape=[jax.ShapeDtypeStruct((N_TOK, D_MODEL), F32), jax.ShapeDtypeStruct((N_TOK, SSM_W), F32),
                   jax.ShapeDtypeStruct((N_TOK, ATTN_W), BF16), jax.ShapeDtypeStruct((N_TOK, 2 * KV_W), F32),
                   jax.ShapeDtypeStruct((BATCH * WINDOW, 2 * KV_W), F32),
                   jax.ShapeDtypeStruct((D_MODEL, D_FF), BF16), jax.ShapeDtypeStruct((D_MODEL, D_FF), BF16),
                   jax.ShapeDtypeStruct((D_FF, D_MODEL), BF16)],
        compiler_params=pltpu.CompilerParams(dimension_semantics=("arbitrary",), vmem_limit_bytes=VMEM_LIMIT),
        name="ffn1_proj",
    )(*xs, g1, wg, wu, wd, gm, w_in, w_in_q, w_in, *cast)


def _discretise(lam_re, lam_im, log_dt):
    dt = jnp.exp(log_dt)
    mag = jnp.exp(lam_re * dt)
    ang = lam_im * dt
    a_re = mag * jnp.cos(ang)
    a_im = mag * jnp.sin(ang)
    den = lam_re * lam_re + lam_im * lam_im
    nr = a_re - 1.0
    k_re = (nr * lam_re + a_im * lam_im) / den
    k_im = (a_im * lam_re - nr * lam_im) / den
    return a_re, a_im, k_re, k_im


def _powers(a_re, a_im, n):
    p_re = [jnp.ones_like(a_re)]
    p_im = [jnp.zeros_like(a_re)]
    for _ in range(n):
        r, i = p_re[-1], p_im[-1]
        p_re.append(r * a_re - i * a_im)
        p_im.append(r * a_im + i * a_re)
    return p_re, p_im


def _ssm_disc_kernel(lre_ref, lim_ref, ldt_ref, are_ref, aim_ref, kre_ref, kim_ref):
    a_re, a_im, k_re, k_im = _discretise(lre_ref[...], lim_ref[...], ldt_ref[...])
    are_ref[...] = a_re
    aim_ref[...] = a_im
    kre_ref[...] = k_re
    kim_ref[...] = k_im


def _ssm_disc(lam_re, lam_im, log_dt):
    n = DEPTH * SSM_G
    ldt = jnp.broadcast_to(log_dt[..., None], lam_re.shape)
    outs = pl.pallas_call(
        _ssm_disc_kernel,
        out_shape=[jax.ShapeDtypeStruct((n, SSM_N), F32)] * 4,
        name="ssm_disc",
    )(lam_re.reshape(n, SSM_N), lam_im.reshape(n, SSM_N), ldt.reshape(n, SSM_N))
    return [o.reshape(DEPTH, SSM_G, SSM_N) for o in outs]


def _ssm_prep_kernel(ar_re, ar_im, kr_re, kr_im, btr_re, btr_im, cr_re, cr_im, af_re, af_im,
                     wpair_ref, bcomp_ref, ccomp_ref, apw_ref):
    a_re, a_im, k_re, k_im = ar_re[...], ar_im[...], kr_re[...], kr_im[...]
    bb_re = k_re * btr_re[...] - k_im * btr_im[...]
    bb_im = k_re * btr_im[...] + k_im * btr_re[...]
    p_re, p_im = _powers(a_re, a_im, SSM_T)
    c_re, c_im = cr_re[...], cr_im[...]
    ca = [jnp.concatenate([c_re * p_re[t] - c_im * p_im[t], c_re * p_im[t] + c_im * p_re[t]], axis=1)
          for t in range(SSM_T + 1)]
    ca_all = jnp.concatenate(ca[:SSM_T], axis=0)
    bb2 = jnp.concatenate([bb_re, -bb_im], axis=1)
    drow = lax.dot_general(bb2, ca_all, (((1,), (1,)), ((), ())),
                           precision=lax.Precision.HIGHEST, preferred_element_type=F32)
    row_g = lax.broadcasted_iota(jnp.int32, drow.shape, 0) // SSM_P
    col_g = (lax.broadcasted_iota(jnp.int32, drow.shape, 1) % LANES) // SSM_P
    drow = jnp.where(row_g == col_g, drow, 0.0).astype(BF16)
    wpair_ref[0:LANES, :] = drow
    wpair_ref[LANES:, 0:LANES] = jnp.zeros((LANES, LANES), BF16)
    wpair_ref[LANES:, LANES:] = drow[:, :(SSM_T - 1) * LANES]

    for s in range(SSM_T):
        w_re, w_im = p_re[SSM_T - 1 - s], p_im[SSM_T - 1 - s]
        for part, v in enumerate((bb_re * w_re - bb_im * w_im, bb_re * w_im + bb_im * w_re)):
            bcomp_ref[(2 * s + part) * LANES:(2 * s + part + 1) * LANES, :] = jnp.concatenate([v, v], axis=1)

    for t in range(SSM_T):
        cat = ca[t + 1].T
        ccomp_ref[t * LANES:(t + 1) * LANES, :] = jnp.concatenate([cat[:SSM_N], -cat[SSM_N:]], axis=0)

    a_re, a_im = af_re[...], af_im[...]
    sq = []
    for _ in range(4):
        a_re, a_im = a_re * a_re - a_im * a_im, 2.0 * a_re * a_im
        sq.append((a_re, a_im))
    apw_ref[0:1, :] = sq[2][0]
    apw_ref[1:2, :] = sq[2][1]
    apw_ref[2:3, :] = sq[3][0]
    apw_ref[3:4, :] = sq[3][1]
    apw_ref[4:8, :] = jnp.zeros((4, SLAB_N), F32)


def _ssm_prep(lam_re, lam_im, log_dt, b_re, b_im, c_re, c_im):
    n = DEPTH * N_SLAB
    a_re, a_im, k_re, k_im = _ssm_disc(lam_re, lam_im, log_dt)
    rows = lambda v: jnp.repeat(v.reshape(n, SLAB_G, 1, SSM_N), SSM_P, axis=2).reshape(n, LANES, SSM_N)
    bt = lambda v: v.transpose(0, 1, 3, 2).reshape(n, LANES, SSM_N)
    flat = lambda v: v.reshape(n, 1, SLAB_N)
    args = ([rows(v) for v in (a_re, a_im, k_re, k_im)]
            + [bt(b_re), bt(b_im), c_re.reshape(n, LANES, SSM_N), c_im.reshape(n, LANES, SSM_N)]
            + [flat(a_re), flat(a_im)])

    def blk(a):
        return pl.BlockSpec((None,) + a.shape[1:], lambda i: (i, 0, 0))

    out_shape = [jax.ShapeDtypeStruct((n, 2 * LANES, SSM_T * LANES), BF16),
                 jax.ShapeDtypeStruct((n, 2 * SSM_T * LANES, LANES), F32),
                 jax.ShapeDtypeStruct((n, SSM_T * LANES, LANES), F32),
                 jax.ShapeDtypeStruct((n, 8, SLAB_N), F32)]
    return pl.pallas_call(
        _ssm_prep_kernel,
        grid=(n,),
        in_specs=[blk(a) for a in args],
        out_specs=[blk(s) for s in out_shape],
        out_shape=out_shape,
        compiler_params=pltpu.CompilerParams(dimension_semantics=("parallel",), vmem_limit_bytes=VMEM_LIMIT),
        name="ssm_prep",
    )(*args)


def _expand_state_in(bcomp_ref, dst_ref, first, count):
    row_grp = lax.broadcasted_iota(jnp.int32, (LANES, LANES), 0) // SSM_P
    lane_half = lax.broadcasted_iota(jnp.int32, (LANES, LANES), 1) // SSM_N
    for j in range(count):
        for part in range(2):
            tile = bcomp_ref[(2 * (first + j) + part) * LANES:(2 * (first + j) + part + 1) * LANES, :]
            for k in range(SLAB_N // LANES):
                dst_ref[j * LANES:(j + 1) * LANES, part * SLAB_N + k * LANES:part * SLAB_N + (k + 1) * LANES] = (
                    jnp.where(row_grp == 2 * k + lane_half, tile, 0.0).astype(dst_ref.dtype))


def _expand_state_out(ccomp_ref, dst_ref, count):
    lane_grp = lax.broadcasted_iota(jnp.int32, (SSM_N, LANES), 1) // SSM_P
    for t in range(count):
        for part in range(2):
            tile = ccomp_ref[t * LANES + part * SSM_N:t * LANES + (part + 1) * SSM_N, :]
            for g in range(SLAB_G):
                dst_ref[part * SLAB_N + g * SSM_N:part * SLAB_N + (g + 1) * SSM_N, t * LANES:(t + 1) * LANES] = (
                    jnp.where(lane_grp == g, tile, 0.0).astype(dst_ref.dtype))


def _ssm_prompt_kernel(u_ref, wpair_ref, bcomp_ref, ccomp_ref, apw_ref, y_ref, hst_ref,
                       bfull_ref, cfull_ref, ub_ref, s_ref, yacc_ref, *col_refs):
    rows = SSM_HALF_B * N_CHUNK
    n_col = SLAB_N // LANES

    @pl.when(pl.program_id(1) == 0)
    def _():
        _expand_state_in(bcomp_ref, bfull_ref, 0, SSM_T)
        _expand_state_out(ccomp_ref, cfull_ref, SSM_T)

    sp_refs, hp_refs = col_refs[:n_col], col_refs[n_col:]
    for t in range(SSM_T):
        ub_ref[:, t * LANES:(t + 1) * LANES] = u_ref[pl.ds(t, rows, stride=SSM_T), :].astype(BF16)
    s_ref[...] = jnp.dot(ub_ref[...], bfull_ref[...], preferred_element_type=F32)
    for b in range(SSM_HALF_B):
        seq = slice(b * N_CHUNK, (b + 1) * N_CHUNK)
        for k in range(n_col):
            sp_refs[k][pl.ds(b, N_CHUNK, stride=8), :] = s_ref[seq, k * LANES:(k + 1) * LANES]
            sp_refs[k][pl.ds(SSM_HALF_B + b, N_CHUNK, stride=8), :] = (
                s_ref[seq, SLAB_N + k * LANES:SLAB_N + (k + 1) * LANES])
    upper = lax.broadcasted_iota(jnp.int32, (8, LANES), 0) < SSM_HALF_B
    a1 = [jnp.broadcast_to(apw_ref[2:3, k * LANES:(k + 1) * LANES], (8, LANES)) for k in range(n_col)]
    a2 = [jnp.where(upper, -1.0, 1.0) * jnp.broadcast_to(apw_ref[3:4, k * LANES:(k + 1) * LANES], (8, LANES))
          for k in range(n_col)]

    def step(c, h):
        r0 = pl.multiple_of(c * 8, 8)
        new = []
        for k in range(n_col):
            hp_refs[k][pl.ds(r0, 8), :] = h[k]
            new.append(a1[k] * h[k] + a2[k] * pltpu.roll(h[k], SSM_HALF_B, axis=0) + sp_refs[k][pl.ds(r0, 8), :])
        return tuple(new)

    h = lax.fori_loop(0, N_CHUNK, step, tuple(jnp.zeros((8, LANES), F32) for _ in range(n_col)), unroll=8)
    hst_ref[...] = jnp.concatenate(h, axis=1)
    hbc = jnp.concatenate(
        [jnp.concatenate([hp_refs[k][pl.ds(part + b, N_CHUNK, stride=8), :]
                          for part in (0, SSM_HALF_B) for k in range(n_col)], axis=1)
         for b in range(SSM_HALF_B)], axis=0)
    yacc_ref[...] = jnp.dot(hbc.astype(BF16), cfull_ref[...], preferred_element_type=F32)
    for t in range(0, SSM_T, 2):
        yacc_ref[:, t * LANES:] += jnp.dot(ub_ref[:, t * LANES:(t + 2) * LANES],
                                           wpair_ref[:, 0:(SSM_T - t) * LANES], preferred_element_type=F32)
    for t in range(SSM_T):
        y_ref[pl.ds(t, rows, stride=SSM_T), :] = yacc_ref[:, t * LANES:(t + 1) * LANES]


def _ssm_prompt(u, wpair, bcomp, ccomp, apw, layer):
    rows = SSM_HALF_B * N_CHUNK
    tok = SSM_HALF_B * SEQ

    def wspec(a):
        return pl.BlockSpec((None,) + a.shape[1:], lambda j, h: (layer * N_SLAB + j, 0, 0))

    return pl.pallas_call(
        _ssm_prompt_kernel,
        grid=(N_SLAB, BATCH // SSM_HALF_B),
        in_specs=[pl.BlockSpec((tok, LANES), lambda j, h: (h, j)), wspec(wpair), wspec(bcomp), wspec(ccomp), wspec(apw)],
        out_specs=[pl.BlockSpec((tok, LANES), lambda j, h: (h, j)),
                   pl.BlockSpec((None, 8, SLAB_N), lambda j, h: (h, 0, j))],
        out_shape=[jax.ShapeDtypeStruct((N_PROMPT, SSM_W), F32),
                   jax.ShapeDtypeStruct((BATCH // SSM_HALF_B, 8, SSM_G * SSM_N), F32)],
        scratch_shapes=[pltpu.VMEM((SSM_T * LANES, 2 * SLAB_N), BF16), pltpu.VMEM((2 * SLAB_N, SSM_T * LANES), BF16),
                        pltpu.VMEM((rows, SSM_T * LANES), BF16), pltpu.VMEM((rows, 2 * SLAB_N), F32),
                        pltpu.VMEM((rows, SSM_T * LANES), F32)]
                       + [pltpu.VMEM((N_CHUNK * 8, LANES), F32)] * (2 * SLAB_N // LANES),
        compiler_params=pltpu.CompilerParams(dimension_semantics=("arbitrary", "arbitrary"),
                                             vmem_limit_bytes=VMEM_LIMIT),
        name="ssm_prompt",
    )(u, wpair, bcomp, ccomp, apw)


def _ssm_sample_kernel(u_ref, h0r_ref, h0i_ref, wpair_ref, bcomp_ref, ccomp_ref, apw_ref,
                       y_ref, hr_ref, hi_ref, bhalf_ref, cfull_ref, uf_ref, yacc_ref):
    _expand_state_in(bcomp_ref, bhalf_ref, 0, DEC_SEQ)
    _expand_state_out(ccomp_ref, cfull_ref, DEC_SEQ)
    for t in range(DEC_SEQ):
        uf_ref[:, t * LANES:(t + 1) * LANES] = u_ref[pl.ds(t, DEC_BATCH, stride=DEC_SEQ), :]
    uf = uf_ref[...]
    h0r = h0r_ref[...]
    h0i = h0i_ref[...]
    ar = apw_ref[0:1, :]
    ai = apw_ref[1:2, :]
    s = jnp.dot(uf, bhalf_ref[...], precision=lax.Precision.HIGHEST, preferred_element_type=F32)
    hr_ref[...] = ar * h0r - ai * h0i + s[:, :SLAB_N]
    hi_ref[...] = ar * h0i + ai * h0r + s[:, SLAB_N:]
    ub = uf.astype(BF16)
    h0 = jnp.concatenate([h0r, h0i], axis=1).astype(BF16)
    yacc_ref[...] = jnp.dot(h0, cfull_ref[...], preferred_element_type=F32)
    for t in range(0, DEC_SEQ, 2):
        yacc_ref[:, t * LANES:] += jnp.dot(ub[:, t * LANES:(t + 2) * LANES],
                                           wpair_ref[:, 0:(DEC_SEQ - t) * LANES], preferred_element_type=F32)
    for t in range(DEC_SEQ):
        y_ref[pl.ds(t, DEC_BATCH, stride=DEC_SEQ), :] = yacc_ref[:, t * LANES:(t + 1) * LANES]


def _ssm_sample(u, h0re, h0im, wpair, bcomp, ccomp, apw, layer):
    width = DEC_SEQ * LANES

    def wspec(r, c):
        return pl.BlockSpec((None, r, c), lambda j: (layer * N_SLAB + j, 0, 0))

    st = pl.BlockSpec((DEC_BATCH, SLAB_N), lambda j: (0, j))
    return pl.pallas_call(
        _ssm_sample_kernel,
        grid=(N_SLAB,),
        in_specs=[pl.BlockSpec((N_SAMPLE, LANES), lambda j: (N_PROMPT // N_SAMPLE, j)), st, st,
                  wspec(2 * LANES, width),
                  pl.BlockSpec((None, 2 * DEC_SEQ * LANES, LANES),
                               lambda j: (layer * N_SLAB + j, SSM_T // DEC_SEQ - 1, 0)),
                  wspec(DEC_SEQ * LANES, LANES), wspec(8, SLAB_N)],
        out_specs=[pl.BlockSpec((N_SAMPLE, LANES), lambda j: (0, j)), st, st],
        out_shape=[jax.ShapeDtypeStruct((N_SAMPLE, SSM_W), F32),
                   jax.ShapeDtypeStruct((DEC_BATCH, SSM_G * SSM_N), F32),
                   jax.ShapeDtypeStruct((DEC_BATCH, SSM_G * SSM_N), F32)],
        scratch_shapes=[pltpu.VMEM((width, 2 * SLAB_N), F32), pltpu.VMEM((2 * SLAB_N, width), BF16),
                        pltpu.VMEM((DEC_BATCH, width), F32), pltpu.VMEM((DEC_BATCH, width), F32)],
        compiler_params=pltpu.CompilerParams(dimension_semantics=("arbitrary",), vmem_limit_bytes=VMEM_LIMIT),
        name="ssm_sample",
    )(u, h0re, h0im, wpair, bcomp, ccomp, apw)


def _mask_cap(valid):
    return jnp.where(valid, jnp.inf, NEG_INF).astype(F32)


def _head_masks():
    lane_kv = lax.broadcasted_iota(jnp.int32, (1, KV_W), 1) // HEAD_DIM
    return [jnp.where(lane_kv == h, HEAD_DIM ** -0.5, 0.0) for h in range(N_KV)], lane_kv == 0


ATTN_Q_TILE = 2048


def _attn_prompt_kernel(q_ref, kvp_ref, kvc_ref, sink_ref, o_ref):
    cols = GRP * WINDOW
    scale_masks, _ = _head_masks()
    kj = lax.broadcasted_iota(jnp.int32, (WINDOW, cols), 0)
    qi = lax.broadcasted_iota(jnp.int32, (WINDOW, cols), 1) & (WINDOW - 1)
    from_prev = kj > qi
    cap_first = _mask_cap(~(from_prev & (pl.program_id(1) == 0)))
    n_blk = ATTN_Q_TILE // WINDOW

    def stage(r):
        cur = kvc_ref[r * WINDOW:(r + 1) * WINDOW, :]
        prev = kvp_ref[...] if r == 0 else kvc_ref[(r - 1) * WINDOW:r * WINDOW, :]
        kk = jnp.concatenate([prev[:, :KV_W], cur[:, :KV_W]], axis=0).astype(BF16)
        v_t = jnp.concatenate([prev[:, KV_W:], cur[:, KV_W:]], axis=0).T.astype(BF16)
        q = q_ref[r * WINDOW:(r + 1) * WINDOW, :].astype(F32)
        s_t = []
        for h in range(N_KV):
            qx = jnp.concatenate([q[:, c * KV_W:(c + 1) * KV_W] * scale_masks[h] for c in range(GRP)], axis=0)
            s_all = lax.dot_general(kk, qx.astype(BF16), (((1,), (1,)), ((), ())), preferred_element_type=F32)
            s_t.append(jnp.where(from_prev, s_all[:WINDOW], s_all[WINDOW:]))
        return s_t, v_t

    def softmax_t(s, first, sink):
        if first:
            s = jnp.minimum(s, cap_first)
        m = jnp.maximum(jnp.max(s, axis=0, keepdims=True), sink)
        e = jnp.exp(s - m)
        denom = jnp.sum(e, axis=0, keepdims=True) + jnp.exp(sink - m)
        e_t = jnp.concatenate([jnp.where(from_prev, e, 0.0), jnp.where(from_prev, 0.0, e)], axis=0)
        return e_t.astype(BF16), denom

    staged = [stage(r) for r in range(n_blk)]
    weights = [[softmax_t(staged[r][0][h], r == 0, sink_ref[h]) for h in range(N_KV)] for r in range(n_blk)]
    for r in range(n_blk):
        v_t = staged[r][1]
        outs = [jnp.dot(v_t, e, preferred_element_type=F32) / denom for e, denom in weights[r]]
        o_t = jnp.concatenate([outs[h][h * HEAD_DIM:(h + 1) * HEAD_DIM] for h in range(N_KV)], axis=0)
        for c in range(GRP):
            o_ref[r * WINDOW:(r + 1) * WINDOW, c * KV_W:(c + 1) * KV_W] = (
                o_t[:, c * WINDOW:(c + 1) * WINDOW].T.astype(o_ref.dtype))


def _attn_prompt(q, kv, sink_col):
    tiles = SEQ // ATTN_Q_TILE
    per_tile = ATTN_Q_TILE // WINDOW
    blocks = SEQ // WINDOW
    return pl.pallas_call(
        _attn_prompt_kernel,
        grid=(BATCH, tiles),
        in_specs=[pl.BlockSpec((ATTN_Q_TILE, ATTN_W), lambda b, j: (b * tiles + j, 0)),
                  pl.BlockSpec((WINDOW, 2 * KV_W), lambda b, j: (b * blocks + jnp.maximum(j * per_tile - 1, 0), 0)),
                  pl.BlockSpec((ATTN_Q_TILE, 2 * KV_W), lambda b, j: (b * tiles + j, 0)),
                  pl.BlockSpec((N_KV, 1, GRP * WINDOW), lambda b, j: (0, 0, 0))],
        out_specs=pl.BlockSpec((ATTN_Q_TILE, ATTN_W), lambda b, j: (b * tiles + j, 0)),
        out_shape=jax.ShapeDtypeStruct((N_PROMPT, ATTN_W), BF16),
        compiler_params=pltpu.CompilerParams(dimension_semantics=("parallel", "arbitrary")),
        name="attn_prompt",
    )(q, kv, kv, sink_col)


SAMPLE_SEQ_BLOCK = 8


def _attn_sample_kernel(q_ref, ckt_ref, cvt_ref, kvn_ref, sink_ref, *rest, n_carried):
    o_ref, nkt_ref, nvt_ref = rest[n_carried:]
    rows = N_HEADS * DEC_SEQ
    scale_masks, first_kv = _head_masks()
    kept = WINDOW - DEC_SEQ
    nt = (((1,), (1,)), ((), ()))
    t = lax.broadcasted_iota(jnp.int32, (rows, WINDOW), 0) & (DEC_SEQ - 1)
    lane = lax.broadcasted_iota(jnp.int32, (rows, WINDOW), 1)
    cap_old = _mask_cap(lane > t)
    cap_new = _mask_cap((lane >= kept) & (lane - kept <= t))
    keep_old = lax.broadcasted_iota(jnp.int32, (HEAD_DIM, WINDOW), 1) < kept
    q_all = q_ref[...].astype(F32)
    sink = jnp.concatenate([sink_ref[h] for h in range(N_KV)], axis=0)
    place = (lax.broadcasted_iota(jnp.int32, (DEC_SEQ, WINDOW), 1)
             == lax.broadcasted_iota(jnp.int32, (DEC_SEQ, WINDOW), 0) + kept).astype(F32)
    toks = [slice(s * DEC_SEQ, (s + 1) * DEC_SEQ) for s in range(SAMPLE_SEQ_BLOCK)]
    new_ts = [lax.dot_general(kvn_ref[tok, :], place, (((0,), (0,)), ((), ())),
                              precision=lax.Precision.HIGHEST, preferred_element_type=F32) for tok in toks]
    scores, values = [], []
    for s, tok in enumerate(toks):
        new_t = new_ts[s]
        for h in range(N_KV):
            ch = slice(h * HEAD_DIM, (h + 1) * HEAD_DIM)
            nkt_ref[s, h] = jnp.where(keep_old, pltpu.roll(ckt_ref[s, h], kept, axis=1), new_t[ch])
            nvt_ref[s, h] = jnp.where(keep_old, pltpu.roll(cvt_ref[s, h], kept, axis=1), new_t[KV_W:][ch])
        q = jnp.concatenate([q_all[tok, c * KV_W:(c + 1) * KV_W] * scale_masks[h]
                             for h in range(N_KV) for c in range(GRP)], axis=0).astype(BF16)
        old_k = ckt_ref[s].reshape(KV_W, WINDOW).astype(BF16)
        old_v = cvt_ref[s].reshape(KV_W, WINDOW).astype(BF16)
        new_k, new_v = new_t[:KV_W].astype(BF16), new_t[KV_W:].astype(BF16)
        scores.append((jnp.dot(q, old_k, preferred_element_type=F32), jnp.dot(q, new_k, preferred_element_type=F32)))
        values.append((old_v, new_v))
    weights = []
    for s_old, s_new in scores:
        s_old = jnp.minimum(s_old, cap_old)
        s_new = jnp.minimum(s_new, cap_new)
        m = jnp.maximum(jnp.maximum(jnp.max(s_old, axis=-1, keepdims=True), jnp.max(s_new, axis=-1, keepdims=True)),
                        sink)
        e_old = jnp.exp(s_old - m)
        e_new = jnp.exp(s_new - m)
        denom = (jnp.sum(e_old, axis=-1, keepdims=True) + jnp.sum(e_new, axis=-1, keepdims=True)
                 + jnp.exp(sink - m))
        weights.append((e_old.astype(BF16), e_new.astype(BF16), denom))
    half = GRP * DEC_SEQ
    for s, tok in enumerate(toks):
        e_old, e_new, denom = weights[s]
        v_old, v_new = values[s]
        o = (lax.dot_general(e_old, v_old, nt, preferred_element_type=F32)
             + lax.dot_general(e_new, v_new, nt, preferred_element_type=F32)) / denom
        for c in range(GRP):
            o_ref[tok, c * KV_W:(c + 1) * KV_W] = jnp.where(
                first_kv, o[c * DEC_SEQ:(c + 1) * DEC_SEQ], o[half + c * DEC_SEQ:half + (c + 1) * DEC_SEQ])


def _attn_sample(q, kv, ck, cv, sink_col, layer, windows):
    sb = SAMPLE_SEQ_BLOCK
    tok = sb * DEC_SEQ
    first = N_PROMPT // tok
    c_spec = pl.BlockSpec((None, sb, N_KV, HEAD_DIM, WINDOW), lambda i: (layer, i, 0, 0, 0))
    in_specs = [pl.BlockSpec((tok, ATTN_W), lambda i: (first + i, 0)), c_spec, c_spec,
                pl.BlockSpec((tok, 2 * KV_W), lambda i: (first + i, 0)),
                pl.BlockSpec((N_KV, GRP * DEC_SEQ, 1), lambda i: (0, 0, 0))]
    args = [q, ck, cv, kv, sink_col]
    aliases = {}
    if windows is not None:
        aliases = {len(args): 1, len(args) + 1: 2}
        in_specs += [pl.BlockSpec(memory_space=pl.ANY)] * 2
        args += list(windows)
    return pl.pallas_call(
        functools.partial(_attn_sample_kernel, n_carried=len(aliases)),
        grid=(DEC_BATCH // sb,),
        in_specs=in_specs,
        out_specs=[pl.BlockSpec((tok, ATTN_W), lambda i: (i, 0)), c_spec, c_spec],
        out_shape=[jax.ShapeDtypeStruct((N_SAMPLE, ATTN_W), F32),
                   jax.ShapeDtypeStruct((DEPTH, DEC_BATCH, N_KV, HEAD_DIM, WINDOW), F32),
                   jax.ShapeDtypeStruct((DEPTH, DEC_BATCH, N_KV, HEAD_DIM, WINDOW), F32)],
        input_output_aliases=aliases,
        compiler_params=pltpu.CompilerParams(dimension_semantics=("parallel",)),
        name="attn_sample",
    )(*args)


def _mix_out(x, y, u, o, d_ref, wglu_ref, gs_ref, ga_ref, wout_s_ref, wout_a_ref):
    y = y + d_ref[...] * u
    y = jax.nn.gelu(y)
    y = y * jax.nn.sigmoid(jnp.dot(y.astype(BF16), wglu_ref[...], preferred_element_type=F32))
    ys = _rms(y, gs_ref[...]).astype(BF16)
    ya = _rms(o, ga_ref[...]).astype(BF16)
    mix = jnp.dot(ys, wout_s_ref[...], preferred_element_type=F32)
    mix = mix + jnp.dot(ya, wout_a_ref[...], preferred_element_type=F32)
    return x + mix


def _tail_kernel(x_ref, yp_ref, ys_ref, u_ref, op_ref, os_ref, pp_ref, ps_ref,
                 d_ref, wglu_ref, gs_ref, ga_ref, wout_s_ref, wout_a_ref,
                 g2_ref, wg_ref, wu_ref, wd_ref, gp_ref, wpg_ref, wpp_ref, gf_ref,
                 *out_refs, final):
    x, y, u, o = x_ref[...], _pick(yp_ref, ys_ref), u_ref[...], _pick(op_ref, os_ref)
    x1 = jnp.concatenate([_mix_out(x[r], y[r], u[r], o[r], d_ref, wglu_ref, gs_ref, ga_ref, wout_s_ref, wout_a_ref)
                          for r in _row_slices(EPILOGUE_SLICES)], axis=0)
    x2 = _half_swiglu(x1, g2_ref, wg_ref, wu_ref, wd_ref)
    p = _pick(pp_ref, ps_ref).astype(BF16)
    x3 = []
    for r in _row_slices(EPILOGUE_SLICES):
        h = _rms(x2[r], gp_ref[...]).astype(BF16)
        gate = jax.nn.sigmoid(jnp.dot(h, wpg_ref[...], preferred_element_type=F32))
        proj = jnp.dot(p[r], wpp_ref[...], preferred_element_type=F32)
        x3.append(x2[r] + proj * gate)
    x3 = jnp.concatenate(x3, axis=0)
    if not final:
        out_refs[0][...] = x3
        return
    y = _rms(x3, gf_ref[...])
    yp_ref, ys_ref = out_refs

    @pl.when(pl.program_id(0) < PROMPT_TILES)
    def _():
        yp_ref[...] = y

    @pl.when(pl.program_id(0) >= PROMPT_TILES)
    def _():
        ys_ref[...] = y


def _layer_tail(x, y_prompt, y_sample, u, o_prompt, o_sample, p_prompt, p_sample, layer,
                d, wglu, gs, ga, wout_ssm, wout_attn, g2, wg, wu, wd, gp, wpg, wpp, gf, final):
    if final:
        out_specs = _pair_specs(D_MODEL)
        out_shape = [jax.ShapeDtypeStruct((N_PROMPT, D_MODEL), F32), jax.ShapeDtypeStruct((N_SAMPLE, D_MODEL), F32)]
    else:
        out_specs = _row_spec(D_MODEL)
        out_shape = jax.ShapeDtypeStruct((N_TOK, D_MODEL), F32)
    p_specs = [pl.BlockSpec((None, TOKEN_TILE, PLE_DIM), lambda i: (layer, jnp.minimum(i, PROMPT_TILES - 1), 0)),
               pl.BlockSpec((None, TOKEN_TILE, PLE_DIM), lambda i: (layer, jnp.maximum(i - PROMPT_TILES, 0), 0))]
    return pl.pallas_call(
        functools.partial(_tail_kernel, final=final),
        grid=(N_TOK // TOKEN_TILE,),
        in_specs=[_row_spec(D_MODEL)] + _pair_specs(SSM_W) + [_row_spec(SSM_W)] + _pair_specs(ATTN_W) + p_specs + [
            _const_spec((1, SSM_W)), _layer_spec((SSM_W, SSM_W), layer), _const_spec((1, SSM_W)),
            _const_spec((1, ATTN_W)),
            _layer_spec((SSM_W, D_MODEL), layer), _layer_spec((ATTN_W, D_MODEL), layer),
            _const_spec((1, D_MODEL)),
            _const_spec((D_MODEL, D_FF)), _const_spec((D_MODEL, D_FF)), _const_spec((D_FF, D_MODEL)),
            _const_spec((1, D_MODEL)), _layer_spec((D_MODEL, D_MODEL), layer), _layer_spec((PLE_DIM, D_MODEL), layer),
            _const_spec((1, D_MODEL))],
        out_specs=out_specs,
        out_shape=out_shape,
        compiler_params=pltpu.CompilerParams(dimension_semantics=("arbitrary",), vmem_limit_bytes=VMEM_LIMIT),
        name="layer_tail",
    )(x, y_prompt, y_sample, u, o_prompt, o_sample, p_prompt, p_sample,
      d, wglu, gs, ga, wout_ssm, wout_attn, g2, wg, wu, wd, gp, wpg, wpp, gf)


def _sink_column(sinks, rows_per_head):
    return jnp.repeat(sinks.astype(F32).reshape(N_KV, GRP), rows_per_head, axis=1)[..., None]


def _head_major(a, axis):
    shape = a.shape
    a = a.reshape(shape[:axis] + (N_KV, GRP, HEAD_DIM) + shape[axis + 1:])
    return jnp.swapaxes(a, axis, axis + 1).reshape(shape)


def kernel(x_prompt, x_sample, cache_k, cache_v, state_ssm_re, state_ssm_im, p_prompt, p_sample, ffn1_norm, ffn1_w_gate, ffn1_w_up, ffn1_w_down, mix_norm, w_in, ssm_lam_re, ssm_lam_im, ssm_log_dt, ssm_b_re, ssm_b_im, ssm_c_re, ssm_c_im, ssm_d, ssm_w_glu, ssm_out_norm, attn_sinks, attn_out_norm, w_out, ffn2_norm, ffn2_w_gate, ffn2_w_up, ffn2_w_down, ple_norm, ple_w_gate, ple_w_proj, final_norm):
    wpair, bcomp, ccomp, apw = _ssm_prep(ssm_lam_re, ssm_lam_im, ssm_log_dt, ssm_b_re, ssm_b_im, ssm_c_re, ssm_c_im)
    row = lambda v: v.reshape(1, -1)
    bf = lambda w: w.astype(BF16)
    xs = (x_prompt.reshape(N_PROMPT, D_MODEL), x_sample.reshape(N_SAMPLE, D_MODEL))
    n_state = SSM_G * SSM_N

    caches = [c.transpose(0, 1, 3, 4, 2) for c in (cache_k, cache_v)]
    windows = None
    ffn1_w = (bf(ffn1_w_gate), bf(ffn1_w_up), bf(ffn1_w_down))
    w_in_bf, w_out_bf, w_glu_bf = bf(w_in), bf(w_out), bf(ssm_w_glu)
    ple_gate_bf, ple_proj_bf = bf(ple_w_gate), bf(ple_w_proj)
    w_in_q = _head_major(w_in_bf[:, :, SSM_W:SSM_W + ATTN_W], 2)
    w_out_attn = _head_major(w_out_bf[:, SSM_W:], 1)

    kp_l, vp_l, hpr_l, hpi_l, hsr_l, hsi_l = [], [], [], [], [], []
    for i in range(DEPTH):
        x, u, q, kv, kv_last, *ffn2_w = _ffn_proj(
            xs, row(ffn1_norm[i]), *ffn1_w, row(mix_norm[i]), w_in_bf, w_in_q,
            cast=(ffn2_w_gate, ffn2_w_up, ffn2_w_down), layer=i)

        y_p, hst = _ssm_prompt(u, wpair, bcomp, ccomp, apw, i)
        hst = hst.reshape(BATCH // SSM_HALF_B, 2, SSM_HALF_B, SSM_G, SSM_N)
        hpr_l.append(hst[:, 0].reshape(BATCH, SSM_G, SSM_N))
        hpi_l.append(hst[:, 1].reshape(BATCH, SSM_G, SSM_N))
        y_s, hsr, hsi = _ssm_sample(u, state_ssm_re[i].reshape(DEC_BATCH, n_state),
                                    state_ssm_im[i].reshape(DEC_BATCH, n_state), wpair, bcomp, ccomp, apw, i)
        hsr_l.append(hsr.reshape(DEC_BATCH, SSM_G, SSM_N))
        hsi_l.append(hsi.reshape(DEC_BATCH, SSM_G, SSM_N))

        kv_last = kv_last.reshape(BATCH, WINDOW, 2, N_KV, HEAD_DIM)
        kp_l.append(kv_last[:, :, 0])
        vp_l.append(kv_last[:, :, 1])
        o_p = _attn_prompt(q, kv, _sink_column(attn_sinks[i], WINDOW).reshape(N_KV, 1, GRP * WINDOW))
        o_s, *windows = _attn_sample(q, kv, caches[0], caches[1], _sink_column(attn_sinks[i], DEC_SEQ), i, windows)

        x = _layer_tail(x, y_p, y_s, u, o_p, o_s,
                        p_prompt.reshape(DEPTH, N_PROMPT, PLE_DIM), p_sample.reshape(DEPTH, N_SAMPLE, PLE_DIM), i,
                        row(ssm_d[i]), w_glu_bf, row(ssm_out_norm[i]), row(_head_major(attn_out_norm[i], 0)),
                        w_out_bf, w_out_attn,
                        row(ffn2_norm[i]), *ffn2_w,
                        row(ple_norm[i]), ple_gate_bf, ple_proj_bf, row(final_norm),
                        final=(i == DEPTH - 1))
        xs = (x,)

    y_prompt, y_sample = x
    ks, vs = (w.transpose(0, 1, 4, 2, 3) for w in windows)
    return (y_prompt.reshape(BATCH, SEQ, D_MODEL), y_sample.reshape(DEC_BATCH, DEC_SEQ, D_MODEL),
            jnp.stack(kp_l), jnp.stack(vp_l), jnp.stack(hpr_l), jnp.stack(hpi_l),
            ks, vs, jnp.stack(hsr_l), jnp.stack(hsi_l))
```

```python
import functools

import jax
import jax.numpy as jnp
from jax import lax
from jax.experimental import pallas as pl
from jax.experimental.pallas import tpu as pltpu

F32 = jnp.float32
BF16 = jnp.bfloat16

D_MODEL = 1024
BATCH = 8
SEQ = 2048
DEPTH = 2
DEC_BATCH = 128
DEC_SEQ = 8
SSM_W = 512
SSM_P = 16
SSM_G = 32
SSM_N = 64
ATTN_W = 512
HEAD_DIM = 64
N_HEADS = 8
N_KV = 2
GRP = N_HEADS // N_KV
KV_W = N_KV * HEAD_DIM
IN_W = SSM_W + ATTN_W + 2 * KV_W
WINDOW = 128
D_FF = 2816
PLE_DIM = 256
EPS = 1e-6
NEG_INF = -1e30

N_PROMPT = BATCH * SEQ
N_SAMPLE = DEC_BATCH * DEC_SEQ
N_TOK = N_PROMPT + N_SAMPLE

LANES = 128
SSM_T = 16
N_CHUNK = SEQ // SSM_T
SLAB_G = LANES // SSM_P
N_SLAB = SSM_G // SLAB_G
SLAB_N = SLAB_G * SSM_N
SSM_HALF_B = BATCH // 2

TOKEN_TILE = 512
PROMPT_TILES = N_PROMPT // TOKEN_TILE
TILES_PER_SEQ = SEQ // TOKEN_TILE
FF_CHUNK = 256
FF_STEPS = D_FF // FF_CHUNK
VMEM_LIMIT = 56 * 1024 * 1024


def _const_spec(shape):
    nd = len(shape)
    return pl.BlockSpec(shape, lambda *_: (0,) * nd, pipeline_mode=pl.Buffered(1))


def _layer_spec(shape, layer, at=None):
    at = at or (0,) * len(shape)
    return pl.BlockSpec((None,) + shape, lambda *_: (layer,) + at, pipeline_mode=pl.Buffered(1))


def _row_spec(width):
    return pl.BlockSpec((TOKEN_TILE, width), lambda i: (i, 0))


def _pair_specs(width):
    return [pl.BlockSpec((TOKEN_TILE, width), lambda i: (jnp.minimum(i, PROMPT_TILES - 1), 0)),
            pl.BlockSpec((TOKEN_TILE, width), lambda i: (jnp.maximum(i - PROMPT_TILES, 0), 0))]


def _pick(prompt_ref, sample_ref):
    return jnp.where(pl.program_id(0) < PROMPT_TILES, prompt_ref[...].astype(F32), sample_ref[...].astype(F32))


EPILOGUE_SLICES = 2


def _row_slices(n):
    rows = TOKEN_TILE // n
    return [slice(i * rows, (i + 1) * rows) for i in range(n)]


def _rms(x, g):
    return x * lax.rsqrt(jnp.mean(x * x, axis=-1, keepdims=True) + EPS) * g


def _half_swiglu(xf, g_ref, wg_ref, wu_ref, wd_ref):
    h = _rms(xf, g_ref[...]).astype(BF16)
    acc = jnp.zeros(xf.shape, F32)
    for c in range(0, D_FF, FF_CHUNK):
        gate = jnp.dot(h, wg_ref[:, c:c + FF_CHUNK], preferred_element_type=F32)
        up = jnp.dot(h, wu_ref[:, c:c + FF_CHUNK], preferred_element_type=F32)
        act = (gate * jax.nn.sigmoid(gate) * up).astype(BF16)
        acc = acc + jnp.dot(act, wd_ref[c:c + FF_CHUNK, :], preferred_element_type=F32)
    return xf + 0.5 * acc


def _ffn_proj_kernel(*refs, paired):
    if paired:
        xp_ref, xs_ref, *refs = refs
        x = _pick(xp_ref, xs_ref)
    else:
        x_ref, *refs = refs
        x = x_ref[...]
    (g1_ref, wg_ref, wu_ref, wd_ref, gm_ref, wu_in_ref, wq_in_ref, wkv_in_ref, *cast_in,
     xo_ref, u_ref, q_ref, kv_ref, kv_tail_ref, cg_ref, cu_ref, cd_ref) = refs

    @pl.when(pl.program_id(0) < FF_STEPS)
    def _():
        for src, dst in zip(cast_in, (cg_ref, cu_ref, cd_ref)):
            dst[...] = src[...].astype(BF16)

    x1 = _half_swiglu(x, g1_ref, wg_ref, wu_ref, wd_ref)
    xo_ref[...] = x1
    for r in _row_slices(EPILOGUE_SLICES):
        h = _rms(x1[r], gm_ref[...]).astype(BF16)
        u_ref[r, :] = jnp.dot(h, wu_in_ref[...], preferred_element_type=F32)
        q_ref[r, :] = jnp.dot(h, wq_in_ref[...], preferred_element_type=F32).astype(BF16)
        kv_ref[r, :] = jnp.dot(h, wkv_in_ref[...], preferred_element_type=F32)

    step = pl.program_id(0)

    @pl.when((step < PROMPT_TILES) & (step % TILES_PER_SEQ == TILES_PER_SEQ - 1))
    def _():
        kv_tail_ref[...] = kv_ref[TOKEN_TILE - WINDOW:, :]


def _ffn_chunk_specs(layer=None):
    chunk = lambda i: jnp.minimum(i, FF_STEPS - 1)
    if layer is None:
        return [pl.BlockSpec((D_MODEL, FF_CHUNK), lambda i: (0, chunk(i))),
                pl.BlockSpec((D_MODEL, FF_CHUNK), lambda i: (0, chunk(i))),
                pl.BlockSpec((FF_CHUNK, D_MODEL), lambda i: (chunk(i), 0))]
    return [pl.BlockSpec((None, D_MODEL, FF_CHUNK), lambda i: (layer, 0, chunk(i))),
            pl.BlockSpec((None, D_MODEL, FF_CHUNK), lambda i: (layer, 0, chunk(i))),
            pl.BlockSpec((None, FF_CHUNK, D_MODEL), lambda i: (layer, chunk(i), 0))]


def _ffn_proj(xs, g1, wg, wu, wd, gm, w_in, w_in_q, cast, layer):
    paired = len(xs) == 2
    x_specs = _pair_specs(D_MODEL) if paired else [_row_spec(D_MODEL)]
    kv_col = (SSM_W + ATTN_W) // (2 * KV_W)
    return pl.pallas_call(
        functools.partial(_ffn_proj_kernel, paired=paired),
        grid=(N_TOK // TOKEN_TILE,),
        in_specs=x_specs + [_const_spec((1, D_MODEL)),
                            _layer_spec((D_MODEL, D_FF), layer), _layer_spec((D_MODEL, D_FF), layer),
                            _layer_spec((D_FF, D_MODEL), layer),
                            _const_spec((1, D_MODEL)), _layer_spec((D_MODEL, SSM_W), layer),
                            _layer_spec((D_MODEL, ATTN_W), layer), _layer_spec((D_MODEL, 2 * KV_W), layer, (0, kv_col))]
                 + _ffn_chunk_specs(layer),
        out_specs=[_row_spec(D_MODEL), _row_spec(SSM_W), _row_spec(ATTN_W), _row_spec(2 * KV_W),
                   pl.BlockSpec((WINDOW, 2 * KV_W), lambda i: (jnp.minimum(i // TILES_PER_SEQ, BATCH - 1), 0))]
                  + _ffn_chunk_specs(),
        out_shape=[jax.ShapeDtypeStruct((N_TOK, D_MODEL), F32), jax.ShapeDtypeStruct((N_TOK, SSM_W), F32),
                   jax.ShapeDtypeStruct((N_TOK, ATTN_W), BF16), jax.ShapeDtypeStruct((N_TOK, 2 * KV_W), F32),
                   jax.ShapeDtypeStruct((BATCH * WINDOW, 2 * KV_W), F32),
                   jax.ShapeDtypeStruct((D_MODEL, D_FF), BF16), jax.ShapeDtypeStruct((D_MODEL, D_FF), BF16),
                   jax.ShapeDtypeStruct((D_FF, D_MODEL), BF16)],
        compiler_params=pltpu.CompilerParams(dimension_semantics=("arbitrary",), vmem_limit_bytes=VMEM_LIMIT),
        name="ffn1_proj",
    )(*xs, g1, wg, wu, wd, gm, w_in, w_in_q, w_in, *cast)


def _discretise(lam_re, lam_im, log_dt):
    dt = jnp.exp(log_dt)
    mag = jnp.exp(lam_re * dt)
    ang = lam_im * dt
    a_re = mag * jnp.cos(ang)
    a_im = mag * jnp.sin(ang)
    den = lam_re * lam_re + lam_im * lam_im
    nr = a_re - 1.0
    k_re = (nr * lam_re + a_im * lam_im) / den
    k_im = (a_im * lam_re - nr * lam_im) / den
    return a_re, a_im, k_re, k_im


def _powers(a_re, a_im, n):
    p_re = [jnp.ones_like(a_re)]
    p_im = [jnp.zeros_like(a_re)]
    for _ in range(n):
        r, i = p_re[-1], p_im[-1]
        p_re.append(r * a_re - i * a_im)
        p_im.append(r * a_im + i * a_re)
    return p_re, p_im


def _ssm_disc_kernel(lre_ref, lim_ref, ldt_ref, are_ref, aim_ref, kre_ref, kim_ref):
    a_re, a_im, k_re, k_im = _discretise(lre_ref[...], lim_ref[...], ldt_ref[...])
    are_ref[...] = a_re
    aim_ref[...] = a_im
    kre_ref[...] = k_re
    kim_ref[...] = k_im


def _ssm_disc(lam_re, lam_im, log_dt):
    n = DEPTH * SSM_G
    ldt = jnp.broadcast_to(log_dt[..., None], lam_re.shape)
    outs = pl.pallas_call(
        _ssm_disc_kernel,
        out_shape=[jax.ShapeDtypeStruct((n, SSM_N), F32)] * 4,
        name="ssm_disc",
    )(lam_re.reshape(n, SSM_N), lam_im.reshape(n, SSM_N), ldt.reshape(n, SSM_N))
    return [o.reshape(DEPTH, SSM_G, SSM_N) for o in outs]


def _ssm_prep_kernel(ar_re, ar_im, kr_re, kr_im, btr_re, btr_im, cr_re, cr_im, af_re, af_im,
                     wpair_ref, bcomp_ref, ccomp_ref, apw_ref):
    a_re, a_im, k_re, k_im = ar_re[...], ar_im[...], kr_re[...], kr_im[...]
    bb_re = k_re * btr_re[...] - k_im * btr_im[...]
    bb_im = k_re * btr_im[...] + k_im * btr_re[...]
    p_re, p_im = _powers(a_re, a_im, SSM_T)
    c_re, c_im = cr_re[...], cr_im[...]
    ca = [jnp.concatenate([c_re * p_re[t] - c_im * p_im[t], c_re * p_im[t] + c_im * p_re[t]], axis=1)
          for t in range(SSM_T + 1)]
    ca_all = jnp.concatenate(ca[:SSM_T], axis=0)
    bb2 = jnp.concatenate([bb_re, -bb_im], axis=1)
    drow = lax.dot_general(bb2, ca_all, (((1,), (1,)), ((), ())),
                           precision=lax.Precision.HIGHEST, preferred_element_type=F32)
    row_g = lax.broadcasted_iota(jnp.int32, drow.shape, 0) // SSM_P
    col_g = (lax.broadcasted_iota(jnp.int32, drow.shape, 1) % LANES) // SSM_P
    drow = jnp.where(row_g == col_g, drow, 0.0).astype(BF16)
    wpair_ref[0:LANES, :] = drow
    wpair_ref[LANES:, 0:LANES] = jnp.zeros((LANES, LANES), BF16)
    wpair_ref[LANES:, LANES:] = drow[:, :(SSM_T - 1) * LANES]

    for s in range(SSM_T):
        w_re, w_im = p_re[SSM_T - 1 - s], p_im[SSM_T - 1 - s]
        for part, v in enumerate((bb_re * w_re - bb_im * w_im, bb_re * w_im + bb_im * w_re)):
            bcomp_ref[(2 * s + part) * LANES:(2 * s + part + 1) * LANES, :] = jnp.concatenate([v, v], axis=1)

    for t in range(SSM_T):
        cat = ca[t + 1].T
        ccomp_ref[t * LANES:(t + 1) * LANES, :] = jnp.concatenate([cat[:SSM_N], -cat[SSM_N:]], axis=0)

    a_re, a_im = af_re[...], af_im[...]
    sq = []
    for _ in range(4):
        a_re, a_im = a_re * a_re - a_im * a_im, 2.0 * a_re * a_im
        sq.append((a_re, a_im))
    apw_ref[0:1, :] = sq[2][0]
    apw_ref[1:2, :] = sq[2][1]
    apw_ref[2:3, :] = sq[3][0]
    apw_ref[3:4, :] = sq[3][1]
    apw_ref[4:8, :] = jnp.zeros((4, SLAB_N), F32)


def _ssm_prep(lam_re, lam_im, log_dt, b_re, b_im, c_re, c_im):
    n = DEPTH * N_SLAB
    a_re, a_im, k_re, k_im = _ssm_disc(lam_re, lam_im, log_dt)
    rows = lambda v: jnp.repeat(v.reshape(n, SLAB_G, 1, SSM_N), SSM_P, axis=2).reshape(n, LANES, SSM_N)
    bt = lambda v: v.transpose(0, 1, 3, 2).reshape(n, LANES, SSM_N)
    flat = lambda v: v.reshape(n, 1, SLAB_N)
    args = ([rows(v) for v in (a_re, a_im, k_re, k_im)]
            + [bt(b_re), bt(b_im), c_re.reshape(n, LANES, SSM_N), c_im.reshape(n, LANES, SSM_N)]
            + [flat(a_re), flat(a_im)])

    def blk(a):
        return pl.BlockSpec((None,) + a.shape[1:], lambda i: (i, 0, 0))

    out_shape = [jax.ShapeDtypeStruct((n, 2 * LANES, SSM_T * LANES), BF16),
                 jax.ShapeDtypeStruct((n, 2 * SSM_T * LANES, LANES), F32),
                 jax.ShapeDtypeStruct((n, SSM_T * LANES, LANES), F32),
                 jax.ShapeDtypeStruct((n, 8, SLAB_N), F32)]
    return pl.pallas_call(
        _ssm_prep_kernel,
        grid=(n,),
        in_specs=[blk(a) for a in args],
        out_specs=[blk(s) for s in out_shape],
        out_shape=out_shape,
        compiler_params=pltpu.CompilerParams(dimension_semantics=("parallel",), vmem_limit_bytes=VMEM_LIMIT),
        name="ssm_prep",
    )(*args)


def _expand_state_in(bcomp_ref, dst_ref, first, count):
    row_grp = lax.broadcasted_iota(jnp.int32, (LANES, LANES), 0) // SSM_P
    lane_half = lax.broadcasted_iota(jnp.int32, (LANES, LANES), 1) // SSM_N
    for j in range(count):
        for part in range(2):
            tile = bcomp_ref[(2 * (first + j) + part) * LANES:(2 * (first + j) + part + 1) * LANES, :]
            for k in range(SLAB_N // LANES):
                dst_ref[j * LANES:(j + 1) * LANES, part * SLAB_N + k * LANES:part * SLAB_N + (k + 1) * LANES] = (
                    jnp.where(row_grp == 2 * k + lane_half, tile, 0.0).astype(dst_ref.dtype))


def _expand_state_out(ccomp_ref, dst_ref, count):
    lane_grp = lax.broadcasted_iota(jnp.int32, (SSM_N, LANES), 1) // SSM_P
    for t in range(count):
        for part in range(2):
            tile = ccomp_ref[t * LANES + part * SSM_N:t * LANES + (part + 1) * SSM_N, :]
            for g in range(SLAB_G):
                dst_ref[part * SLAB_N + g * SSM_N:part * SLAB_N + (g + 1) * SSM_N, t * LANES:(t + 1) * LANES] = (
                    jnp.where(lane_grp == g, tile, 0.0).astype(dst_ref.dtype))


def _ssm_prompt_kernel(u_ref, wpair_ref, bcomp_ref, ccomp_ref, apw_ref, y_ref, hst_ref,
                       bfull_ref, cfull_ref, ub_ref, s_ref, yacc_ref, *col_refs):
    rows = SSM_HALF_B * N_CHUNK
    n_col = SLAB_N // LANES

    @pl.when(pl.program_id(1) == 0)
    def _():
        _expand_state_in(bcomp_ref, bfull_ref, 0, SSM_T)
        _expand_state_out(ccomp_ref, cfull_ref, SSM_T)

    sp_refs, hp_refs = col_refs[:n_col], col_refs[n_col:]
    for t in range(SSM_T):
        ub_ref[:, t * LANES:(t + 1) * LANES] = u_ref[pl.ds(t, rows, stride=SSM_T), :].astype(BF16)
    s_ref[...] = jnp.dot(ub_ref[...], bfull_ref[...], preferred_element_type=F32)
    for b in range(SSM_HALF_B):
        seq = slice(b * N_CHUNK, (b + 1) * N_CHUNK)
        for k in range(n_col):
            sp_refs[k][pl.ds(b, N_CHUNK, stride=8), :] = s_ref[seq, k * LANES:(k + 1) * LANES]
            sp_refs[k][pl.ds(SSM_HALF_B + b, N_CHUNK, stride=8), :] = (
                s_ref[seq, SLAB_N + k * LANES:SLAB_N + (k + 1) * LANES])
    upper = lax.broadcasted_iota(jnp.int32, (8, LANES), 0) < SSM_HALF_B
    a1 = [jnp.broadcast_to(apw_ref[2:3, k * LANES:(k + 1) * LANES], (8, LANES)) for k in range(n_col)]
    a2 = [jnp.where(upper, -1.0, 1.0) * jnp.broadcast_to(apw_ref[3:4, k * LANES:(k + 1) * LANES], (8, LANES))
          for k in range(n_col)]

    def step(c, h):
        r0 = pl.multiple_of(c * 8, 8)
        new = []
        for k in range(n_col):
            hp_refs[k][pl.ds(r0, 8), :] = h[k]
            new.append(a1[k] * h[k] + a2[k] * pltpu.roll(h[k], SSM_HALF_B, axis=0) + sp_refs[k][pl.ds(r0, 8), :])
        return tuple(new)

    h = lax.fori_loop(0, N_CHUNK, step, tuple(jnp.zeros((8, LANES), F32) for _ in range(n_col)), unroll=8)
    hst_ref[...] = jnp.concatenate(h, axis=1)
    hbc = jnp.concatenate(
        [jnp.concatenate([hp_refs[k][pl.ds(part + b, N_CHUNK, stride=8), :]
                          for part in (0, SSM_HALF_B) for k in range(n_col)], axis=1)
         for b in range(SSM_HALF_B)], axis=0)
    yacc_ref[...] = jnp.dot(hbc.astype(BF16), cfull_ref[...], preferred_element_type=F32)
    for t in range(0, SSM_T, 2):
        yacc_ref[:, t * LANES:] += jnp.dot(ub_ref[:, t * LANES:(t + 2) * LANES],
                                           wpair_ref[:, 0:(SSM_T - t) * LANES], preferred_element_type=F32)
    for t in range(SSM_T):
        y_ref[pl.ds(t, rows, stride=SSM_T), :] = yacc_ref[:, t * LANES:(t + 1) * LANES]


def _ssm_prompt(u, wpair, bcomp, ccomp, apw, layer):
    rows = SSM_HALF_B * N_CHUNK
    tok = SSM_HALF_B * SEQ

    def wspec(a):
        return pl.BlockSpec((None,) + a.shape[1:], lambda j, h: (layer * N_SLAB + j, 0, 0))

    return pl.pallas_call(
        _ssm_prompt_kernel,
        grid=(N_SLAB, BATCH // SSM_HALF_B),
        in_specs=[pl.BlockSpec((tok, LANES), lambda j, h: (h, j)), wspec(wpair), wspec(bcomp), wspec(ccomp), wspec(apw)],
        out_specs=[pl.BlockSpec((tok, LANES), lambda j, h: (h, j)),
                   pl.BlockSpec((None, 8, SLAB_N), lambda j, h: (h, 0, j))],
        out_shape=[jax.ShapeDtypeStruct((N_PROMPT, SSM_W), F32),
                   jax.ShapeDtypeStruct((BATCH // SSM_HALF_B, 8, SSM_G * SSM_N), F32)],
        scratch_shapes=[pltpu.VMEM((SSM_T * LANES, 2 * SLAB_N), BF16), pltpu.VMEM((2 * SLAB_N, SSM_T * LANES), BF16),
                        pltpu.VMEM((rows, SSM_T * LANES), BF16), pltpu.VMEM((rows, 2 * SLAB_N), F32),
                        pltpu.VMEM((rows, SSM_T * LANES), F32)]
                       + [pltpu.VMEM((N_CHUNK * 8, LANES), F32)] * (2 * SLAB_N // LANES),
        compiler_params=pltpu.CompilerParams(dimension_semantics=("arbitrary", "arbitrary"),
                                             vmem_limit_bytes=VMEM_LIMIT),
        name="ssm_prompt",
    )(u, wpair, bcomp, ccomp, apw)


def _ssm_sample_kernel(u_ref, h0r_ref, h0i_ref, wpair_ref, bcomp_ref, ccomp_ref, apw_ref,
                       y_ref, hr_ref, hi_ref, bhalf_ref, cfull_ref, uf_ref, yacc_ref):
    _expand_state_in(bcomp_ref, bhalf_ref, 0, DEC_SEQ)
    _expand_state_out(ccomp_ref, cfull_ref, DEC_SEQ)
    for t in range(DEC_SEQ):
        uf_ref[:, t * LANES:(t + 1) * LANES] = u_ref[pl.ds(t, DEC_BATCH, stride=DEC_SEQ), :]
    uf = uf_ref[...]
    h0r = h0r_ref[...]
    h0i = h0i_ref[...]
    ar = apw_ref[0:1, :]
    ai = apw_ref[1:2, :]
    s = jnp.dot(uf, bhalf_ref[...], precision=lax.Precision.HIGHEST, preferred_element_type=F32)
    hr_ref[...] = ar * h0r - ai * h0i + s[:, :SLAB_N]
    hi_ref[...] = ar * h0i + ai * h0r + s[:, SLAB_N:]
    ub = uf.astype(BF16)
    h0 = jnp.concatenate([h0r, h0i], axis=1).astype(BF16)
    yacc_ref[...] = jnp.dot(h0, cfull_ref[...], preferred_element_type=F32)
    for t in range(0, DEC_SEQ, 2):
        yacc_ref[:, t * LANES:] += jnp.dot(ub[:, t * LANES:(t + 2) * LANES],
                                           wpair_ref[:, 0:(DEC_SEQ - t) * LANES], preferred_element_type=F32)
    for t in range(DEC_SEQ):
        y_ref[pl.ds(t, DEC_BATCH, stride=DEC_SEQ), :] = yacc_ref[:, t * LANES:(t + 1) * LANES]


def _ssm_sample(u, h0re, h0im, wpair, bcomp, ccomp, apw, layer):
    width = DEC_SEQ * LANES

    def wspec(r, c):
        return pl.BlockSpec((None, r, c), lambda j: (layer * N_SLAB + j, 0, 0))

    st = pl.BlockSpec((DEC_BATCH, SLAB_N), lambda j: (0, j))
    return pl.pallas_call(
        _ssm_sample_kernel,
        grid=(N_SLAB,),
        in_specs=[pl.BlockSpec((N_SAMPLE, LANES), lambda j: (N_PROMPT // N_SAMPLE, j)), st, st,
                  wspec(2 * LANES, width),
                  pl.BlockSpec((None, 2 * DEC_SEQ * LANES, LANES),
                               lambda j: (layer * N_SLAB + j, SSM_T // DEC_SEQ - 1, 0)),
                  wspec(DEC_SEQ * LANES, LANES), wspec(8, SLAB_N)],
        out_specs=[pl.BlockSpec((N_SAMPLE, LANES), lambda j: (0, j)), st, st],
        out_shape=[jax.ShapeDtypeStruct((N_SAMPLE, SSM_W), F32),
                   jax.ShapeDtypeStruct((DEC_BATCH, SSM_G * SSM_N), F32),
                   jax.ShapeDtypeStruct((DEC_BATCH, SSM_G * SSM_N), F32)],
        scratch_shapes=[pltpu.VMEM((width, 2 * SLAB_N), F32), pltpu.VMEM((2 * SLAB_N, width), BF16),
                        pltpu.VMEM((DEC_BATCH, width), F32), pltpu.VMEM((DEC_BATCH, width), F32)],
        compiler_params=pltpu.CompilerParams(dimension_semantics=("arbitrary",), vmem_limit_bytes=VMEM_LIMIT),
        name="ssm_sample",
    )(u, h0re, h0im, wpair, bcomp, ccomp, apw)


def _mask_cap(valid):
    return jnp.where(valid, jnp.inf, NEG_INF).astype(F32)


def _head_masks():
    lane_kv = lax.broadcasted_iota(jnp.int32, (1, KV_W), 1) // HEAD_DIM
    return [jnp.where(lane_kv == h, HEAD_DIM ** -0.5, 0.0) for h in range(N_KV)], lane_kv == 0


ATTN_Q_TILE = 2048


def _attn_prompt_kernel(q_ref, kvp_ref, kvc_ref, sink_ref, o_ref):
    cols = GRP * WINDOW
    scale_masks, _ = _head_masks()
    kj = lax.broadcasted_iota(jnp.int32, (WINDOW, cols), 0)
    qi = lax.broadcasted_iota(jnp.int32, (WINDOW, cols), 1) & (WINDOW - 1)
    from_prev = kj > qi
    cap_first = _mask_cap(~(from_prev & (pl.program_id(1) == 0)))
    n_blk = ATTN_Q_TILE // WINDOW

    def stage(r):
        cur = kvc_ref[r * WINDOW:(r + 1) * WINDOW, :]
        prev = kvp_ref[...] if r == 0 else kvc_ref[(r - 1) * WINDOW:r * WINDOW, :]
        kk = jnp.concatenate([prev[:, :KV_W], cur[:, :KV_W]], axis=0).astype(BF16)
        v_t = jnp.concatenate([prev[:, KV_W:], cur[:, KV_W:]], axis=0).T.astype(BF16)
        q = q_ref[r * WINDOW:(r + 1) * WINDOW, :].astype(F32)
        s_t = []
        for h in range(N_KV):
            qx = jnp.concatenate([q[:, c * KV_W:(c + 1) * KV_W] * scale_masks[h] for c in range(GRP)], axis=0)
            s_all = lax.dot_general(kk, qx.astype(BF16), (((1,), (1,)), ((), ())), preferred_element_type=F32)
            s_t.append(jnp.where(from_prev, s_all[:WINDOW], s_all[WINDOW:]))
        return s_t, v_t

    def softmax_t(s, first, sink):
        if first:
            s = jnp.minimum(s, cap_first)
        m = jnp.maximum(jnp.max(s, axis=0, keepdims=True), sink)
        e = jnp.exp(s - m)
        denom = jnp.sum(e, axis=0, keepdims=True) + jnp.exp(sink - m)
        e_t = jnp.concatenate([jnp.where(from_prev, e, 0.0), jnp.where(from_prev, 0.0, e)], axis=0)
        return e_t.astype(BF16), denom

    staged = [stage(r) for r in range(n_blk)]
    weights = [[softmax_t(staged[r][0][h], r == 0, sink_ref[h]) for h in range(N_KV)] for r in range(n_blk)]
    for r in range(n_blk):
        v_t = staged[r][1]
        outs = [jnp.dot(v_t, e, preferred_element_type=F32) / denom for e, denom in weights[r]]
        o_t = jnp.concatenate([outs[h][h * HEAD_DIM:(h + 1) * HEAD_DIM] for h in range(N_KV)], axis=0)
        for c in range(GRP):
            o_ref[r * WINDOW:(r + 1) * WINDOW, c * KV_W:(c + 1) * KV_W] = (
                o_t[:, c * WINDOW:(c + 1) * WINDOW].T.astype(o_ref.dtype))


def _attn_prompt(q, kv, sink_col):
    tiles = SEQ // ATTN_Q_TILE
    per_tile = ATTN_Q_TILE // WINDOW
    blocks = SEQ // WINDOW
    return pl.pallas_call(
        _attn_prompt_kernel,
        grid=(BATCH, tiles),
        in_specs=[pl.BlockSpec((ATTN_Q_TILE, ATTN_W), lambda b, j: (b * tiles + j, 0)),
                  pl.BlockSpec((WINDOW, 2 * KV_W), lambda b, j: (b * blocks + jnp.maximum(j * per_tile - 1, 0), 0)),
                  pl.BlockSpec((ATTN_Q_TILE, 2 * KV_W), lambda b, j: (b * tiles + j, 0)),
                  pl.BlockSpec((N_KV, 1, GRP * WINDOW), lambda b, j: (0, 0, 0))],
        out_specs=pl.BlockSpec((ATTN_Q_TILE, ATTN_W), lambda b, j: (b * tiles + j, 0)),
        out_shape=jax.ShapeDtypeStruct((N_PROMPT, ATTN_W), BF16),
        compiler_params=pltpu.CompilerParams(dimension_semantics=("parallel", "arbitrary")),
        name="attn_prompt",
    )(q, kv, kv, sink_col)


SAMPLE_SEQ_BLOCK = 8


def _attn_sample_kernel(q_ref, ckt_ref, cvt_ref, kvn_ref, sink_ref, *rest, n_carried):
    o_ref, nkt_ref, nvt_ref = rest[n_carried:]
    rows = N_HEADS * DEC_SEQ
    scale_masks, first_kv = _head_masks()
    kept = WINDOW - DEC_SEQ
    nt = (((1,), (1,)), ((), ()))
    t = lax.broadcasted_iota(jnp.int32, (rows, WINDOW), 0) & (DEC_SEQ - 1)
    lane = lax.broadcasted_iota(jnp.int32, (rows, WINDOW), 1)
    cap_old = _mask_cap(lane > t)
    cap_new = _mask_cap((lane >= kept) & (lane - kept <= t))
    keep_old = lax.broadcasted_iota(jnp.int32, (HEAD_DIM, WINDOW), 1) < kept
    q_all = q_ref[...].astype(F32)
    sink = jnp.concatenate([sink_ref[h] for h in range(N_KV)], axis=0)
    place = (lax.broadcasted_iota(jnp.int32, (DEC_SEQ, WINDOW), 1)
             == lax.broadcasted_iota(jnp.int32, (DEC_SEQ, WINDOW), 0) + kept).astype(F32)
    toks = [slice(s * DEC_SEQ, (s + 1) * DEC_SEQ) for s in range(SAMPLE_SEQ_BLOCK)]
    new_ts = [lax.dot_general(kvn_ref[tok, :], place, (((0,), (0,)), ((), ())),
                              precision=lax.Precision.HIGHEST, preferred_element_type=F32) for tok in toks]
    scores, values = [], []
    for s, tok in enumerate(toks):
        new_t = new_ts[s]
        for h in range(N_KV):
            ch = slice(h * HEAD_DIM, (h + 1) * HEAD_DIM)
            nkt_ref[s, h] = jnp.where(keep_old, pltpu.roll(ckt_ref[s, h], kept, axis=1), new_t[ch])
            nvt_ref[s, h] = jnp.where(keep_old, pltpu.roll(cvt_ref[s, h], kept, axis=1), new_t[KV_W:][ch])
        q = jnp.concatenate([q_all[tok, c * KV_W:(c + 1) * KV_W] * scale_masks[h]
                             for h in range(N_KV) for c in range(GRP)], axis=0).astype(BF16)
        old_k = ckt_ref[s].reshape(KV_W, WINDOW).astype(BF16)
        old_v = cvt_ref[s].reshape(KV_W, WINDOW).astype(BF16)
        new_k, new_v = new_t[:KV_W].astype(BF16), new_t[KV_W:].astype(BF16)
        scores.append((jnp.dot(q, old_k, preferred_element_type=F32), jnp.dot(q, new_k, preferred_element_type=F32)))
        values.append((old_v, new_v))
    weights = []
    for s_old, s_new in scores:
        s_old = jnp.minimum(s_old, cap_old)
        s_new = jnp.minimum(s_new, cap_new)
        m = jnp.maximum(jnp.maximum(jnp.max(s_old, axis=-1, keepdims=True), jnp.max(s_new, axis=-1, keepdims=True)),
                        sink)
        e_old = jnp.exp(s_old - m)
        e_new = jnp.exp(s_new - m)
        denom = (jnp.sum(e_old, axis=-1, keepdims=True) + jnp.sum(e_new, axis=-1, keepdims=True)
                 + jnp.exp(sink - m))
        weights.append((e_old.astype(BF16), e_new.astype(BF16), denom))
    half = GRP * DEC_SEQ
    for s, tok in enumerate(toks):
        e_old, e_new, denom = weights[s]
        v_old, v_new = values[s]
        o = (lax.dot_general(e_old, v_old, nt, preferred_element_type=F32)
             + lax.dot_general(e_new, v_new, nt, preferred_element_type=F32)) / denom
        for c in range(GRP):
            o_ref[tok, c * KV_W:(c + 1) * KV_W] = jnp.where(
                first_kv, o[c * DEC_SEQ:(c + 1) * DEC_SEQ], o[half + c * DEC_SEQ:half + (c + 1) * DEC_SEQ])


def _attn_sample(q, kv, ck, cv, sink_col, layer, windows):
    sb = SAMPLE_SEQ_BLOCK
    tok = sb * DEC_SEQ
    first = N_PROMPT // tok
    c_spec = pl.BlockSpec((None, sb, N_KV, HEAD_DIM, WINDOW), lambda i: (layer, i, 0, 0, 0))
    in_specs = [pl.BlockSpec((tok, ATTN_W), lambda i: (first + i, 0)), c_spec, c_spec,
                pl.BlockSpec((tok, 2 * KV_W), lambda i: (first + i, 0)),
                pl.BlockSpec((N_KV, GRP * DEC_SEQ, 1), lambda i: (0, 0, 0))]
    args = [q, ck, cv, kv, sink_col]
    aliases = {}
    if windows is not None:
        aliases = {len(args): 1, len(args) + 1: 2}
        in_specs += [pl.BlockSpec(memory_space=pl.ANY)] * 2
        args += list(windows)
    return pl.pallas_call(
        functools.partial(_attn_sample_kernel, n_carried=len(aliases)),
        grid=(DEC_BATCH // sb,),
        in_specs=in_specs,
        out_specs=[pl.BlockSpec((tok, ATTN_W), lambda i: (i, 0)), c_spec, c_spec],
        out_shape=[jax.ShapeDtypeStruct((N_SAMPLE, ATTN_W), F32),
                   jax.ShapeDtypeStruct((DEPTH, DEC_BATCH, N_KV, HEAD_DIM, WINDOW), F32),
                   jax.ShapeDtypeStruct((DEPTH, DEC_BATCH, N_KV, HEAD_DIM, WINDOW), F32)],
        input_output_aliases=aliases,
        compiler_params=pltpu.CompilerParams(dimension_semantics=("parallel",)),
        name="attn_sample",
    )(*args)


def _mix_out(x, y, u, o, d_ref, wglu_ref, gs_ref, ga_ref, wout_s_ref, wout_a_ref):
    y = y + d_ref[...] * u
    y = jax.nn.gelu(y)
    y = y * jax.nn.sigmoid(jnp.dot(y.astype(BF16), wglu_ref[...], preferred_element_type=F32))
    ys = _rms(y, gs_ref[...]).astype(BF16)
    ya = _rms(o, ga_ref[...]).astype(BF16)
    mix = jnp.dot(ys, wout_s_ref[...], preferred_element_type=F32)
    mix = mix + jnp.dot(ya, wout_a_ref[...], preferred_element_type=F32)
    return x + mix


def _tail_kernel(x_ref, yp_ref, ys_ref, u_ref, op_ref, os_ref, pp_ref, ps_ref,
                 d_ref, wglu_ref, gs_ref, ga_ref, wout_s_ref, wout_a_ref,
                 g2_ref, wg_ref, wu_ref, wd_ref, gp_ref, wpg_ref, wpp_ref, gf_ref,
                 *out_refs, final):
    x, y, u, o = x_ref[...], _pick(yp_ref, ys_ref), u_ref[...], _pick(op_ref, os_ref)
    x1 = jnp.concatenate([_mix_out(x[r], y[r], u[r], o[r], d_ref, wglu_ref, gs_ref, ga_ref, wout_s_ref, wout_a_ref)
                          for r in _row_slices(EPILOGUE_SLICES)], axis=0)
    proj = jnp.dot(_pick(pp_ref, ps_ref).astype(BF16), wpp_ref[...], preferred_element_type=F32)
    x2 = _half_swiglu(x1, g2_ref, wg_ref, wu_ref, wd_ref)
    x3 = []
    for r in _row_slices(EPILOGUE_SLICES):
        h = _rms(x2[r], gp_ref[...]).astype(BF16)
        gate = jax.nn.sigmoid(jnp.dot(h, wpg_ref[...], preferred_element_type=F32))
        x3.append(x2[r] + proj[r] * gate)
    x3 = jnp.concatenate(x3, axis=0)
    if not final:
        out_refs[0][...] = x3
        return
    y = _rms(x3, gf_ref[...])
    yp_ref, ys_ref = out_refs

    @pl.when(pl.program_id(0) < PROMPT_TILES)
    def _():
        yp_ref[...] = y

    @pl.when(pl.program_id(0) >= PROMPT_TILES)
    def _():
        ys_ref[...] = y


def _layer_tail(x, y_prompt, y_sample, u, o_prompt, o_sample, p_prompt, p_sample, layer,
                d, wglu, gs, ga, wout_ssm, wout_attn, g2, wg, wu, wd, gp, wpg, wpp, gf, final):
    if final:
        out_specs = _pair_specs(D_MODEL)
        out_shape = [jax.ShapeDtypeStruct((N_PROMPT, D_MODEL), F32), jax.ShapeDtypeStruct((N_SAMPLE, D_MODEL), F32)]
    else:
        out_specs = _row_spec(D_MODEL)
        out_shape = jax.ShapeDtypeStruct((N_TOK, D_MODEL), F32)
    p_specs = [pl.BlockSpec((None, TOKEN_TILE, PLE_DIM), lambda i: (layer, jnp.minimum(i, PROMPT_TILES - 1), 0)),
               pl.BlockSpec((None, TOKEN_TILE, PLE_DIM), lambda i: (layer, jnp.maximum(i - PROMPT_TILES, 0), 0))]
    return pl.pallas_call(
        functools.partial(_tail_kernel, final=final),
        grid=(N_TOK // TOKEN_TILE,),
        in_specs=[_row_spec(D_MODEL)] + _pair_specs(SSM_W) + [_row_spec(SSM_W)] + _pair_specs(ATTN_W) + p_specs + [
            _const_spec((1, SSM_W)), _layer_spec((SSM_W, SSM_W), layer), _const_spec((1, SSM_W)),
            _const_spec((1, ATTN_W)),
            _layer_spec((SSM_W, D_MODEL), layer), _layer_spec((ATTN_W, D_MODEL), layer),
            _const_spec((1, D_MODEL)),
            _const_spec((D_MODEL, D_FF)), _const_spec((D_MODEL, D_FF)), _const_spec((D_FF, D_MODEL)),
            _const_spec((1, D_MODEL)), _layer_spec((D_MODEL, D_MODEL), layer), _layer_spec((PLE_DIM, D_MODEL), layer),
            _const_spec((1, D_MODEL))],
        out_specs=out_specs,
        out_shape=out_shape,
        compiler_params=pltpu.CompilerParams(dimension_semantics=("arbitrary",), vmem_limit_bytes=VMEM_LIMIT),
        name="layer_tail",
    )(x, y_prompt, y_sample, u, o_prompt, o_sample, p_prompt, p_sample,
      d, wglu, gs, ga, wout_ssm, wout_attn, g2, wg, wu, wd, gp, wpg, wpp, gf)


def _sink_column(sinks, rows_per_head):
    return jnp.repeat(sinks.astype(F32).reshape(N_KV, GRP), rows_per_head, axis=1)[..., None]


def _head_major(a, axis):
    shape = a.shape
    a = a.reshape(shape[:axis] + (N_KV, GRP, HEAD_DIM) + shape[axis + 1:])
    return jnp.swapaxes(a, axis, axis + 1).reshape(shape)


def kernel(x_prompt, x_sample, cache_k, cache_v, state_ssm_re, state_ssm_im, p_prompt, p_sample, ffn1_norm, ffn1_w_gate, ffn1_w_up, ffn1_w_down, mix_norm, w_in, ssm_lam_re, ssm_lam_im, ssm_log_dt, ssm_b_re, ssm_b_im, ssm_c_re, ssm_c_im, ssm_d, ssm_w_glu, ssm_out_norm, attn_sinks, attn_out_norm, w_out, ffn2_norm, ffn2_w_gate, ffn2_w_up, ffn2_w_down, ple_norm, ple_w_gate, ple_w_proj, final_norm):
    wpair, bcomp, ccomp, apw = _ssm_prep(ssm_lam_re, ssm_lam_im, ssm_log_dt, ssm_b_re, ssm_b_im, ssm_c_re, ssm_c_im)
    row = lambda v: v.reshape(1, -1)
    bf = lambda w: w.astype(BF16)
    xs = (x_prompt.reshape(N_PROMPT, D_MODEL), x_sample.reshape(N_SAMPLE, D_MODEL))
    n_state = SSM_G * SSM_N

    caches = [c.transpose(0, 1, 3, 4, 2) for c in (cache_k, cache_v)]
    windows = None
    ffn1_w = (bf(ffn1_w_gate), bf(ffn1_w_up), bf(ffn1_w_down))
    w_in_bf, w_out_bf, w_glu_bf = bf(w_in), bf(w_out), bf(ssm_w_glu)
    ple_gate_bf, ple_proj_bf = bf(ple_w_gate), bf(ple_w_proj)
    w_in_q = _head_major(w_in_bf[:, :, SSM_W:SSM_W + ATTN_W], 2)
    w_out_attn = _head_major(w_out_bf[:, SSM_W:], 1)

    kp_l, vp_l, hpr_l, hpi_l, hsr_l, hsi_l = [], [], [], [], [], []
    for i in range(DEPTH):
        x, u, q, kv, kv_last, *ffn2_w = _ffn_proj(
            xs, row(ffn1_norm[i]), *ffn1_w, row(mix_norm[i]), w_in_bf, w_in_q,
            cast=(ffn2_w_gate, ffn2_w_up, ffn2_w_down), layer=i)

        y_p, hst = _ssm_prompt(u, wpair, bcomp, ccomp, apw, i)
        hst = hst.reshape(BATCH // SSM_HALF_B, 2, SSM_HALF_B, SSM_G, SSM_N)
        hpr_l.append(hst[:, 0].reshape(BATCH, SSM_G, SSM_N))
        hpi_l.append(hst[:, 1].reshape(BATCH, SSM_G, SSM_N))
        y_s, hsr, hsi = _ssm_sample(u, state_ssm_re[i].reshape(DEC_BATCH, n_state),
                                    state_ssm_im[i].reshape(DEC_BATCH, n_state), wpair, bcomp, ccomp, apw, i)
        hsr_l.append(hsr.reshape(DEC_BATCH, SSM_G, SSM_N))
        hsi_l.append(hsi.reshape(DEC_BATCH, SSM_G, SSM_N))

        kv_last = kv_last.reshape(BATCH, WINDOW, 2, N_KV, HEAD_DIM)
        kp_l.append(kv_last[:, :, 0])
        vp_l.append(kv_last[:, :, 1])
        o_p = _attn_prompt(q, kv, _sink_column(attn_sinks[i], WINDOW).reshape(N_KV, 1, GRP * WINDOW))
        o_s, *windows = _attn_sample(q, kv, caches[0], caches[1], _sink_column(attn_sinks[i], DEC_SEQ), i, windows)

        x = _layer_tail(x, y_p, y_s, u, o_p, o_s,
                        p_prompt.reshape(DEPTH, N_PROMPT, PLE_DIM), p_sample.reshape(DEPTH, N_SAMPLE, PLE_DIM), i,
                        row(ssm_d[i]), w_glu_bf, row(ssm_out_norm[i]), row(_head_major(attn_out_norm[i], 0)),
                        w_out_bf, w_out_attn,
                        row(ffn2_norm[i]), *ffn2_w,
                        row(ple_norm[i]), ple_gate_bf, ple_proj_bf, row(final_norm),
                        final=(i == DEPTH - 1))
        xs = (x,)

    y_prompt, y_sample = x
    ks, vs = (w.transpose(0, 1, 4, 2, 3) for w in windows)
    return (y_prompt.reshape(BATCH, SEQ, D_MODEL), y_sample.reshape(DEC_BATCH, DEC_SEQ, D_MODEL),
            jnp.stack(kp_l), jnp.stack(vp_l), jnp.stack(hpr_l), jnp.stack(hpi_l),
            ks, vs, jnp.stack(hsr_l), jnp.stack(hsi_l))
```
